```python
import jax, jax.numpy as jnp
from jax import lax
import numpy as np

D_MODEL = 2048
BATCH = 8
SEQ = 8192
DEPTH = 4

GRID_W = 64
CTX_LEN = 256

N_BRANCH = 4
BRANCH_W = 512
HEAD_DIM = 64
NA_HEADS = BRANCH_W // HEAD_DIM
NA_WIN_ROWS = 8
NA_WIN_COLS = 16
POOL_SIZES = (2, 4, 8, 16)
POOL_GROUP = BRANCH_W // len(POOL_SIZES)
GQA_Q_HEADS = BRANCH_W // HEAD_DIM
GQA_KV_HEADS = 2
Q_BLOCK = 128
ROPE_THETA = 10000.0
CONV_WIDTH = 31
EPS = 1e-6

PROJ_PARTS = (
    ("a_k", BRANCH_W), ("a_v", BRANCH_W),
    ("c_k", GQA_KV_HEADS * HEAD_DIM), ("c_v", GQA_KV_HEADS * HEAD_DIM),
    ("a_q", BRANCH_W), ("c_q", BRANCH_W),
    ("a_gate", BRANCH_W), ("b_in", BRANCH_W), ("b_gate", BRANCH_W),
    ("c_gate", BRANCH_W), ("d_glu", 2 * BRANCH_W), ("d_gate", BRANCH_W),
    ("merge", N_BRANCH * D_MODEL),
)
KV_COLS = 2 * BRANCH_W + 2 * GQA_KV_HEADS * HEAD_DIM
IN_COLS = sum(w for _, w in PROJ_PARTS)

kernel_name = "hybrid_parallel_gated_diffusion_block"


def split_proj(p):
    out = {}
    off = 0
    for name, w in PROJ_PARTS:
        if off + w > p.shape[-1]:
            break
        out[name] = p[..., off:off + w]
        off += w
    return out


def rms_norm(x, g):
    xf = x.astype(jnp.float32)
    y = xf * lax.rsqrt(jnp.mean(xf * xf, axis=-1, keepdims=True) + EPS)
    return (y * g).astype(x.dtype)


def layer_norm(x, g, b):
    xf = x.astype(jnp.float32)
    mu = jnp.mean(xf, axis=-1, keepdims=True)
    var = jnp.mean(jnp.square(xf - mu), axis=-1, keepdims=True)
    return ((xf - mu) * lax.rsqrt(var + EPS) * g + b).astype(x.dtype)


def modulate(x, g, shift, scale):
    return rms_norm(x, g) * (1 + scale) + shift


def _rope_axis(x, pos):
    half = x.shape[-1] // 2
    freqs = ROPE_THETA ** (-jnp.arange(half, dtype=jnp.float32) / half)
    ang = pos.astype(jnp.float32)[:, None] * freqs[None, :]
    cos = jnp.cos(ang)[:, None, :]
    sin = jnp.sin(ang)[:, None, :]
    xf = x.astype(jnp.float32)
    x1, x2 = xf[..., :half], xf[..., half:]
    return jnp.concatenate([x1 * cos - x2 * sin, x2 * cos + x1 * sin], axis=-1).astype(x.dtype)


def rope_2d(x, rows, cols):
    a = x.shape[-1] // 2
    return jnp.concatenate([_rope_axis(x[..., :a], rows), _rope_axis(x[..., a:], cols)], axis=-1)


def neighbourhood_attention(q, k, v, k_ctx, v_ctx, rpb):
    B, S, H, hd = q.shape
    R = S // GRID_W
    wr = min(NA_WIN_ROWS, R)
    r = jnp.arange(R)
    row_start = jnp.clip(r - wr // 2, 0, R - wr)
    band_rows = row_start[:, None] + jnp.arange(wr)[None, :]
    cq = jnp.arange(GRID_W)
    col_start = jnp.clip(cq - NA_WIN_COLS // 2, 0, GRID_W - NA_WIN_COLS)
    kc = jnp.arange(GRID_W)
    col_ok = (kc[None, :] >= col_start[:, None]) & (kc[None, :] < col_start[:, None] + NA_WIN_COLS)
    row_off = band_rows - r[:, None] + (NA_WIN_ROWS - 1)
    col_off = jnp.clip(kc[None, :] - cq[:, None], -(NA_WIN_COLS - 1), NA_WIN_COLS - 1) + (NA_WIN_COLS - 1)
    bias = rpb[:, row_off[:, None, :, None], col_off[None, :, None, :]].astype(jnp.float32)
    bias = jnp.where(col_ok[None, None, :, None, :], bias, -jnp.inf)
    scale = hd ** -0.5
    qg = q.reshape(B, R, GRID_W, H, hd)
    kg = k.reshape(B, R, GRID_W, H, hd)[:, band_rows]
    vg = v.reshape(B, R, GRID_W, H, hd)[:, band_rows]
    s_band = jnp.einsum('brqhd,brjkhd->bhrqjk', qg, kg).astype(jnp.float32) * scale + bias[None]
    s_ctx = jnp.einsum('brqhd,bchd->bhrqc', qg, k_ctx).astype(jnp.float32) * scale
    nb = wr * GRID_W
    s = jnp.concatenate([s_band.reshape(B, H, R, GRID_W, nb), s_ctx], axis=-1)
    p = jax.nn.softmax(s, axis=-1).astype(v.dtype)
    p_band = p[..., :nb].reshape(B, H, R, GRID_W, wr, GRID_W)
    p_ctx = p[..., nb:]
    o = jnp.einsum('bhrqjk,brjkhd->brqhd', p_band, vg) + jnp.einsum('bhrqc,bchd->brqhd', p_ctx, v_ctx)
    return o.reshape(B, S, H, hd)


def dense_attention(q, k, v):
    B, L, Hq, hd = q.shape
    Hk = k.shape[2]
    qg = q.reshape(B, L, Hk, Hq // Hk, hd)
    s = jnp.einsum('bqkgd,bnkd->bkgqn', qg, k).astype(jnp.float32) * (hd ** -0.5)
    p = jax.nn.softmax(s, axis=-1).astype(v.dtype)
    return jnp.einsum('bkgqn,bnkd->bqkgd', p, v).reshape(B, L, Hq, hd)


def gqa_latent_attention(q, k, v, k_ctx, v_ctx):
    B, S, Hq, hd = q.shape
    k_all = jnp.concatenate([k, k_ctx], axis=1)
    v_all = jnp.concatenate([v, v_ctx], axis=1)
    nblk = S // Q_BLOCK
    qb = jnp.moveaxis(q.reshape(B, nblk, Q_BLOCK, Hq, hd), 1, 0)
    o = lax.map(lambda qi: dense_attention(qi, k_all, v_all), qb)
    return jnp.moveaxis(o, 0, 1).reshape(B, S, Hq, hd)


def multiscale_pool(u, w_pool, pool_scale):
    B, L, W = u.shape
    uf = u.astype(jnp.float32)
    csum = jnp.concatenate([jnp.zeros((B, 1, W), jnp.float32), jnp.cumsum(uf, axis=1)], axis=1)
    t = jnp.arange(L)
    outs = []
    for gi, ksz in enumerate(POOL_SIZES):
        sl = slice(gi * POOL_GROUP, (gi + 1) * POOL_GROUP)
        lo = jnp.clip(t - ksz // 2, 0, L - 1)
        hi = jnp.clip(t + ksz - 1 - ksz // 2, 0, L - 1)
        cg = csum[..., sl]
        mean = (cg[:, hi + 1] - cg[:, lo]) / (hi - lo + 1).astype(jnp.float32)[None, :, None]
        d = (mean - uf[..., sl]).astype(u.dtype)
        outs.append(d @ w_pool[gi])
    return jnp.concatenate(outs, axis=-1) * pool_scale


def conformer_conv(glu_in, conv_w, conv_b, ln_g, ln_b, w_pw):
    a, g = jnp.split(glu_in, 2, axis=-1)
    u = a * jax.nn.sigmoid(g)
    y = lax.conv_general_dilated(
        u, conv_w[:, None, :].astype(u.dtype), window_strides=(1,),
        padding=((CONV_WIDTH // 2, CONV_WIDTH // 2),),
        dimension_numbers=('NWC', 'WIO', 'NWC'), feature_group_count=BRANCH_W) + conv_b
    y = jax.nn.silu(layer_norm(y, ln_g, ln_b))
    return y @ w_pw


def branch_merge(p, o_a, o_b, o_c, o_d, w_branch, w_out):
    outs = (o_a * jax.nn.silu(p['a_gate']), o_b * jax.nn.silu(p['b_gate']),
            o_c * jax.nn.silu(p['c_gate']), o_d * jax.nn.silu(p['d_gate']))
    gates = jax.nn.sigmoid(p['merge'])
    y = 0
    for bi in range(N_BRANCH):
        y = y + gates[..., bi * D_MODEL:(bi + 1) * D_MODEL] * (outs[bi] @ w_branch[bi])
    return y @ w_out


def _fwd_setup_inputs(seed: int = 0) -> dict:
    key = jax.random.key(seed)
    ks = jax.random.split(key, 24)
    f = jnp.float32
    nrm = lambda k, shape, s: jax.random.normal(k, shape, f) * s
    return {
        "x": nrm(ks[0], (BATCH, SEQ, D_MODEL), 1.0),
        "c": nrm(ks[1], (BATCH, D_MODEL), 1.0),
        "ctx": nrm(ks[2], (BATCH, CTX_LEN, D_MODEL), 1.0),
        "c_ctx": nrm(ks[3], (D_MODEL,), 1.0),
        "w_ada": nrm(ks[4], (DEPTH, D_MODEL, 3 * D_MODEL), D_MODEL ** -0.5),
        "b_ada": nrm(ks[5], (DEPTH, 3 * D_MODEL), 0.01),
        "g_pre": 1.0 + nrm(ks[6], (DEPTH, D_MODEL), 0.05),
        "g_post": 1.0 + nrm(ks[7], (DEPTH, D_MODEL), 0.05),
        "w_in": nrm(ks[8], (DEPTH, D_MODEL, IN_COLS), D_MODEL ** -0.5),
        "na_rpb": nrm(ks[9], (DEPTH, NA_HEADS, 2 * NA_WIN_ROWS - 1, 2 * NA_WIN_COLS - 1), 0.1),
        "pool_w": nrm(ks[10], (DEPTH, len(POOL_SIZES), POOL_GROUP, POOL_GROUP), POOL_GROUP ** -0.5),
        "pool_scale": 1.0 + nrm(ks[11], (DEPTH, BRANCH_W), 0.1),
        "q_norm": 1.0 + nrm(ks[12], (DEPTH, HEAD_DIM), 0.05),
        "k_norm": 1.0 + nrm(ks[13], (DEPTH, HEAD_DIM), 0.05),
        "conv_w": nrm(ks[14], (DEPTH, CONV_WIDTH, BRANCH_W), CONV_WIDTH ** -0.5),
        "conv_b": nrm(ks[15], (DEPTH, BRANCH_W), 0.01),
        "conv_ln_g": 1.0 + nrm(ks[16], (DEPTH, BRANCH_W), 0.05),
        "conv_ln_b": nrm(ks[17], (DEPTH, BRANCH_W), 0.01),
        "conv_pw": nrm(ks[18], (DEPTH, BRANCH_W, BRANCH_W), BRANCH_W ** -0.5),
        "w_branch": nrm(ks[19], (DEPTH, N_BRANCH, BRANCH_W, D_MODEL), BRANCH_W ** -0.5),
        "w_out": nrm(ks[20], (DEPTH, D_MODEL, D_MODEL), D_MODEL ** -0.5),
    }


def _fwd_reference(x, c, ctx, c_ctx, w_ada, b_ada, g_pre, g_post, w_in, na_rpb, pool_w, pool_scale,
              q_norm, k_norm, conv_w, conv_b, conv_ln_g, conv_ln_b, conv_pw, w_branch, w_out):
    B, S, _ = x.shape
    Cn = ctx.shape[1]
    t = jnp.arange(S)
    rows = t // GRID_W
    cols = t % GRID_W
    hd = HEAD_DIM
    for l in range(DEPTH):
        last = l == DEPTH - 1
        shift, scale, gate = jnp.split(jax.nn.silu(c) @ w_ada[l] + b_ada[l], 3, axis=-1)
        shift_c, scale_c, gate_c = jnp.split(jax.nn.silu(c_ctx) @ w_ada[l] + b_ada[l], 3, axis=-1)
        h = modulate(x, g_pre[l], shift[:, None], scale[:, None])
        hc = modulate(ctx, g_pre[l], shift_c, scale_c)
        p = split_proj(h @ w_in[l])
        pc = split_proj(hc @ (w_in[l, :, :KV_COLS] if last else w_in[l]))
        ka_c = pc['a_k'].reshape(B, Cn, NA_HEADS, hd)
        va_c = pc['a_v'].reshape(B, Cn, NA_HEADS, hd)
        kc_c = rms_norm(pc['c_k'].reshape(B, Cn, GQA_KV_HEADS, hd), k_norm[l])
        vc_c = pc['c_v'].reshape(B, Cn, GQA_KV_HEADS, hd)
        o_a = neighbourhood_attention(p['a_q'].reshape(B, S, NA_HEADS, hd), p['a_k'].reshape(B, S, NA_HEADS, hd),
                                      p['a_v'].reshape(B, S, NA_HEADS, hd), ka_c, va_c, na_rpb[l])
        o_b = multiscale_pool(p['b_in'], pool_w[l], pool_scale[l])
        qc = rope_2d(rms_norm(p['c_q'].reshape(B, S, GQA_Q_HEADS, hd), q_norm[l]), rows, cols)
        kc = rope_2d(rms_norm(p['c_k'].reshape(B, S, GQA_KV_HEADS, hd), k_norm[l]), rows, cols)
        vc = p['c_v'].reshape(B, S, GQA_KV_HEADS, hd)
        o_c = gqa_latent_attention(qc, kc, vc, kc_c, vc_c)
        o_d = conformer_conv(p['d_glu'], conv_w[l], conv_b[l], conv_ln_g[l], conv_ln_b[l], conv_pw[l])
        y = branch_merge(p, o_a.reshape(B, S, BRANCH_W), o_b, o_c.reshape(B, S, BRANCH_W), o_d,
                         w_branch[l], w_out[l])
        x_next = x + gate[:, None] * rms_norm(y, g_post[l])
        if not last:
            o_a_c = dense_attention(pc['a_q'].reshape(B, Cn, NA_HEADS, hd), ka_c, va_c)
            o_b_c = multiscale_pool(pc['b_in'], pool_w[l], pool_scale[l])
            qc_c = rms_norm(pc['c_q'].reshape(B, Cn, GQA_Q_HEADS, hd), q_norm[l])
            o_c_c = dense_attention(qc_c, kc_c, vc_c)
            o_d_c = conformer_conv(pc['d_glu'], conv_w[l], conv_b[l], conv_ln_g[l], conv_ln_b[l], conv_pw[l])
            y_c = branch_merge(pc, o_a_c.reshape(B, Cn, BRANCH_W), o_b_c, o_c_c.reshape(B, Cn, BRANCH_W),
                               o_d_c, w_branch[l], w_out[l])
            ctx = ctx + gate_c * rms_norm(y_c, g_post[l])
        x = x_next
    return x


import jax as _jax
import jax.numpy as _jnp

TWIN_FORMAT = 'train_step'
FWD_PARAMS = ['x', 'c', 'ctx', 'c_ctx', 'w_ada', 'b_ada', 'g_pre', 'g_post', 'w_in', 'na_rpb', 'pool_w', 'pool_scale', 'q_norm', 'k_norm', 'conv_w', 'conv_b', 'conv_ln_g', 'conv_ln_b', 'conv_pw', 'w_branch', 'w_out']
TWIN_WEIGHTS = ['c_ctx', 'w_ada', 'b_ada', 'g_pre', 'g_post', 'w_in', 'na_rpb', 'pool_w', 'pool_scale', 'q_norm', 'k_norm', 'conv_w', 'conv_b', 'conv_ln_g', 'conv_ln_b', 'conv_pw', 'w_branch', 'w_out']
TWIN_DIFF_INPUT = 'x'
TWIN_INPUTS = ['x', 'c', 'ctx', 'c_ctx', 'w_ada', 'b_ada', 'g_pre', 'g_post', 'w_in', 'na_rpb', 'pool_w', 'pool_scale', 'q_norm', 'k_norm', 'conv_w', 'conv_b', 'conv_ln_g', 'conv_ln_b', 'conv_pw', 'w_branch', 'w_out', 'loss_target', 'm_c_ctx', 'm_w_ada', 'm_b_ada', 'm_g_pre', 'm_g_post', 'm_w_in', 'm_na_rpb', 'm_pool_w', 'm_pool_scale', 'm_q_norm', 'm_k_norm', 'm_conv_w', 'm_conv_b', 'm_conv_ln_g', 'm_conv_ln_b', 'm_conv_pw', 'm_w_branch', 'm_w_out', 'v_c_ctx', 'v_w_ada', 'v_b_ada', 'v_g_pre', 'v_g_post', 'v_w_in', 'v_na_rpb', 'v_pool_w', 'v_pool_scale', 'v_q_norm', 'v_k_norm', 'v_conv_w', 'v_conv_b', 'v_conv_ln_g', 'v_conv_ln_b', 'v_conv_pw', 'v_w_branch', 'v_w_out']
TWIN_OUTPUTS = ['loss', 'grad_x', 'grad_c_ctx', 'grad_w_ada', 'grad_b_ada', 'grad_g_pre', 'grad_g_post', 'grad_w_in', 'grad_na_rpb', 'grad_pool_w', 'grad_pool_scale', 'grad_q_norm', 'grad_k_norm', 'grad_conv_w', 'grad_conv_b', 'grad_conv_ln_g', 'grad_conv_ln_b', 'grad_conv_pw', 'grad_w_branch', 'grad_w_out', 'delta_c_ctx', 'delta_w_ada', 'delta_b_ada', 'delta_g_pre', 'delta_g_post', 'delta_w_in', 'delta_na_rpb', 'delta_pool_w', 'delta_pool_scale', 'delta_q_norm', 'delta_k_norm', 'delta_conv_w', 'delta_conv_b', 'delta_conv_ln_g', 'delta_conv_ln_b', 'delta_conv_pw', 'delta_w_branch', 'delta_w_out', 'new_m_c_ctx', 'new_m_w_ada', 'new_m_b_ada', 'new_m_g_pre', 'new_m_g_post', 'new_m_w_in', 'new_m_na_rpb', 'new_m_pool_w', 'new_m_pool_scale', 'new_m_q_norm', 'new_m_k_norm', 'new_m_conv_w', 'new_m_conv_b', 'new_m_conv_ln_g', 'new_m_conv_ln_b', 'new_m_conv_pw', 'new_m_w_branch', 'new_m_w_out', 'new_v_c_ctx', 'new_v_w_ada', 'new_v_b_ada', 'new_v_g_pre', 'new_v_g_post', 'new_v_w_in', 'new_v_na_rpb', 'new_v_pool_w', 'new_v_pool_scale', 'new_v_q_norm', 'new_v_k_norm', 'new_v_conv_w', 'new_v_conv_b', 'new_v_conv_ln_g', 'new_v_conv_ln_b', 'new_v_conv_pw', 'new_v_w_branch', 'new_v_w_out']
TWIN_LEAF_KINDS = {'loss': 'loss', 'grad_x': 'grad_x', 'grad_c_ctx': 'grad_w', 'grad_w_ada': 'grad_w', 'grad_b_ada': 'grad_w', 'grad_g_pre': 'grad_w', 'grad_g_post': 'grad_w', 'grad_w_in': 'grad_w', 'grad_na_rpb': 'grad_w', 'grad_pool_w': 'grad_w', 'grad_pool_scale': 'grad_w', 'grad_q_norm': 'grad_w', 'grad_k_norm': 'grad_w', 'grad_conv_w': 'grad_w', 'grad_conv_b': 'grad_w', 'grad_conv_ln_g': 'grad_w', 'grad_conv_ln_b': 'grad_w', 'grad_conv_pw': 'grad_w', 'grad_w_branch': 'grad_w', 'grad_w_out': 'grad_w', 'delta_c_ctx': 'delta_w', 'delta_w_ada': 'delta_w', 'delta_b_ada': 'delta_w', 'delta_g_pre': 'delta_w', 'delta_g_post': 'delta_w', 'delta_w_in': 'delta_w', 'delta_na_rpb': 'delta_w', 'delta_pool_w': 'delta_w', 'delta_pool_scale': 'delta_w', 'delta_q_norm': 'delta_w', 'delta_k_norm': 'delta_w', 'delta_conv_w': 'delta_w', 'delta_conv_b': 'delta_w', 'delta_conv_ln_g': 'delta_w', 'delta_conv_ln_b': 'delta_w', 'delta_conv_pw': 'delta_w', 'delta_w_branch': 'delta_w', 'delta_w_out': 'delta_w', 'new_m_c_ctx': 'new_m', 'new_m_w_ada': 'new_m', 'new_m_b_ada': 'new_m', 'new_m_g_pre': 'new_m', 'new_m_g_post': 'new_m', 'new_m_w_in': 'new_m', 'new_m_na_rpb': 'new_m', 'new_m_pool_w': 'new_m', 'new_m_pool_scale': 'new_m', 'new_m_q_norm': 'new_m', 'new_m_k_norm': 'new_m', 'new_m_conv_w': 'new_m', 'new_m_conv_b': 'new_m', 'new_m_conv_ln_g': 'new_m', 'new_m_conv_ln_b': 'new_m', 'new_m_conv_pw': 'new_m', 'new_m_w_branch': 'new_m', 'new_m_w_out': 'new_m', 'new_v_c_ctx': 'new_v', 'new_v_w_ada': 'new_v', 'new_v_b_ada': 'new_v', 'new_v_g_pre': 'new_v', 'new_v_g_post': 'new_v', 'new_v_w_in': 'new_v', 'new_v_na_rpb': 'new_v', 'new_v_pool_w': 'new_v', 'new_v_pool_scale': 'new_v', 'new_v_q_norm': 'new_v', 'new_v_k_norm': 'new_v', 'new_v_conv_w': 'new_v', 'new_v_conv_b': 'new_v', 'new_v_conv_ln_g': 'new_v', 'new_v_conv_ln_b': 'new_v', 'new_v_conv_pw': 'new_v', 'new_v_w_branch': 'new_v', 'new_v_w_out': 'new_v'}


def _forward(args):
    return _fwd_reference(*[args[k] for k in FWD_PARAMS])


def _output_shape():
    def fwd():
        inp = _fwd_setup_inputs(0)
        return _fwd_reference(*[inp[k] for k in FWD_PARAMS])
    out = _jax.eval_shape(fwd)
    return out.shape, out.dtype

N_MICROBATCH = 1
ADAM_LR = 0.001
ADAM_B1 = 0.9
ADAM_B2 = 0.999
ADAM_EPS = 1e-08
ADAM_WD = 0.01
ADAM_STEP = 10
PER_EXAMPLE_BATCH_AXIS = {'x': 0, 'c': 0, 'ctx': 0, 'loss_target': 0}
SHARED_INPUTS = []
_WEIGHT_DTYPES = {'c_ctx': _jnp.float32, 'w_ada': _jnp.float32, 'b_ada': _jnp.float32, 'g_pre': _jnp.float32, 'g_post': _jnp.float32, 'w_in': _jnp.float32, 'na_rpb': _jnp.float32, 'pool_w': _jnp.float32, 'pool_scale': _jnp.float32, 'q_norm': _jnp.float32, 'k_norm': _jnp.float32, 'conv_w': _jnp.float32, 'conv_b': _jnp.float32, 'conv_ln_g': _jnp.float32, 'conv_ln_b': _jnp.float32, 'conv_pw': _jnp.float32, 'w_branch': _jnp.float32, 'w_out': _jnp.float32}
MOMENT_SCALE = {'c_ctx': 4.828234e-01, 'w_ada': 2.455330e+00, 'b_ada': 5.171764e+00, 'g_pre': 3.202950e-01, 'g_post': 1.309794e+01, 'w_in': 2.789666e-01, 'na_rpb': 1.336181e-02, 'pool_w': 3.022401e-01, 'pool_scale': 3.649814e-01, 'q_norm': 1.219424e-01, 'k_norm': 1.256255e-01, 'conv_w': 3.421603e-01, 'conv_b': 1.153472e+00, 'conv_ln_g': 6.291990e-01, 'conv_ln_b': 8.515445e-01, 'conv_pw': 4.785598e-01, 'w_branch': 3.068727e-01, 'w_out': 6.294576e-01}


def _to_microbatches(a, axis):
    t = _jnp.moveaxis(a, axis, 0)
    t = t.reshape((N_MICROBATCH, t.shape[0] // N_MICROBATCH) + t.shape[1:])
    return _jnp.moveaxis(t, 1, axis + 1)


def setup_inputs(seed: int = 0) -> dict:
    inp = _fwd_setup_inputs(seed)
    key = _jax.random.fold_in(_jax.random.key(seed), 7919)
    shape, _ = _output_shape()
    out = dict(inp)
    out["loss_target"] = _jax.random.normal(_jax.random.fold_in(key, 0), shape, _jnp.float32)
    for i, name in enumerate(TWIN_WEIGHTS):
        w = inp[name].astype(_jnp.float32)
        if MOMENT_SCALE is None:
            s = _jnp.sqrt(_jnp.mean(_jnp.square(w)) + 1e-30)
        else:
            s = MOMENT_SCALE[name]
        km, kv = _jax.random.split(_jax.random.fold_in(key, i + 1))
        out[name] = w
        out["m_" + name] = s * _jax.random.normal(km, w.shape, _jnp.float32)
        out["v_" + name] = (s * s) * _jax.random.uniform(kv, w.shape, _jnp.float32, 0.5, 1.5)
    if N_MICROBATCH > 1:
        for name, axis in PER_EXAMPLE_BATCH_AXIS.items():
            out[name] = _to_microbatches(out[name], axis)
    return {'x': out['x'], 'c': out['c'], 'ctx': out['ctx'], 'c_ctx': out['c_ctx'], 'w_ada': out['w_ada'], 'b_ada': out['b_ada'], 'g_pre': out['g_pre'], 'g_post': out['g_post'], 'w_in': out['w_in'], 'na_rpb': out['na_rpb'], 'pool_w': out['pool_w'], 'pool_scale': out['pool_scale'], 'q_norm': out['q_norm'], 'k_norm': out['k_norm'], 'conv_w': out['conv_w'], 'conv_b': out['conv_b'], 'conv_ln_g': out['conv_ln_g'], 'conv_ln_b': out['conv_ln_b'], 'conv_pw': out['conv_pw'], 'w_branch': out['w_branch'], 'w_out': out['w_out'], 'loss_target': out['loss_target'], 'm_c_ctx': out['m_c_ctx'], 'm_w_ada': out['m_w_ada'], 'm_b_ada': out['m_b_ada'], 'm_g_pre': out['m_g_pre'], 'm_g_post': out['m_g_post'], 'm_w_in': out['m_w_in'], 'm_na_rpb': out['m_na_rpb'], 'm_pool_w': out['m_pool_w'], 'm_pool_scale': out['m_pool_scale'], 'm_q_norm': out['m_q_norm'], 'm_k_norm': out['m_k_norm'], 'm_conv_w': out['m_conv_w'], 'm_conv_b': out['m_conv_b'], 'm_conv_ln_g': out['m_conv_ln_g'], 'm_conv_ln_b': out['m_conv_ln_b'], 'm_conv_pw': out['m_conv_pw'], 'm_w_branch': out['m_w_branch'], 'm_w_out': out['m_w_out'], 'v_c_ctx': out['v_c_ctx'], 'v_w_ada': out['v_w_ada'], 'v_b_ada': out['v_b_ada'], 'v_g_pre': out['v_g_pre'], 'v_g_post': out['v_g_post'], 'v_w_in': out['v_w_in'], 'v_na_rpb': out['v_na_rpb'], 'v_pool_w': out['v_pool_w'], 'v_pool_scale': out['v_pool_scale'], 'v_q_norm': out['v_q_norm'], 'v_k_norm': out['v_k_norm'], 'v_conv_w': out['v_conv_w'], 'v_conv_b': out['v_conv_b'], 'v_conv_ln_g': out['v_conv_ln_g'], 'v_conv_ln_b': out['v_conv_ln_b'], 'v_conv_pw': out['v_conv_pw'], 'v_w_branch': out['v_w_branch'], 'v_w_out': out['v_w_out']}


def _loss(weights, diff, rest, loss_target):
    with _jax.named_scope("forward"):
        args = {**rest, TWIN_DIFF_INPUT: diff, **{k: w.astype(_WEIGHT_DTYPES[k]) for k, w in weights.items()}}
        y = _forward(args)
    with _jax.named_scope("loss_head"):
        err = _jnp.square(y.astype(_jnp.float32) - loss_target)
        return 0.5 * _jnp.sum(_jnp.mean(err, axis=-1)) if err.ndim else 0.5 * err


def _adamw(w, g, m, v):
    m = ADAM_B1 * m + (1.0 - ADAM_B1) * g
    v = ADAM_B2 * v + (1.0 - ADAM_B2) * _jnp.square(g)
    m_hat = m / (1.0 - ADAM_B1 ** ADAM_STEP)
    v_hat = v / (1.0 - ADAM_B2 ** ADAM_STEP)
    delta = -ADAM_LR * (m_hat / (_jnp.sqrt(v_hat) + ADAM_EPS) + ADAM_WD * w)
    return delta, m, v


def reference(x, c, ctx, c_ctx, w_ada, b_ada, g_pre, g_post, w_in, na_rpb, pool_w, pool_scale, q_norm, k_norm, conv_w, conv_b, conv_ln_g, conv_ln_b, conv_pw, w_branch, w_out, loss_target, m_c_ctx, m_w_ada, m_b_ada, m_g_pre, m_g_post, m_w_in, m_na_rpb, m_pool_w, m_pool_scale, m_q_norm, m_k_norm, m_conv_w, m_conv_b, m_conv_ln_g, m_conv_ln_b, m_conv_pw, m_w_branch, m_w_out, v_c_ctx, v_w_ada, v_b_ada, v_g_pre, v_g_post, v_w_in, v_na_rpb, v_pool_w, v_pool_scale, v_q_norm, v_k_norm, v_conv_w, v_conv_b, v_conv_ln_g, v_conv_ln_b, v_conv_pw, v_w_branch, v_w_out):
    given = dict(x=x, c=c, ctx=ctx, c_ctx=c_ctx, w_ada=w_ada, b_ada=b_ada, g_pre=g_pre, g_post=g_post, w_in=w_in, na_rpb=na_rpb, pool_w=pool_w, pool_scale=pool_scale, q_norm=q_norm, k_norm=k_norm, conv_w=conv_w, conv_b=conv_b, conv_ln_g=conv_ln_g, conv_ln_b=conv_ln_b, conv_pw=conv_pw, w_branch=w_branch, w_out=w_out, loss_target=loss_target, m_c_ctx=m_c_ctx, m_w_ada=m_w_ada, m_b_ada=m_b_ada, m_g_pre=m_g_pre, m_g_post=m_g_post, m_w_in=m_w_in, m_na_rpb=m_na_rpb, m_pool_w=m_pool_w, m_pool_scale=m_pool_scale, m_q_norm=m_q_norm, m_k_norm=m_k_norm, m_conv_w=m_conv_w, m_conv_b=m_conv_b, m_conv_ln_g=m_conv_ln_g, m_conv_ln_b=m_conv_ln_b, m_conv_pw=m_conv_pw, m_w_branch=m_w_branch, m_w_out=m_w_out, v_c_ctx=v_c_ctx, v_w_ada=v_w_ada, v_b_ada=v_b_ada, v_g_pre=v_g_pre, v_g_post=v_g_post, v_w_in=v_w_in, v_na_rpb=v_na_rpb, v_pool_w=v_pool_w, v_pool_scale=v_pool_scale, v_q_norm=v_q_norm, v_k_norm=v_k_norm, v_conv_w=v_conv_w, v_conv_b=v_conv_b, v_conv_ln_g=v_conv_ln_g, v_conv_ln_b=v_conv_ln_b, v_conv_pw=v_conv_pw, v_w_branch=v_w_branch, v_w_out=v_w_out)
    weights = {n: given[n] for n in TWIN_WEIGHTS}
    shared = {n: given[n] for n in SHARED_INPUTS}
    per_example = {n: given[n] for n in ['x', 'c', 'ctx']}
    grad_fn = _jax.value_and_grad(_loss, argnums=(0, 1))

    def one_microbatch(ex, loss_target):
        ex = dict(ex)
        diff = ex.pop(TWIN_DIFF_INPUT)
        return grad_fn(weights, diff, {**shared, **ex}, loss_target)

    if N_MICROBATCH == 1:
        loss, (grad_w, grad_x) = one_microbatch(per_example, given["loss_target"])
    else:
        def body(carry, xs):
            loss_sum, grad_sum = carry
            l_k, (gw_k, gx_k) = one_microbatch(xs[0], xs[1])
            with _jax.named_scope("update"):
                return (loss_sum + l_k, _jax.tree.map(_jnp.add, grad_sum, gw_k)), gx_k

        init = (_jnp.zeros((), _jnp.float32), _jax.tree.map(_jnp.zeros_like, weights))
        (loss, grad_w), grad_x = _jax.lax.scan(body, init, (per_example, given["loss_target"]))
    with _jax.named_scope("update"):
        delta_w, new_m, new_v = {}, {}, {}
        for n in TWIN_WEIGHTS:
            delta_w[n], new_m[n], new_v[n] = _adamw(weights[n], grad_w[n], given["m_" + n], given["v_" + n])
    return (loss, grad_x, *[grad_w[n] for n in TWIN_WEIGHTS], *[delta_w[n] for n in TWIN_WEIGHTS],
            *[new_m[n] for n in TWIN_WEIGHTS], *[new_v[n] for n in TWIN_WEIGHTS])
```

```python
import functools

import numpy as np
import jax
import jax.numpy as jnp
from jax import lax
from jax.experimental import pallas as pl
from jax.experimental.pallas import tpu as pltpu

F32 = jnp.float32
BF = jnp.bfloat16
HI = lax.Precision.HIGHEST

GRID_W = 64
BRANCH_W = 512
HEAD_DIM = 64
NA_WIN_ROWS = 8
NA_WIN_COLS = 16
NA_TILE_ROWS = 2
NA_BAND = NA_WIN_ROWS + NA_TILE_ROWS - 1
CONV_WIDTH = 31
CONV_PAD = 32
EPS = 1e-6
ROPE_THETA = 10000.0
NEG = -1e30
N_DEV = 8
LANES = 128
VMEM_LIMIT_BYTES = 56 * 1024 * 1024

ADAM_LR, ADAM_B1, ADAM_B2, ADAM_EPS, ADAM_WD, ADAM_STEP = 0.001, 0.9, 0.999, 1e-08, 0.01, 10

CB = dict(a_k=0, a_v=4, c_k=8, c_v=9, a_q=10, c_q=14, a_gate=18, b_in=22, b_gate=26, c_gate=30, d_glu=34, d_gate=42, merge=46)
KV_COLS = 1280
MESH = pl.DeviceIdType.MESH


def _pick(n, cands):
    for c in cands:
        if n % c == 0:
            return c
    raise ValueError(f"no tile for {n} among {cands}")


def _cp(*sem):
    return pltpu.CompilerParams(dimension_semantics=sem if sem else None, vmem_limit_bytes=VMEM_LIMIT_BYTES)


def _dg(x, y, cx, cy):
    return lax.dot_general(x.astype(BF), y.astype(BF), (((cx,), (cy,)), ((), ())), preferred_element_type=F32)


@functools.partial(jax.custom_vjp, nondiff_argnums=(2, 3))
def bdot(a, b, ca=1, cb=0):
    return _dg(a, b, ca, cb)


def _bdot_fwd(a, b, ca, cb):
    return _dg(a, b, ca, cb), (a, b)


def _bdot_bwd(ca, cb, res, g):
    a, b = res
    jb = 1 if cb == 0 else 0
    ia = 0 if ca == 1 else 1
    da = _dg(g, b, 1, jb) if ca == 1 else _dg(b, g, jb, 1)
    db = _dg(a, g, ia, 0) if cb == 0 else _dg(g, a, 0, ia)
    return da.astype(a.dtype), db.astype(b.dtype)


bdot.defvjp(_bdot_fwd, _bdot_bwd)


def hdot(a, b):
    return jnp.dot(a, b, precision=HI, preferred_element_type=F32)


@jax.custom_vjp
def swap64(x):
    return pltpu.roll(x, 64, 1)


swap64.defvjp(lambda x: (swap64(x), None), lambda _, g: (swap64(g),))


@jax.custom_vjp
def partner(x):
    w = x.shape[-1]
    lane = lax.broadcasted_iota(jnp.int32, x.shape, 1)
    return jnp.where((lane % 32) < 16, pltpu.roll(x, w - 16, 1), pltpu.roll(x, 16, 1))


partner.defvjp(lambda x: (partner(x), None), lambda _, g: (partner(g),))


@functools.partial(jax.custom_vjp, nondiff_argnums=(1,))
def split_rows(x, n):
    k = x.shape[0] // n
    return tuple(x[i * k:(i + 1) * k] for i in range(n))


split_rows.defvjp(lambda x, n: (split_rows(x, n), None), lambda n, _, gs: (jnp.concatenate(gs, axis=0),))


def _lane(shape):
    return lax.broadcasted_iota(jnp.int32, shape, len(shape) - 1)


def _silu(x):
    return x * jax.nn.sigmoid(x)


def _normrope(x, w, cos, sin, bd):
    ms = hdot(x * x, bd)
    y = x * lax.rsqrt(ms + EPS) * w
    return y * cos + partner(y) * sin


def matmul(name, a, b, *, ca=1, cb=0, out_dtype=F32, tm, tn, tk):
    M = a.shape[1 - ca]
    K = a.shape[ca]
    N = b.shape[1 - cb]
    assert b.shape[cb] == K and M % tm == 0 and N % tn == 0 and K % tk == 0, (name, a.shape, b.shape)
    nk = K // tk
    a_spec = pl.BlockSpec((tm, tk), lambda i, j, k: (i, k)) if ca == 1 else pl.BlockSpec((tk, tm), lambda i, j, k: (k, i))
    b_spec = pl.BlockSpec((tk, tn), lambda i, j, k: (k, j)) if cb == 0 else pl.BlockSpec((tn, tk), lambda i, j, k: (j, k))

    def kern(a_ref, b_ref, o_ref, *scr):
        part = _dg(a_ref[...], b_ref[...], ca, cb)
        if nk == 1:
            o_ref[...] = part.astype(out_dtype)
            return
        acc_ref, = scr
        k = pl.program_id(2)

        @pl.when(k == 0)
        def _():
            acc_ref[...] = part

        @pl.when(k > 0)
        def _():
            acc_ref[...] += part

        @pl.when(k == nk - 1)
        def _():
            o_ref[...] = acc_ref[...].astype(out_dtype)

    return pl.pallas_call(
        kern, grid=(M // tm, N // tn, nk), in_specs=[a_spec, b_spec],
        out_specs=pl.BlockSpec((tm, tn), lambda i, j, k: (i, j)),
        out_shape=jax.ShapeDtypeStruct((M, N), out_dtype),
        scratch_shapes=[] if nk == 1 else [pltpu.VMEM((tm, tn), F32)],
        compiler_params=_cp("parallel", "parallel", "arbitrary"), name=name,
    )(a, b)


def rowcall(name, body, T, tm, ncol, nctx, rows, consts=(), segs=(), outs=(), accs=(), saccs=()):
    nctxb = nctx // tm
    assert T % tm == 0 and nctx % tm == 0
    nr, nc, ns = len(rows), len(consts), len(segs)

    def seg_map(i, j):
        return (jnp.where(i < nctxb, 0, 1), 0, 0)

    def col_map(cf):
        return lambda i, j: (i, cf(j))

    in_specs, args = [], []
    for arr, bw, cf in rows:
        in_specs.append(pl.BlockSpec((tm, bw), col_map(cf)))
        args.append(arr)
    for arr in consts:
        in_specs.append(pl.BlockSpec(arr.shape, lambda i, j, nd=arr.ndim: (0,) * nd))
        args.append(arr)
    for arr in segs:
        in_specs.append(pl.BlockSpec((None, 1, arr.shape[-1]), seg_map))
        args.append(arr)
    out_shape, out_specs, aliases, out_dtypes = [], [], {}, []
    for o in outs:
        if o[0] == "into":
            _, arr, bw, cf = o
            aliases[len(args)] = len(out_shape)
            in_specs.append(pl.BlockSpec(memory_space=pl.ANY))
            args.append(arr)
            out_shape.append(jax.ShapeDtypeStruct(arr.shape, arr.dtype))
            out_dtypes.append(arr.dtype)
        else:
            cols, dt, bw, cf = o
            out_shape.append(jax.ShapeDtypeStruct((T, cols), dt))
            out_dtypes.append(dt)
        out_specs.append(pl.BlockSpec((tm, bw), col_map(cf)))
    for shp in accs:
        out_shape.append(jax.ShapeDtypeStruct(shp, F32))
        out_specs.append(pl.BlockSpec(shp, lambda i, j, nd=len(shp): (0,) * nd))
    for w in saccs:
        out_shape.append(jax.ShapeDtypeStruct((2, 1, w), F32))
        out_specs.append(pl.BlockSpec((None, 1, w), seg_map))
    n_in = len(args)
    no, na, nsa = len(outs), len(accs), len(saccs)

    def kern(*refs):
        i, j = pl.program_id(0), pl.program_id(1)
        ins = refs[:nr + nc + ns]
        orefs = refs[n_in:n_in + no]
        arefs = refs[n_in + no:n_in + no + na]
        srefs = refs[n_in + no + na:n_in + no + na + nsa]
        ov, av, sv = body(i, j, [r[...] for r in ins[:nr]], [r[...] for r in ins[nr:nr + nc]], [r[...] for r in ins[nr + nc:]])
        for r, v, dt in zip(orefs, ov, out_dtypes):
            r[...] = v.astype(dt)
        if na:
            @pl.when((i == 0) & (j == 0))
            def _():
                for r in arefs:
                    r[...] = jnp.zeros_like(r)
            for r, v in zip(arefs, av):
                r[...] += v
        if nsa:
            @pl.when(((i == 0) | (i == nctxb)) & (j == 0))
            def _():
                for r in srefs:
                    r[...] = jnp.zeros_like(r)
            for r, v in zip(srefs, sv):
                r[...] += v

    res = pl.pallas_call(
        kern, grid=(T // tm, ncol), in_specs=in_specs, out_specs=out_specs, out_shape=out_shape,
        input_output_aliases=aliases, compiler_params=_cp("arbitrary", "arbitrary"), name=name,
    )(*args)
    return res


def _c(k):
    return lambda j: k


def f_modulate(x, g, sh, sc):
    y = x * lax.rsqrt(jnp.mean(x * x, axis=-1, keepdims=True) + EPS) * g
    return y * (1.0 + sc) + sh


def modulate_fwd(X, g_pre, shift, scale, Cn, tm):
    T, D = X.shape

    def body(i, j, r, c, s):
        return [f_modulate(r[0], c[0], s[0], s[1])], [], []

    return rowcall("modulate_fwd", body, T, tm, 1, Cn, [(X, D, _c(0))], [g_pre], [shift, scale], [(D, BF, D, _c(0))])[0]


def modulate_bwd(X, dH, dXn, g_pre, shift, scale, Cn, tm):
    T, D = X.shape

    def body(i, j, r, c, s):
        _, vjp = jax.vjp(f_modulate, r[0], c[0], s[0], s[1])
        dx, dg, dsh, dsc = vjp(r[1])
        return [dx + r[2]], [dg], [dsh, dsc]

    return rowcall("modulate_bwd", body, T, tm, 1, Cn, [(X, D, _c(0)), (dH, D, _c(0)), (dXn, D, _c(0))], [g_pre], [shift, scale],
                   [(D, F32, D, _c(0))], [(1, D)], [D, D])


def f_post(z, gp, gate):
    return gate * (z * lax.rsqrt(jnp.mean(z * z, axis=-1, keepdims=True) + EPS) * gp)


def post_fwd(X, Z, g_post, gate, Cn, tm):
    T, D = X.shape

    def body(i, j, r, c, s):
        return [r[0] + f_post(r[1], c[0], s[0])], [], []

    return rowcall("post_fwd", body, T, tm, 1, Cn, [(X, D, _c(0)), (Z, D, _c(0))], [g_post], [gate], [(D, F32, D, _c(0))])[0]


def post_bwd(Z, dXn, g_post, gate, Cn, tm):
    T, D = Z.shape

    def body(i, j, r, c, s):
        _, vjp = jax.vjp(f_post, r[0], c[0], s[0])
        dz, dgp, dgate = vjp(r[1])
        return [dz], [dgp], [dgate]

    return rowcall("post_bwd", body, T, tm, 1, Cn, [(Z, D, _c(0)), (dXn, D, _c(0))], [g_post], [gate],
                   [(D, BF, D, _c(0))], [(1, D)], [D])


def loss_call(XL, tgt, Cn, tm):
    T, D = XL.shape
    nctxb = Cn // tm

    def body(i, j, r, c, s):
        diff = jnp.where(i >= nctxb, r[0] - r[1], 0.0)
        per_row = jnp.mean(diff * diff, axis=-1, keepdims=True)
        tot = 0.5 * jnp.sum(per_row, axis=0, keepdims=True)
        return [diff / D], [jnp.broadcast_to(tot, (1, LANES))], []

    return rowcall("loss", body, T, tm, 1, Cn, [(XL, D, _c(0)), (tgt, D, _c(0))], [], [], [(D, F32, D, _c(0))], [(1, LANES)])


def f_prep_q(x, qn, cos, sin, bd, jj):
    y = _normrope(x, qn, cos, sin, bd) * 0.125
    lane = _lane(y.shape)
    a = jnp.where(lane < 64, y, 0.0)
    b = jnp.where(lane >= 64, y, 0.0)
    g0 = jj < 2
    return jnp.concatenate([jnp.where(g0, a, swap64(a)), jnp.where(g0, swap64(b), b)], axis=1)


def prep_q_fwd(P, cos, sin, qn, bd, Cn, tm):
    T = P.shape[0]

    def body(i, j, r, c, s):
        return [f_prep_q(r[0], c[0], r[1], r[2], c[1], j)], [], []

    return rowcall("prep_q_fwd", body, T, tm, 4, Cn, [(P, 128, lambda j: CB["c_q"] + j), (cos, 128, _c(0)), (sin, 128, _c(0))], [qn, bd], [],
                   [(1024, BF, 256, lambda j: j)])[0]


def prep_q_bwd(P, dQ, dP, cos, sin, qn, bd, Cn, tm):
    T = P.shape[0]

    def body(i, j, r, c, s):
        _, vjp = jax.vjp(lambda x, w: f_prep_q(x, w, r[1], r[2], c[1], j), r[0], c[0])
        dx, dw = vjp(r[3])
        return [dx], [dw], []

    return rowcall("prep_q_bwd", body, T, tm, 4, Cn,
                   [(P, 128, lambda j: CB["c_q"] + j), (cos, 128, _c(0)), (sin, 128, _c(0)), (dQ, 256, lambda j: j)], [qn, bd], [],
                   [("into", dP, 128, lambda j: CB["c_q"] + j)], [(1, 128)])


def f_prep_kv(x, kn, cos, sin, bd, jj):
    return jnp.where(jj == 0, _normrope(x, kn, cos, sin, bd), x)


def prep_kv_fwd(P, cos, sin, kn, bd, Cn, tm):
    T = P.shape[0]

    def body(i, j, r, c, s):
        return [f_prep_kv(r[0], c[0], r[1], r[2], c[1], j)], [], []

    return rowcall("prep_kv_fwd", body, T, tm, 2, Cn, [(P, 128, lambda j: CB["c_k"] + j), (cos, 128, _c(0)), (sin, 128, _c(0))], [kn, bd], [],
                   [(256, BF, 128, lambda j: j)])[0]


def prep_kv_bwd(P, dKV, dP, cos, sin, kn, bd, Cn, tm):
    T = P.shape[0]

    def body(i, j, r, c, s):
        _, vjp = jax.vjp(lambda x, w: f_prep_kv(x, w, r[1], r[2], c[1], j), r[0], c[0])
        dx, dw = vjp(r[3])
        return [dx], [dw], []

    return rowcall("prep_kv_bwd", body, T, tm, 2, Cn,
                   [(P, 128, lambda j: CB["c_k"] + j), (cos, 128, _c(0)), (sin, 128, _c(0)), (dKV, 128, lambda j: j)], [kn, bd], [],
                   [("into", dP, 128, lambda j: CB["c_k"] + j)], [(1, 128)])


def f_conv_post(y, lg, lb, w):
    mu = jnp.mean(y, axis=-1, keepdims=True)
    var = jnp.mean(jnp.square(y - mu), axis=-1, keepdims=True)
    h = (y - mu) * lax.rsqrt(var + EPS) * lg + lb
    return bdot(_silu(h), w)


def conv_post_fwd(Yc, lg, lb, w, Cn, tm):
    T = Yc.shape[0]

    def body(i, j, r, c, s):
        return [f_conv_post(r[0], c[0], c[1], c[2])], [], []

    return rowcall("conv_post_fwd", body, T, tm, 1, Cn, [(Yc, 512, _c(0))], [lg, lb, w], [], [(512, F32, 512, _c(0))])[0]


def conv_post_bwd(Yc, dO, lg, lb, w, Cn, tm):
    T = Yc.shape[0]

    def body(i, j, r, c, s):
        _, vjp = jax.vjp(f_conv_post, r[0], c[0], c[1], c[2])
        dy, dlg, dlb, dw = vjp(r[1])
        return [dy], [dlg, dlb, dw], []

    return rowcall("conv_post_bwd", body, T, tm, 1, Cn, [(Yc, 512, _c(0)), (dO, 512, _c(3))], [lg, lb, w], [],
                   [(512, F32, 512, _c(0))], [(1, 512), (1, 512), (512, 512)])


def _gate_blk(j):
    b = j // 2
    base = jnp.where(b == 0, CB["a_gate"] // 2, jnp.where(b == 1, CB["b_gate"] // 2, jnp.where(b == 2, CB["c_gate"] // 2, CB["d_gate"] // 2)))
    return base + j % 2


def _sel4(j, vals):
    b = j // 2
    return jnp.where(b == 0, vals[0], jnp.where(b == 1, vals[1], jnp.where(b == 2, vals[2], vals[3])))


def gate_outs_fwd(P, oa, ob, oc, od, Cn, tm):
    T = P.shape[0]

    def body(i, j, r, c, s):
        return [_sel4(j, r[:4]) * _silu(r[4])], [], []

    half = lambda j: j % 2
    return rowcall("gate_outs_fwd", body, T, tm, 8, Cn,
                   [(oa, 256, half), (ob, 256, half), (oc, 256, half), (od, 256, half), (P, 256, _gate_blk)], [], [],
                   [(2048, BF, 256, lambda j: j)])[0]


def gate_outs_bwd(P, oa, ob, oc, od, dG, dP, Cn, tm):
    T = P.shape[0]

    def body(i, j, r, c, s):
        o = _sel4(j, r[:4])
        _, vjp = jax.vjp(lambda oo, gg: oo * _silu(gg), o, r[4])
        do, dg = vjp(r[5])
        return [do, dg], [], []

    half = lambda j: j % 2
    return rowcall("gate_outs_bwd", body, T, tm, 8, Cn,
                   [(oa, 256, half), (ob, 256, half), (oc, 256, half), (od, 256, half), (P, 256, _gate_blk), (dG, 256, lambda j: j)], [], [],
                   [(2048, F32, 256, lambda j: j), ("into", dP, 256, _gate_blk)])


def merge_fwd(P, U, Cn, tm):
    T = P.shape[0]
    D4 = U.shape[1]
    nb = D4 // 4 // 256
    mb = CB["merge"] // 2

    def body(i, j, r, c, s):
        y = 0.0
        for b in range(4):
            y = y + jax.nn.sigmoid(r[b]) * r[4 + b]
        return [y], [], []

    rows = [(P, 256, (lambda j, b=b: mb + b * nb + j)) for b in range(4)] + [(U, 256, (lambda j, b=b: b * nb + j)) for b in range(4)]
    return rowcall("merge_fwd", body, T, tm, nb, Cn, rows, [], [], [(D4 // 4, BF, 256, lambda j: j)])[0]


def merge_bwd(P, U, dY, dP, Cn, tm):
    T = P.shape[0]
    D4 = U.shape[1]
    nb = D4 // 4 // 256
    mb = CB["merge"] // 2

    def body(i, j, r, c, s):
        _, vjp = jax.vjp(lambda m, u: jax.nn.sigmoid(m) * u, r[0], r[1])
        dm, du = vjp(r[2])
        return [du, dm], [], []

    return rowcall("merge_bwd", body, T, tm, 4 * nb, Cn,
                   [(P, 256, lambda j: mb + j), (U, 256, lambda j: j), (dY, 256, lambda j: j % nb)], [], [],
                   [(D4, BF, 256, lambda j: j), ("into", dP, 256, lambda j: mb + j)])


def assemble_na(dK, dV, dQ, dP, Cn, tm):
    T = dP.shape[0]

    def body(i, j, r, c, s):
        return [jnp.where(j < 4, r[0], jnp.where(j < 8, r[1], r[2]))], [], []

    return rowcall("assemble_na", body, T, tm, 12, Cn,
                   [(dK, 128, lambda j: jnp.minimum(j, 3)), (dV, 128, lambda j: jnp.clip(j - 4, 0, 3)), (dQ, 128, lambda j: jnp.clip(j - 8, 0, 3))],
                   [], [], [("into", dP, 128, lambda j: jnp.where(j < 8, j, j + 2))])[0]


def branch_fwd(G, Wb, tm):
    T = G.shape[0]
    D = Wb.shape[2]

    def kern(g_ref, w_ref, o_ref):
        o_ref[...] = _dg(g_ref[...], w_ref[...], 1, 0)

    return pl.pallas_call(
        kern, grid=(T // tm, 4),
        in_specs=[pl.BlockSpec((tm, 512), lambda i, b: (i, b)), pl.BlockSpec((None, 512, D), lambda i, b: (b, 0, 0))],
        out_specs=pl.BlockSpec((tm, D), lambda i, b: (i, b)), out_shape=jax.ShapeDtypeStruct((T, 4 * D), F32),
        compiler_params=_cp("parallel", "arbitrary"), name="branch_fwd")(G, Wb)


def branch_dg(dU, Wb, tm):
    T = dU.shape[0]
    D = Wb.shape[2]

    def kern(u_ref, w_ref, o_ref):
        o_ref[...] = _dg(u_ref[...], w_ref[...], 1, 1)

    return pl.pallas_call(
        kern, grid=(T // tm, 4),
        in_specs=[pl.BlockSpec((tm, D), lambda i, b: (i, b)), pl.BlockSpec((None, 512, D), lambda i, b: (b, 0, 0))],
        out_specs=pl.BlockSpec((tm, 512), lambda i, b: (i, b)), out_shape=jax.ShapeDtypeStruct((T, 2048), F32),
        compiler_params=_cp("parallel", "arbitrary"), name="branch_dg")(dU, Wb)


def branch_dw(G, dU, tk):
    T = G.shape[0]
    D = dU.shape[1] // 4
    nk = T // tk

    def kern(g_ref, u_ref, o_ref):
        k = pl.program_id(1)
        part = _dg(g_ref[...], u_ref[...], 0, 0)

        @pl.when(k == 0)
        def _():
            o_ref[...] = part

        @pl.when(k > 0)
        def _():
            o_ref[...] += part

    return pl.pallas_call(
        kern, grid=(4, nk),
        in_specs=[pl.BlockSpec((tk, 512), lambda b, k: (k, b)), pl.BlockSpec((tk, D), lambda b, k: (k, b))],
        out_specs=pl.BlockSpec((None, 512, D), lambda b, k: (b, 0, 0)), out_shape=jax.ShapeDtypeStruct((4, 512, D), F32),
        compiler_params=_cp("parallel", "arbitrary"), name="branch_dw")(G, dU)


def na_tables(S, Cn):
    R = S // GRID_W
    ntile = (Cn + S) // 128
    nct = Cn // 128
    rs = np.clip(np.arange(R) - NA_WIN_ROWS // 2, 0, R - NA_WIN_ROWS)
    seen, pats, cls, ws = {}, [], [], []
    for t in range(ntile):
        pat = -np.ones((NA_TILE_ROWS, NA_BAND), np.int64)
        w0 = 0
        if t >= nct:
            r0 = NA_TILE_ROWS * (t - nct)
            w0 = min(rs[r0], R - NA_BAND)
            for a in range(NA_TILE_ROWS):
                for j in range(NA_BAND):
                    kr = w0 + j
                    if rs[r0 + a] <= kr < rs[r0 + a] + NA_WIN_ROWS:
                        pat[a, j] = kr - (r0 + a) + NA_WIN_ROWS - 1
        key = pat.tobytes()
        if key not in seen:
            seen[key] = len(pats)
            pats.append(pat)
        cls.append(seen[key])
        ws.append(Cn + GRID_W * int(w0))
    cls = np.asarray(cls, np.int32)
    first = np.asarray([1 if t == 0 or cls[t] != cls[t - 1] else 0 for t in range(ntile)], np.int32)
    assert len(set(cls[first == 1].tolist())) == int(first.sum())
    pats = np.stack(pats)
    ncls = pats.shape[0]
    nrow = -(-(ncls * NA_TILE_ROWS * NA_BAND) // 128) * 128
    m1 = np.zeros((nrow, 128), np.float32)
    flat = pats.reshape(-1)
    for k, dr in enumerate(flat):
        if dr >= 0:
            m1[k, dr] = 1.0
    qc = np.arange(GRID_W)
    col_start = np.clip(qc - NA_WIN_COLS // 2, 0, GRID_W - NA_WIN_COLS)
    kc = np.arange(GRID_W)
    col_ok = (kc[None, :] >= col_start[:, None]) & (kc[None, :] < col_start[:, None] + NA_WIN_COLS)
    m2 = np.zeros((128, GRID_W * GRID_W), np.float32)
    for q in range(GRID_W):
        for k in range(GRID_W):
            if col_ok[q, k]:
                m2[k - q + NA_WIN_COLS - 1, q * GRID_W + k] = 1.0
    valid = (pats >= 0)[:, :, :, None, None] & col_ok[None, None, None]
    vmask = np.transpose(valid, (0, 2, 4, 1, 3)).reshape(ncls, 1, NA_BAND * GRID_W, 1, NA_TILE_ROWS * GRID_W)
    vmask = np.broadcast_to(vmask, (ncls, 1, NA_BAND * GRID_W, 2, NA_TILE_ROWS * GRID_W)).reshape(ncls, 1, NA_BAND * GRID_W, 256)
    return dict(cls=cls, ws=np.asarray(ws, np.int32), first=first, m1=m1, m2=m2, vmask=vmask, ncls=ncls, nrow=nrow)


def rpb_map(name, x, left, right):
    H = x.shape[0]

    def kern(x_ref, l_ref, r_ref, o_ref):
        o_ref[...] = hdot(hdot(l_ref[...], x_ref[...]), r_ref[...])

    return pl.pallas_call(
        kern, grid=(H,),
        in_specs=[pl.BlockSpec((None,) + x.shape[1:], lambda h: (h, 0, 0)), pl.BlockSpec(left.shape, lambda h: (0, 0)),
                  pl.BlockSpec(right.shape, lambda h: (0, 0))],
        out_specs=pl.BlockSpec((None, left.shape[0], right.shape[1]), lambda h: (h, 0, 0)),
        out_shape=jax.ShapeDtypeStruct((H, left.shape[0], right.shape[1]), F32), compiler_params=_cp("parallel"), name=name)(x, left, right)


def bias_table(rpb, tb):
    H = rpb.shape[0]
    xp = jnp.zeros((H, 128, 128), F32).at[:, :rpb.shape[1], :rpb.shape[2]].set(rpb)
    a = rpb_map("rpb_expand", xp, jnp.asarray(tb["m1"]), jnp.asarray(tb["m2"]))
    ncls = tb["ncls"]
    a = a[:, :ncls * NA_TILE_ROWS * NA_BAND].reshape(H // 2, 2, ncls, NA_TILE_ROWS, NA_BAND, GRID_W, GRID_W)
    a = jnp.transpose(a, (2, 0, 4, 6, 1, 3, 5)).reshape(ncls, H // 2, NA_BAND * GRID_W, 256)
    return jnp.where(jnp.asarray(tb["vmask"]), a, NEG)


def bias_table_grad(dbt, tb):
    ncls = tb["ncls"]
    H2 = dbt.shape[1]
    d = dbt.reshape(ncls, H2, NA_BAND, GRID_W, 2, NA_TILE_ROWS, GRID_W)
    d = jnp.transpose(d, (1, 4, 0, 5, 2, 6, 3)).reshape(H2 * 2, ncls * NA_TILE_ROWS * NA_BAND, GRID_W * GRID_W)
    d = jnp.pad(d, ((0, 0), (0, tb["nrow"] - d.shape[1]), (0, 0)))
    g = rpb_map("rpb_reduce", d, jnp.asarray(tb["m1"].T.copy()), jnp.asarray(tb["m2"].T.copy()))
    return g[:, :2 * NA_WIN_ROWS - 1, :2 * NA_WIN_COLS - 1]


def f_na(q, kb, vb, kc, vc, bias_t):
    d = lax.broadcasted_iota(jnp.int32, (128, 1), 0)
    q_t = q.T
    qbd = jnp.concatenate([jnp.where(d < 64, q_t, 0.0), jnp.where(d >= 64, q_t, 0.0)], axis=1)
    sb = bdot(kb, qbd) * 0.125 + bias_t
    sc = bdot(kc, qbd) * 0.125
    m = jnp.maximum(jnp.max(sb, axis=0, keepdims=True), jnp.max(sc, axis=0, keepdims=True))
    eb = jnp.exp(sb - m)
    ec = jnp.exp(sc - m)
    den = jnp.sum(eb, axis=0, keepdims=True) + jnp.sum(ec, axis=0, keepdims=True)
    of = bdot(eb / den, vb, 0, 0) + bdot(ec / den, vc, 0, 0)
    o0, o1 = split_rows(of, 2)
    return jnp.where(_lane(o0.shape) < 64, o0, o1)


def _na_specs(T, Cn, nband):
    qs = pl.BlockSpec((128, 128), lambda hp, t, *_: (t, CB["a_q"] + hp))
    ks = pl.BlockSpec((T, 128), lambda hp, t, *_: (0, CB["a_k"] + hp))
    vs = pl.BlockSpec((T, 128), lambda hp, t, *_: (0, CB["a_v"] + hp))
    bs = pl.BlockSpec((None, None, nband, 256), lambda hp, t, cls, ws, first: (cls[t], hp, 0, 0))
    return qs, ks, vs, bs


def na_fwd(P, bt, tb, Cn):
    T = P.shape[0]
    nband = NA_BAND * GRID_W
    qs, ks, vs, bs = _na_specs(T, Cn, nband)

    def kern(cls, ws, first, q_ref, k_ref, v_ref, b_ref, o_ref):
        w0 = pl.multiple_of(ws[pl.program_id(1)], 64)
        o_ref[...] = f_na(q_ref[...], k_ref[pl.ds(w0, nband), :], v_ref[pl.ds(w0, nband), :], k_ref[0:Cn, :], v_ref[0:Cn, :], b_ref[...])

    return pl.pallas_call(
        kern, grid_spec=pltpu.PrefetchScalarGridSpec(
            num_scalar_prefetch=3, grid=(4, T // 128), in_specs=[qs, ks, vs, bs],
            out_specs=pl.BlockSpec((128, 128), lambda hp, t, *_: (t, hp))),
        out_shape=jax.ShapeDtypeStruct((T, 512), F32), compiler_params=_cp("arbitrary", "arbitrary"), name="na_fwd",
    )(jnp.asarray(tb["cls"]), jnp.asarray(tb["ws"]), jnp.asarray(tb["first"]), P, P, P, bt)


def na_bwd(P, bt, dO, tb, Cn):
    T = P.shape[0]
    nband = NA_BAND * GRID_W
    qs, ks, vs, bs = _na_specs(T, Cn, nband)
    dos = pl.BlockSpec((128, 128), lambda hp, t, *_: (t, hp))

    def kern(cls, ws, first, q_ref, k_ref, v_ref, b_ref, do_ref, dq_ref, dk_ref, dv_ref, db_ref):
        t = pl.program_id(1)
        w0 = pl.multiple_of(ws[t], 64)
        band = pl.ds(w0, nband)
        _, vjp = jax.vjp(f_na, q_ref[...], k_ref[band, :], v_ref[band, :], k_ref[0:Cn, :], v_ref[0:Cn, :], b_ref[...])
        dq, dkb, dvb, dkc, dvc, db = vjp(do_ref[...])
        dq_ref[...] = dq

        @pl.when(t == 0)
        def _():
            dk_ref[...] = jnp.zeros_like(dk_ref)
            dv_ref[...] = jnp.zeros_like(dv_ref)

        dk_ref[band, :] += dkb
        dv_ref[band, :] += dvb
        dk_ref[0:Cn, :] += dkc
        dv_ref[0:Cn, :] += dvc

        @pl.when(first[t] == 1)
        def _():
            db_ref[...] = db

        @pl.when(first[t] == 0)
        def _():
            db_ref[...] += db

    full = lambda hp, t, *_: (0, hp)
    return pl.pallas_call(
        kern, grid_spec=pltpu.PrefetchScalarGridSpec(
            num_scalar_prefetch=3, grid=(4, T // 128), in_specs=[qs, ks, vs, bs, dos],
            out_specs=[pl.BlockSpec((128, 128), lambda hp, t, *_: (t, hp)), pl.BlockSpec((T, 128), full), pl.BlockSpec((T, 128), full),
                       pl.BlockSpec((None, None, nband, 256), lambda hp, t, cls, ws, first: (cls[t], hp, 0, 0))]),
        out_shape=[jax.ShapeDtypeStruct((T, 512), F32), jax.ShapeDtypeStruct((T, 512), F32), jax.ShapeDtypeStruct((T, 512), F32),
                   jax.ShapeDtypeStruct(bt.shape, F32)],
        compiler_params=_cp("arbitrary", "arbitrary"), name="na_bwd",
    )(jnp.asarray(tb["cls"]), jnp.asarray(tb["ws"]), jnp.asarray(tb["first"]), P, P, P, bt, dO)


def _expand_heads(blk, g):
    out = []
    for p in range(2):
        pair = blk[:, 128 * p:128 * (p + 1)]
        lane = _lane(pair.shape)
        a = jnp.where(lane < 64, pair, 0.0)
        b = jnp.where(lane >= 64, pair, 0.0)
        out.append(jnp.where(g == 0, a, pltpu.roll(a, 64, 1)))
        out.append(jnp.where(g == 0, pltpu.roll(b, 64, 1), b))
    return out


def _compact_heads(hs, g):
    out = []
    for p in range(2):
        e, o = hs[2 * p], hs[2 * p + 1]
        lane = _lane(e.shape)
        e0 = jnp.where(g == 0, e, pltpu.roll(e, 64, 1))
        o1 = jnp.where(g == 0, pltpu.roll(o, 64, 1), o)
        out.append(jnp.where(lane < 64, e0, o1))
    return jnp.concatenate(out, axis=1)


def _kv_map(Cn, tq, tk, col):
    nq_ctx = Cn // tq
    last_ctx = (Cn - 1) // tk

    def f(g, qi, kj):
        return (jnp.where(qi < nq_ctx, jnp.minimum(kj, last_ctx), kj), col)
    return f


def flash_fwd(Qp, KV, Cn, tq, tk):
    T = Qp.shape[0]
    nq, nk = T // tq, T // tk
    nq_ctx = Cn // tq

    def kern(q_ref, k_ref, v_ref, o_ref, lse_ref, m_s, l_s, acc_s):
        g, qi, kj = pl.program_id(0), pl.program_id(1), pl.program_id(2)

        @pl.when(kj == 0)
        def _():
            m_s[...] = jnp.full_like(m_s, NEG)
            l_s[...] = jnp.zeros_like(l_s)
            acc_s[...] = jnp.zeros_like(acc_s)

        @pl.when((qi >= nq_ctx) | (kj * tk < Cn))
        def _():
            q4 = jnp.concatenate([q_ref[:, 128 * h:128 * (h + 1)] for h in range(4)], axis=0)
            s = _dg(q4, k_ref[...], 1, 1)
            key = kj * tk + lax.broadcasted_iota(jnp.int32, (1, tk), 1)
            s = jnp.where((qi >= nq_ctx) | (key < Cn), s, NEG)
            m_old = m_s[...]
            m_new = jnp.maximum(m_old, jnp.max(s, axis=1, keepdims=True))
            alpha = jnp.exp(m_old - m_new)
            p = jnp.exp(s - m_new)
            l_s[...] = alpha * l_s[...] + jnp.sum(p, axis=1, keepdims=True)
            acc_s[...] = alpha * acc_s[...] + _dg(p, v_ref[...], 1, 0)
            m_s[...] = m_new

        @pl.when(kj == nk - 1)
        def _():
            o4 = acc_s[...] / l_s[...]
            o_ref[...] = _compact_heads([o4[tq * h:tq * (h + 1)] for h in range(4)], g)
            lse = m_s[...] + jnp.log(l_s[...])
            lse_ref[...] = jnp.concatenate([jnp.broadcast_to(lse[tq * h:tq * (h + 1)], (tq, 128)) for h in range(4)], axis=1)

    return pl.pallas_call(
        kern, grid=(2, nq, nk),
        in_specs=[pl.BlockSpec((tq, 512), lambda g, qi, kj: (qi, g)), pl.BlockSpec((tk, 128), _kv_map(Cn, tq, tk, 0)),
                  pl.BlockSpec((tk, 128), _kv_map(Cn, tq, tk, 1))],
        out_specs=[pl.BlockSpec((tq, 256), lambda g, qi, kj: (qi, g)), pl.BlockSpec((tq, 512), lambda g, qi, kj: (qi, g))],
        out_shape=[jax.ShapeDtypeStruct((T, 512), F32), jax.ShapeDtypeStruct((T, 1024), F32)],
        scratch_shapes=[pltpu.VMEM((4 * tq, 1), F32), pltpu.VMEM((4 * tq, 1), F32), pltpu.VMEM((4 * tq, 128), F32)],
        compiler_params=_cp("arbitrary", "arbitrary", "arbitrary"), name="flash_fwd")(Qp, KV, KV)


def flash_bwd(Qp, KV, O, LSE, dOall, Cn, tq, tk):
    T = Qp.shape[0]
    nq, nk = T // tq, T // tk
    nq_ctx = Cn // tq

    def kern(q_ref, k_ref, v_ref, o_ref, lse_ref, do_ref, dq_ref, dkv_ref, dq_s, do_s, dl_s):
        g, qi, kj = pl.program_id(0), pl.program_id(1), pl.program_id(2)

        @pl.when((g == 0) & (qi == 0) & (kj == 0))
        def _():
            dkv_ref[...] = jnp.zeros_like(dkv_ref)

        @pl.when(kj == 0)
        def _():
            do4 = jnp.concatenate(_expand_heads(do_ref[...], g), axis=0)
            o4 = jnp.concatenate(_expand_heads(o_ref[...], g), axis=0)
            do_s[...] = do4
            dl_s[...] = jnp.sum(do4 * o4, axis=1, keepdims=True)
            dq_s[...] = jnp.zeros_like(dq_s)

        @pl.when((qi >= nq_ctx) | (kj * tk < Cn))
        def _():
            q4 = jnp.concatenate([q_ref[:, 128 * h:128 * (h + 1)] for h in range(4)], axis=0)
            lse4 = jnp.concatenate([jnp.max(lse_ref[:, 128 * h:128 * (h + 1)], axis=1, keepdims=True) for h in range(4)], axis=0)
            k = k_ref[...]
            v = v_ref[...]
            s = _dg(q4, k, 1, 1)
            key = kj * tk + lax.broadcasted_iota(jnp.int32, (1, tk), 1)
            s = jnp.where((qi >= nq_ctx) | (key < Cn), s, NEG)
            p = jnp.exp(s - lse4)
            do4 = do_s[...]
            dv = _dg(p, do4, 0, 0)
            dp = _dg(do4, v, 1, 1)
            ds = p * (dp - dl_s[...])
            dq_s[...] += _dg(ds, k, 1, 0)
            dk = _dg(ds, q4, 0, 0)
            rows = pl.ds(pl.multiple_of(kj * tk, tk), tk)
            dkv_ref[rows, 0:128] += dk
            dkv_ref[rows, 128:256] += dv

        @pl.when(kj == nk - 1)
        def _():
            dq = dq_s[...]
            dq_ref[...] = jnp.concatenate([dq[tq * h:tq * (h + 1)] for h in range(4)], axis=1)

    return pl.pallas_call(
        kern, grid=(2, nq, nk),
        in_specs=[pl.BlockSpec((tq, 512), lambda g, qi, kj: (qi, g)), pl.BlockSpec((tk, 128), _kv_map(Cn, tq, tk, 0)),
                  pl.BlockSpec((tk, 128), _kv_map(Cn, tq, tk, 1)), pl.BlockSpec((tq, 256), lambda g, qi, kj: (qi, g)),
                  pl.BlockSpec((tq, 512), lambda g, qi, kj: (qi, g)), pl.BlockSpec((tq, 256), lambda g, qi, kj: (qi, 4 + g))],
        out_specs=[pl.BlockSpec((tq, 512), lambda g, qi, kj: (qi, g)), pl.BlockSpec((T, 256), lambda g, qi, kj: (0, 0))],
        out_shape=[jax.ShapeDtypeStruct((T, 1024), F32), jax.ShapeDtypeStruct((T, 256), F32)],
        scratch_shapes=[pltpu.VMEM((4 * tq, 128), F32), pltpu.VMEM((4 * tq, 128), F32), pltpu.VMEM((4 * tq, 1), F32)],
        compiler_params=_cp("arbitrary", "arbitrary", "arbitrary"), name="flash_bwd")(Qp, KV, KV, O, LSE, dOall)


def _pool_band(t0, w0, n, win, gi, Cn, T):
    i = lax.broadcasted_iota(jnp.int32, (n, win), 0)
    jx = lax.broadcasted_iota(jnp.int32, (n, win), 1)
    t = t0 + i
    tp = w0 + jx
    half = lax.shift_left(jnp.int32(1), gi)
    lo = jnp.maximum(t - half, jnp.where(t < Cn, 0, Cn))
    hi = jnp.minimum(t + half - 1, jnp.where(t < Cn, Cn, T) - 1)
    cnt = (hi - lo + 1).astype(F32)
    return jnp.where((tp >= lo) & (tp <= hi), 1.0 / cnt, 0.0) - jnp.where(tp == t, 1.0, 0.0)


def _pool_geom(T):
    n = _pick(T, (256, 128))
    return n, n + 128


def pool_fwd(P, pw, ps, Cn):
    T = P.shape[0]
    n, win = _pool_geom(T)

    def kern(u_ref, w_ref, s_ref, o_ref):
        gi = pl.program_id(0)

        def blk(b, carry):
            t0 = pl.multiple_of(b * n, n)
            w0 = pl.multiple_of(jnp.clip(t0 - 64, 0, T - win), 64)
            d = hdot(_pool_band(t0, w0, n, win, gi, Cn, T), u_ref[pl.ds(w0, win), :])
            o_ref[pl.ds(t0, n), :] = bdot(d, w_ref[...]) * s_ref[...]
            return carry

        lax.fori_loop(0, T // n, blk, 0)

    return pl.pallas_call(
        kern, grid=(4,),
        in_specs=[pl.BlockSpec((T, 128), lambda g: (0, CB["b_in"] + g)), pl.BlockSpec((None, 128, 128), lambda g: (g, 0, 0)),
                  pl.BlockSpec((1, 128), lambda g: (0, g))],
        out_specs=pl.BlockSpec((T, 128), lambda g: (0, g)), out_shape=jax.ShapeDtypeStruct((T, 512), F32),
        compiler_params=_cp("arbitrary"), name="pool_fwd")(P, pw, ps)


def pool_bwd(P, pw, ps, dOall, dP, Cn):
    T = P.shape[0]
    n, win = _pool_geom(T)

    def kern(u_ref, w_ref, s_ref, do_ref, dp_in, du_ref, dw_ref, ds_ref):
        gi = pl.program_id(0)
        du_ref[...] = jnp.zeros_like(du_ref)
        dw_ref[...] = jnp.zeros_like(dw_ref)
        ds_ref[...] = jnp.zeros_like(ds_ref)

        def blk(b, carry):
            t0 = pl.multiple_of(b * n, n)
            w0 = pl.multiple_of(jnp.clip(t0 - 64, 0, T - win), 64)
            band = _pool_band(t0, w0, n, win, gi, Cn, T)
            _, vjp = jax.vjp(lambda uw, w, s: bdot(hdot(band, uw), w) * s, u_ref[pl.ds(w0, win), :], w_ref[...], s_ref[...])
            duw, dw, ds = vjp(do_ref[pl.ds(t0, n), :])
            du_ref[pl.ds(w0, win), :] += duw
            dw_ref[...] += dw
            ds_ref[...] += ds
            return carry

        lax.fori_loop(0, T // n, blk, 0)

    return pl.pallas_call(
        kern, grid=(4,),
        in_specs=[pl.BlockSpec((T, 128), lambda g: (0, CB["b_in"] + g)), pl.BlockSpec((None, 128, 128), lambda g: (g, 0, 0)),
                  pl.BlockSpec((1, 128), lambda g: (0, g)), pl.BlockSpec((T, 128), lambda g: (0, 4 + g)), pl.BlockSpec(memory_space=pl.ANY)],
        out_specs=[pl.BlockSpec((T, 128), lambda g: (0, CB["b_in"] + g)), pl.BlockSpec((None, 128, 128), lambda g: (g, 0, 0)),
                   pl.BlockSpec((1, 128), lambda g: (0, g))],
        out_shape=[jax.ShapeDtypeStruct(dP.shape, F32), jax.ShapeDtypeStruct((4, 128, 128), F32), jax.ShapeDtypeStruct((1, 512), F32)],
        input_output_aliases={4: 0}, compiler_params=_cp("arbitrary"), name="pool_bwd")(P, pw, ps, dOall, dP)


def _conv_geom(T):
    n = _pick(T, (256, 128))
    return n, n + 32


def _shifted(win_val, off, n):
    w = win_val.shape[0]
    return pltpu.roll(win_val, (w - off) % w, 0)[:n]


def _seg_bounds(t, Cn, T):
    return jnp.where(t < Cn, 0, Cn), jnp.where(t < Cn, Cn, T)


def conv_fwd(P, cw, cb, Cn):
    T = P.shape[0]
    n, win = _conv_geom(T)

    def kern(a_ref, g_ref, w_ref, b_ref, y_ref):
        def blk(b, carry):
            t0 = pl.multiple_of(b * n, n)
            w0 = pl.multiple_of(jnp.clip(t0 - 16, 0, T - win), 8)
            rows = pl.ds(w0, win)
            u = a_ref[rows, :] * jax.nn.sigmoid(g_ref[rows, :])
            t = t0 + lax.broadcasted_iota(jnp.int32, (n, 1), 0)
            lo, hi = _seg_bounds(t, Cn, T)
            acc = jnp.zeros((n, 128), F32)
            for k in range(CONV_WIDTH):
                src = t + (k - CONV_WIDTH // 2)
                sh = _shifted(u, t0 - w0 + k - CONV_WIDTH // 2, n)
                acc = acc + jnp.where((src >= lo) & (src < hi), sh, 0.0) * w_ref[k:k + 1, :]
            y_ref[pl.ds(t0, n), :] = acc + b_ref[...]
            return carry

        lax.fori_loop(0, T // n, blk, 0)

    slab = lambda off: pl.BlockSpec((T, 128), lambda c: (0, off + c))
    return pl.pallas_call(
        kern, grid=(4,),
        in_specs=[slab(CB["d_glu"]), slab(CB["d_glu"] + 4), pl.BlockSpec((CONV_PAD, 128), lambda c: (0, c)), pl.BlockSpec((1, 128), lambda c: (0, c))],
        out_specs=pl.BlockSpec((T, 128), lambda c: (0, c)), out_shape=jax.ShapeDtypeStruct((T, 512), F32),
        compiler_params=_cp("arbitrary"), name="conv_fwd")(P, P, cw, cb)


def conv_bwd(P, cw, dY, dP, Cn):
    T = P.shape[0]
    n, win = _conv_geom(T)
    half = CONV_WIDTH // 2

    def kern(a_ref, g_ref, w_ref, dy_ref, dp_in, dg_ref, dw_ref, db_ref):
        j = pl.program_id(0)

        @pl.when(j < 4)
        def _():
            dw_ref[...] = jnp.zeros_like(dw_ref)
            db_ref[...] = jnp.zeros_like(db_ref)

        def blk(b, carry):
            t0 = pl.multiple_of(b * n, n)
            w0 = pl.multiple_of(jnp.clip(t0 - 16, 0, T - win), 8)
            rows = pl.ds(w0, win)
            cur = pl.ds(t0, n)
            dyw = dy_ref[rows, :]
            t = t0 + lax.broadcasted_iota(jnp.int32, (n, 1), 0)
            lo, hi = _seg_bounds(t, Cn, T)
            du = jnp.zeros((n, 128), F32)
            for k in range(CONV_WIDTH):
                src = t - (k - half)
                sh = _shifted(dyw, t0 - w0 - (k - half), n)
                du = du + jnp.where((src >= lo) & (src < hi), sh, 0.0) * w_ref[k:k + 1, :]
            a = a_ref[cur, :]
            sig = jax.nn.sigmoid(g_ref[cur, :])
            dg_ref[cur, :] = jnp.where(j < 4, du * sig, du * a * sig * (1.0 - sig))

            @pl.when(j < 4)
            def _():
                u = a_ref[rows, :] * jax.nn.sigmoid(g_ref[rows, :])
                dyc = dy_ref[cur, :]
                for k in range(CONV_WIDTH):
                    src = t + (k - half)
                    sh = _shifted(u, t0 - w0 + k - half, n)
                    dw_ref[k:k + 1, :] += jnp.sum(jnp.where((src >= lo) & (src < hi), sh, 0.0) * dyc, axis=0, keepdims=True)
                db_ref[...] += jnp.sum(dyc, axis=0, keepdims=True)

            return carry

        lax.fori_loop(0, T // n, blk, 0)

    c4 = lambda j: (0, jnp.minimum(j, 3))
    return pl.pallas_call(
        kern, grid=(8,),
        in_specs=[pl.BlockSpec((T, 128), lambda j: (0, CB["d_glu"] + j % 4)), pl.BlockSpec((T, 128), lambda j: (0, CB["d_glu"] + 4 + j % 4)),
                  pl.BlockSpec((CONV_PAD, 128), lambda j: (0, j % 4)), pl.BlockSpec((T, 128), lambda j: (0, j % 4)),
                  pl.BlockSpec(memory_space=pl.ANY)],
        out_specs=[pl.BlockSpec((T, 128), lambda j: (0, CB["d_glu"] + j)), pl.BlockSpec((CONV_PAD, 128), c4), pl.BlockSpec((1, 128), c4)],
        out_shape=[jax.ShapeDtypeStruct(dP.shape, F32), jax.ShapeDtypeStruct((CONV_PAD, 512), F32), jax.ShapeDtypeStruct((1, 512), F32)],
        input_output_aliases={4: 0}, compiler_params=_cp("arbitrary"), name="conv_bwd")(P, P, cw, dY, dP)


def ada_fwd(cvec, w, b):
    L, D, wc = w.shape

    def kern(c_ref, w_ref, b_ref, o_ref):
        o_ref[...] = _dg(_silu(c_ref[...]), w_ref[...], 1, 0) + b_ref[...]

    return pl.pallas_call(
        kern, grid=(L,),
        in_specs=[pl.BlockSpec((16, D), lambda l: (0, 0)), pl.BlockSpec((None, D, wc), lambda l: (l, 0, 0)), pl.BlockSpec((None, 1, wc), lambda l: (l, 0, 0))],
        out_specs=pl.BlockSpec((None, 16, wc), lambda l: (l, 0, 0)), out_shape=jax.ShapeDtypeStruct((L, 16, wc), F32),
        compiler_params=_cp("arbitrary"), name="ada_fwd")(cvec, w, b)


def ada_bwd(cvec, w, dm):
    L, D, wc = w.shape

    def kern(c_ref, w_ref, d_ref, gw_ref, ds_ref):
        l = pl.program_id(0)
        d = d_ref[...]
        gw_ref[...] = _dg(_silu(c_ref[...]), d, 0, 0)
        part = _dg(d, w_ref[...], 1, 1)

        @pl.when(l == 0)
        def _():
            ds_ref[...] = part

        @pl.when(l > 0)
        def _():
            ds_ref[...] += part

    return pl.pallas_call(
        kern, grid=(L,),
        in_specs=[pl.BlockSpec((16, D), lambda l: (0, 0)), pl.BlockSpec((None, D, wc), lambda l: (l, 0, 0)), pl.BlockSpec((None, 16, wc), lambda l: (l, 0, 0))],
        out_specs=[pl.BlockSpec((None, D, wc), lambda l: (l, 0, 0)), pl.BlockSpec((16, D), lambda l: (0, 0))],
        out_shape=[jax.ShapeDtypeStruct((L, D, wc), F32), jax.ShapeDtypeStruct((16, D), F32)],
        compiler_params=_cp("arbitrary"), name="ada_bwd")(cvec, w, dm)


def silu_grad(cc, parts):
    D = cc.shape[1]

    def kern(c_ref, p_ref, o_ref):
        tot = p_ref[0]
        for k in range(1, N_DEV):
            tot = tot + p_ref[k]
        _, vjp = jax.vjp(_silu, c_ref[...])
        o_ref[...] = vjp(tot)[0]

    return pl.pallas_call(kern, out_shape=jax.ShapeDtypeStruct((1, D), F32), name="silu_grad")(cc, parts)


def _flip(v, f):
    return 1 - v if f else v


_REL = ((0, 0), (1, 0), (0, 1), (1, 1))


def allgather(name, x):
    def body(x_ref, out_ref, send_sems, recv_sems, local_sem):
        mx, my, mc = lax.axis_index("x"), lax.axis_index("y"), lax.axis_index("c")
        me, sibling = (mx, my, mc), (mx, my, 1 - mc)
        chips = [(_flip(mx, fx), _flip(my, fy)) for fx, fy in _REL[1:]]

        def slot(px, py, pc):
            return out_ref.at[4 * px + 2 * py + pc]

        def copy(k, block, to, src=None):
            return pltpu.make_async_remote_copy(
                src_ref=slot(*block) if src is None else src, dst_ref=slot(*block),
                send_sem=send_sems.at[k], recv_sem=recv_sems.at[k], device_id=to, device_id_type=MESH)

        mine = pltpu.make_async_copy(x_ref, slot(*me), local_sem)
        mine.start()
        first = [copy(0, me, sibling, src=x_ref)] + [copy(1 + j, me, (*chip, mc), src=x_ref) for j, chip in enumerate(chips)]
        for cp in first:
            cp.start()
        passed = [copy(4 + j, (*chip, mc), sibling) for j, chip in enumerate(chips)]
        for j, chip in enumerate(chips):
            copy(1 + j, (*chip, mc), me).wait_recv()
            passed[j].start()
        copy(0, sibling, me).wait_recv()
        for j, chip in enumerate(chips):
            copy(4 + j, (*chip, 1 - mc), me).wait_recv()
        for cp in first + passed:
            cp.wait_send()
        mine.wait()

    return pl.pallas_call(
        body, out_shape=jax.ShapeDtypeStruct((N_DEV,) + x.shape, x.dtype),
        in_specs=[pl.BlockSpec(memory_space=pl.ANY)], out_specs=pl.BlockSpec(memory_space=pl.ANY),
        scratch_shapes=[pltpu.SemaphoreType.DMA((7,)), pltpu.SemaphoreType.DMA((7,)), pltpu.SemaphoreType.DMA(())], name=name)(x)


def rs_exchange_sibling(buf):
    _, R, C = buf.shape

    def body(buf_ref, out_ref, send_sems, recv_sems):
        mx, my, mc = lax.axis_index("x"), lax.axis_index("y"), lax.axis_index("c")
        sibling = (mx, my, 1 - mc)
        cps = []
        for j, (fx, fy) in enumerate(_REL):
            d = 4 * _flip(mx, fx) + 2 * _flip(my, fy) + (1 - mc)
            cps.append(pltpu.make_async_remote_copy(src_ref=buf_ref.at[d], dst_ref=out_ref.at[j], send_sem=send_sems.at[j],
                                                    recv_sem=recv_sems.at[j], device_id=sibling, device_id_type=MESH))
        for cp in cps:
            cp.start()
        for cp in cps:
            cp.wait_recv()
        for cp in cps:
            cp.wait_send()

    return pl.pallas_call(
        body, out_shape=jax.ShapeDtypeStruct((4, R, C), buf.dtype),
        in_specs=[pl.BlockSpec(memory_space=pl.ANY)], out_specs=pl.BlockSpec(memory_space=pl.ANY),
        scratch_shapes=[pltpu.SemaphoreType.DMA((4,)), pltpu.SemaphoreType.DMA((4,))], name="rs_exchange_sibling")(buf)


def rs_chip_sum(buf, recv, idx, tr):
    _, R, C = buf.shape

    def kern(idx_ref, b_ref, r_ref, own_ref, sb_ref):
        j = pl.program_id(1)
        s = b_ref[...] + r_ref[...]
        sb_ref[...] = s.astype(BF)

        @pl.when(j == 0)
        def _():
            own_ref[...] = s

    return pl.pallas_call(
        kern, grid_spec=pltpu.PrefetchScalarGridSpec(
            num_scalar_prefetch=1, grid=(R // tr, 4),
            in_specs=[pl.BlockSpec((None, tr, C), lambda r, j, idx: (idx[j], r, 0)), pl.BlockSpec((None, tr, C), lambda r, j, idx: (j, r, 0))],
            out_specs=[pl.BlockSpec((tr, C), lambda r, j, idx: (r, 0)), pl.BlockSpec((None, tr, C), lambda r, j, idx: (j, r, 0))]),
        out_shape=[jax.ShapeDtypeStruct((R, C), F32), jax.ShapeDtypeStruct((4, R, C), BF)],
        compiler_params=_cp("arbitrary", "arbitrary"), name="rs_chip_sum")(idx, buf, recv)


def rs_exchange_chips(sb):
    _, R, C = sb.shape

    def body(sb_ref, out_ref, send_sems, recv_sems):
        mx, my, mc = lax.axis_index("x"), lax.axis_index("y"), lax.axis_index("c")
        cps = []
        for j, (fx, fy) in enumerate(_REL):
            if j == 0:
                continue
            cps.append(pltpu.make_async_remote_copy(src_ref=sb_ref.at[j], dst_ref=out_ref.at[j], send_sem=send_sems.at[j - 1],
                                                    recv_sem=recv_sems.at[j - 1], device_id=(_flip(mx, fx), _flip(my, fy), mc),
                                                    device_id_type=MESH))
        for cp in cps:
            cp.start()
        for cp in cps:
            cp.wait_recv()
        for cp in cps:
            cp.wait_send()

    return pl.pallas_call(
        body, out_shape=jax.ShapeDtypeStruct((4, R, C), sb.dtype),
        in_specs=[pl.BlockSpec(memory_space=pl.ANY)], out_specs=pl.BlockSpec(memory_space=pl.ANY),
        scratch_shapes=[pltpu.SemaphoreType.DMA((3,)), pltpu.SemaphoreType.DMA((3,))], name="rs_exchange_chips")(sb)


def adamw(name, parts, w, m, v, tr):
    R, C = w.shape
    in_specs, args = [], []
    for arr, lead in parts:
        if lead is None:
            in_specs.append(pl.BlockSpec((tr, C), lambda r: (r, 0)))
        else:
            in_specs.append(pl.BlockSpec((None, tr, C), lambda r, k=lead: (k, r, 0)))
        args.append(arr)
    npart = len(parts)
    blk = pl.BlockSpec((tr, C), lambda r: (r, 0))

    def kern(*refs):
        g = refs[0][...].astype(F32)
        for r in refs[1:npart]:
            g = g + r[...].astype(F32)
        w_ref, m_ref, v_ref, g_out, d_out, m_out, v_out = refs[npart:]
        mn = ADAM_B1 * m_ref[...] + (1.0 - ADAM_B1) * g
        vn = ADAM_B2 * v_ref[...] + (1.0 - ADAM_B2) * jnp.square(g)
        m_hat = mn / (1.0 - ADAM_B1 ** ADAM_STEP)
        v_hat = vn / (1.0 - ADAM_B2 ** ADAM_STEP)
        g_out[...] = g
        d_out[...] = -ADAM_LR * (m_hat / (jnp.sqrt(v_hat) + ADAM_EPS) + ADAM_WD * w_ref[...])
        m_out[...] = mn
        v_out[...] = vn

    return pl.pallas_call(
        kern, grid=(R // tr,), in_specs=in_specs + [blk, blk, blk], out_specs=[blk] * 4,
        out_shape=[jax.ShapeDtypeStruct((R, C), F32)] * 4, compiler_params=_cp("parallel"), name=name)(*args, w, m, v)


def _pack(arrs):
    flat = [a.reshape(-1) for a in arrs]
    n = sum(f.shape[0] for f in flat)
    pad = (-n) % (8 * LANES)
    if pad:
        flat.append(jnp.zeros((pad,), flat[0].dtype))
    return jnp.concatenate(flat).reshape(-1, LANES)


def _unpack(packed, shapes):
    flat = packed.reshape(-1)
    out, off = [], 0
    for s in shapes:
        n = int(np.prod(s))
        out.append(flat[off:off + n].reshape(s))
        off += n
    return out


def _row_tile(R):
    return _pick(R, (2048, 1024, 512, 256, 128, 64, 32, 16, 8))


BIG = ("w_in", "w_branch", "w_out", "conv_pw", "conv_w")
SMALL = ("g_pre", "g_post", "na_rpb", "pool_w", "pool_scale", "q_norm", "k_norm", "conv_b", "conv_ln_g", "conv_ln_b")
WEIGHTS = ['c_ctx', 'w_ada', 'b_ada', 'g_pre', 'g_post', 'w_in', 'na_rpb', 'pool_w', 'pool_scale', 'q_norm', 'k_norm', 'conv_w', 'conv_b',
           'conv_ln_g', 'conv_ln_b', 'conv_pw', 'w_branch', 'w_out']


def _rope_tables(S, Cn):
    t = np.arange(S)
    pos = np.stack([t // GRID_W, t % GRID_W], 1).astype(np.float64)
    lane = np.arange(128)
    within = lane % 64
    axis = within // 32
    f = (within % 32) % 16
    freqs = ROPE_THETA ** (-np.arange(16, dtype=np.float32) / 16)
    ang = pos[:, axis].astype(np.float32) * freqs[f][None, :]
    cos = np.cos(ang).astype(np.float32)
    sin = np.sin(ang).astype(np.float32) * np.where((within % 32) < 16, -1.0, 1.0).astype(np.float32)[None, :]
    cos = np.concatenate([np.ones((Cn, 128), np.float32), cos])
    sin = np.concatenate([np.zeros((Cn, 128), np.float32), sin])
    return jnp.asarray(cos), jnp.asarray(sin)


def _big_shard_flat(w_in, w_branch, w_out, conv_pw, conv_w, L):
    out = []
    for l in range(L):
        cw = jnp.zeros((CONV_PAD, conv_w.shape[2]), conv_w.dtype).at[:CONV_WIDTH].set(conv_w[l])
        out += [w_in[l], w_branch[l], w_out[l], conv_pw[l], cw]
    return out


def kernel(x, c, ctx, c_ctx, w_ada, b_ada, g_pre, g_post, w_in, na_rpb, pool_w, pool_scale, q_norm, k_norm, conv_w, conv_b, conv_ln_g, conv_ln_b, conv_pw, w_branch, w_out, loss_target, m_c_ctx, m_w_ada, m_b_ada, m_g_pre, m_g_post, m_w_in, m_na_rpb, m_pool_w, m_pool_scale, m_q_norm, m_k_norm, m_conv_w, m_conv_b, m_conv_ln_g, m_conv_ln_b, m_conv_pw, m_w_branch, m_w_out, v_c_ctx, v_w_ada, v_b_ada, v_g_pre, v_g_post, v_w_in, v_na_rpb, v_pool_w, v_pool_scale, v_q_norm, v_k_norm, v_conv_w, v_conv_b, v_conv_ln_g, v_conv_ln_b, v_conv_pw, v_w_branch, v_w_out):
    W = dict(c_ctx=c_ctx, w_ada=w_ada, b_ada=b_ada, g_pre=g_pre, g_post=g_post, w_in=w_in, na_rpb=na_rpb, pool_w=pool_w, pool_scale=pool_scale,
             q_norm=q_norm, k_norm=k_norm, conv_w=conv_w, conv_b=conv_b, conv_ln_g=conv_ln_g, conv_ln_b=conv_ln_b, conv_pw=conv_pw,
             w_branch=w_branch, w_out=w_out)
    Mo = dict(c_ctx=m_c_ctx, w_ada=m_w_ada, b_ada=m_b_ada, g_pre=m_g_pre, g_post=m_g_post, w_in=m_w_in, na_rpb=m_na_rpb, pool_w=m_pool_w,
              pool_scale=m_pool_scale, q_norm=m_q_norm, k_norm=m_k_norm, conv_w=m_conv_w, conv_b=m_conv_b, conv_ln_g=m_conv_ln_g,
              conv_ln_b=m_conv_ln_b, conv_pw=m_conv_pw, w_branch=m_w_branch, w_out=m_w_out)
    Vo = dict(c_ctx=v_c_ctx, w_ada=v_w_ada, b_ada=v_b_ada, g_pre=v_g_pre, g_post=v_g_post, w_in=v_w_in, na_rpb=v_na_rpb, pool_w=v_pool_w,
              pool_scale=v_pool_scale, q_norm=v_q_norm, k_norm=v_k_norm, conv_w=v_conv_w, conv_b=v_conv_b, conv_ln_g=v_conv_ln_g,
              conv_ln_b=v_conv_ln_b, conv_pw=v_conv_pw, w_branch=v_w_branch, w_out=v_w_out)

    S, D = x.shape[1], x.shape[2]
    Cn = ctx.shape[1]
    T = Cn + S
    L = w_in.shape[0]
    IN = w_in.shape[2] * N_DEV
    mx, my, mc = lax.axis_index("x"), lax.axis_index("y"), lax.axis_index("c")
    me = 4 * mx + 2 * my + mc
    tm = _pick(Cn, (256, 128))
    tmw = 128
    tmm = _pick(T, (768, 256, 128))
    tq = _pick(Cn, (256, 128))
    tk = _pick(T, (768, 384, 128))

    cg = allgather("gather_c", c)
    cvec = jnp.zeros((16, D), F32).at[0].set(c_ctx).at[1:1 + N_DEV].set(cg[:, 0])
    wc = w_ada.shape[2]
    b_sh = lax.dynamic_slice_in_dim(b_ada, me * wc, wc, axis=1)[:, None, :]
    modp = ada_fwd(cvec, w_ada, b_sh)
    modg = allgather("gather_mod", modp.reshape(L * 16, wc)).reshape(N_DEV, L, 16, wc)
    mod_full = jnp.transpose(modg, (1, 2, 0, 3)).reshape(L, 16, N_DEV * wc)
    mod2 = jnp.stack([mod_full[:, 0], lax.dynamic_index_in_dim(mod_full, 1 + me, axis=1, keepdims=False)], axis=1)
    shift, scale, gate = [mod2[:, :, None, k * D:(k + 1) * D] for k in range(3)]

    big_own = _big_shard_flat(w_in, w_branch, w_out, conv_pw, conv_w, L)
    big_shapes = [a.shape for a in big_own]
    wpack = _pack(big_own)
    wg = allgather("gather_weights", wpack.astype(BF))
    wgf = wg.reshape(N_DEV, -1)
    Win, Wb, Wo, Cpw, Cw = [], [], [], [], []
    off = 0
    for l in range(L):
        parts = []
        for s in big_shapes[5 * l:5 * l + 5]:
            n = int(np.prod(s))
            parts.append(wgf[:, off:off + n].reshape((N_DEV,) + tuple(s)))
            off += n
        Win.append(jnp.transpose(parts[0], (1, 0, 2)).reshape(D, IN))
        Wb.append(jnp.transpose(parts[1], (1, 2, 0, 3)).reshape(4, BRANCH_W, D))
        Wo.append(parts[2].reshape(D, D))
        Cpw.append(parts[3].reshape(BRANCH_W, BRANCH_W).astype(F32))
        Cw.append(jnp.transpose(parts[4], (1, 0, 2)).reshape(CONV_PAD, BRANCH_W).astype(F32))

    cos, sin = _rope_tables(S, Cn)
    tb = na_tables(S, Cn)
    bd = jnp.asarray(np.kron(np.eye(2, dtype=np.float32), np.full((64, 64), 1.0 / 64, np.float32)))
    row2 = lambda a: a.reshape(1, -1)

    X = jnp.concatenate([ctx[0], x[0]], axis=0)
    saved = []
    for l in range(L):
        gp = row2(g_pre[l])
        H = modulate_fwd(X, gp, shift[l], scale[l], Cn, tm)
        P = matmul("proj_in", H, Win[l], out_dtype=F32, tm=tmm, tn=_pick(IN, (1280, 1152, 768, 384, 128)), tk=D)
        bt = bias_table(na_rpb[l], tb)
        oa = na_fwd(P, bt, tb, Cn)
        ob = pool_fwd(P, pool_w[l], row2(pool_scale[l]), Cn)
        qn = row2(jnp.tile(q_norm[l], 2))
        kn = row2(jnp.tile(k_norm[l], 2))
        Qp = prep_q_fwd(P, cos, sin, qn, bd, Cn, tm)
        KV = prep_kv_fwd(P, cos, sin, kn, bd, Cn, tm)
        oc, LSE = flash_fwd(Qp, KV, Cn, tq, tk)
        Yc = conv_fwd(P, Cw[l], row2(conv_b[l]), Cn)
        od = conv_post_fwd(Yc, row2(conv_ln_g[l]), row2(conv_ln_b[l]), Cpw[l], Cn, tm)
        G = gate_outs_fwd(P, oa, ob, oc, od, Cn, tm)
        U = branch_fwd(G, Wb[l], tmm)
        Y = merge_fwd(P, U, Cn, tm)
        Z = matmul("proj_out", Y, Wo[l], out_dtype=F32, tm=tmm, tn=D, tk=D)
        Xn = post_fwd(X, Z, row2(g_post[l]), gate[l], Cn, tm)
        saved.append(dict(X=X, H=H, P=P, bt=bt, oa=oa, ob=ob, oc=oc, od=od, Qp=Qp, KV=KV, LSE=LSE, Yc=Yc, G=G, U=U, Y=Y, Z=Z, qn=qn, kn=kn))
        X = Xn

    tgt = jnp.concatenate([jnp.zeros((Cn, D), F32), loss_target[0]], axis=0)
    dX, loss_acc = loss_call(X, tgt, Cn, tm)
    loss = lax.psum(loss_acc[0, 0], ("x", "y", "c"))

    gsm = {n: [None] * L for n in SMALL}
    gbig = {n: [None] * L for n in BIG}
    dmod = [None] * L
    for l in reversed(range(L)):
        sv = saved[l]
        P = sv["P"]
        dZ, dgp, dgate = post_bwd(sv["Z"], dX, row2(g_post[l]), gate[l], Cn, tm)
        gsm["g_post"][l] = dgp[0]
        dY = matmul("proj_out_dy", dZ, Wo[l], cb=1, out_dtype=F32, tm=tmm, tn=D, tk=D)
        gbig["w_out"][l] = matmul("proj_out_dw", sv["Y"], dZ, ca=0, cb=0, tm=_pick(D, (1024, 512, 256)), tn=D, tk=tmm)
        dP = jnp.zeros((T, IN), F32)
        dU, dP = merge_bwd(P, sv["U"], dY, dP, Cn, tm)
        dG = branch_dg(dU, Wb[l], tmm)
        gbig["w_branch"][l] = branch_dw(sv["G"], dU, tmm)
        dO, dP = gate_outs_bwd(P, sv["oa"], sv["ob"], sv["oc"], sv["od"], dG, dP, Cn, tm)
        dYc, dlg, dlb, dcpw = conv_post_bwd(sv["Yc"], dO, row2(conv_ln_g[l]), row2(conv_ln_b[l]), Cpw[l], Cn, tm)
        gsm["conv_ln_g"][l], gsm["conv_ln_b"][l], gbig["conv_pw"][l] = dlg[0], dlb[0], dcpw
        dP, dcw, dcb = conv_bwd(P, Cw[l], dYc, dP, Cn)
        gbig["conv_w"][l], gsm["conv_b"][l] = dcw, dcb[0]
        dQp, dKV = flash_bwd(sv["Qp"], sv["KV"], sv["oc"], sv["LSE"], dO, Cn, tq, tk)
        dP, dqn = prep_q_bwd(P, dQp, dP, cos, sin, sv["qn"], bd, Cn, tm)
        dP, dkn = prep_kv_bwd(P, dKV, dP, cos, sin, sv["kn"], bd, Cn, tm)
        gsm["q_norm"][l] = dqn[0, :64] + dqn[0, 64:]
        gsm["k_norm"][l] = dkn[0, :64] + dkn[0, 64:]
        dP, dpw, dps = pool_bwd(P, pool_w[l], row2(pool_scale[l]), dO, dP, Cn)
        gsm["pool_w"][l], gsm["pool_scale"][l] = dpw, dps[0]
        dQa, dKa, dVa, dbt = na_bwd(P, sv["bt"], dO, tb, Cn)
        dP = assemble_na(dKa, dVa, dQa, dP, Cn, tm)
        gsm["na_rpb"][l] = bias_table_grad(dbt, tb)
        dH = matmul("proj_in_dh", dP, Win[l], cb=1, out_dtype=F32, tm=tmm, tn=D, tk=_pick(IN, (1280, 1152, 768, 384, 128)))
        gbig["w_in"][l] = matmul("proj_in_dw", sv["H"], dP, ca=0, cb=0, tm=_pick(D, (1024, 512, 256)), tn=_pick(IN, (1280, 1152, 768, 384, 128)), tk=tmm)
        dX, dgpre, dsh, dsc = modulate_bwd(sv["X"], dH, dX, row2(g_pre[l]), shift[l], scale[l], Cn, tm)
        gsm["g_pre"][l] = dgpre[0]
        dmod[l] = jnp.concatenate([dsh[:, 0], dsc[:, 0], dgate[:, 0]], axis=1)
    grad_x = dX[Cn:][None]

    dest = []
    for l in range(L):
        gi = gbig["w_in"][l].reshape(D, N_DEV, IN // N_DEV).transpose(1, 0, 2).reshape(N_DEV, -1)
        gb = gbig["w_branch"][l].reshape(4, BRANCH_W, N_DEV, D // N_DEV).transpose(2, 0, 1, 3).reshape(N_DEV, -1)
        go = gbig["w_out"][l].reshape(N_DEV, -1)
        gpw = gbig["conv_pw"][l].reshape(N_DEV, -1)
        gcw = gbig["conv_w"][l].reshape(CONV_PAD, N_DEV, BRANCH_W // N_DEV).transpose(1, 0, 2).reshape(N_DEV, -1)
        dest += [gi, gb, go, gpw, gcw]
    nflat = sum(d.shape[1] for d in dest)
    padn = wpack.shape[0] * LANES - nflat
    if padn:
        dest.append(jnp.zeros((N_DEV, padn), F32))
    gbuf = jnp.concatenate(dest, axis=1).reshape(N_DEV, -1, LANES)
    Rb = gbuf.shape[1]
    trb = _row_tile(Rb)
    recv1 = rs_exchange_sibling(gbuf)
    idx = jnp.stack([4 * _flip(mx, fx) + 2 * _flip(my, fy) + mc for fx, fy in _REL]).astype(jnp.int32)
    own, sb = rs_chip_sum(gbuf, recv1, idx, trb)
    recv2 = rs_exchange_chips(sb)
    mpack = _pack(_big_shard_flat(Mo["w_in"], Mo["w_branch"], Mo["w_out"], Mo["conv_pw"], Mo["conv_w"], L))
    vpack = _pack(_big_shard_flat(Vo["w_in"], Vo["w_branch"], Vo["w_out"], Vo["conv_pw"], Vo["conv_w"], L))
    big_res = adamw("adamw_big", [(own, None), (recv2, 1), (recv2, 2), (recv2, 3)], wpack, mpack, vpack, trb)
    big_out = {}
    unp = [_unpack(r, big_shapes) for r in big_res]
    for k, n in enumerate(BIG):
        if n == "conv_w":
            big_out[n] = [jnp.stack([u[5 * l + k][:CONV_WIDTH] for l in range(L)]) for u in unp]
        else:
            big_out[n] = [jnp.stack([u[5 * l + k] for l in range(L)]) for u in unp]

    small_own = [jnp.stack(gsm[n]) for n in SMALL]
    small_shapes = [a.shape for a in small_own]
    dmod_all = jnp.stack(dmod)
    spack = _pack(small_own + [dmod_all])
    sg = allgather("gather_small", spack)
    Rs = spack.shape[0]
    nsmall = sum(int(np.prod(s)) for s in small_shapes)
    dmod_g = sg.reshape(N_DEV, -1)[:, nsmall:nsmall + dmod_all.size].reshape(N_DEV, L, 2, N_DEV, wc)
    dm_ctx = dmod_g[0, :, 0, :, :]
    for k in range(1, N_DEV):
        dm_ctx = dm_ctx + dmod_g[k, :, 0, :, :]
    dm_ctx = lax.dynamic_index_in_dim(dm_ctx, me, axis=1, keepdims=False)
    dm_b = jnp.transpose(lax.dynamic_index_in_dim(dmod_g[:, :, 1], me, axis=2, keepdims=False), (1, 0, 2))
    dmp = jnp.zeros((L, 16, wc), F32).at[:, 0].set(dm_ctx).at[:, 1:1 + N_DEV].set(dm_b)
    g_wada, dsilu = ada_bwd(cvec, w_ada, dmp)
    cpart = allgather("gather_cctx", dsilu[0:1])
    g_cctx = silu_grad(c_ctx[None], cpart)[0]

    smallw = _pack([W[n] for n in SMALL])
    smallm = _pack([Mo[n] for n in SMALL])
    smallv = _pack([Vo[n] for n in SMALL])
    Rsm = smallw.shape[0]
    small_res = adamw("adamw_small", [(sg[:, :Rsm], k) for k in range(N_DEV)], smallw, smallm, smallv, _row_tile(Rsm))
    small_unp = [_unpack(r, small_shapes) for r in small_res]

    db_parts = dmod_g.reshape(N_DEV, L, 2, N_DEV * wc)
    bshape = (L * N_DEV * wc // LANES, LANES)
    bparts = [(db_parts[k, :, r].reshape(bshape), None) for k in range(N_DEV) for r in range(2)]
    bada_res = adamw("adamw_bada", bparts, b_ada.reshape(bshape), Mo["b_ada"].reshape(bshape), Vo["b_ada"].reshape(bshape), _row_tile(bshape[0]))
    wada_shape = (w_ada.size // LANES, LANES)
    wada_res = adamw("adamw_wada", [(g_wada.reshape(wada_shape), None)], w_ada.reshape(wada_shape), Mo["w_ada"].reshape(wada_shape),
                     Vo["w_ada"].reshape(wada_shape), _row_tile(wada_shape[0]))
    cshape = (D // LANES, LANES)
    cctx_res = adamw("adamw_cctx", [(g_cctx.reshape(cshape), None)], c_ctx.reshape(cshape), Mo["c_ctx"].reshape(cshape), Vo["c_ctx"].reshape(cshape),
                     _row_tile(cshape[0]) if cshape[0] % 8 == 0 else cshape[0])

    res = {}
    for n in BIG:
        res[n] = big_out[n]
    for k, n in enumerate(SMALL):
        res[n] = [u[k] for u in small_unp]
    res["b_ada"] = [r.reshape(b_ada.shape) for r in bada_res]
    res["w_ada"] = [r.reshape(w_ada.shape) for r in wada_res]
    res["c_ctx"] = [r.reshape(c_ctx.shape) for r in cctx_res]
    outs = [loss, grad_x]
    for k in range(4):
        outs += [res[n][k] for n in WEIGHTS]
    return tuple(outs)
```

```python
import functools

import numpy as np
import jax
import jax.numpy as jnp
from jax import lax
from jax.experimental import pallas as pl
from jax.experimental.pallas import tpu as pltpu

F32 = jnp.float32
BF = jnp.bfloat16
HI = lax.Precision.HIGHEST

GRID_W = 64
BRANCH_W = 512
HEAD_DIM = 64
NA_WIN_ROWS = 8
NA_WIN_COLS = 16
NA_TILE_ROWS = 2
NA_BAND = NA_WIN_ROWS + NA_TILE_ROWS - 1
CONV_WIDTH = 31
CONV_PAD = 32
EPS = 1e-6
ROPE_THETA = 10000.0
NEG = -1e30
N_DEV = 8
LANES = 128
VMEM_LIMIT_BYTES = 56 * 1024 * 1024

ADAM_LR, ADAM_B1, ADAM_B2, ADAM_EPS, ADAM_WD, ADAM_STEP = 0.001, 0.9, 0.999, 1e-08, 0.01, 10

CB = dict(a_k=0, a_v=4, c_k=8, c_v=9, a_q=10, c_q=14, a_gate=18, b_in=22, b_gate=26, c_gate=30, d_glu=34, d_gate=42, merge=46)
KV_COLS = 1280
MESH = pl.DeviceIdType.MESH


def _pick(n, cands):
    for c in cands:
        if n % c == 0:
            return c
    raise ValueError(f"no tile for {n} among {cands}")


def _cp(*sem):
    return pltpu.CompilerParams(dimension_semantics=sem if sem else None, vmem_limit_bytes=VMEM_LIMIT_BYTES)


def _dg(x, y, cx, cy):
    return lax.dot_general(x.astype(BF), y.astype(BF), (((cx,), (cy,)), ((), ())), preferred_element_type=F32)


@functools.partial(jax.custom_vjp, nondiff_argnums=(2, 3))
def bdot(a, b, ca=1, cb=0):
    return _dg(a, b, ca, cb)


def _bdot_fwd(a, b, ca, cb):
    return _dg(a, b, ca, cb), (a, b)


def _bdot_bwd(ca, cb, res, g):
    a, b = res
    jb = 1 if cb == 0 else 0
    ia = 0 if ca == 1 else 1
    da = _dg(g, b, 1, jb) if ca == 1 else _dg(b, g, jb, 1)
    db = _dg(a, g, ia, 0) if cb == 0 else _dg(g, a, 0, ia)
    return da.astype(a.dtype), db.astype(b.dtype)


bdot.defvjp(_bdot_fwd, _bdot_bwd)


def hdot(a, b):
    return jnp.dot(a, b, precision=HI, preferred_element_type=F32)


@jax.custom_vjp
def swap64(x):
    return pltpu.roll(x, 64, 1)


swap64.defvjp(lambda x: (swap64(x), None), lambda _, g: (swap64(g),))


@jax.custom_vjp
def partner(x):
    w = x.shape[-1]
    lane = lax.broadcasted_iota(jnp.int32, x.shape, 1)
    return jnp.where((lane % 32) < 16, pltpu.roll(x, w - 16, 1), pltpu.roll(x, 16, 1))


partner.defvjp(lambda x: (partner(x), None), lambda _, g: (partner(g),))


@functools.partial(jax.custom_vjp, nondiff_argnums=(1,))
def split_rows(x, n):
    k = x.shape[0] // n
    return tuple(x[i * k:(i + 1) * k] for i in range(n))


split_rows.defvjp(lambda x, n: (split_rows(x, n), None), lambda n, _, gs: (jnp.concatenate(gs, axis=0),))


def _lane(shape):
    return lax.broadcasted_iota(jnp.int32, shape, len(shape) - 1)


def _silu(x):
    return x * jax.nn.sigmoid(x)


def _normrope(x, w, cos, sin, bd):
    ms = hdot(x * x, bd)
    y = x * lax.rsqrt(ms + EPS) * w
    return y * cos + partner(y) * sin


def matmul(name, a, b, *, ca=1, cb=0, out_dtype=F32, tm, tn, tk):
    M = a.shape[1 - ca]
    K = a.shape[ca]
    N = b.shape[1 - cb]
    assert b.shape[cb] == K and M % tm == 0 and N % tn == 0 and K % tk == 0, (name, a.shape, b.shape)
    nk = K // tk
    a_spec = pl.BlockSpec((tm, tk), lambda i, j, k: (i, k)) if ca == 1 else pl.BlockSpec((tk, tm), lambda i, j, k: (k, i))
    b_spec = pl.BlockSpec((tk, tn), lambda i, j, k: (k, j)) if cb == 0 else pl.BlockSpec((tn, tk), lambda i, j, k: (j, k))

    def kern(a_ref, b_ref, o_ref, *scr):
        part = _dg(a_ref[...], b_ref[...], ca, cb)
        if nk == 1:
            o_ref[...] = part.astype(out_dtype)
            return
        acc_ref, = scr
        k = pl.program_id(2)

        @pl.when(k == 0)
        def _():
            acc_ref[...] = part

        @pl.when(k > 0)
        def _():
            acc_ref[...] += part

        @pl.when(k == nk - 1)
        def _():
            o_ref[...] = acc_ref[...].astype(out_dtype)

    return pl.pallas_call(
        kern, grid=(M // tm, N // tn, nk), in_specs=[a_spec, b_spec],
        out_specs=pl.BlockSpec((tm, tn), lambda i, j, k: (i, j)),
        out_shape=jax.ShapeDtypeStruct((M, N), out_dtype),
        scratch_shapes=[] if nk == 1 else [pltpu.VMEM((tm, tn), F32)],
        compiler_params=_cp("parallel", "parallel", "arbitrary"), name=name,
    )(a, b)


def rowcall(name, body, T, tm, ncol, nctx, rows, consts=(), segs=(), outs=(), accs=(), saccs=()):
    nctxb = nctx // tm
    assert T % tm == 0 and (nctx % tm == 0 or not (segs or saccs))
    nr, nc, ns = len(rows), len(consts), len(segs)

    def seg_map(i, j):
        return (jnp.where(i < nctxb, 0, 1), 0, 0)

    def col_map(cf):
        return lambda i, j: (i, cf(j))

    in_specs, args = [], []
    for arr, bw, cf in rows:
        in_specs.append(pl.BlockSpec((tm, bw), col_map(cf)))
        args.append(arr)
    for arr in consts:
        in_specs.append(pl.BlockSpec(arr.shape, lambda i, j, nd=arr.ndim: (0,) * nd))
        args.append(arr)
    for arr in segs:
        in_specs.append(pl.BlockSpec((None, 1, arr.shape[-1]), seg_map))
        args.append(arr)
    out_shape, out_specs, aliases, out_dtypes = [], [], {}, []
    for o in outs:
        if o[0] == "into":
            _, arr, bw, cf = o
            aliases[len(args)] = len(out_shape)
            in_specs.append(pl.BlockSpec(memory_space=pl.ANY))
            args.append(arr)
            out_shape.append(jax.ShapeDtypeStruct(arr.shape, arr.dtype))
            out_dtypes.append(arr.dtype)
        else:
            cols, dt, bw, cf = o
            out_shape.append(jax.ShapeDtypeStruct((T, cols), dt))
            out_dtypes.append(dt)
        out_specs.append(pl.BlockSpec((tm, bw), col_map(cf)))
    for shp in accs:
        out_shape.append(jax.ShapeDtypeStruct(shp, F32))
        out_specs.append(pl.BlockSpec(shp, lambda i, j, nd=len(shp): (0,) * nd))
    for w in saccs:
        out_shape.append(jax.ShapeDtypeStruct((2, 1, w), F32))
        out_specs.append(pl.BlockSpec((None, 1, w), seg_map))
    n_in = len(args)
    no, na, nsa = len(outs), len(accs), len(saccs)

    def kern(*refs):
        i, j = pl.program_id(0), pl.program_id(1)
        ins = refs[:nr + nc + ns]
        orefs = refs[n_in:n_in + no]
        arefs = refs[n_in + no:n_in + no + na]
        srefs = refs[n_in + no + na:n_in + no + na + nsa]
        ov, av, sv = body(i, j, [r[...] for r in ins[:nr]], [r[...] for r in ins[nr:nr + nc]], [r[...] for r in ins[nr + nc:]])
        for r, v, dt in zip(orefs, ov, out_dtypes):
            r[...] = v.astype(dt)
        if na:
            @pl.when((i == 0) & (j == 0))
            def _():
                for r in arefs:
                    r[...] = jnp.zeros_like(r)
            for r, v in zip(arefs, av):
                r[...] += v
        if nsa:
            @pl.when(((i == 0) | (i == nctxb)) & (j == 0))
            def _():
                for r in srefs:
                    r[...] = jnp.zeros_like(r)
            for r, v in zip(srefs, sv):
                r[...] += v

    res = pl.pallas_call(
        kern, grid=(T // tm, ncol), in_specs=in_specs, out_specs=out_specs, out_shape=out_shape,
        input_output_aliases=aliases, compiler_params=_cp("arbitrary", "arbitrary"), name=name,
    )(*args)
    return res


def _c(k):
    return lambda j: k


def f_modulate(x, g, sh, sc):
    y = x * lax.rsqrt(jnp.mean(x * x, axis=-1, keepdims=True) + EPS) * g
    return y * (1.0 + sc) + sh


def modulate_fwd(X, g_pre, shift, scale, Cn, tm):
    T, D = X.shape

    def body(i, j, r, c, s):
        return [f_modulate(r[0], c[0], s[0], s[1])], [], []

    return rowcall("modulate_fwd", body, T, tm, 1, Cn, [(X, D, _c(0))], [g_pre], [shift, scale], [(D, BF, D, _c(0))])[0]


def modulate_bwd(X, dH, dXn, g_pre, shift, scale, Cn, tm):
    T, D = X.shape

    def body(i, j, r, c, s):
        _, vjp = jax.vjp(f_modulate, r[0], c[0], s[0], s[1])
        dx, dg, dsh, dsc = vjp(r[1])
        return [dx + r[2]], [dg], [dsh, dsc]

    return rowcall("modulate_bwd", body, T, tm, 1, Cn, [(X, D, _c(0)), (dH, D, _c(0)), (dXn, D, _c(0))], [g_pre], [shift, scale],
                   [(D, F32, D, _c(0))], [(1, D)], [D, D])


def f_post(z, gp, gate):
    return gate * (z * lax.rsqrt(jnp.mean(z * z, axis=-1, keepdims=True) + EPS) * gp)


def post_fwd(X, Z, g_post, gate, Cn, tm):
    T, D = X.shape

    def body(i, j, r, c, s):
        return [r[0] + f_post(r[1], c[0], s[0])], [], []

    return rowcall("post_fwd", body, T, tm, 1, Cn, [(X, D, _c(0)), (Z, D, _c(0))], [g_post], [gate], [(D, F32, D, _c(0))])[0]


def post_bwd(Z, dXn, g_post, gate, Cn, tm):
    T, D = Z.shape

    def body(i, j, r, c, s):
        _, vjp = jax.vjp(f_post, r[0], c[0], s[0])
        dz, dgp, dgate = vjp(r[1])
        return [dz], [dgp], [dgate]

    return rowcall("post_bwd", body, T, tm, 1, Cn, [(Z, D, _c(0)), (dXn, D, _c(0))], [g_post], [gate],
                   [(D, BF, D, _c(0))], [(1, D)], [D])


def loss_call(XL, tgt, Cn, tm):
    T, D = XL.shape
    nctxb = Cn // tm

    def body(i, j, r, c, s):
        diff = jnp.where(i >= nctxb, r[0] - r[1], 0.0)
        per_row = jnp.mean(diff * diff, axis=-1, keepdims=True)
        tot = 0.5 * jnp.sum(per_row, axis=0, keepdims=True)
        return [diff / D], [jnp.broadcast_to(tot, (1, LANES))], []

    return rowcall("loss", body, T, tm, 1, Cn, [(XL, D, _c(0)), (tgt, D, _c(0))], [], [], [(D, F32, D, _c(0))], [(1, LANES)])


def f_prep_q(x, qn, cos, sin, bd, jj):
    y = _normrope(x, qn, cos, sin, bd) * 0.125
    lane = _lane(y.shape)
    a = jnp.where(lane < 64, y, 0.0)
    b = jnp.where(lane >= 64, y, 0.0)
    g0 = jj < 2
    return jnp.concatenate([jnp.where(g0, a, swap64(a)), jnp.where(g0, swap64(b), b)], axis=1)


def prep_q_fwd(P, cos, sin, qn, bd, Cn, tm):
    T = P.shape[0]

    def body(i, j, r, c, s):
        return [f_prep_q(r[0], c[0], r[1], r[2], c[1], j)], [], []

    return rowcall("prep_q_fwd", body, T, tm, 4, Cn, [(P, 128, lambda j: CB["c_q"] + j), (cos, 128, _c(0)), (sin, 128, _c(0))], [qn, bd], [],
                   [(1024, BF, 256, lambda j: j)])[0]


def prep_q_bwd(P, dQ, dP, cos, sin, qn, bd, Cn, tm):
    T = P.shape[0]

    def body(i, j, r, c, s):
        _, vjp = jax.vjp(lambda x, w: f_prep_q(x, w, r[1], r[2], c[1], j), r[0], c[0])
        dx, dw = vjp(r[3])
        return [dx], [dw], []

    return rowcall("prep_q_bwd", body, T, tm, 4, Cn,
                   [(P, 128, lambda j: CB["c_q"] + j), (cos, 128, _c(0)), (sin, 128, _c(0)), (dQ, 256, lambda j: j)], [qn, bd], [],
                   [("into", dP, 128, lambda j: CB["c_q"] + j)], [(1, 128)])


def f_prep_kv(x, kn, cos, sin, bd, jj):
    return jnp.where(jj == 0, _normrope(x, kn, cos, sin, bd), x)


def prep_kv_fwd(P, cos, sin, kn, bd, Cn, tm):
    T = P.shape[0]

    def body(i, j, r, c, s):
        return [f_prep_kv(r[0], c[0], r[1], r[2], c[1], j)], [], []

    return rowcall("prep_kv_fwd", body, T, tm, 2, Cn, [(P, 128, lambda j: CB["c_k"] + j), (cos, 128, _c(0)), (sin, 128, _c(0))], [kn, bd], [],
                   [(256, BF, 128, lambda j: j)])[0]


def prep_kv_bwd(P, dKV, dP, cos, sin, kn, bd, Cn, tm):
    T = P.shape[0]

    def body(i, j, r, c, s):
        _, vjp = jax.vjp(lambda x, w: f_prep_kv(x, w, r[1], r[2], c[1], j), r[0], c[0])
        dx, dw = vjp(r[3])
        return [dx], [dw], []

    return rowcall("prep_kv_bwd", body, T, tm, 2, Cn,
                   [(P, 128, lambda j: CB["c_k"] + j), (cos, 128, _c(0)), (sin, 128, _c(0)), (dKV, 128, lambda j: j)], [kn, bd], [],
                   [("into", dP, 128, lambda j: CB["c_k"] + j)], [(1, 128)])


def f_conv_post(y, lg, lb, w):
    mu = jnp.mean(y, axis=-1, keepdims=True)
    var = jnp.mean(jnp.square(y - mu), axis=-1, keepdims=True)
    h = (y - mu) * lax.rsqrt(var + EPS) * lg + lb
    return bdot(_silu(h), w)


def conv_post_fwd(Yc, lg, lb, w, Cn, tm):
    T = Yc.shape[0]

    def body(i, j, r, c, s):
        return [f_conv_post(r[0], c[0], c[1], c[2])], [], []

    return rowcall("conv_post_fwd", body, T, tm, 1, Cn, [(Yc, 512, _c(0))], [lg, lb, w], [], [(512, F32, 512, _c(0))])[0]


def conv_post_bwd(Yc, dO, lg, lb, w, Cn, tm):
    T = Yc.shape[0]

    def body(i, j, r, c, s):
        _, vjp = jax.vjp(f_conv_post, r[0], c[0], c[1], c[2])
        dy, dlg, dlb, dw = vjp(r[1])
        return [dy], [dlg, dlb, dw], []

    return rowcall("conv_post_bwd", body, T, tm, 1, Cn, [(Yc, 512, _c(0)), (dO, 512, _c(3))], [lg, lb, w], [],
                   [(512, F32, 512, _c(0))], [(1, 512), (1, 512), (512, 512)])


def _gate_blk(j):
    b = j // 2
    base = jnp.where(b == 0, CB["a_gate"] // 2, jnp.where(b == 1, CB["b_gate"] // 2, jnp.where(b == 2, CB["c_gate"] // 2, CB["d_gate"] // 2)))
    return base + j % 2


def _sel4(j, vals):
    b = j // 2
    return jnp.where(b == 0, vals[0], jnp.where(b == 1, vals[1], jnp.where(b == 2, vals[2], vals[3])))


def gate_outs_fwd(P, oa, ob, oc, od, Cn, tm):
    T = P.shape[0]

    def body(i, j, r, c, s):
        return [_sel4(j, r[:4]) * _silu(r[4])], [], []

    half = lambda j: j % 2
    return rowcall("gate_outs_fwd", body, T, tm, 8, Cn,
                   [(oa, 256, half), (ob, 256, half), (oc, 256, half), (od, 256, half), (P, 256, _gate_blk)], [], [],
                   [(2048, BF, 256, lambda j: j)])[0]


def gate_outs_bwd(P, oa, ob, oc, od, dG, dP, Cn, tm):
    T = P.shape[0]

    def body(i, j, r, c, s):
        o = _sel4(j, r[:4])
        _, vjp = jax.vjp(lambda oo, gg: oo * _silu(gg), o, r[4])
        do, dg = vjp(r[5])
        return [do, dg], [], []

    half = lambda j: j % 2
    return rowcall("gate_outs_bwd", body, T, tm, 8, Cn,
                   [(oa, 256, half), (ob, 256, half), (oc, 256, half), (od, 256, half), (P, 256, _gate_blk), (dG, 256, lambda j: j)], [], [],
                   [(2048, F32, 256, lambda j: j), ("into", dP, 256, _gate_blk)])


def merge_fwd(P, U, Cn, tm):
    T = P.shape[0]
    D4 = U.shape[1]
    nb = D4 // 4 // 256
    mb = CB["merge"] // 2

    def body(i, j, r, c, s):
        y = 0.0
        for b in range(4):
            y = y + jax.nn.sigmoid(r[b]) * r[4 + b]
        return [y], [], []

    rows = [(P, 256, (lambda j, b=b: mb + b * nb + j)) for b in range(4)] + [(U, 256, (lambda j, b=b: b * nb + j)) for b in range(4)]
    return rowcall("merge_fwd", body, T, tm, nb, Cn, rows, [], [], [(D4 // 4, BF, 256, lambda j: j)])[0]


def merge_bwd(P, U, dY, Cn, tm):
    T, IN = P.shape
    D4 = U.shape[1]
    nb = D4 // 4 // 256
    mb = CB["merge"] // 2

    def body(i, j, r, c, s):
        _, vjp = jax.vjp(lambda m, u: jax.nn.sigmoid(m) * u, r[0], r[1])
        dm, du = vjp(r[2])
        return [du, dm], [], []

    return rowcall("merge_bwd", body, T, tm, 4 * nb, Cn,
                   [(P, 256, lambda j: mb + j), (U, 256, lambda j: j), (dY, 256, lambda j: j % nb)], [], [],
                   [(D4, BF, 256, lambda j: j), (IN, F32, 256, lambda j: mb + j)])


def assemble_na(dK, dV, dQ, dP, Cn, tm):
    T = dP.shape[0]

    def body(i, j, r, c, s):
        return [jnp.where(j < 4, r[0], jnp.where(j < 8, r[1], r[2]))], [], []

    return rowcall("assemble_na", body, T, tm, 12, Cn,
                   [(dK, 128, lambda j: jnp.minimum(j, 3)), (dV, 128, lambda j: jnp.clip(j - 4, 0, 3)), (dQ, 128, lambda j: jnp.clip(j - 8, 0, 3))],
                   [], [], [("into", dP, 128, lambda j: jnp.where(j < 8, j, j + 2))])[0]


def branch_fwd(G, Wb, tm):
    T = G.shape[0]
    D = Wb.shape[2]

    def kern(g_ref, w_ref, o_ref):
        o_ref[...] = _dg(g_ref[...], w_ref[...], 1, 0)

    return pl.pallas_call(
        kern, grid=(T // tm, 4),
        in_specs=[pl.BlockSpec((tm, 512), lambda i, b: (i, b)), pl.BlockSpec((None, 512, D), lambda i, b: (b, 0, 0))],
        out_specs=pl.BlockSpec((tm, D), lambda i, b: (i, b)), out_shape=jax.ShapeDtypeStruct((T, 4 * D), F32),
        compiler_params=_cp("parallel", "arbitrary"), name="branch_fwd")(G, Wb)


def branch_dg(dU, Wb, tm):
    T = dU.shape[0]
    D = Wb.shape[2]

    def kern(u_ref, w_ref, o_ref):
        o_ref[...] = _dg(u_ref[...], w_ref[...], 1, 1)

    return pl.pallas_call(
        kern, grid=(T // tm, 4),
        in_specs=[pl.BlockSpec((tm, D), lambda i, b: (i, b)), pl.BlockSpec((None, 512, D), lambda i, b: (b, 0, 0))],
        out_specs=pl.BlockSpec((tm, 512), lambda i, b: (i, b)), out_shape=jax.ShapeDtypeStruct((T, 2048), F32),
        compiler_params=_cp("parallel", "arbitrary"), name="branch_dg")(dU, Wb)


def branch_dw(G, dU, tk):
    T = G.shape[0]
    D = dU.shape[1] // 4
    nk = T // tk

    def kern(g_ref, u_ref, o_ref):
        k = pl.program_id(1)
        part = _dg(g_ref[...], u_ref[...], 0, 0)

        @pl.when(k == 0)
        def _():
            o_ref[...] = part

        @pl.when(k > 0)
        def _():
            o_ref[...] += part

    return pl.pallas_call(
        kern, grid=(4, nk),
        in_specs=[pl.BlockSpec((tk, 512), lambda b, k: (k, b)), pl.BlockSpec((tk, D), lambda b, k: (k, b))],
        out_specs=pl.BlockSpec((None, 512, D), lambda b, k: (b, 0, 0)), out_shape=jax.ShapeDtypeStruct((4, 512, D), F32),
        compiler_params=_cp("parallel", "arbitrary"), name="branch_dw")(G, dU)


def na_tables(S, Cn):
    R = S // GRID_W
    ntile = (Cn + S) // 128
    nct = Cn // 128
    rs = np.clip(np.arange(R) - NA_WIN_ROWS // 2, 0, R - NA_WIN_ROWS)
    seen, pats, cls, ws = {}, [], [], []
    for t in range(ntile):
        pat = -np.ones((NA_TILE_ROWS, NA_BAND), np.int64)
        w0 = 0
        if t >= nct:
            r0 = NA_TILE_ROWS * (t - nct)
            w0 = min(rs[r0], R - NA_BAND)
            for a in range(NA_TILE_ROWS):
                for j in range(NA_BAND):
                    kr = w0 + j
                    if rs[r0 + a] <= kr < rs[r0 + a] + NA_WIN_ROWS:
                        pat[a, j] = kr - (r0 + a) + NA_WIN_ROWS - 1
        key = pat.tobytes()
        if key not in seen:
            seen[key] = len(pats)
            pats.append(pat)
        cls.append(seen[key])
        ws.append(Cn + GRID_W * int(w0))
    cls = np.asarray(cls, np.int32)
    first = np.asarray([1 if t == 0 or cls[t] != cls[t - 1] else 0 for t in range(ntile)], np.int32)
    assert len(set(cls[first == 1].tolist())) == int(first.sum())
    pats = np.stack(pats)
    ncls = pats.shape[0]
    nrow = -(-(ncls * NA_TILE_ROWS * NA_BAND) // 128) * 128
    m1 = np.zeros((nrow, 128), np.float32)
    flat = pats.reshape(-1)
    for k, dr in enumerate(flat):
        if dr >= 0:
            m1[k, dr] = 1.0
    qc = np.arange(GRID_W)
    col_start = np.clip(qc - NA_WIN_COLS // 2, 0, GRID_W - NA_WIN_COLS)
    kc = np.arange(GRID_W)
    col_ok = (kc[None, :] >= col_start[:, None]) & (kc[None, :] < col_start[:, None] + NA_WIN_COLS)
    m2 = np.zeros((128, GRID_W * GRID_W), np.float32)
    for q in range(GRID_W):
        for k in range(GRID_W):
            if col_ok[q, k]:
                m2[k - q + NA_WIN_COLS - 1, q * GRID_W + k] = 1.0
    valid = (pats >= 0)[:, :, :, None, None] & col_ok[None, None, None]
    vmask = np.transpose(valid, (0, 2, 4, 1, 3)).reshape(ncls, 1, NA_BAND * GRID_W, 1, NA_TILE_ROWS * GRID_W)
    vmask = np.broadcast_to(vmask, (ncls, 1, NA_BAND * GRID_W, 2, NA_TILE_ROWS * GRID_W)).reshape(ncls, 1, NA_BAND * GRID_W, 256)
    return dict(cls=cls, ws=np.asarray(ws, np.int32), first=first, m1=m1, m2=m2, vmask=vmask, ncls=ncls, nrow=nrow)


def rpb_map(name, x, left, right):
    H = x.shape[0]

    def kern(x_ref, l_ref, r_ref, o_ref):
        o_ref[...] = hdot(hdot(l_ref[...], x_ref[...]), r_ref[...])

    return pl.pallas_call(
        kern, grid=(H,),
        in_specs=[pl.BlockSpec((None,) + x.shape[1:], lambda h: (h, 0, 0)), pl.BlockSpec(left.shape, lambda h: (0, 0)),
                  pl.BlockSpec(right.shape, lambda h: (0, 0))],
        out_specs=pl.BlockSpec((None, left.shape[0], right.shape[1]), lambda h: (h, 0, 0)),
        out_shape=jax.ShapeDtypeStruct((H, left.shape[0], right.shape[1]), F32), compiler_params=_cp("parallel"), name=name)(x, left, right)


def bias_table(rpb, tb):
    H = rpb.shape[0]
    xp = jnp.zeros((H, 128, 128), F32).at[:, :rpb.shape[1], :rpb.shape[2]].set(rpb)
    a = rpb_map("rpb_expand", xp, jnp.asarray(tb["m1"]), jnp.asarray(tb["m2"]))
    ncls = tb["ncls"]
    a = a[:, :ncls * NA_TILE_ROWS * NA_BAND].reshape(H // 2, 2, ncls, NA_TILE_ROWS, NA_BAND, GRID_W, GRID_W)
    a = jnp.transpose(a, (2, 0, 4, 6, 1, 3, 5)).reshape(ncls, H // 2, NA_BAND * GRID_W, 256)
    return jnp.where(jnp.asarray(tb["vmask"]), a, NEG)


def bias_table_grad(dbt, tb):
    ncls = tb["ncls"]
    H2 = dbt.shape[1]
    d = dbt.reshape(ncls, H2, NA_BAND, GRID_W, 2, NA_TILE_ROWS, GRID_W)
    d = jnp.transpose(d, (1, 4, 0, 5, 2, 6, 3)).reshape(H2 * 2, ncls * NA_TILE_ROWS * NA_BAND, GRID_W * GRID_W)
    d = jnp.pad(d, ((0, 0), (0, tb["nrow"] - d.shape[1]), (0, 0)))
    g = rpb_map("rpb_reduce", d, jnp.asarray(tb["m1"].T.copy()), jnp.asarray(tb["m2"].T.copy()))
    return g[:, :2 * NA_WIN_ROWS - 1, :2 * NA_WIN_COLS - 1]


def f_na(q, kb, vb, kc, vc, bias_t):
    d = lax.broadcasted_iota(jnp.int32, (128, 1), 0)
    q_t = q.T
    qbd = jnp.concatenate([jnp.where(d < 64, q_t, 0.0), jnp.where(d >= 64, q_t, 0.0)], axis=1)
    sb = bdot(kb, qbd) * 0.125 + bias_t
    sc = bdot(kc, qbd) * 0.125
    m = jnp.maximum(jnp.max(sb, axis=0, keepdims=True), jnp.max(sc, axis=0, keepdims=True))
    eb = jnp.exp(sb - m)
    ec = jnp.exp(sc - m)
    den = jnp.sum(eb, axis=0, keepdims=True) + jnp.sum(ec, axis=0, keepdims=True)
    of = bdot(eb / den, vb, 0, 0) + bdot(ec / den, vc, 0, 0)
    o0, o1 = split_rows(of, 2)
    return jnp.where(_lane(o0.shape) < 64, o0, o1)


def _na_specs(T, Cn, nband):
    qs = pl.BlockSpec((128, 128), lambda hp, t, *_: (t, CB["a_q"] + hp))
    ks = pl.BlockSpec((T, 128), lambda hp, t, *_: (0, CB["a_k"] + hp))
    vs = pl.BlockSpec((T, 128), lambda hp, t, *_: (0, CB["a_v"] + hp))
    bs = pl.BlockSpec((None, None, nband, 256), lambda hp, t, cls, ws, first: (cls[t], hp, 0, 0))
    return qs, ks, vs, bs


def na_fwd(P, bt, tb, Cn):
    T = P.shape[0]
    nband = NA_BAND * GRID_W
    qs, ks, vs, bs = _na_specs(T, Cn, nband)

    def kern(cls, ws, first, q_ref, k_ref, v_ref, b_ref, o_ref):
        w0 = pl.multiple_of(ws[pl.program_id(1)], 64)
        o_ref[...] = f_na(q_ref[...], k_ref[pl.ds(w0, nband), :], v_ref[pl.ds(w0, nband), :], k_ref[0:Cn, :], v_ref[0:Cn, :], b_ref[...])

    return pl.pallas_call(
        kern, grid_spec=pltpu.PrefetchScalarGridSpec(
            num_scalar_prefetch=3, grid=(4, T // 128), in_specs=[qs, ks, vs, bs],
            out_specs=pl.BlockSpec((128, 128), lambda hp, t, *_: (t, hp))),
        out_shape=jax.ShapeDtypeStruct((T, 512), F32), compiler_params=_cp("arbitrary", "arbitrary"), name="na_fwd",
    )(jnp.asarray(tb["cls"]), jnp.asarray(tb["ws"]), jnp.asarray(tb["first"]), P, P, P, bt)


def na_bwd(P, bt, dO, tb, Cn):
    T = P.shape[0]
    nband = NA_BAND * GRID_W
    qs, ks, vs, bs = _na_specs(T, Cn, nband)
    dos = pl.BlockSpec((128, 128), lambda hp, t, *_: (t, hp))

    def kern(cls, ws, first, q_ref, k_ref, v_ref, b_ref, do_ref, dq_ref, dk_ref, dv_ref, db_ref):
        t = pl.program_id(1)
        w0 = pl.multiple_of(ws[t], 64)
        band = pl.ds(w0, nband)
        _, vjp = jax.vjp(f_na, q_ref[...], k_ref[band, :], v_ref[band, :], k_ref[0:Cn, :], v_ref[0:Cn, :], b_ref[...])
        dq, dkb, dvb, dkc, dvc, db = vjp(do_ref[...])
        dq_ref[...] = dq

        @pl.when(t == 0)
        def _():
            dk_ref[...] = jnp.zeros_like(dk_ref)
            dv_ref[...] = jnp.zeros_like(dv_ref)

        dk_ref[band, :] += dkb
        dv_ref[band, :] += dvb
        dk_ref[0:Cn, :] += dkc
        dv_ref[0:Cn, :] += dvc

        @pl.when(first[t] == 1)
        def _():
            db_ref[...] = db

        @pl.when(first[t] == 0)
        def _():
            db_ref[...] += db

    full = lambda hp, t, *_: (0, hp)
    return pl.pallas_call(
        kern, grid_spec=pltpu.PrefetchScalarGridSpec(
            num_scalar_prefetch=3, grid=(4, T // 128), in_specs=[qs, ks, vs, bs, dos],
            out_specs=[pl.BlockSpec((128, 128), lambda hp, t, *_: (t, hp)), pl.BlockSpec((T, 128), full), pl.BlockSpec((T, 128), full),
                       pl.BlockSpec((None, None, nband, 256), lambda hp, t, cls, ws, first: (cls[t], hp, 0, 0))]),
        out_shape=[jax.ShapeDtypeStruct((T, 512), F32), jax.ShapeDtypeStruct((T, 512), F32), jax.ShapeDtypeStruct((T, 512), F32),
                   jax.ShapeDtypeStruct(bt.shape, F32)],
        compiler_params=_cp("arbitrary", "arbitrary"), name="na_bwd",
    )(jnp.asarray(tb["cls"]), jnp.asarray(tb["ws"]), jnp.asarray(tb["first"]), P, P, P, bt, dO)


def _expand_heads(blk, g):
    out = []
    for p in range(2):
        pair = blk[:, 128 * p:128 * (p + 1)]
        lane = _lane(pair.shape)
        a = jnp.where(lane < 64, pair, 0.0)
        b = jnp.where(lane >= 64, pair, 0.0)
        out.append(jnp.where(g == 0, a, pltpu.roll(a, 64, 1)))
        out.append(jnp.where(g == 0, pltpu.roll(b, 64, 1), b))
    return out


def _compact_heads(hs, g):
    out = []
    for p in range(2):
        e, o = hs[2 * p], hs[2 * p + 1]
        lane = _lane(e.shape)
        e0 = jnp.where(g == 0, e, pltpu.roll(e, 64, 1))
        o1 = jnp.where(g == 0, pltpu.roll(o, 64, 1), o)
        out.append(jnp.where(lane < 64, e0, o1))
    return jnp.concatenate(out, axis=1)


def _kv_map(Cn, tq, tk, col):
    nq_ctx = Cn // tq
    last_ctx = (Cn - 1) // tk

    def f(g, qi, kj):
        return (jnp.where(qi < nq_ctx, jnp.minimum(kj, last_ctx), kj), col)
    return f


def flash_fwd(Qp, KV, Cn, tq, tk):
    T = Qp.shape[0]
    nq, nk = T // tq, T // tk
    nq_ctx = Cn // tq

    def kern(q_ref, k_ref, v_ref, o_ref, lse_ref, m_s, l_s, acc_s):
        g, qi, kj = pl.program_id(0), pl.program_id(1), pl.program_id(2)

        @pl.when(kj == 0)
        def _():
            m_s[...] = jnp.full_like(m_s, NEG)
            l_s[...] = jnp.zeros_like(l_s)
            acc_s[...] = jnp.zeros_like(acc_s)

        @pl.when((qi >= nq_ctx) | (kj * tk < Cn))
        def _():
            q4 = jnp.concatenate([q_ref[:, 128 * h:128 * (h + 1)] for h in range(4)], axis=0)
            s = _dg(q4, k_ref[...], 1, 1)
            key = kj * tk + lax.broadcasted_iota(jnp.int32, (1, tk), 1)
            s = jnp.where((qi >= nq_ctx) | (key < Cn), s, NEG)
            m_old = m_s[...]
            m_new = jnp.maximum(m_old, jnp.max(s, axis=1, keepdims=True))
            alpha = jnp.exp(m_old - m_new)
            p = jnp.exp(s - m_new)
            l_s[...] = alpha * l_s[...] + jnp.sum(p, axis=1, keepdims=True)
            acc_s[...] = alpha * acc_s[...] + _dg(p, v_ref[...], 1, 0)
            m_s[...] = m_new

        @pl.when(kj == nk - 1)
        def _():
            o4 = acc_s[...] / l_s[...]
            o_ref[...] = _compact_heads([o4[tq * h:tq * (h + 1)] for h in range(4)], g)
            lse = m_s[...] + jnp.log(l_s[...])
            lse_ref[...] = jnp.concatenate([jnp.broadcast_to(lse[tq * h:tq * (h + 1)], (tq, 128)) for h in range(4)], axis=1)

    return pl.pallas_call(
        kern, grid=(2, nq, nk),
        in_specs=[pl.BlockSpec((tq, 512), lambda g, qi, kj: (qi, g)), pl.BlockSpec((tk, 128), _kv_map(Cn, tq, tk, 0)),
                  pl.BlockSpec((tk, 128), _kv_map(Cn, tq, tk, 1))],
        out_specs=[pl.BlockSpec((tq, 256), lambda g, qi, kj: (qi, g)), pl.BlockSpec((tq, 512), lambda g, qi, kj: (qi, g))],
        out_shape=[jax.ShapeDtypeStruct((T, 512), F32), jax.ShapeDtypeStruct((T, 1024), F32)],
        scratch_shapes=[pltpu.VMEM((4 * tq, 1), F32), pltpu.VMEM((4 * tq, 1), F32), pltpu.VMEM((4 * tq, 128), F32)],
        compiler_params=_cp("arbitrary", "arbitrary", "arbitrary"), name="flash_fwd")(Qp, KV, KV)


def flash_bwd(Qp, KV, O, LSE, dOall, Cn, tq, tk):
    T = Qp.shape[0]
    nq, nk = T // tq, T // tk
    nq_ctx = Cn // tq

    def kern(q_ref, k_ref, v_ref, o_ref, lse_ref, do_ref, dq_ref, dkv_ref, dq_s, do_s, dl_s):
        g, qi, kj = pl.program_id(0), pl.program_id(1), pl.program_id(2)

        @pl.when((g == 0) & (qi == 0) & (kj == 0))
        def _():
            dkv_ref[...] = jnp.zeros_like(dkv_ref)

        @pl.when(kj == 0)
        def _():
            do4 = jnp.concatenate(_expand_heads(do_ref[...], g), axis=0)
            o4 = jnp.concatenate(_expand_heads(o_ref[...], g), axis=0)
            do_s[...] = do4
            dl_s[...] = jnp.sum(do4 * o4, axis=1, keepdims=True)
            dq_s[...] = jnp.zeros_like(dq_s)

        @pl.when((qi >= nq_ctx) | (kj * tk < Cn))
        def _():
            q4 = jnp.concatenate([q_ref[:, 128 * h:128 * (h + 1)] for h in range(4)], axis=0)
            lse4 = jnp.concatenate([jnp.max(lse_ref[:, 128 * h:128 * (h + 1)], axis=1, keepdims=True) for h in range(4)], axis=0)
            k = k_ref[...]
            v = v_ref[...]
            s = _dg(q4, k, 1, 1)
            key = kj * tk + lax.broadcasted_iota(jnp.int32, (1, tk), 1)
            s = jnp.where((qi >= nq_ctx) | (key < Cn), s, NEG)
            p = jnp.exp(s - lse4)
            do4 = do_s[...]
            dv = _dg(p, do4, 0, 0)
            dp = _dg(do4, v, 1, 1)
            ds = p * (dp - dl_s[...])
            dq_s[...] += _dg(ds, k, 1, 0)
            dk = _dg(ds, q4, 0, 0)
            rows = pl.ds(pl.multiple_of(kj * tk, tk), tk)
            dkv_ref[rows, 0:128] += dk
            dkv_ref[rows, 128:256] += dv

        @pl.when(kj == nk - 1)
        def _():
            dq = dq_s[...]
            dq_ref[...] = jnp.concatenate([dq[tq * h:tq * (h + 1)] for h in range(4)], axis=1)

    return pl.pallas_call(
        kern, grid=(2, nq, nk),
        in_specs=[pl.BlockSpec((tq, 512), lambda g, qi, kj: (qi, g)), pl.BlockSpec((tk, 128), _kv_map(Cn, tq, tk, 0)),
                  pl.BlockSpec((tk, 128), _kv_map(Cn, tq, tk, 1)), pl.BlockSpec((tq, 256), lambda g, qi, kj: (qi, g)),
                  pl.BlockSpec((tq, 512), lambda g, qi, kj: (qi, g)), pl.BlockSpec((tq, 256), lambda g, qi, kj: (qi, 4 + g))],
        out_specs=[pl.BlockSpec((tq, 512), lambda g, qi, kj: (qi, g)), pl.BlockSpec((T, 256), lambda g, qi, kj: (0, 0))],
        out_shape=[jax.ShapeDtypeStruct((T, 1024), F32), jax.ShapeDtypeStruct((T, 256), F32)],
        scratch_shapes=[pltpu.VMEM((4 * tq, 128), F32), pltpu.VMEM((4 * tq, 128), F32), pltpu.VMEM((4 * tq, 1), F32)],
        compiler_params=_cp("arbitrary", "arbitrary", "arbitrary"), name="flash_bwd")(Qp, KV, KV, O, LSE, dOall)


def _pool_band(t0, w0, n, win, gi, Cn, T):
    i = lax.broadcasted_iota(jnp.int32, (n, win), 0)
    jx = lax.broadcasted_iota(jnp.int32, (n, win), 1)
    t = t0 + i
    tp = w0 + jx
    half = lax.shift_left(jnp.int32(1), gi)
    lo = jnp.maximum(t - half, jnp.where(t < Cn, 0, Cn))
    hi = jnp.minimum(t + half - 1, jnp.where(t < Cn, Cn, T) - 1)
    cnt = (hi - lo + 1).astype(F32)
    return jnp.where((tp >= lo) & (tp <= hi), 1.0 / cnt, 0.0) - jnp.where(tp == t, 1.0, 0.0)


def _pool_geom(T):
    n = _pick(T, (256, 128))
    return n, n + 128


def pool_fwd(P, pw, ps, Cn):
    T = P.shape[0]
    n, win = _pool_geom(T)

    def kern(u_ref, w_ref, s_ref, o_ref):
        gi = pl.program_id(0)

        def blk(b, carry):
            t0 = pl.multiple_of(b * n, n)
            w0 = pl.multiple_of(jnp.clip(t0 - 64, 0, T - win), 64)
            d = hdot(_pool_band(t0, w0, n, win, gi, Cn, T), u_ref[pl.ds(w0, win), :])
            o_ref[pl.ds(t0, n), :] = bdot(d, w_ref[...]) * s_ref[...]
            return carry

        lax.fori_loop(0, T // n, blk, 0)

    return pl.pallas_call(
        kern, grid=(4,),
        in_specs=[pl.BlockSpec((T, 128), lambda g: (0, CB["b_in"] + g)), pl.BlockSpec((None, 128, 128), lambda g: (g, 0, 0)),
                  pl.BlockSpec((1, 128), lambda g: (0, g))],
        out_specs=pl.BlockSpec((T, 128), lambda g: (0, g)), out_shape=jax.ShapeDtypeStruct((T, 512), F32),
        compiler_params=_cp("arbitrary"), name="pool_fwd")(P, pw, ps)


def pool_bwd(P, pw, ps, dOall, dP, Cn):
    T = P.shape[0]
    n, win = _pool_geom(T)

    def kern(u_ref, w_ref, s_ref, do_ref, dp_in, du_ref, dw_ref, ds_ref):
        gi = pl.program_id(0)
        du_ref[...] = jnp.zeros_like(du_ref)
        dw_ref[...] = jnp.zeros_like(dw_ref)
        ds_ref[...] = jnp.zeros_like(ds_ref)

        def blk(b, carry):
            t0 = pl.multiple_of(b * n, n)
            w0 = pl.multiple_of(jnp.clip(t0 - 64, 0, T - win), 64)
            band = _pool_band(t0, w0, n, win, gi, Cn, T)
            _, vjp = jax.vjp(lambda uw, w, s: bdot(hdot(band, uw), w) * s, u_ref[pl.ds(w0, win), :], w_ref[...], s_ref[...])
            duw, dw, ds = vjp(do_ref[pl.ds(t0, n), :])
            du_ref[pl.ds(w0, win), :] += duw
            dw_ref[...] += dw
            ds_ref[...] += ds
            return carry

        lax.fori_loop(0, T // n, blk, 0)

    return pl.pallas_call(
        kern, grid=(4,),
        in_specs=[pl.BlockSpec((T, 128), lambda g: (0, CB["b_in"] + g)), pl.BlockSpec((None, 128, 128), lambda g: (g, 0, 0)),
                  pl.BlockSpec((1, 128), lambda g: (0, g)), pl.BlockSpec((T, 128), lambda g: (0, 4 + g)), pl.BlockSpec(memory_space=pl.ANY)],
        out_specs=[pl.BlockSpec((T, 128), lambda g: (0, CB["b_in"] + g)), pl.BlockSpec((None, 128, 128), lambda g: (g, 0, 0)),
                   pl.BlockSpec((1, 128), lambda g: (0, g))],
        out_shape=[jax.ShapeDtypeStruct(dP.shape, F32), jax.ShapeDtypeStruct((4, 128, 128), F32), jax.ShapeDtypeStruct((1, 512), F32)],
        input_output_aliases={4: 0}, compiler_params=_cp("arbitrary"), name="pool_bwd")(P, pw, ps, dOall, dP)


def _conv_geom(T):
    n = _pick(T, (256, 128))
    return n, n + 32


def _shifted(win_val, off, n):
    w = win_val.shape[0]
    return pltpu.roll(win_val, (w - off) % w, 0)[:n]


def _seg_bounds(t, Cn, T):
    return jnp.where(t < Cn, 0, Cn), jnp.where(t < Cn, Cn, T)


def conv_fwd(P, cw, cb, Cn):
    T = P.shape[0]
    n, win = _conv_geom(T)

    def kern(a_ref, g_ref, w_ref, b_ref, y_ref):
        def blk(b, carry):
            t0 = pl.multiple_of(b * n, n)
            w0 = pl.multiple_of(jnp.clip(t0 - 16, 0, T - win), 8)
            rows = pl.ds(w0, win)
            u = a_ref[rows, :] * jax.nn.sigmoid(g_ref[rows, :])
            t = t0 + lax.broadcasted_iota(jnp.int32, (n, 1), 0)
            lo, hi = _seg_bounds(t, Cn, T)
            acc = jnp.zeros((n, 128), F32)
            for k in range(CONV_WIDTH):
                src = t + (k - CONV_WIDTH // 2)
                sh = _shifted(u, t0 - w0 + k - CONV_WIDTH // 2, n)
                acc = acc + jnp.where((src >= lo) & (src < hi), sh, 0.0) * w_ref[k:k + 1, :]
            y_ref[pl.ds(t0, n), :] = acc + b_ref[...]
            return carry

        lax.fori_loop(0, T // n, blk, 0)

    slab = lambda off: pl.BlockSpec((T, 128), lambda c: (0, off + c))
    return pl.pallas_call(
        kern, grid=(4,),
        in_specs=[slab(CB["d_glu"]), slab(CB["d_glu"] + 4), pl.BlockSpec((CONV_PAD, 128), lambda c: (0, c)), pl.BlockSpec((1, 128), lambda c: (0, c))],
        out_specs=pl.BlockSpec((T, 128), lambda c: (0, c)), out_shape=jax.ShapeDtypeStruct((T, 512), F32),
        compiler_params=_cp("arbitrary"), name="conv_fwd")(P, P, cw, cb)


def conv_bwd(P, cw, dY, dP, Cn):
    T = P.shape[0]
    n, win = _conv_geom(T)
    half = CONV_WIDTH // 2

    def kern(a_ref, g_ref, w_ref, dy_ref, dp_in, da_ref, dg_ref, dw_ref, db_ref):
        dw_ref[...] = jnp.zeros_like(dw_ref)
        db_ref[...] = jnp.zeros_like(db_ref)

        def blk(b, carry):
            t0 = pl.multiple_of(b * n, n)
            w0 = pl.multiple_of(jnp.clip(t0 - 16, 0, T - win), 8)
            rows = pl.ds(w0, win)
            cur = pl.ds(t0, n)
            dyw = dy_ref[rows, :]
            t = t0 + lax.broadcasted_iota(jnp.int32, (n, 1), 0)
            lo, hi = _seg_bounds(t, Cn, T)
            du = jnp.zeros((n, 128), F32)
            for k in range(CONV_WIDTH):
                src = t - (k - half)
                sh = _shifted(dyw, t0 - w0 - (k - half), n)
                du = du + jnp.where((src >= lo) & (src < hi), sh, 0.0) * w_ref[k:k + 1, :]
            a = a_ref[cur, :]
            sig = jax.nn.sigmoid(g_ref[cur, :])
            da_ref[cur, :] = du * sig
            dg_ref[cur, :] = du * a * sig * (1.0 - sig)
            u = a_ref[rows, :] * jax.nn.sigmoid(g_ref[rows, :])
            dyc = dy_ref[cur, :]
            for k in range(CONV_WIDTH):
                src = t + (k - half)
                sh = _shifted(u, t0 - w0 + k - half, n)
                dw_ref[k:k + 1, :] += jnp.sum(jnp.where((src >= lo) & (src < hi), sh, 0.0) * dyc, axis=0, keepdims=True)
            db_ref[...] += jnp.sum(dyc, axis=0, keepdims=True)
            return carry

        lax.fori_loop(0, T // n, blk, 0)

    slab = lambda off: pl.BlockSpec((T, 128), lambda c: (0, off + c))
    return pl.pallas_call(
        kern, grid=(4,),
        in_specs=[slab(CB["d_glu"]), slab(CB["d_glu"] + 4), pl.BlockSpec((CONV_PAD, 128), lambda c: (0, c)), slab(0),
                  pl.BlockSpec(memory_space=pl.ANY)],
        out_specs=[slab(CB["d_glu"]), slab(0), pl.BlockSpec((CONV_PAD, 128), lambda c: (0, c)), pl.BlockSpec((1, 128), lambda c: (0, c))],
        out_shape=[jax.ShapeDtypeStruct(dP.shape, F32), jax.ShapeDtypeStruct((T, 512), F32), jax.ShapeDtypeStruct((CONV_PAD, 512), F32),
                   jax.ShapeDtypeStruct((1, 512), F32)],
        input_output_aliases={4: 0}, compiler_params=_cp("arbitrary"), name="conv_bwd")(P, P, cw, dY, dP)


def copy_cols(name, src, dP, col0, Cn, tm):
    T = dP.shape[0]

    def body(i, j, r, c, s):
        return [r[0]], [], []

    return rowcall(name, body, T, tm, src.shape[1] // 128, Cn, [(src, 128, lambda j: j)], [], [], [("into", dP, 128, lambda j: col0 + j)])[0]


def ada_fwd(cvec, w, b):
    L, D, wc = w.shape

    def kern(c_ref, w_ref, b_ref, o_ref):
        o_ref[...] = _dg(_silu(c_ref[...]), w_ref[...], 1, 0) + b_ref[...]

    return pl.pallas_call(
        kern, grid=(L,),
        in_specs=[pl.BlockSpec((16, D), lambda l: (0, 0)), pl.BlockSpec((None, D, wc), lambda l: (l, 0, 0)), pl.BlockSpec((None, 1, wc), lambda l: (l, 0, 0))],
        out_specs=pl.BlockSpec((None, 16, wc), lambda l: (l, 0, 0)), out_shape=jax.ShapeDtypeStruct((L, 16, wc), F32),
        compiler_params=_cp("arbitrary"), name="ada_fwd")(cvec, w, b)


def ada_bwd(cvec, w, dm):
    L, D, wc = w.shape

    def kern(c_ref, w_ref, d_ref, gw_ref, ds_ref):
        l = pl.program_id(0)
        d = d_ref[...]
        gw_ref[...] = _dg(_silu(c_ref[...]), d, 0, 0)
        part = _dg(d, w_ref[...], 1, 1)

        @pl.when(l == 0)
        def _():
            ds_ref[...] = part

        @pl.when(l > 0)
        def _():
            ds_ref[...] += part

    return pl.pallas_call(
        kern, grid=(L,),
        in_specs=[pl.BlockSpec((16, D), lambda l: (0, 0)), pl.BlockSpec((None, D, wc), lambda l: (l, 0, 0)), pl.BlockSpec((None, 16, wc), lambda l: (l, 0, 0))],
        out_specs=[pl.BlockSpec((None, D, wc), lambda l: (l, 0, 0)), pl.BlockSpec((16, D), lambda l: (0, 0))],
        out_shape=[jax.ShapeDtypeStruct((L, D, wc), F32), jax.ShapeDtypeStruct((16, D), F32)],
        compiler_params=_cp("arbitrary"), name="ada_bwd")(cvec, w, dm)


def silu_grad(cc, parts):
    D = cc.shape[1]

    def kern(c_ref, p_ref, o_ref):
        tot = p_ref[0]
        for k in range(1, N_DEV):
            tot = tot + p_ref[k]
        _, vjp = jax.vjp(_silu, c_ref[...])
        o_ref[...] = vjp(tot)[0]

    return pl.pallas_call(kern, out_shape=jax.ShapeDtypeStruct((1, D), F32), name="silu_grad")(cc, parts)


def _flip(v, f):
    return 1 - v if f else v


_REL = ((0, 0), (1, 0), (0, 1), (1, 1))


def allgather(name, x):
    def body(x_ref, out_ref, send_sems, recv_sems, local_sem):
        mx, my, mc = lax.axis_index("x"), lax.axis_index("y"), lax.axis_index("c")
        me, sibling = (mx, my, mc), (mx, my, 1 - mc)
        chips = [(_flip(mx, fx), _flip(my, fy)) for fx, fy in _REL[1:]]

        def slot(px, py, pc):
            return out_ref.at[4 * px + 2 * py + pc]

        def copy(k, block, to, src=None):
            return pltpu.make_async_remote_copy(
                src_ref=slot(*block) if src is None else src, dst_ref=slot(*block),
                send_sem=send_sems.at[k], recv_sem=recv_sems.at[k], device_id=to, device_id_type=MESH)

        mine = pltpu.make_async_copy(x_ref, slot(*me), local_sem)
        mine.start()
        first = [copy(0, me, sibling, src=x_ref)] + [copy(1 + j, me, (*chip, mc), src=x_ref) for j, chip in enumerate(chips)]
        for cp in first:
            cp.start()
        passed = [copy(4 + j, (*chip, mc), sibling) for j, chip in enumerate(chips)]
        for j, chip in enumerate(chips):
            copy(1 + j, (*chip, mc), me).wait_recv()
            passed[j].start()
        copy(0, sibling, me).wait_recv()
        for j, chip in enumerate(chips):
            copy(4 + j, (*chip, 1 - mc), me).wait_recv()
        for cp in first + passed:
            cp.wait_send()
        mine.wait()

    return pl.pallas_call(
        body, out_shape=jax.ShapeDtypeStruct((N_DEV,) + x.shape, x.dtype),
        in_specs=[pl.BlockSpec(memory_space=pl.ANY)], out_specs=pl.BlockSpec(memory_space=pl.ANY),
        scratch_shapes=[pltpu.SemaphoreType.DMA((7,)), pltpu.SemaphoreType.DMA((7,)), pltpu.SemaphoreType.DMA(())], name=name)(x)


def cols_from_devs(name, xg, l, nrow):
    _, _, C = xg.shape
    tr = _row_tile(nrow, N_DEV * C, xg.dtype.itemsize, 16, 8 << 20)
    nblk = nrow // tr

    def body(x_ref, o_ref):
        for d in range(N_DEV):
            o_ref[:, d * C:(d + 1) * C] = x_ref[d]

    return pl.pallas_call(
        body, grid=(nblk,), in_specs=[pl.BlockSpec((N_DEV, tr, C), lambda i: (0, l * nblk + i, 0))],
        out_specs=pl.BlockSpec((tr, N_DEV * C), lambda i: (i, 0)), out_shape=jax.ShapeDtypeStruct((nrow, N_DEV * C), xg.dtype),
        compiler_params=_cp("parallel"), name=name)(xg)


def cols_to_devs(name, g, l, L, prev):
    nrow, W = g.shape
    C = W // N_DEV
    tr = _row_tile(nrow, W, 4, 8, 8 << 20)
    nblk = nrow // tr

    def body(x_ref, *rest):
        o_ref = rest[-1]
        for d in range(N_DEV):
            o_ref[d] = x_ref[:, d * C:(d + 1) * C]

    in_specs, args, aliases = [pl.BlockSpec((tr, W), lambda i: (i, 0))], [g], {}
    if prev is not None:
        in_specs.append(pl.BlockSpec(memory_space=pl.ANY))
        args.append(prev)
        aliases = {1: 0}
    return pl.pallas_call(
        body, grid=(nblk,), in_specs=in_specs, out_specs=pl.BlockSpec((N_DEV, tr, C), lambda i: (0, l * nblk + i, 0)),
        out_shape=jax.ShapeDtypeStruct((N_DEV, L * nrow, C), F32), input_output_aliases=aliases,
        compiler_params=_cp("parallel"), name=name)(*args)


def rs_exchange_sibling(tag, buf):
    _, R, C = buf.shape

    def body(buf_ref, out_ref, send_sems, recv_sems):
        mx, my, mc = lax.axis_index("x"), lax.axis_index("y"), lax.axis_index("c")
        sibling = (mx, my, 1 - mc)
        cps = []
        for j, (fx, fy) in enumerate(_REL):
            d = 4 * _flip(mx, fx) + 2 * _flip(my, fy) + (1 - mc)
            cps.append(pltpu.make_async_remote_copy(src_ref=buf_ref.at[d], dst_ref=out_ref.at[j], send_sem=send_sems.at[j],
                                                    recv_sem=recv_sems.at[j], device_id=sibling, device_id_type=MESH))
        for cp in cps:
            cp.start()
        for cp in cps:
            cp.wait_recv()
        for cp in cps:
            cp.wait_send()

    return pl.pallas_call(
        body, out_shape=jax.ShapeDtypeStruct((4, R, C), buf.dtype),
        in_specs=[pl.BlockSpec(memory_space=pl.ANY)], out_specs=pl.BlockSpec(memory_space=pl.ANY),
        scratch_shapes=[pltpu.SemaphoreType.DMA((4,)), pltpu.SemaphoreType.DMA((4,))], name="rs_sibling_" + tag)(buf)


def rs_chip_sum(tag, buf, recv, idx, tr):
    _, R, C = buf.shape

    def kern(idx_ref, b_ref, r_ref, own_ref, sb_ref):
        j = pl.program_id(1)
        s = b_ref[...] + r_ref[...]
        sb_ref[...] = s.astype(BF)

        @pl.when(j == 0)
        def _():
            own_ref[...] = s

    return pl.pallas_call(
        kern, grid_spec=pltpu.PrefetchScalarGridSpec(
            num_scalar_prefetch=1, grid=(R // tr, 4),
            in_specs=[pl.BlockSpec((None, tr, C), lambda r, j, idx: (idx[j], r, 0)), pl.BlockSpec((None, tr, C), lambda r, j, idx: (j, r, 0))],
            out_specs=[pl.BlockSpec((tr, C), lambda r, j, idx: (r, 0)), pl.BlockSpec((None, tr, C), lambda r, j, idx: (j, r, 0))]),
        out_shape=[jax.ShapeDtypeStruct((R, C), F32), jax.ShapeDtypeStruct((4, R, C), BF)],
        compiler_params=_cp("arbitrary", "arbitrary"), name="rs_chip_sum_" + tag)(idx, buf, recv)


def rs_exchange_chips(tag, sb):
    _, R, C = sb.shape

    def body(sb_ref, out_ref, send_sems, recv_sems):
        mx, my, mc = lax.axis_index("x"), lax.axis_index("y"), lax.axis_index("c")
        cps = []
        for j, (fx, fy) in enumerate(_REL):
            if j == 0:
                continue
            cps.append(pltpu.make_async_remote_copy(src_ref=sb_ref.at[j], dst_ref=out_ref.at[j], send_sem=send_sems.at[j - 1],
                                                    recv_sem=recv_sems.at[j - 1], device_id=(_flip(mx, fx), _flip(my, fy), mc),
                                                    device_id_type=MESH))
        for cp in cps:
            cp.start()
        for cp in cps:
            cp.wait_recv()
        for cp in cps:
            cp.wait_send()

    return pl.pallas_call(
        body, out_shape=jax.ShapeDtypeStruct((4, R, C), sb.dtype),
        in_specs=[pl.BlockSpec(memory_space=pl.ANY)], out_specs=pl.BlockSpec(memory_space=pl.ANY),
        scratch_shapes=[pltpu.SemaphoreType.DMA((3,)), pltpu.SemaphoreType.DMA((3,))], name="rs_chips_" + tag)(sb)


def adamw(name, parts, w, m, v, tr):
    R, C = w.shape
    in_specs, args = [], []
    for arr, lead in parts:
        if lead is None:
            in_specs.append(pl.BlockSpec((tr, C), lambda r: (r, 0)))
        else:
            in_specs.append(pl.BlockSpec((None, tr, C), lambda r, k=lead: (k, r, 0)))
        args.append(arr)
    npart = len(parts)
    blk = pl.BlockSpec((tr, C), lambda r: (r, 0))

    def kern(*refs):
        g = refs[0][...].astype(F32)
        for r in refs[1:npart]:
            g = g + r[...].astype(F32)
        w_ref, m_ref, v_ref, g_out, d_out, m_out, v_out = refs[npart:]
        mn = ADAM_B1 * m_ref[...] + (1.0 - ADAM_B1) * g
        vn = ADAM_B2 * v_ref[...] + (1.0 - ADAM_B2) * jnp.square(g)
        m_hat = mn / (1.0 - ADAM_B1 ** ADAM_STEP)
        v_hat = vn / (1.0 - ADAM_B2 ** ADAM_STEP)
        g_out[...] = g
        d_out[...] = -ADAM_LR * (m_hat / (jnp.sqrt(v_hat) + ADAM_EPS) + ADAM_WD * w_ref[...])
        m_out[...] = mn
        v_out[...] = vn

    return pl.pallas_call(
        kern, grid=(R // tr,), in_specs=in_specs + [blk, blk, blk], out_specs=[blk] * 4,
        out_shape=[jax.ShapeDtypeStruct((R, C), F32)] * 4, compiler_params=_cp("parallel"), name=name)(*args, w, m, v)


def _pack(arrs):
    flat = [a.reshape(-1) for a in arrs]
    n = sum(f.shape[0] for f in flat)
    pad = (-n) % (8 * LANES)
    if pad:
        flat.append(jnp.zeros((pad,), flat[0].dtype))
    return jnp.concatenate(flat).reshape(-1, LANES)


def _unpack(packed, shapes):
    flat = packed.reshape(-1)
    out, off = [], 0
    for s in shapes:
        n = int(np.prod(s))
        out.append(flat[off:off + n].reshape(s))
        off += n
    return out


def _row_tile(R, C=LANES, itemsize=4, mult=16, target=1 << 20):
    best = None
    for t in range(mult, R + 1, mult):
        if R % t == 0 and t * C * itemsize <= target:
            best = t
    return best or R


BIG = ("w_in", "w_branch", "w_out", "conv_pw", "conv_w")
SMALL = ("g_pre", "g_post", "na_rpb", "pool_w", "pool_scale", "q_norm", "k_norm", "conv_b", "conv_ln_g", "conv_ln_b")
WEIGHTS = ['c_ctx', 'w_ada', 'b_ada', 'g_pre', 'g_post', 'w_in', 'na_rpb', 'pool_w', 'pool_scale', 'q_norm', 'k_norm', 'conv_w', 'conv_b',
           'conv_ln_g', 'conv_ln_b', 'conv_pw', 'w_branch', 'w_out']


def _rope_tables(S, Cn):
    t = np.arange(S)
    pos = np.stack([t // GRID_W, t % GRID_W], 1).astype(np.float64)
    lane = np.arange(128)
    within = lane % 64
    axis = within // 32
    f = (within % 32) % 16
    freqs = ROPE_THETA ** (-np.arange(16, dtype=np.float32) / 16)
    ang = pos[:, axis].astype(np.float32) * freqs[f][None, :]
    cos = np.cos(ang).astype(np.float32)
    sin = np.sin(ang).astype(np.float32) * np.where((within % 32) < 16, -1.0, 1.0).astype(np.float32)[None, :]
    cos = np.concatenate([np.ones((Cn, 128), np.float32), cos])
    sin = np.concatenate([np.zeros((Cn, 128), np.float32), sin])
    return jnp.asarray(cos), jnp.asarray(sin)


def _own_2d(w_in, w_branch, w_out, conv_pw, conv_w):
    L = w_in.shape[0]
    cwp = jnp.zeros((L, CONV_PAD, conv_w.shape[2]), conv_w.dtype).at[:, :CONV_WIDTH].set(conv_w)
    return dict(w_in=w_in.reshape(-1, w_in.shape[-1]), w_branch=w_branch.reshape(-1, w_branch.shape[-1]),
                w_out=w_out.reshape(-1, w_out.shape[-1]), conv=_pack([conv_pw, cwp]))


def kernel(x, c, ctx, c_ctx, w_ada, b_ada, g_pre, g_post, w_in, na_rpb, pool_w, pool_scale, q_norm, k_norm, conv_w, conv_b, conv_ln_g, conv_ln_b, conv_pw, w_branch, w_out, loss_target, m_c_ctx, m_w_ada, m_b_ada, m_g_pre, m_g_post, m_w_in, m_na_rpb, m_pool_w, m_pool_scale, m_q_norm, m_k_norm, m_conv_w, m_conv_b, m_conv_ln_g, m_conv_ln_b, m_conv_pw, m_w_branch, m_w_out, v_c_ctx, v_w_ada, v_b_ada, v_g_pre, v_g_post, v_w_in, v_na_rpb, v_pool_w, v_pool_scale, v_q_norm, v_k_norm, v_conv_w, v_conv_b, v_conv_ln_g, v_conv_ln_b, v_conv_pw, v_w_branch, v_w_out):
    W = dict(c_ctx=c_ctx, w_ada=w_ada, b_ada=b_ada, g_pre=g_pre, g_post=g_post, w_in=w_in, na_rpb=na_rpb, pool_w=pool_w, pool_scale=pool_scale,
             q_norm=q_norm, k_norm=k_norm, conv_w=conv_w, conv_b=conv_b, conv_ln_g=conv_ln_g, conv_ln_b=conv_ln_b, conv_pw=conv_pw,
             w_branch=w_branch, w_out=w_out)
    Mo = dict(c_ctx=m_c_ctx, w_ada=m_w_ada, b_ada=m_b_ada, g_pre=m_g_pre, g_post=m_g_post, w_in=m_w_in, na_rpb=m_na_rpb, pool_w=m_pool_w,
              pool_scale=m_pool_scale, q_norm=m_q_norm, k_norm=m_k_norm, conv_w=m_conv_w, conv_b=m_conv_b, conv_ln_g=m_conv_ln_g,
              conv_ln_b=m_conv_ln_b, conv_pw=m_conv_pw, w_branch=m_w_branch, w_out=m_w_out)
    Vo = dict(c_ctx=v_c_ctx, w_ada=v_w_ada, b_ada=v_b_ada, g_pre=v_g_pre, g_post=v_g_post, w_in=v_w_in, na_rpb=v_na_rpb, pool_w=v_pool_w,
              pool_scale=v_pool_scale, q_norm=v_q_norm, k_norm=v_k_norm, conv_w=v_conv_w, conv_b=v_conv_b, conv_ln_g=v_conv_ln_g,
              conv_ln_b=v_conv_ln_b, conv_pw=v_conv_pw, w_branch=v_w_branch, w_out=v_w_out)

    S, D = x.shape[1], x.shape[2]
    Cn = ctx.shape[1]
    T = Cn + S
    L = w_in.shape[0]
    IN = w_in.shape[2] * N_DEV
    mx, my, mc = lax.axis_index("x"), lax.axis_index("y"), lax.axis_index("c")
    me = 4 * mx + 2 * my + mc
    tm = _pick(Cn, (256, 128))
    tme = _pick(T, (768, 256, 128))
    tmm = _pick(T, (768, 256, 128))
    tq = _pick(Cn, (256, 128))
    tk = _pick(T, (768, 384, 128))

    cg = allgather("gather_c", c)
    cvec = jnp.zeros((16, D), F32).at[0].set(c_ctx).at[1:1 + N_DEV].set(cg[:, 0])
    wc = w_ada.shape[2]
    b_sh = lax.dynamic_slice_in_dim(b_ada, me * wc, wc, axis=1)[:, None, :]
    modp = ada_fwd(cvec, w_ada, b_sh)
    modg = allgather("gather_mod", modp.reshape(L * 16, wc)).reshape(N_DEV, L, 16, wc)
    mod_full = jnp.transpose(modg, (1, 2, 0, 3)).reshape(L, 16, N_DEV * wc)
    mod2 = jnp.stack([mod_full[:, 0], lax.dynamic_index_in_dim(mod_full, 1 + me, axis=1, keepdims=False)], axis=1)
    shift, scale, gate = [mod2[:, :, None, k * D:(k + 1) * D] for k in range(3)]

    own2 = _own_2d(w_in, w_branch, w_out, conv_pw, conv_w)
    win_g = allgather("gather_w_in", own2["w_in"].astype(BF))
    wbr_g = allgather("gather_w_branch", own2["w_branch"].astype(BF))
    wout_g = allgather("gather_w_out", own2["w_out"].astype(BF))
    conv_g = allgather("gather_conv", own2["conv"].astype(BF)).reshape(N_DEV, -1)
    npw = L * (BRANCH_W // N_DEV) * BRANCH_W
    cpw_g = conv_g[:, :npw].reshape(N_DEV, L, BRANCH_W // N_DEV, BRANCH_W)
    cw_g = conv_g[:, npw:npw + L * CONV_PAD * (BRANCH_W // N_DEV)].reshape(N_DEV, L, CONV_PAD, BRANCH_W // N_DEV)
    Win = [cols_from_devs("w_in_cols", win_g, l, D) for l in range(L)]
    Wb = [cols_from_devs("w_branch_cols", wbr_g, l, 4 * BRANCH_W).reshape(4, BRANCH_W, D) for l in range(L)]
    Wo = [wout_g.reshape(N_DEV, L, D // N_DEV, D)[:, l].reshape(D, D) for l in range(L)]
    Cpw = [cpw_g[:, l].reshape(BRANCH_W, BRANCH_W).astype(F32) for l in range(L)]
    Cw = [jnp.transpose(cw_g[:, l], (1, 0, 2)).reshape(CONV_PAD, BRANCH_W).astype(F32) for l in range(L)]

    cos, sin = _rope_tables(S, Cn)
    tb = na_tables(S, Cn)
    bd = jnp.asarray(np.kron(np.eye(2, dtype=np.float32), np.full((64, 64), 1.0 / 64, np.float32)))
    row2 = lambda a: a.reshape(1, -1)

    X = jnp.concatenate([ctx[0], x[0]], axis=0)
    saved = []
    for l in range(L):
        gp = row2(g_pre[l])
        H = modulate_fwd(X, gp, shift[l], scale[l], Cn, tm)
        P = matmul("proj_in", H, Win[l], out_dtype=F32, tm=tmm, tn=_pick(IN, (1280, 1152, 768, 384, 128)), tk=D)
        bt = bias_table(na_rpb[l], tb)
        oa = na_fwd(P, bt, tb, Cn)
        ob = pool_fwd(P, pool_w[l], row2(pool_scale[l]), Cn)
        qn = row2(jnp.tile(q_norm[l], 2))
        kn = row2(jnp.tile(k_norm[l], 2))
        Qp = prep_q_fwd(P, cos, sin, qn, bd, Cn, tm)
        KV = prep_kv_fwd(P, cos, sin, kn, bd, Cn, tm)
        oc, LSE = flash_fwd(Qp, KV, Cn, tq, tk)
        Yc = conv_fwd(P, Cw[l], row2(conv_b[l]), Cn)
        od = conv_post_fwd(Yc, row2(conv_ln_g[l]), row2(conv_ln_b[l]), Cpw[l], Cn, tm)
        G = gate_outs_fwd(P, oa, ob, oc, od, Cn, tme)
        U = branch_fwd(G, Wb[l], tmm)
        Y = merge_fwd(P, U, Cn, tme)
        Z = matmul("proj_out", Y, Wo[l], out_dtype=F32, tm=tmm, tn=D, tk=D)
        Xn = post_fwd(X, Z, row2(g_post[l]), gate[l], Cn, tm)
        saved.append(dict(X=X, H=H, P=P, bt=bt, oa=oa, ob=ob, oc=oc, od=od, Qp=Qp, KV=KV, LSE=LSE, Yc=Yc, G=G, U=U, Y=Y, Z=Z, qn=qn, kn=kn))
        X = Xn

    tgt = jnp.concatenate([jnp.zeros((Cn, D), F32), loss_target[0]], axis=0)
    dX, loss_acc = loss_call(X, tgt, Cn, tm)
    loss = lax.psum(loss_acc[0, 0], ("x", "y", "c"))

    gsm = {n: [None] * L for n in SMALL}
    gbig = {n: [None] * L for n in BIG}
    g_in = g_br = None
    dmod = [None] * L
    for l in reversed(range(L)):
        sv = saved[l]
        P = sv["P"]
        dZ, dgp, dgate = post_bwd(sv["Z"], dX, row2(g_post[l]), gate[l], Cn, tm)
        gsm["g_post"][l] = dgp[0]
        dY = matmul("proj_out_dy", dZ, Wo[l], cb=1, out_dtype=F32, tm=tmm, tn=D, tk=D)
        gbig["w_out"][l] = matmul("proj_out_dw", sv["Y"], dZ, ca=0, cb=0, tm=_pick(D, (1024, 512, 256)), tn=D, tk=tmm)
        dU, dP = merge_bwd(P, sv["U"], dY, Cn, tme)
        dG = branch_dg(dU, Wb[l], tmm)
        g_br = cols_to_devs("w_branch_devs", branch_dw(sv["G"], dU, tmm).reshape(4 * BRANCH_W, D), l, L, g_br)
        dO, dP = gate_outs_bwd(P, sv["oa"], sv["ob"], sv["oc"], sv["od"], dG, dP, Cn, tme)
        dYc, dlg, dlb, dcpw = conv_post_bwd(sv["Yc"], dO, row2(conv_ln_g[l]), row2(conv_ln_b[l]), Cpw[l], Cn, tm)
        gsm["conv_ln_g"][l], gsm["conv_ln_b"][l], gbig["conv_pw"][l] = dlg[0], dlb[0], dcpw
        dP, dGg, dcw, dcb = conv_bwd(P, Cw[l], dYc, dP, Cn)
        dP = copy_cols("copy_glu_gate", dGg, dP, CB["d_glu"] + 4, Cn, tme)
        gbig["conv_w"][l], gsm["conv_b"][l] = dcw, dcb[0]
        dQp, dKV = flash_bwd(sv["Qp"], sv["KV"], sv["oc"], sv["LSE"], dO, Cn, tq, tk)
        dP, dqn = prep_q_bwd(P, dQp, dP, cos, sin, sv["qn"], bd, Cn, tm)
        dP, dkn = prep_kv_bwd(P, dKV, dP, cos, sin, sv["kn"], bd, Cn, tm)
        gsm["q_norm"][l] = dqn[0, :64] + dqn[0, 64:]
        gsm["k_norm"][l] = dkn[0, :64] + dkn[0, 64:]
        dP, dpw, dps = pool_bwd(P, pool_w[l], row2(pool_scale[l]), dO, dP, Cn)
        gsm["pool_w"][l], gsm["pool_scale"][l] = dpw, dps[0]
        dQa, dKa, dVa, dbt = na_bwd(P, sv["bt"], dO, tb, Cn)
        dP = assemble_na(dKa, dVa, dQa, dP, Cn, tme)
        gsm["na_rpb"][l] = bias_table_grad(dbt, tb)
        dH = matmul("proj_in_dh", dP, Win[l], cb=1, out_dtype=F32, tm=tmm, tn=D, tk=_pick(IN, (1280, 1152, 768, 384, 128)))
        dWin = matmul("proj_in_dw", sv["H"], dP, ca=0, cb=0, tm=_pick(D, (1024, 512, 256)), tn=_pick(IN, (1280, 1152, 768, 384, 128)), tk=tmm)
        g_in = cols_to_devs("w_in_devs", dWin, l, L, g_in)
        dX, dgpre, dsh, dsc = modulate_bwd(sv["X"], dH, dX, row2(g_pre[l]), shift[l], scale[l], Cn, tm)
        gsm["g_pre"][l] = dgpre[0]
        dmod[l] = jnp.concatenate([dsh[:, 0], dsc[:, 0], dgate[:, 0]], axis=1)
    grad_x = dX[Cn:][None]

    idx = jnp.stack([4 * _flip(mx, fx) + 2 * _flip(my, fy) + mc for fx, fy in _REL]).astype(jnp.int32)
    g_out = jnp.transpose(jnp.stack(gbig["w_out"]).reshape(L, N_DEV, D // N_DEV, D), (1, 0, 2, 3)).reshape(N_DEV, L * (D // N_DEV), D)
    g_pw = jnp.transpose(jnp.stack(gbig["conv_pw"]).reshape(L, N_DEV, -1), (1, 0, 2)).reshape(N_DEV, -1)
    g_cw = jnp.transpose(jnp.stack(gbig["conv_w"]).reshape(L, CONV_PAD, N_DEV, BRANCH_W // N_DEV), (2, 0, 1, 3)).reshape(N_DEV, -1)
    g_conv = jnp.concatenate([g_pw, g_cw], axis=1)
    g_conv = jnp.pad(g_conv, ((0, 0), (0, own2["conv"].size - g_conv.shape[1]))).reshape(N_DEV, -1, LANES)
    gbufs = dict(w_in=g_in, w_branch=g_br, w_out=g_out, conv=g_conv)
    mom2 = _own_2d(Mo["w_in"], Mo["w_branch"], Mo["w_out"], Mo["conv_pw"], Mo["conv_w"])
    var2 = _own_2d(Vo["w_in"], Vo["w_branch"], Vo["w_out"], Vo["conv_pw"], Vo["conv_w"])
    big2 = {}
    for tag in ("w_in", "w_branch", "w_out", "conv"):
        R2, C2 = own2[tag].shape
        tr2 = _row_tile(R2, C2)
        recv1 = rs_exchange_sibling(tag, gbufs[tag])
        own, sb = rs_chip_sum(tag, gbufs[tag], recv1, idx, tr2)
        recv2 = rs_exchange_chips(tag, sb)
        big2[tag] = adamw("adamw_" + tag, [(own, None), (recv2, 1), (recv2, 2), (recv2, 3)], own2[tag], mom2[tag], var2[tag], tr2)
    cshapes = [(L, BRANCH_W // N_DEV, BRANCH_W), (L, CONV_PAD, BRANCH_W // N_DEV)]
    conv_unp = [_unpack(r, cshapes) for r in big2["conv"]]
    big_out = dict(w_in=[r.reshape(w_in.shape) for r in big2["w_in"]], w_branch=[r.reshape(w_branch.shape) for r in big2["w_branch"]],
                   w_out=[r.reshape(w_out.shape) for r in big2["w_out"]], conv_pw=[u[0] for u in conv_unp],
                   conv_w=[u[1][:, :CONV_WIDTH] for u in conv_unp])

    small_own = [jnp.stack(gsm[n]) for n in SMALL]
    small_shapes = [a.shape for a in small_own]
    dmod_all = jnp.stack(dmod)
    spack = _pack(small_own + [dmod_all])
    sg = allgather("gather_small", spack)
    Rs = spack.shape[0]
    nsmall = sum(int(np.prod(s)) for s in small_shapes)
    dmod_g = sg.reshape(N_DEV, -1)[:, nsmall:nsmall + dmod_all.size].reshape(N_DEV, L, 2, N_DEV, wc)
    dm_ctx = dmod_g[0, :, 0, :, :]
    for k in range(1, N_DEV):
        dm_ctx = dm_ctx + dmod_g[k, :, 0, :, :]
    dm_ctx = lax.dynamic_index_in_dim(dm_ctx, me, axis=1, keepdims=False)
    dm_b = jnp.transpose(lax.dynamic_index_in_dim(dmod_g[:, :, 1], me, axis=2, keepdims=False), (1, 0, 2))
    dmp = jnp.zeros((L, 16, wc), F32).at[:, 0].set(dm_ctx).at[:, 1:1 + N_DEV].set(dm_b)
    g_wada, dsilu = ada_bwd(cvec, w_ada, dmp)
    cpart = allgather("gather_cctx", dsilu[0:1])
    g_cctx = silu_grad(c_ctx[None], cpart)[0]

    smallw = _pack([W[n] for n in SMALL])
    smallm = _pack([Mo[n] for n in SMALL])
    smallv = _pack([Vo[n] for n in SMALL])
    Rsm = smallw.shape[0]
    small_res = adamw("adamw_small", [(sg[:, :Rsm], k) for k in range(N_DEV)], smallw, smallm, smallv, _row_tile(Rsm))
    small_unp = [_unpack(r, small_shapes) for r in small_res]

    db_parts = dmod_g.reshape(N_DEV, L, 2, N_DEV * wc)
    bshape = (L * N_DEV * wc // LANES, LANES)
    bparts = [(db_parts[k, :, r].reshape(bshape), None) for k in range(N_DEV) for r in range(2)]
    bada_res = adamw("adamw_bada", bparts, b_ada.reshape(bshape), Mo["b_ada"].reshape(bshape), Vo["b_ada"].reshape(bshape), _row_tile(bshape[0]))
    wada_shape = (w_ada.size // LANES, LANES)
    wada_res = adamw("adamw_wada", [(g_wada.reshape(wada_shape), None)], w_ada.reshape(wada_shape), Mo["w_ada"].reshape(wada_shape),
                     Vo["w_ada"].reshape(wada_shape), _row_tile(wada_shape[0]))
    cshape = (D // LANES, LANES)
    cctx_res = adamw("adamw_cctx", [(g_cctx.reshape(cshape), None)], c_ctx.reshape(cshape), Mo["c_ctx"].reshape(cshape), Vo["c_ctx"].reshape(cshape),
                     _row_tile(cshape[0]) if cshape[0] % 8 == 0 else cshape[0])

    res = {}
    for n in BIG:
        res[n] = big_out[n]
    for k, n in enumerate(SMALL):
        res[n] = [u[k] for u in small_unp]
    res["b_ada"] = [r.reshape(b_ada.shape) for r in bada_res]
    res["w_ada"] = [r.reshape(w_ada.shape) for r in wada_res]
    res["c_ctx"] = [r.reshape(c_ctx.shape) for r in cctx_res]
    outs = [loss, grad_x]
    for k in range(4):
        outs += [res[n][k] for n in WEIGHTS]
    return tuple(outs)
```

```python
import functools

import numpy as np
import jax
import jax.numpy as jnp
from jax import lax
from jax.experimental import pallas as pl
from jax.experimental.pallas import tpu as pltpu

F32 = jnp.float32
BF = jnp.bfloat16
HI = lax.Precision.HIGHEST

GRID_W = 64
BRANCH_W = 512
HEAD_DIM = 64
NA_WIN_ROWS = 8
NA_WIN_COLS = 16
NA_TILE_ROWS = 2
NA_BAND = NA_WIN_ROWS + NA_TILE_ROWS - 1
CONV_WIDTH = 31
CONV_PAD = 32
EPS = 1e-6
ROPE_THETA = 10000.0
NEG = -1e30
N_DEV = 8
LANES = 128
VMEM_LIMIT_BYTES = 56 * 1024 * 1024

ADAM_LR, ADAM_B1, ADAM_B2, ADAM_EPS, ADAM_WD, ADAM_STEP = 0.001, 0.9, 0.999, 1e-08, 0.01, 10

CB = dict(a_k=0, a_v=4, c_k=8, c_v=9, a_q=10, c_q=14, a_gate=18, b_in=22, b_gate=26, c_gate=30, d_glu=34, d_gate=42, merge=46)
KV_COLS = 1280
MESH = pl.DeviceIdType.MESH


def _pick(n, cands):
    for c in cands:
        if n % c == 0:
            return c
    raise ValueError(f"no tile for {n} among {cands}")


def _cp(*sem):
    return pltpu.CompilerParams(dimension_semantics=sem if sem else None, vmem_limit_bytes=VMEM_LIMIT_BYTES)


def _dg(x, y, cx, cy):
    return lax.dot_general(x.astype(BF), y.astype(BF), (((cx,), (cy,)), ((), ())), preferred_element_type=F32)


@functools.partial(jax.custom_vjp, nondiff_argnums=(2, 3))
def bdot(a, b, ca=1, cb=0):
    return _dg(a, b, ca, cb)


def _bdot_fwd(a, b, ca, cb):
    return _dg(a, b, ca, cb), (a, b)


def _bdot_bwd(ca, cb, res, g):
    a, b = res
    jb = 1 if cb == 0 else 0
    ia = 0 if ca == 1 else 1
    da = _dg(g, b, 1, jb) if ca == 1 else _dg(b, g, jb, 1)
    db = _dg(a, g, ia, 0) if cb == 0 else _dg(g, a, 0, ia)
    return da.astype(a.dtype), db.astype(b.dtype)


bdot.defvjp(_bdot_fwd, _bdot_bwd)


def hdot(a, b):
    return jnp.dot(a, b, precision=HI, preferred_element_type=F32)


@jax.custom_vjp
def swap64(x):
    return pltpu.roll(x, 64, 1)


swap64.defvjp(lambda x: (swap64(x), None), lambda _, g: (swap64(g),))


@jax.custom_vjp
def partner(x):
    w = x.shape[-1]
    lane = lax.broadcasted_iota(jnp.int32, x.shape, 1)
    return jnp.where((lane % 32) < 16, pltpu.roll(x, w - 16, 1), pltpu.roll(x, 16, 1))


partner.defvjp(lambda x: (partner(x), None), lambda _, g: (partner(g),))


@functools.partial(jax.custom_vjp, nondiff_argnums=(1,))
def split_rows(x, n):
    k = x.shape[0] // n
    return tuple(x[i * k:(i + 1) * k] for i in range(n))


split_rows.defvjp(lambda x, n: (split_rows(x, n), None), lambda n, _, gs: (jnp.concatenate(gs, axis=0),))


def _lane(shape):
    return lax.broadcasted_iota(jnp.int32, shape, len(shape) - 1)


def _silu(x):
    return x * jax.nn.sigmoid(x)


def _normrope(x, w, cos, sin, bd):
    ms = hdot(x * x, bd)
    y = x * lax.rsqrt(ms + EPS) * w
    return y * cos + partner(y) * sin


def matmul(name, a, b, *, ca=1, cb=0, out_dtype=F32, tm, tn, tk):
    M = a.shape[1 - ca]
    K = a.shape[ca]
    N = b.shape[1 - cb]
    assert b.shape[cb] == K and M % tm == 0 and N % tn == 0 and K % tk == 0, (name, a.shape, b.shape)
    nk = K // tk
    a_spec = pl.BlockSpec((tm, tk), lambda i, j, k: (i, k)) if ca == 1 else pl.BlockSpec((tk, tm), lambda i, j, k: (k, i))
    b_spec = pl.BlockSpec((tk, tn), lambda i, j, k: (k, j)) if cb == 0 else pl.BlockSpec((tn, tk), lambda i, j, k: (j, k))

    def kern(a_ref, b_ref, o_ref, *scr):
        part = _dg(a_ref[...], b_ref[...], ca, cb)
        if nk == 1:
            o_ref[...] = part.astype(out_dtype)
            return
        acc_ref, = scr
        k = pl.program_id(2)

        @pl.when(k == 0)
        def _():
            acc_ref[...] = part

        @pl.when(k > 0)
        def _():
            acc_ref[...] += part

        @pl.when(k == nk - 1)
        def _():
            o_ref[...] = acc_ref[...].astype(out_dtype)

    return pl.pallas_call(
        kern, grid=(M // tm, N // tn, nk), in_specs=[a_spec, b_spec],
        out_specs=pl.BlockSpec((tm, tn), lambda i, j, k: (i, j)),
        out_shape=jax.ShapeDtypeStruct((M, N), out_dtype),
        scratch_shapes=[] if nk == 1 else [pltpu.VMEM((tm, tn), F32)],
        compiler_params=_cp("parallel", "parallel", "arbitrary"), name=name,
    )(a, b)


def rowcall(name, body, T, tm, ncol, nctx, rows, consts=(), segs=(), outs=(), accs=(), saccs=()):
    nctxb = nctx // tm
    assert T % tm == 0 and (nctx % tm == 0 or not (segs or saccs))
    nr, nc, ns = len(rows), len(consts), len(segs)

    def seg_map(i, j):
        return (jnp.where(i < nctxb, 0, 1), 0, 0)

    def col_map(cf):
        return lambda i, j: (i, cf(j))

    in_specs, args = [], []
    for arr, bw, cf in rows:
        in_specs.append(pl.BlockSpec((tm, bw), col_map(cf)))
        args.append(arr)
    for arr in consts:
        in_specs.append(pl.BlockSpec(arr.shape, lambda i, j, nd=arr.ndim: (0,) * nd))
        args.append(arr)
    for arr in segs:
        in_specs.append(pl.BlockSpec((None, 1, arr.shape[-1]), seg_map))
        args.append(arr)
    out_shape, out_specs, aliases, out_dtypes = [], [], {}, []
    for o in outs:
        if o[0] == "into":
            _, arr, bw, cf = o
            aliases[len(args)] = len(out_shape)
            in_specs.append(pl.BlockSpec(memory_space=pl.ANY))
            args.append(arr)
            out_shape.append(jax.ShapeDtypeStruct(arr.shape, arr.dtype))
            out_dtypes.append(arr.dtype)
        else:
            cols, dt, bw, cf = o
            out_shape.append(jax.ShapeDtypeStruct((T, cols), dt))
            out_dtypes.append(dt)
        out_specs.append(pl.BlockSpec((tm, bw), col_map(cf)))
    for shp in accs:
        out_shape.append(jax.ShapeDtypeStruct(shp, F32))
        out_specs.append(pl.BlockSpec(shp, lambda i, j, nd=len(shp): (0,) * nd))
    for w in saccs:
        out_shape.append(jax.ShapeDtypeStruct((2, 1, w), F32))
        out_specs.append(pl.BlockSpec((None, 1, w), seg_map))
    n_in = len(args)
    no, na, nsa = len(outs), len(accs), len(saccs)

    def kern(*refs):
        i, j = pl.program_id(0), pl.program_id(1)
        ins = refs[:nr + nc + ns]
        orefs = refs[n_in:n_in + no]
        arefs = refs[n_in + no:n_in + no + na]
        srefs = refs[n_in + no + na:n_in + no + na + nsa]
        ov, av, sv = body(i, j, [r[...] for r in ins[:nr]], [r[...] for r in ins[nr:nr + nc]], [r[...] for r in ins[nr + nc:]])
        for r, v, dt in zip(orefs, ov, out_dtypes):
            r[...] = v.astype(dt)
        if na:
            @pl.when((i == 0) & (j == 0))
            def _():
                for r in arefs:
                    r[...] = jnp.zeros_like(r)
            for r, v in zip(arefs, av):
                r[...] += v
        if nsa:
            @pl.when(((i == 0) | (i == nctxb)) & (j == 0))
            def _():
                for r in srefs:
                    r[...] = jnp.zeros_like(r)
            for r, v in zip(srefs, sv):
                r[...] += v

    res = pl.pallas_call(
        kern, grid=(T // tm, ncol), in_specs=in_specs, out_specs=out_specs, out_shape=out_shape,
        input_output_aliases=aliases, compiler_params=_cp("arbitrary", "arbitrary"), name=name,
    )(*args)
    return res


def _c(k):
    return lambda j: k


def f_modulate(x, g, sh, sc):
    y = x * lax.rsqrt(jnp.mean(x * x, axis=-1, keepdims=True) + EPS) * g
    return y * (1.0 + sc) + sh


def modulate_fwd(X, g_pre, shift, scale, Cn, tm):
    T, D = X.shape

    def body(i, j, r, c, s):
        return [f_modulate(r[0], c[0], s[0], s[1])], [], []

    return rowcall("modulate_fwd", body, T, tm, 1, Cn, [(X, D, _c(0))], [g_pre], [shift, scale], [(D, BF, D, _c(0))])[0]


def modulate_bwd(X, dH, dXn, g_pre, shift, scale, Cn, tm):
    T, D = X.shape

    def body(i, j, r, c, s):
        _, vjp = jax.vjp(f_modulate, r[0], c[0], s[0], s[1])
        dx, dg, dsh, dsc = vjp(r[1])
        return [dx + r[2]], [dg], [dsh, dsc]

    return rowcall("modulate_bwd", body, T, tm, 1, Cn, [(X, D, _c(0)), (dH, D, _c(0)), (dXn, D, _c(0))], [g_pre], [shift, scale],
                   [(D, F32, D, _c(0))], [(1, D)], [D, D])


def f_post(z, gp, gate):
    return gate * (z * lax.rsqrt(jnp.mean(z * z, axis=-1, keepdims=True) + EPS) * gp)


def post_fwd(X, Z, g_post, gate, Cn, tm):
    T, D = X.shape

    def body(i, j, r, c, s):
        return [r[0] + f_post(r[1], c[0], s[0])], [], []

    return rowcall("post_fwd", body, T, tm, 1, Cn, [(X, D, _c(0)), (Z, D, _c(0))], [g_post], [gate], [(D, F32, D, _c(0))])[0]


def post_bwd(Z, dXn, g_post, gate, Cn, tm):
    T, D = Z.shape

    def body(i, j, r, c, s):
        _, vjp = jax.vjp(f_post, r[0], c[0], s[0])
        dz, dgp, dgate = vjp(r[1])
        return [dz], [dgp], [dgate]

    return rowcall("post_bwd", body, T, tm, 1, Cn, [(Z, D, _c(0)), (dXn, D, _c(0))], [g_post], [gate],
                   [(D, BF, D, _c(0))], [(1, D)], [D])


def loss_call(XL, tgt, Cn, tm):
    T, D = XL.shape
    nctxb = Cn // tm

    def body(i, j, r, c, s):
        diff = jnp.where(i >= nctxb, r[0] - r[1], 0.0)
        per_row = jnp.mean(diff * diff, axis=-1, keepdims=True)
        tot = 0.5 * jnp.sum(per_row, axis=0, keepdims=True)
        return [diff / D], [jnp.broadcast_to(tot, (1, LANES))], []

    return rowcall("loss", body, T, tm, 1, Cn, [(XL, D, _c(0)), (tgt, D, _c(0))], [], [], [(D, F32, D, _c(0))], [(1, LANES)])


def f_prep_q(x, qn, cos, sin, bd, jj):
    y = _normrope(x, qn, cos, sin, bd) * 0.125
    lane = _lane(y.shape)
    a = jnp.where(lane < 64, y, 0.0)
    b = jnp.where(lane >= 64, y, 0.0)
    g0 = jj < 2
    return jnp.concatenate([jnp.where(g0, a, swap64(a)), jnp.where(g0, swap64(b), b)], axis=1)


def prep_q_fwd(P, cos, sin, qn, bd, Cn, tm):
    T = P.shape[0]

    def body(i, j, r, c, s):
        return [f_prep_q(r[0], c[0], r[1], r[2], c[1], j)], [], []

    return rowcall("prep_q_fwd", body, T, tm, 4, Cn, [(P, 128, lambda j: CB["c_q"] + j), (cos, 128, _c(0)), (sin, 128, _c(0))], [qn, bd], [],
                   [(1024, BF, 256, lambda j: j)])[0]


def prep_q_bwd(P, dQ, dP, cos, sin, qn, bd, Cn, tm):
    T = P.shape[0]

    def body(i, j, r, c, s):
        _, vjp = jax.vjp(lambda x, w: f_prep_q(x, w, r[1], r[2], c[1], j), r[0], c[0])
        dx, dw = vjp(r[3])
        return [dx], [dw], []

    return rowcall("prep_q_bwd", body, T, tm, 4, Cn,
                   [(P, 128, lambda j: CB["c_q"] + j), (cos, 128, _c(0)), (sin, 128, _c(0)), (dQ, 256, lambda j: j)], [qn, bd], [],
                   [("into", dP, 128, lambda j: CB["c_q"] + j)], [(1, 128)])


def f_prep_kv(x, kn, cos, sin, bd, jj):
    return jnp.where(jj == 0, _normrope(x, kn, cos, sin, bd), x)


def prep_kv_fwd(P, cos, sin, kn, bd, Cn, tm):
    T = P.shape[0]

    def body(i, j, r, c, s):
        return [f_prep_kv(r[0], c[0], r[1], r[2], c[1], j)], [], []

    return rowcall("prep_kv_fwd", body, T, tm, 2, Cn, [(P, 128, lambda j: CB["c_k"] + j), (cos, 128, _c(0)), (sin, 128, _c(0))], [kn, bd], [],
                   [(256, BF, 128, lambda j: j)])[0]


def prep_kv_bwd(P, dKV, dP, cos, sin, kn, bd, Cn, tm):
    T = P.shape[0]

    def body(i, j, r, c, s):
        _, vjp = jax.vjp(lambda x, w: f_prep_kv(x, w, r[1], r[2], c[1], j), r[0], c[0])
        dx, dw = vjp(r[3])
        return [dx], [dw], []

    return rowcall("prep_kv_bwd", body, T, tm, 2, Cn,
                   [(P, 128, lambda j: CB["c_k"] + j), (cos, 128, _c(0)), (sin, 128, _c(0)), (dKV, 128, lambda j: j)], [kn, bd], [],
                   [("into", dP, 128, lambda j: CB["c_k"] + j)], [(1, 128)])


def f_conv_post(y, lg, lb, w):
    mu = jnp.mean(y, axis=-1, keepdims=True)
    var = jnp.mean(jnp.square(y - mu), axis=-1, keepdims=True)
    h = (y - mu) * lax.rsqrt(var + EPS) * lg + lb
    return bdot(_silu(h), w)


def conv_post_fwd(Yc, lg, lb, w, Cn, tm):
    T = Yc.shape[0]

    def body(i, j, r, c, s):
        return [f_conv_post(r[0], c[0], c[1], c[2])], [], []

    return rowcall("conv_post_fwd", body, T, tm, 1, Cn, [(Yc, 512, _c(0))], [lg, lb, w], [], [(512, F32, 512, _c(0))])[0]


def conv_post_bwd(Yc, dO, lg, lb, w, Cn, tm):
    T = Yc.shape[0]

    def body(i, j, r, c, s):
        _, vjp = jax.vjp(f_conv_post, r[0], c[0], c[1], c[2])
        dy, dlg, dlb, dw = vjp(r[1])
        return [dy], [dlg, dlb, dw], []

    return rowcall("conv_post_bwd", body, T, tm, 1, Cn, [(Yc, 512, _c(0)), (dO, 512, _c(3))], [lg, lb, w], [],
                   [(512, F32, 512, _c(0))], [(1, 512), (1, 512), (512, 512)])


def _gate_blk(j):
    b = j // 2
    base = jnp.where(b == 0, CB["a_gate"] // 2, jnp.where(b == 1, CB["b_gate"] // 2, jnp.where(b == 2, CB["c_gate"] // 2, CB["d_gate"] // 2)))
    return base + j % 2


def _sel4(j, vals):
    b = j // 2
    return jnp.where(b == 0, vals[0], jnp.where(b == 1, vals[1], jnp.where(b == 2, vals[2], vals[3])))


def gate_outs_fwd(P, oa, ob, oc, od, Cn, tm):
    T = P.shape[0]

    def body(i, j, r, c, s):
        return [_sel4(j, r[:4]) * _silu(r[4])], [], []

    half = lambda j: j % 2
    return rowcall("gate_outs_fwd", body, T, tm, 8, Cn,
                   [(oa, 256, half), (ob, 256, half), (oc, 256, half), (od, 256, half), (P, 256, _gate_blk)], [], [],
                   [(2048, BF, 256, lambda j: j)])[0]


def gate_outs_bwd(P, oa, ob, oc, od, dG, dP, Cn, tm):
    T = P.shape[0]

    def body(i, j, r, c, s):
        o = _sel4(j, r[:4])
        _, vjp = jax.vjp(lambda oo, gg: oo * _silu(gg), o, r[4])
        do, dg = vjp(r[5])
        return [do, dg], [], []

    half = lambda j: j % 2
    return rowcall("gate_outs_bwd", body, T, tm, 8, Cn,
                   [(oa, 256, half), (ob, 256, half), (oc, 256, half), (od, 256, half), (P, 256, _gate_blk), (dG, 256, lambda j: j)], [], [],
                   [(2048, F32, 256, lambda j: j), ("into", dP, 256, _gate_blk)])


def merge_fwd(P, U, Cn, tm):
    T = P.shape[0]
    D4 = U.shape[1]
    nb = D4 // 4 // 256
    mb = CB["merge"] // 2

    def body(i, j, r, c, s):
        y = 0.0
        for b in range(4):
            y = y + jax.nn.sigmoid(r[b]) * r[4 + b]
        return [y], [], []

    rows = [(P, 256, (lambda j, b=b: mb + b * nb + j)) for b in range(4)] + [(U, 256, (lambda j, b=b: b * nb + j)) for b in range(4)]
    return rowcall("merge_fwd", body, T, tm, nb, Cn, rows, [], [], [(D4 // 4, BF, 256, lambda j: j)])[0]


def merge_bwd(P, U, dY, Cn, tm):
    T, IN = P.shape
    D4 = U.shape[1]
    nb = D4 // 4 // 256
    mb = CB["merge"] // 2

    def body(i, j, r, c, s):
        _, vjp = jax.vjp(lambda m, u: jax.nn.sigmoid(m) * u, r[0], r[1])
        dm, du = vjp(r[2])
        return [du, dm], [], []

    return rowcall("merge_bwd", body, T, tm, 4 * nb, Cn,
                   [(P, 256, lambda j: mb + j), (U, 256, lambda j: j), (dY, 256, lambda j: j % nb)], [], [],
                   [(D4, BF, 256, lambda j: j), (IN, F32, 256, lambda j: mb + j)])


def assemble_na(dK, dV, dQ, dP, Cn, tm):
    T = dP.shape[0]

    def body(i, j, r, c, s):
        return [jnp.where(j < 4, r[0], jnp.where(j < 8, r[1], r[2]))], [], []

    return rowcall("assemble_na", body, T, tm, 12, Cn,
                   [(dK, 128, lambda j: jnp.minimum(j, 3)), (dV, 128, lambda j: jnp.clip(j - 4, 0, 3)), (dQ, 128, lambda j: jnp.clip(j - 8, 0, 3))],
                   [], [], [("into", dP, 128, lambda j: jnp.where(j < 8, j, j + 2))])[0]


def branch_fwd(G, Wb, tm):
    T = G.shape[0]
    D = Wb.shape[2]

    def kern(g_ref, w_ref, o_ref):
        o_ref[...] = _dg(g_ref[...], w_ref[...], 1, 0)

    return pl.pallas_call(
        kern, grid=(T // tm, 4),
        in_specs=[pl.BlockSpec((tm, 512), lambda i, b: (i, b)), pl.BlockSpec((None, 512, D), lambda i, b: (b, 0, 0))],
        out_specs=pl.BlockSpec((tm, D), lambda i, b: (i, b)), out_shape=jax.ShapeDtypeStruct((T, 4 * D), F32),
        compiler_params=_cp("parallel", "arbitrary"), name="branch_fwd")(G, Wb)


def branch_dg(dU, Wb, tm):
    T = dU.shape[0]
    D = Wb.shape[2]

    def kern(u_ref, w_ref, o_ref):
        o_ref[...] = _dg(u_ref[...], w_ref[...], 1, 1)

    return pl.pallas_call(
        kern, grid=(T // tm, 4),
        in_specs=[pl.BlockSpec((tm, D), lambda i, b: (i, b)), pl.BlockSpec((None, 512, D), lambda i, b: (b, 0, 0))],
        out_specs=pl.BlockSpec((tm, 512), lambda i, b: (i, b)), out_shape=jax.ShapeDtypeStruct((T, 2048), F32),
        compiler_params=_cp("parallel", "arbitrary"), name="branch_dg")(dU, Wb)


def branch_dw(G, dU, tk):
    T = G.shape[0]
    D = dU.shape[1] // 4
    nk = T // tk

    def kern(g_ref, u_ref, o_ref):
        k = pl.program_id(1)
        part = _dg(g_ref[...], u_ref[...], 0, 0)

        @pl.when(k == 0)
        def _():
            o_ref[...] = part

        @pl.when(k > 0)
        def _():
            o_ref[...] += part

    return pl.pallas_call(
        kern, grid=(4, nk),
        in_specs=[pl.BlockSpec((tk, 512), lambda b, k: (k, b)), pl.BlockSpec((tk, D), lambda b, k: (k, b))],
        out_specs=pl.BlockSpec((None, 512, D), lambda b, k: (b, 0, 0)), out_shape=jax.ShapeDtypeStruct((4, 512, D), F32),
        compiler_params=_cp("parallel", "arbitrary"), name="branch_dw")(G, dU)


def na_tables(S, Cn):
    R = S // GRID_W
    ntile = (Cn + S) // 128
    nct = Cn // 128
    rs = np.clip(np.arange(R) - NA_WIN_ROWS // 2, 0, R - NA_WIN_ROWS)
    seen, pats, cls, ws = {}, [], [], []
    for t in range(ntile):
        pat = -np.ones((NA_TILE_ROWS, NA_BAND), np.int64)
        w0 = 0
        if t >= nct:
            r0 = NA_TILE_ROWS * (t - nct)
            w0 = min(rs[r0], R - NA_BAND)
            for a in range(NA_TILE_ROWS):
                for j in range(NA_BAND):
                    kr = w0 + j
                    if rs[r0 + a] <= kr < rs[r0 + a] + NA_WIN_ROWS:
                        pat[a, j] = kr - (r0 + a) + NA_WIN_ROWS - 1
        key = pat.tobytes()
        if key not in seen:
            seen[key] = len(pats)
            pats.append(pat)
        cls.append(seen[key])
        ws.append(Cn + GRID_W * int(w0))
    cls = np.asarray(cls, np.int32)
    first = np.asarray([1 if t == 0 or cls[t] != cls[t - 1] else 0 for t in range(ntile)], np.int32)
    assert len(set(cls[first == 1].tolist())) == int(first.sum())
    pats = np.stack(pats)
    ncls = pats.shape[0]
    nrow = -(-(ncls * NA_TILE_ROWS * NA_BAND) // 128) * 128
    m1 = np.zeros((nrow, 128), np.float32)
    flat = pats.reshape(-1)
    for k, dr in enumerate(flat):
        if dr >= 0:
            m1[k, dr] = 1.0
    qc = np.arange(GRID_W)
    col_start = np.clip(qc - NA_WIN_COLS // 2, 0, GRID_W - NA_WIN_COLS)
    kc = np.arange(GRID_W)
    col_ok = (kc[None, :] >= col_start[:, None]) & (kc[None, :] < col_start[:, None] + NA_WIN_COLS)
    m2 = np.zeros((128, GRID_W * GRID_W), np.float32)
    for q in range(GRID_W):
        for k in range(GRID_W):
            if col_ok[q, k]:
                m2[k - q + NA_WIN_COLS - 1, q * GRID_W + k] = 1.0
    valid = (pats >= 0)[:, :, :, None, None] & col_ok[None, None, None]
    vmask = np.transpose(valid, (0, 2, 4, 1, 3)).reshape(ncls, 1, NA_BAND * GRID_W, 1, NA_TILE_ROWS * GRID_W)
    vmask = np.broadcast_to(vmask, (ncls, 1, NA_BAND * GRID_W, 2, NA_TILE_ROWS * GRID_W)).reshape(ncls, 1, NA_BAND * GRID_W, 256)
    return dict(cls=cls, ws=np.asarray(ws, np.int32), first=first, m1=m1, m2=m2, vmask=vmask, ncls=ncls, nrow=nrow)


def rpb_map(name, x, left, right):
    H = x.shape[0]

    def kern(x_ref, l_ref, r_ref, o_ref):
        o_ref[...] = hdot(hdot(l_ref[...], x_ref[...]), r_ref[...])

    return pl.pallas_call(
        kern, grid=(H,),
        in_specs=[pl.BlockSpec((None,) + x.shape[1:], lambda h: (h, 0, 0)), pl.BlockSpec(left.shape, lambda h: (0, 0)),
                  pl.BlockSpec(right.shape, lambda h: (0, 0))],
        out_specs=pl.BlockSpec((None, left.shape[0], right.shape[1]), lambda h: (h, 0, 0)),
        out_shape=jax.ShapeDtypeStruct((H, left.shape[0], right.shape[1]), F32), compiler_params=_cp("parallel"), name=name)(x, left, right)


def bias_table(rpb, tb):
    H = rpb.shape[0]
    xp = jnp.zeros((H, 128, 128), F32).at[:, :rpb.shape[1], :rpb.shape[2]].set(rpb)
    a = rpb_map("rpb_expand", xp, jnp.asarray(tb["m1"]), jnp.asarray(tb["m2"]))
    ncls = tb["ncls"]
    a = a[:, :ncls * NA_TILE_ROWS * NA_BAND].reshape(H // 2, 2, ncls, NA_TILE_ROWS, NA_BAND, GRID_W, GRID_W)
    a = jnp.transpose(a, (2, 0, 4, 6, 1, 3, 5)).reshape(ncls, H // 2, NA_BAND * GRID_W, 256)
    return jnp.where(jnp.asarray(tb["vmask"]), a, NEG)


def bias_table_grad(dbt, tb):
    ncls = tb["ncls"]
    H2 = dbt.shape[1]
    d = dbt.reshape(ncls, H2, NA_BAND, GRID_W, 2, NA_TILE_ROWS, GRID_W)
    d = jnp.transpose(d, (1, 4, 0, 5, 2, 6, 3)).reshape(H2 * 2, ncls * NA_TILE_ROWS * NA_BAND, GRID_W * GRID_W)
    d = jnp.pad(d, ((0, 0), (0, tb["nrow"] - d.shape[1]), (0, 0)))
    g = rpb_map("rpb_reduce", d, jnp.asarray(tb["m1"].T.copy()), jnp.asarray(tb["m2"].T.copy()))
    return g[:, :2 * NA_WIN_ROWS - 1, :2 * NA_WIN_COLS - 1]


def f_na(q, kb, vb, kc, vc, bias_t):
    d = lax.broadcasted_iota(jnp.int32, (128, 1), 0)
    q_t = q.T
    qbd = jnp.concatenate([jnp.where(d < 64, q_t, 0.0), jnp.where(d >= 64, q_t, 0.0)], axis=1)
    sb = bdot(kb, qbd) * 0.125 + bias_t
    sc = bdot(kc, qbd) * 0.125
    m = jnp.maximum(jnp.max(sb, axis=0, keepdims=True), jnp.max(sc, axis=0, keepdims=True))
    eb = jnp.exp(sb - m)
    ec = jnp.exp(sc - m)
    den = jnp.sum(eb, axis=0, keepdims=True) + jnp.sum(ec, axis=0, keepdims=True)
    of = bdot(eb / den, vb, 0, 0) + bdot(ec / den, vc, 0, 0)
    o0, o1 = split_rows(of, 2)
    return jnp.where(_lane(o0.shape) < 64, o0, o1)


def _na_specs(T, Cn, nband):
    qs = pl.BlockSpec((128, 128), lambda hp, t, *_: (t, CB["a_q"] + hp))
    ks = pl.BlockSpec((T, 128), lambda hp, t, *_: (0, CB["a_k"] + hp))
    vs = pl.BlockSpec((T, 128), lambda hp, t, *_: (0, CB["a_v"] + hp))
    bs = pl.BlockSpec((None, None, nband, 256), lambda hp, t, cls, ws, first: (cls[t], hp, 0, 0))
    return qs, ks, vs, bs


def na_fwd(P, bt, tb, Cn):
    T = P.shape[0]
    nband = NA_BAND * GRID_W
    qs, ks, vs, bs = _na_specs(T, Cn, nband)

    def kern(cls, ws, first, q_ref, k_ref, v_ref, b_ref, o_ref):
        w0 = pl.multiple_of(ws[pl.program_id(1)], 64)
        o_ref[...] = f_na(q_ref[...], k_ref[pl.ds(w0, nband), :], v_ref[pl.ds(w0, nband), :], k_ref[0:Cn, :], v_ref[0:Cn, :], b_ref[...])

    return pl.pallas_call(
        kern, grid_spec=pltpu.PrefetchScalarGridSpec(
            num_scalar_prefetch=3, grid=(4, T // 128), in_specs=[qs, ks, vs, bs],
            out_specs=pl.BlockSpec((128, 128), lambda hp, t, *_: (t, hp))),
        out_shape=jax.ShapeDtypeStruct((T, 512), F32), compiler_params=_cp("arbitrary", "arbitrary"), name="na_fwd",
    )(jnp.asarray(tb["cls"]), jnp.asarray(tb["ws"]), jnp.asarray(tb["first"]), P, P, P, bt)


def na_bwd(P, bt, dO, tb, Cn):
    T = P.shape[0]
    nband = NA_BAND * GRID_W
    qs, ks, vs, bs = _na_specs(T, Cn, nband)
    dos = pl.BlockSpec((128, 128), lambda hp, t, *_: (t, hp))

    def kern(cls, ws, first, q_ref, k_ref, v_ref, b_ref, do_ref, dq_ref, dk_ref, dv_ref, db_ref):
        t = pl.program_id(1)
        w0 = pl.multiple_of(ws[t], 64)
        band = pl.ds(w0, nband)
        _, vjp = jax.vjp(f_na, q_ref[...], k_ref[band, :], v_ref[band, :], k_ref[0:Cn, :], v_ref[0:Cn, :], b_ref[...])
        dq, dkb, dvb, dkc, dvc, db = vjp(do_ref[...])
        dq_ref[...] = dq

        @pl.when(t == 0)
        def _():
            dk_ref[...] = jnp.zeros_like(dk_ref)
            dv_ref[...] = jnp.zeros_like(dv_ref)

        dk_ref[band, :] += dkb
        dv_ref[band, :] += dvb
        dk_ref[0:Cn, :] += dkc
        dv_ref[0:Cn, :] += dvc

        @pl.when(first[t] == 1)
        def _():
            db_ref[...] = db

        @pl.when(first[t] == 0)
        def _():
            db_ref[...] += db

    full = lambda hp, t, *_: (0, hp)
    return pl.pallas_call(
        kern, grid_spec=pltpu.PrefetchScalarGridSpec(
            num_scalar_prefetch=3, grid=(4, T // 128), in_specs=[qs, ks, vs, bs, dos],
            out_specs=[pl.BlockSpec((128, 128), lambda hp, t, *_: (t, hp)), pl.BlockSpec((T, 128), full), pl.BlockSpec((T, 128), full),
                       pl.BlockSpec((None, None, nband, 256), lambda hp, t, cls, ws, first: (cls[t], hp, 0, 0))]),
        out_shape=[jax.ShapeDtypeStruct((T, 512), F32), jax.ShapeDtypeStruct((T, 512), F32), jax.ShapeDtypeStruct((T, 512), F32),
                   jax.ShapeDtypeStruct(bt.shape, F32)],
        compiler_params=_cp("arbitrary", "arbitrary"), name="na_bwd",
    )(jnp.asarray(tb["cls"]), jnp.asarray(tb["ws"]), jnp.asarray(tb["first"]), P, P, P, bt, dO)


def _expand_heads(blk, g):
    out = []
    for p in range(2):
        pair = blk[:, 128 * p:128 * (p + 1)]
        lane = _lane(pair.shape)
        a = jnp.where(lane < 64, pair, 0.0)
        b = jnp.where(lane >= 64, pair, 0.0)
        out.append(jnp.where(g == 0, a, pltpu.roll(a, 64, 1)))
        out.append(jnp.where(g == 0, pltpu.roll(b, 64, 1), b))
    return out


def _compact_heads(hs, g):
    out = []
    for p in range(2):
        e, o = hs[2 * p], hs[2 * p + 1]
        lane = _lane(e.shape)
        e0 = jnp.where(g == 0, e, pltpu.roll(e, 64, 1))
        o1 = jnp.where(g == 0, pltpu.roll(o, 64, 1), o)
        out.append(jnp.where(lane < 64, e0, o1))
    return jnp.concatenate(out, axis=1)


def _kv_map(Cn, tq, tk, col):
    nq_ctx = Cn // tq
    last_ctx = (Cn - 1) // tk

    def f(g, qi, kj):
        return (jnp.where(qi < nq_ctx, jnp.minimum(kj, last_ctx), kj), col)
    return f


def flash_fwd(Qp, KV, Cn, tq, tk):
    T = Qp.shape[0]
    nq, nk = T // tq, T // tk
    nq_ctx = Cn // tq

    def kern(q_ref, k_ref, v_ref, o_ref, lse_ref, m_s, l_s, acc_s):
        g, qi, kj = pl.program_id(0), pl.program_id(1), pl.program_id(2)

        @pl.when(kj == 0)
        def _():
            m_s[...] = jnp.full_like(m_s, NEG)
            l_s[...] = jnp.zeros_like(l_s)
            acc_s[...] = jnp.zeros_like(acc_s)

        def step(masked):
            k = k_ref[...]
            v = v_ref[...]
            if masked:
                valid = kj * tk + lax.broadcasted_iota(jnp.int32, (1, tk), 1) < Cn
            for h in range(4):
                rows = pl.ds(tq * h, tq)
                s = _dg(q_ref[:, 128 * h:128 * (h + 1)], k, 1, 1)
                if masked:
                    s = jnp.where(valid, s, NEG)
                m_old = m_s[rows, :]
                m_new = jnp.maximum(m_old, jnp.max(s, axis=1, keepdims=True))
                alpha = jnp.exp(m_old - m_new)
                p = jnp.exp(s - m_new)
                l_s[rows, :] = alpha * l_s[rows, :] + jnp.sum(p, axis=1, keepdims=True)
                acc_s[rows, :] = alpha * acc_s[rows, :] + _dg(p, v, 1, 0)
                m_s[rows, :] = m_new

        @pl.when(qi >= nq_ctx)
        def _():
            step(False)

        @pl.when((qi < nq_ctx) & (kj * tk < Cn))
        def _():
            step(True)

        @pl.when(kj == nk - 1)
        def _():
            o4 = acc_s[...] / l_s[...]
            o_ref[...] = _compact_heads([o4[tq * h:tq * (h + 1)] for h in range(4)], g)
            lse = m_s[...] + jnp.log(l_s[...])
            lse_ref[...] = jnp.concatenate([jnp.broadcast_to(lse[tq * h:tq * (h + 1)], (tq, 128)) for h in range(4)], axis=1)

    return pl.pallas_call(
        kern, grid=(2, nq, nk),
        in_specs=[pl.BlockSpec((tq, 512), lambda g, qi, kj: (qi, g)), pl.BlockSpec((tk, 128), _kv_map(Cn, tq, tk, 0)),
                  pl.BlockSpec((tk, 128), _kv_map(Cn, tq, tk, 1))],
        out_specs=[pl.BlockSpec((tq, 256), lambda g, qi, kj: (qi, g)), pl.BlockSpec((tq, 512), lambda g, qi, kj: (qi, g))],
        out_shape=[jax.ShapeDtypeStruct((T, 512), F32), jax.ShapeDtypeStruct((T, 1024), F32)],
        scratch_shapes=[pltpu.VMEM((4 * tq, 1), F32), pltpu.VMEM((4 * tq, 1), F32), pltpu.VMEM((4 * tq, 128), F32)],
        compiler_params=_cp("arbitrary", "arbitrary", "arbitrary"), name="flash_fwd")(Qp, KV, KV)


def flash_bwd(Qp, KV, O, LSE, dOall, Cn, tq, tk):
    T = Qp.shape[0]
    nq, nk = T // tq, T // tk
    nq_ctx = Cn // tq

    def kern(q_ref, k_ref, v_ref, o_ref, lse_ref, do_ref, dq_ref, dkv_ref, dq_s, do_s, dl_s, ls_s):
        g, qi, kj = pl.program_id(0), pl.program_id(1), pl.program_id(2)

        @pl.when((g == 0) & (qi == 0) & (kj == 0))
        def _():
            dkv_ref[...] = jnp.zeros_like(dkv_ref)

        @pl.when(kj == 0)
        def _():
            do4 = jnp.concatenate(_expand_heads(do_ref[...], g), axis=0)
            o4 = jnp.concatenate(_expand_heads(o_ref[...], g), axis=0)
            do_s[...] = do4.astype(BF)
            dl_s[...] = jnp.sum(do4 * o4, axis=1, keepdims=True)
            ls_s[...] = jnp.concatenate([jnp.max(lse_ref[:, 128 * h:128 * (h + 1)], axis=1, keepdims=True) for h in range(4)], axis=0)
            dq_s[...] = jnp.zeros_like(dq_s)

        def step(masked):
            k = k_ref[...]
            v = v_ref[...]
            if masked:
                valid = kj * tk + lax.broadcasted_iota(jnp.int32, (1, tk), 1) < Cn
            dk = dv = None
            for h in range(4):
                rows = pl.ds(tq * h, tq)
                qh = q_ref[:, 128 * h:128 * (h + 1)]
                s = _dg(qh, k, 1, 1)
                if masked:
                    s = jnp.where(valid, s, NEG)
                p = jnp.exp(s - ls_s[rows, :])
                doh = do_s[rows, :]
                dvh = _dg(p, doh, 0, 0)
                dp = _dg(doh, v, 1, 1)
                ds = p * (dp - dl_s[rows, :])
                dq_s[rows, :] += _dg(ds, k, 1, 0)
                dkh = _dg(ds, qh, 0, 0)
                dk = dkh if dk is None else dk + dkh
                dv = dvh if dv is None else dv + dvh
            krows = pl.ds(pl.multiple_of(kj * tk, tk), tk)
            dkv_ref[krows, 0:128] += dk
            dkv_ref[krows, 128:256] += dv

        @pl.when(qi >= nq_ctx)
        def _():
            step(False)

        @pl.when((qi < nq_ctx) & (kj * tk < Cn))
        def _():
            step(True)

        @pl.when(kj == nk - 1)
        def _():
            dq = dq_s[...]
            dq_ref[...] = jnp.concatenate([dq[tq * h:tq * (h + 1)] for h in range(4)], axis=1)

    return pl.pallas_call(
        kern, grid=(2, nq, nk),
        in_specs=[pl.BlockSpec((tq, 512), lambda g, qi, kj: (qi, g)), pl.BlockSpec((tk, 128), _kv_map(Cn, tq, tk, 0)),
                  pl.BlockSpec((tk, 128), _kv_map(Cn, tq, tk, 1)), pl.BlockSpec((tq, 256), lambda g, qi, kj: (qi, g)),
                  pl.BlockSpec((tq, 512), lambda g, qi, kj: (qi, g)), pl.BlockSpec((tq, 256), lambda g, qi, kj: (qi, 4 + g))],
        out_specs=[pl.BlockSpec((tq, 512), lambda g, qi, kj: (qi, g)), pl.BlockSpec((T, 256), lambda g, qi, kj: (0, 0))],
        out_shape=[jax.ShapeDtypeStruct((T, 1024), F32), jax.ShapeDtypeStruct((T, 256), F32)],
        scratch_shapes=[pltpu.VMEM((4 * tq, 128), F32), pltpu.VMEM((4 * tq, 128), BF), pltpu.VMEM((4 * tq, 1), F32), pltpu.VMEM((4 * tq, 1), F32)],
        compiler_params=_cp("arbitrary", "arbitrary", "arbitrary"), name="flash_bwd")(Qp, KV, KV, O, LSE, dOall)


def _pool_band(t0, w0, n, win, gi, Cn, T):
    i = lax.broadcasted_iota(jnp.int32, (n, win), 0)
    jx = lax.broadcasted_iota(jnp.int32, (n, win), 1)
    t = t0 + i
    tp = w0 + jx
    half = lax.shift_left(jnp.int32(1), gi)
    lo = jnp.maximum(t - half, jnp.where(t < Cn, 0, Cn))
    hi = jnp.minimum(t + half - 1, jnp.where(t < Cn, Cn, T) - 1)
    cnt = (hi - lo + 1).astype(F32)
    return jnp.where((tp >= lo) & (tp <= hi), 1.0 / cnt, 0.0) - jnp.where(tp == t, 1.0, 0.0)


def _pool_geom(T):
    n = _pick(T, (256, 128))
    return n, n + 128


def pool_fwd(P, pw, ps, Cn):
    T = P.shape[0]
    n, win = _pool_geom(T)

    def kern(u_ref, w_ref, s_ref, o_ref):
        gi = pl.program_id(0)

        def blk(b, carry):
            t0 = pl.multiple_of(b * n, n)
            w0 = pl.multiple_of(jnp.clip(t0 - 64, 0, T - win), 64)
            d = hdot(_pool_band(t0, w0, n, win, gi, Cn, T), u_ref[pl.ds(w0, win), :])
            o_ref[pl.ds(t0, n), :] = bdot(d, w_ref[...]) * s_ref[...]
            return carry

        lax.fori_loop(0, T // n, blk, 0)

    return pl.pallas_call(
        kern, grid=(4,),
        in_specs=[pl.BlockSpec((T, 128), lambda g: (0, CB["b_in"] + g)), pl.BlockSpec((None, 128, 128), lambda g: (g, 0, 0)),
                  pl.BlockSpec((1, 128), lambda g: (0, g))],
        out_specs=pl.BlockSpec((T, 128), lambda g: (0, g)), out_shape=jax.ShapeDtypeStruct((T, 512), F32),
        compiler_params=_cp("arbitrary"), name="pool_fwd")(P, pw, ps)


def pool_bwd(P, pw, ps, dOall, dP, Cn):
    T = P.shape[0]
    n, win = _pool_geom(T)

    def kern(u_ref, w_ref, s_ref, do_ref, dp_in, du_ref, dw_ref, ds_ref):
        gi = pl.program_id(0)
        du_ref[...] = jnp.zeros_like(du_ref)
        dw_ref[...] = jnp.zeros_like(dw_ref)
        ds_ref[...] = jnp.zeros_like(ds_ref)

        def blk(b, carry):
            t0 = pl.multiple_of(b * n, n)
            w0 = pl.multiple_of(jnp.clip(t0 - 64, 0, T - win), 64)
            band = _pool_band(t0, w0, n, win, gi, Cn, T)
            _, vjp = jax.vjp(lambda uw, w, s: bdot(hdot(band, uw), w) * s, u_ref[pl.ds(w0, win), :], w_ref[...], s_ref[...])
            duw, dw, ds = vjp(do_ref[pl.ds(t0, n), :])
            du_ref[pl.ds(w0, win), :] += duw
            dw_ref[...] += dw
            ds_ref[...] += ds
            return carry

        lax.fori_loop(0, T // n, blk, 0)

    return pl.pallas_call(
        kern, grid=(4,),
        in_specs=[pl.BlockSpec((T, 128), lambda g: (0, CB["b_in"] + g)), pl.BlockSpec((None, 128, 128), lambda g: (g, 0, 0)),
                  pl.BlockSpec((1, 128), lambda g: (0, g)), pl.BlockSpec((T, 128), lambda g: (0, 4 + g)), pl.BlockSpec(memory_space=pl.ANY)],
        out_specs=[pl.BlockSpec((T, 128), lambda g: (0, CB["b_in"] + g)), pl.BlockSpec((None, 128, 128), lambda g: (g, 0, 0)),
                   pl.BlockSpec((1, 128), lambda g: (0, g))],
        out_shape=[jax.ShapeDtypeStruct(dP.shape, F32), jax.ShapeDtypeStruct((4, 128, 128), F32), jax.ShapeDtypeStruct((1, 512), F32)],
        input_output_aliases={4: 0}, compiler_params=_cp("arbitrary"), name="pool_bwd")(P, pw, ps, dOall, dP)


CONV_HALO = 16


def _conv_block(T):
    return _pick(T, (256, 128))


def _stage(dst_ref, src, t0, n, T, Cn):
    h = CONV_HALO
    dst_ref[h:h + n, :] = src(pl.ds(t0, n))
    left_ok = (t0 != 0) & (t0 != Cn)
    right_ok = (t0 + n != Cn) & (t0 + n != T)
    lo = pl.multiple_of(jnp.maximum(t0 - h, 0), 8)
    hi = pl.multiple_of(jnp.minimum(t0 + n, T - h), 8)
    dst_ref[0:h, :] = jnp.where(left_ok, src(pl.ds(lo, h)), 0.0)
    dst_ref[h + n:2 * h + n, :] = jnp.where(right_ok, src(pl.ds(hi, h)), 0.0)


def conv_fwd(P, cw, cb, Cn):
    T = P.shape[0]
    n = _conv_block(T)
    off = CONV_HALO - CONV_WIDTH // 2

    def kern(a_ref, g_ref, w_ref, b_ref, y_ref, us):
        def blk(b, carry):
            t0 = pl.multiple_of(b * n, n)
            _stage(us, lambda r: a_ref[r, :] * jax.nn.sigmoid(g_ref[r, :]), t0, n, T, Cn)
            acc = jnp.zeros((n, 128), F32)
            for k in range(CONV_WIDTH):
                acc = acc + us[k + off:k + off + n, :] * w_ref[k:k + 1, :]
            y_ref[pl.ds(t0, n), :] = acc + b_ref[...]
            return carry

        lax.fori_loop(0, T // n, blk, 0)

    slab = lambda o: pl.BlockSpec((T, 128), lambda c: (0, o + c))
    return pl.pallas_call(
        kern, grid=(4,),
        in_specs=[slab(CB["d_glu"]), slab(CB["d_glu"] + 4), pl.BlockSpec((CONV_PAD, 128), lambda c: (0, c)), pl.BlockSpec((1, 128), lambda c: (0, c))],
        out_specs=pl.BlockSpec((T, 128), lambda c: (0, c)), out_shape=jax.ShapeDtypeStruct((T, 512), F32),
        scratch_shapes=[pltpu.VMEM((n + 2 * CONV_HALO, 128), F32)],
        compiler_params=_cp("arbitrary"), name="conv_fwd")(P, P, cw, cb)


def conv_bwd(P, cw, dY, dP, Cn):
    T = P.shape[0]
    n = _conv_block(T)
    off = CONV_HALO - CONV_WIDTH // 2
    back = CONV_HALO + CONV_WIDTH // 2

    def kern(a_ref, g_ref, w_ref, dy_ref, dp_in, da_ref, dg_ref, dw_ref, db_ref, us, dys):
        dw_ref[...] = jnp.zeros_like(dw_ref)
        db_ref[...] = jnp.zeros_like(db_ref)

        def blk(b, carry):
            t0 = pl.multiple_of(b * n, n)
            cur = pl.ds(t0, n)
            _stage(us, lambda r: a_ref[r, :] * jax.nn.sigmoid(g_ref[r, :]), t0, n, T, Cn)
            _stage(dys, lambda r: dy_ref[r, :], t0, n, T, Cn)
            dyc = dy_ref[cur, :]
            du = jnp.zeros((n, 128), F32)
            for k in range(CONV_WIDTH):
                du = du + dys[back - k:back - k + n, :] * w_ref[k:k + 1, :]
                dw_ref[k:k + 1, :] += jnp.sum(us[k + off:k + off + n, :] * dyc, axis=0, keepdims=True)
            a = a_ref[cur, :]
            sig = jax.nn.sigmoid(g_ref[cur, :])
            da_ref[cur, :] = du * sig
            dg_ref[cur, :] = du * a * sig * (1.0 - sig)
            db_ref[...] += jnp.sum(dyc, axis=0, keepdims=True)
            return carry

        lax.fori_loop(0, T // n, blk, 0)

    slab = lambda o: pl.BlockSpec((T, 128), lambda c: (0, o + c))
    return pl.pallas_call(
        kern, grid=(4,),
        in_specs=[slab(CB["d_glu"]), slab(CB["d_glu"] + 4), pl.BlockSpec((CONV_PAD, 128), lambda c: (0, c)), slab(0),
                  pl.BlockSpec(memory_space=pl.ANY)],
        out_specs=[slab(CB["d_glu"]), slab(0), pl.BlockSpec((CONV_PAD, 128), lambda c: (0, c)), pl.BlockSpec((1, 128), lambda c: (0, c))],
        out_shape=[jax.ShapeDtypeStruct(dP.shape, F32), jax.ShapeDtypeStruct((T, 512), F32), jax.ShapeDtypeStruct((CONV_PAD, 512), F32),
                   jax.ShapeDtypeStruct((1, 512), F32)],
        scratch_shapes=[pltpu.VMEM((n + 2 * CONV_HALO, 128), F32), pltpu.VMEM((n + 2 * CONV_HALO, 128), F32)],
        input_output_aliases={4: 0}, compiler_params=_cp("arbitrary"), name="conv_bwd")(P, P, cw, dY, dP)


def copy_cols(name, src, dP, col0, Cn, tm):
    T = dP.shape[0]

    def body(i, j, r, c, s):
        return [r[0]], [], []

    return rowcall(name, body, T, tm, src.shape[1] // 128, Cn, [(src, 128, lambda j: j)], [], [], [("into", dP, 128, lambda j: col0 + j)])[0]


def ada_fwd(cvec, w, b):
    L, D, wc = w.shape

    def kern(c_ref, w_ref, b_ref, o_ref):
        o_ref[...] = _dg(_silu(c_ref[...]), w_ref[...], 1, 0) + b_ref[...]

    return pl.pallas_call(
        kern, grid=(L,),
        in_specs=[pl.BlockSpec((16, D), lambda l: (0, 0)), pl.BlockSpec((None, D, wc), lambda l: (l, 0, 0)), pl.BlockSpec((None, 1, wc), lambda l: (l, 0, 0))],
        out_specs=pl.BlockSpec((None, 16, wc), lambda l: (l, 0, 0)), out_shape=jax.ShapeDtypeStruct((L, 16, wc), F32),
        compiler_params=_cp("arbitrary"), name="ada_fwd")(cvec, w, b)


def ada_bwd(cvec, w, dm):
    L, D, wc = w.shape

    def kern(c_ref, w_ref, d_ref, gw_ref, ds_ref):
        l = pl.program_id(0)
        d = d_ref[...]
        gw_ref[...] = _dg(_silu(c_ref[...]), d, 0, 0)
        part = _dg(d, w_ref[...], 1, 1)

        @pl.when(l == 0)
        def _():
            ds_ref[...] = part

        @pl.when(l > 0)
        def _():
            ds_ref[...] += part

    return pl.pallas_call(
        kern, grid=(L,),
        in_specs=[pl.BlockSpec((16, D), lambda l: (0, 0)), pl.BlockSpec((None, D, wc), lambda l: (l, 0, 0)), pl.BlockSpec((None, 16, wc), lambda l: (l, 0, 0))],
        out_specs=[pl.BlockSpec((None, D, wc), lambda l: (l, 0, 0)), pl.BlockSpec((16, D), lambda l: (0, 0))],
        out_shape=[jax.ShapeDtypeStruct((L, D, wc), F32), jax.ShapeDtypeStruct((16, D), F32)],
        compiler_params=_cp("arbitrary"), name="ada_bwd")(cvec, w, dm)


def silu_grad(cc, parts):
    D = cc.shape[1]

    def kern(c_ref, p_ref, o_ref):
        tot = p_ref[0]
        for k in range(1, N_DEV):
            tot = tot + p_ref[k]
        _, vjp = jax.vjp(_silu, c_ref[...])
        o_ref[...] = vjp(tot)[0]

    return pl.pallas_call(kern, out_shape=jax.ShapeDtypeStruct((1, D), F32), name="silu_grad")(cc, parts)


def _flip(v, f):
    return 1 - v if f else v


_REL = ((0, 0), (1, 0), (0, 1), (1, 1))


def allgather(name, x):
    def body(x_ref, out_ref, send_sems, recv_sems, local_sem):
        mx, my, mc = lax.axis_index("x"), lax.axis_index("y"), lax.axis_index("c")
        me, sibling = (mx, my, mc), (mx, my, 1 - mc)
        chips = [(_flip(mx, fx), _flip(my, fy)) for fx, fy in _REL[1:]]

        def slot(px, py, pc):
            return out_ref.at[4 * px + 2 * py + pc]

        def copy(k, block, to, src=None):
            return pltpu.make_async_remote_copy(
                src_ref=slot(*block) if src is None else src, dst_ref=slot(*block),
                send_sem=send_sems.at[k], recv_sem=recv_sems.at[k], device_id=to, device_id_type=MESH)

        mine = pltpu.make_async_copy(x_ref, slot(*me), local_sem)
        mine.start()
        first = [copy(0, me, sibling, src=x_ref)] + [copy(1 + j, me, (*chip, mc), src=x_ref) for j, chip in enumerate(chips)]
        for cp in first:
            cp.start()
        passed = [copy(4 + j, (*chip, mc), sibling) for j, chip in enumerate(chips)]
        for j, chip in enumerate(chips):
            copy(1 + j, (*chip, mc), me).wait_recv()
            passed[j].start()
        copy(0, sibling, me).wait_recv()
        for j, chip in enumerate(chips):
            copy(4 + j, (*chip, 1 - mc), me).wait_recv()
        for cp in first + passed:
            cp.wait_send()
        mine.wait()

    return pl.pallas_call(
        body, out_shape=jax.ShapeDtypeStruct((N_DEV,) + x.shape, x.dtype),
        in_specs=[pl.BlockSpec(memory_space=pl.ANY)], out_specs=pl.BlockSpec(memory_space=pl.ANY),
        scratch_shapes=[pltpu.SemaphoreType.DMA((7,)), pltpu.SemaphoreType.DMA((7,)), pltpu.SemaphoreType.DMA(())], name=name)(x)


def cols_from_devs(name, xg, l, nrow):
    _, _, C = xg.shape
    tr = _row_tile(nrow, N_DEV * C, xg.dtype.itemsize, 16, 8 << 20)
    nblk = nrow // tr

    def body(x_ref, o_ref):
        for d in range(N_DEV):
            o_ref[:, d * C:(d + 1) * C] = x_ref[d]

    return pl.pallas_call(
        body, grid=(nblk,), in_specs=[pl.BlockSpec((N_DEV, tr, C), lambda i: (0, l * nblk + i, 0))],
        out_specs=pl.BlockSpec((tr, N_DEV * C), lambda i: (i, 0)), out_shape=jax.ShapeDtypeStruct((nrow, N_DEV * C), xg.dtype),
        compiler_params=_cp("parallel"), name=name)(xg)


def cols_to_devs(name, g, l, L, prev):
    nrow, W = g.shape
    C = W // N_DEV
    tr = _row_tile(nrow, W, 4, 8, 8 << 20)
    nblk = nrow // tr

    def body(x_ref, *rest):
        o_ref = rest[-1]
        for d in range(N_DEV):
            o_ref[d] = x_ref[:, d * C:(d + 1) * C]

    in_specs, args, aliases = [pl.BlockSpec((tr, W), lambda i: (i, 0))], [g], {}
    if prev is not None:
        in_specs.append(pl.BlockSpec(memory_space=pl.ANY))
        args.append(prev)
        aliases = {1: 0}
    return pl.pallas_call(
        body, grid=(nblk,), in_specs=in_specs, out_specs=pl.BlockSpec((N_DEV, tr, C), lambda i: (0, l * nblk + i, 0)),
        out_shape=jax.ShapeDtypeStruct((N_DEV, L * nrow, C), F32), input_output_aliases=aliases,
        compiler_params=_cp("parallel"), name=name)(*args)


def rs_exchange_sibling(tag, buf):
    _, R, C = buf.shape

    def body(buf_ref, out_ref, send_sems, recv_sems):
        mx, my, mc = lax.axis_index("x"), lax.axis_index("y"), lax.axis_index("c")
        sibling = (mx, my, 1 - mc)
        cps = []
        for j, (fx, fy) in enumerate(_REL):
            d = 4 * _flip(mx, fx) + 2 * _flip(my, fy) + (1 - mc)
            cps.append(pltpu.make_async_remote_copy(src_ref=buf_ref.at[d], dst_ref=out_ref.at[j], send_sem=send_sems.at[j],
                                                    recv_sem=recv_sems.at[j], device_id=sibling, device_id_type=MESH))
        for cp in cps:
            cp.start()
        for cp in cps:
            cp.wait_recv()
        for cp in cps:
            cp.wait_send()

    return pl.pallas_call(
        body, out_shape=jax.ShapeDtypeStruct((4, R, C), buf.dtype),
        in_specs=[pl.BlockSpec(memory_space=pl.ANY)], out_specs=pl.BlockSpec(memory_space=pl.ANY),
        scratch_shapes=[pltpu.SemaphoreType.DMA((4,)), pltpu.SemaphoreType.DMA((4,))], name="rs_sibling_" + tag)(buf)


def rs_chip_sum(tag, buf, recv, idx, tr):
    _, R, C = buf.shape

    def kern(idx_ref, b_ref, r_ref, own_ref, sb_ref):
        j = pl.program_id(1)
        s = b_ref[...] + r_ref[...]
        sb_ref[...] = s.astype(BF)

        @pl.when(j == 0)
        def _():
            own_ref[...] = s

    return pl.pallas_call(
        kern, grid_spec=pltpu.PrefetchScalarGridSpec(
            num_scalar_prefetch=1, grid=(R // tr, 4),
            in_specs=[pl.BlockSpec((None, tr, C), lambda r, j, idx: (idx[j], r, 0)), pl.BlockSpec((None, tr, C), lambda r, j, idx: (j, r, 0))],
            out_specs=[pl.BlockSpec((tr, C), lambda r, j, idx: (r, 0)), pl.BlockSpec((None, tr, C), lambda r, j, idx: (j, r, 0))]),
        out_shape=[jax.ShapeDtypeStruct((R, C), F32), jax.ShapeDtypeStruct((4, R, C), BF)],
        compiler_params=_cp("arbitrary", "arbitrary"), name="rs_chip_sum_" + tag)(idx, buf, recv)


def rs_exchange_chips(tag, sb):
    _, R, C = sb.shape

    def body(sb_ref, out_ref, send_sems, recv_sems):
        mx, my, mc = lax.axis_index("x"), lax.axis_index("y"), lax.axis_index("c")
        cps = []
        for j, (fx, fy) in enumerate(_REL):
            if j == 0:
                continue
            cps.append(pltpu.make_async_remote_copy(src_ref=sb_ref.at[j], dst_ref=out_ref.at[j], send_sem=send_sems.at[j - 1],
                                                    recv_sem=recv_sems.at[j - 1], device_id=(_flip(mx, fx), _flip(my, fy), mc),
                                                    device_id_type=MESH))
        for cp in cps:
            cp.start()
        for cp in cps:
            cp.wait_recv()
        for cp in cps:
            cp.wait_send()

    return pl.pallas_call(
        body, out_shape=jax.ShapeDtypeStruct((4, R, C), sb.dtype),
        in_specs=[pl.BlockSpec(memory_space=pl.ANY)], out_specs=pl.BlockSpec(memory_space=pl.ANY),
        scratch_shapes=[pltpu.SemaphoreType.DMA((3,)), pltpu.SemaphoreType.DMA((3,))], name="rs_chips_" + tag)(sb)


def adamw(name, parts, w, m, v, tr):
    R, C = w.shape
    in_specs, args = [], []
    for arr, lead in parts:
        if lead is None:
            in_specs.append(pl.BlockSpec((tr, C), lambda r: (r, 0)))
        else:
            in_specs.append(pl.BlockSpec((None, tr, C), lambda r, k=lead: (k, r, 0)))
        args.append(arr)
    npart = len(parts)
    blk = pl.BlockSpec((tr, C), lambda r: (r, 0))

    def kern(*refs):
        g = refs[0][...].astype(F32)
        for r in refs[1:npart]:
            g = g + r[...].astype(F32)
        w_ref, m_ref, v_ref, g_out, d_out, m_out, v_out = refs[npart:]
        mn = ADAM_B1 * m_ref[...] + (1.0 - ADAM_B1) * g
        vn = ADAM_B2 * v_ref[...] + (1.0 - ADAM_B2) * jnp.square(g)
        m_hat = mn / (1.0 - ADAM_B1 ** ADAM_STEP)
        v_hat = vn / (1.0 - ADAM_B2 ** ADAM_STEP)
        g_out[...] = g
        d_out[...] = -ADAM_LR * (m_hat / (jnp.sqrt(v_hat) + ADAM_EPS) + ADAM_WD * w_ref[...])
        m_out[...] = mn
        v_out[...] = vn

    return pl.pallas_call(
        kern, grid=(R // tr,), in_specs=in_specs + [blk, blk, blk], out_specs=[blk] * 4,
        out_shape=[jax.ShapeDtypeStruct((R, C), F32)] * 4, compiler_params=_cp("parallel"), name=name)(*args, w, m, v)


def _pack(arrs):
    flat = [a.reshape(-1) for a in arrs]
    n = sum(f.shape[0] for f in flat)
    pad = (-n) % (8 * LANES)
    if pad:
        flat.append(jnp.zeros((pad,), flat[0].dtype))
    return jnp.concatenate(flat).reshape(-1, LANES)


def _unpack(packed, shapes):
    flat = packed.reshape(-1)
    out, off = [], 0
    for s in shapes:
        n = int(np.prod(s))
        out.append(flat[off:off + n].reshape(s))
        off += n
    return out


def _row_tile(R, C=LANES, itemsize=4, mult=16, target=1 << 20):
    best = None
    for t in range(mult, R + 1, mult):
        if R % t == 0 and t * C * itemsize <= target:
            best = t
    return best or R


BIG = ("w_in", "w_branch", "w_out", "conv_pw", "conv_w")
SMALL = ("g_pre", "g_post", "na_rpb", "pool_w", "pool_scale", "q_norm", "k_norm", "conv_b", "conv_ln_g", "conv_ln_b")
WEIGHTS = ['c_ctx', 'w_ada', 'b_ada', 'g_pre', 'g_post', 'w_in', 'na_rpb', 'pool_w', 'pool_scale', 'q_norm', 'k_norm', 'conv_w', 'conv_b',
           'conv_ln_g', 'conv_ln_b', 'conv_pw', 'w_branch', 'w_out']


def _rope_tables(S, Cn):
    t = np.arange(S)
    pos = np.stack([t // GRID_W, t % GRID_W], 1).astype(np.float64)
    lane = np.arange(128)
    within = lane % 64
    axis = within // 32
    f = (within % 32) % 16
    freqs = ROPE_THETA ** (-np.arange(16, dtype=np.float32) / 16)
    ang = pos[:, axis].astype(np.float32) * freqs[f][None, :]
    cos = np.cos(ang).astype(np.float32)
    sin = np.sin(ang).astype(np.float32) * np.where((within % 32) < 16, -1.0, 1.0).astype(np.float32)[None, :]
    cos = np.concatenate([np.ones((Cn, 128), np.float32), cos])
    sin = np.concatenate([np.zeros((Cn, 128), np.float32), sin])
    return jnp.asarray(cos), jnp.asarray(sin)


def _own_2d(w_in, w_branch, w_out, conv_pw, conv_w):
    L = w_in.shape[0]
    cwp = jnp.zeros((L, CONV_PAD, conv_w.shape[2]), conv_w.dtype).at[:, :CONV_WIDTH].set(conv_w)
    return dict(w_in=w_in.reshape(-1, w_in.shape[-1]), w_branch=w_branch.reshape(-1, w_branch.shape[-1]),
                w_out=w_out.reshape(-1, w_out.shape[-1]), conv=_pack([conv_pw, cwp]))


def kernel(x, c, ctx, c_ctx, w_ada, b_ada, g_pre, g_post, w_in, na_rpb, pool_w, pool_scale, q_norm, k_norm, conv_w, conv_b, conv_ln_g, conv_ln_b, conv_pw, w_branch, w_out, loss_target, m_c_ctx, m_w_ada, m_b_ada, m_g_pre, m_g_post, m_w_in, m_na_rpb, m_pool_w, m_pool_scale, m_q_norm, m_k_norm, m_conv_w, m_conv_b, m_conv_ln_g, m_conv_ln_b, m_conv_pw, m_w_branch, m_w_out, v_c_ctx, v_w_ada, v_b_ada, v_g_pre, v_g_post, v_w_in, v_na_rpb, v_pool_w, v_pool_scale, v_q_norm, v_k_norm, v_conv_w, v_conv_b, v_conv_ln_g, v_conv_ln_b, v_conv_pw, v_w_branch, v_w_out):
    W = dict(c_ctx=c_ctx, w_ada=w_ada, b_ada=b_ada, g_pre=g_pre, g_post=g_post, w_in=w_in, na_rpb=na_rpb, pool_w=pool_w, pool_scale=pool_scale,
             q_norm=q_norm, k_norm=k_norm, conv_w=conv_w, conv_b=conv_b, conv_ln_g=conv_ln_g, conv_ln_b=conv_ln_b, conv_pw=conv_pw,
             w_branch=w_branch, w_out=w_out)
    Mo = dict(c_ctx=m_c_ctx, w_ada=m_w_ada, b_ada=m_b_ada, g_pre=m_g_pre, g_post=m_g_post, w_in=m_w_in, na_rpb=m_na_rpb, pool_w=m_pool_w,
              pool_scale=m_pool_scale, q_norm=m_q_norm, k_norm=m_k_norm, conv_w=m_conv_w, conv_b=m_conv_b, conv_ln_g=m_conv_ln_g,
              conv_ln_b=m_conv_ln_b, conv_pw=m_conv_pw, w_branch=m_w_branch, w_out=m_w_out)
    Vo = dict(c_ctx=v_c_ctx, w_ada=v_w_ada, b_ada=v_b_ada, g_pre=v_g_pre, g_post=v_g_post, w_in=v_w_in, na_rpb=v_na_rpb, pool_w=v_pool_w,
              pool_scale=v_pool_scale, q_norm=v_q_norm, k_norm=v_k_norm, conv_w=v_conv_w, conv_b=v_conv_b, conv_ln_g=v_conv_ln_g,
              conv_ln_b=v_conv_ln_b, conv_pw=v_conv_pw, w_branch=v_w_branch, w_out=v_w_out)

    S, D = x.shape[1], x.shape[2]
    Cn = ctx.shape[1]
    T = Cn + S
    L = w_in.shape[0]
    IN = w_in.shape[2] * N_DEV
    mx, my, mc = lax.axis_index("x"), lax.axis_index("y"), lax.axis_index("c")
    me = 4 * mx + 2 * my + mc
    tm = _pick(Cn, (256, 128))
    tme = _pick(T, (768, 256, 128))
    tmm = _pick(T, (768, 256, 128))
    tq = _pick(Cn, (256, 128))
    tk = _pick(T, (4224, 768, 384, 128))

    cg = allgather("gather_c", c)
    cvec = jnp.zeros((16, D), F32).at[0].set(c_ctx).at[1:1 + N_DEV].set(cg[:, 0])
    wc = w_ada.shape[2]
    b_sh = lax.dynamic_slice_in_dim(b_ada, me * wc, wc, axis=1)[:, None, :]
    modp = ada_fwd(cvec, w_ada, b_sh)
    modg = allgather("gather_mod", modp.reshape(L * 16, wc)).reshape(N_DEV, L, 16, wc)
    mod_full = jnp.transpose(modg, (1, 2, 0, 3)).reshape(L, 16, N_DEV * wc)
    mod2 = jnp.stack([mod_full[:, 0], lax.dynamic_index_in_dim(mod_full, 1 + me, axis=1, keepdims=False)], axis=1)
    shift, scale, gate = [mod2[:, :, None, k * D:(k + 1) * D] for k in range(3)]

    own2 = _own_2d(w_in, w_branch, w_out, conv_pw, conv_w)
    win_g = allgather("gather_w_in", own2["w_in"].astype(BF))
    wbr_g = allgather("gather_w_branch", own2["w_branch"].astype(BF))
    wout_g = allgather("gather_w_out", own2["w_out"].astype(BF))
    conv_g = allgather("gather_conv", own2["conv"].astype(BF)).reshape(N_DEV, -1)
    npw = L * (BRANCH_W // N_DEV) * BRANCH_W
    cpw_g = conv_g[:, :npw].reshape(N_DEV, L, BRANCH_W // N_DEV, BRANCH_W)
    cw_g = conv_g[:, npw:npw + L * CONV_PAD * (BRANCH_W // N_DEV)].reshape(N_DEV, L, CONV_PAD, BRANCH_W // N_DEV)
    Win = [cols_from_devs("w_in_cols", win_g, l, D) for l in range(L)]
    Wb = [cols_from_devs("w_branch_cols", wbr_g, l, 4 * BRANCH_W).reshape(4, BRANCH_W, D) for l in range(L)]
    Wo = [wout_g.reshape(N_DEV, L, D // N_DEV, D)[:, l].reshape(D, D) for l in range(L)]
    Cpw = [cpw_g[:, l].reshape(BRANCH_W, BRANCH_W).astype(F32) for l in range(L)]
    Cw = [jnp.transpose(cw_g[:, l], (1, 0, 2)).reshape(CONV_PAD, BRANCH_W).astype(F32) for l in range(L)]

    cos, sin = _rope_tables(S, Cn)
    tb = na_tables(S, Cn)
    bd = jnp.asarray(np.kron(np.eye(2, dtype=np.float32), np.full((64, 64), 1.0 / 64, np.float32)))
    row2 = lambda a: a.reshape(1, -1)

    X = jnp.concatenate([ctx[0], x[0]], axis=0)
    saved = []
    for l in range(L):
        gp = row2(g_pre[l])
        H = modulate_fwd(X, gp, shift[l], scale[l], Cn, tm)
        P = matmul("proj_in", H, Win[l], out_dtype=F32, tm=tmm, tn=_pick(IN, (1280, 1152, 768, 384, 128)), tk=D)
        bt = bias_table(na_rpb[l], tb)
        oa = na_fwd(P, bt, tb, Cn)
        ob = pool_fwd(P, pool_w[l], row2(pool_scale[l]), Cn)
        qn = row2(jnp.tile(q_norm[l], 2))
        kn = row2(jnp.tile(k_norm[l], 2))
        Qp = prep_q_fwd(P, cos, sin, qn, bd, Cn, tm)
        KV = prep_kv_fwd(P, cos, sin, kn, bd, Cn, tm)
        oc, LSE = flash_fwd(Qp, KV, Cn, tq, tk)
        Yc = conv_fwd(P, Cw[l], row2(conv_b[l]), Cn)
        od = conv_post_fwd(Yc, row2(conv_ln_g[l]), row2(conv_ln_b[l]), Cpw[l], Cn, tm)
        G = gate_outs_fwd(P, oa, ob, oc, od, Cn, tme)
        U = branch_fwd(G, Wb[l], tmm)
        Y = merge_fwd(P, U, Cn, tme)
        Z = matmul("proj_out", Y, Wo[l], out_dtype=F32, tm=tmm, tn=D, tk=D)
        Xn = post_fwd(X, Z, row2(g_post[l]), gate[l], Cn, tm)
        saved.append(dict(X=X, H=H, P=P, bt=bt, oa=oa, ob=ob, oc=oc, od=od, Qp=Qp, KV=KV, LSE=LSE, Yc=Yc, G=G, U=U, Y=Y, Z=Z, qn=qn, kn=kn))
        X = Xn

    tgt = jnp.concatenate([jnp.zeros((Cn, D), F32), loss_target[0]], axis=0)
    dX, loss_acc = loss_call(X, tgt, Cn, tm)
    loss = lax.psum(loss_acc[0, 0], ("x", "y", "c"))

    gsm = {n: [None] * L for n in SMALL}
    gbig = {n: [None] * L for n in BIG}
    g_in = g_br = None
    dmod = [None] * L
    for l in reversed(range(L)):
        sv = saved[l]
        P = sv["P"]
        dZ, dgp, dgate = post_bwd(sv["Z"], dX, row2(g_post[l]), gate[l], Cn, tm)
        gsm["g_post"][l] = dgp[0]
        dY = matmul("proj_out_dy", dZ, Wo[l], cb=1, out_dtype=F32, tm=tmm, tn=D, tk=D)
        gbig["w_out"][l] = matmul("proj_out_dw", sv["Y"], dZ, ca=0, cb=0, tm=_pick(D, (1024, 512, 256)), tn=D, tk=tmm)
        dU, dP = merge_bwd(P, sv["U"], dY, Cn, tme)
        dG = branch_dg(dU, Wb[l], tmm)
        g_br = cols_to_devs("w_branch_devs", branch_dw(sv["G"], dU, tmm).reshape(4 * BRANCH_W, D), l, L, g_br)
        dO, dP = gate_outs_bwd(P, sv["oa"], sv["ob"], sv["oc"], sv["od"], dG, dP, Cn, tme)
        dYc, dlg, dlb, dcpw = conv_post_bwd(sv["Yc"], dO, row2(conv_ln_g[l]), row2(conv_ln_b[l]), Cpw[l], Cn, tm)
        gsm["conv_ln_g"][l], gsm["conv_ln_b"][l], gbig["conv_pw"][l] = dlg[0], dlb[0], dcpw
        dP, dGg, dcw, dcb = conv_bwd(P, Cw[l], dYc, dP, Cn)
        dP = copy_cols("copy_glu_gate", dGg, dP, CB["d_glu"] + 4, Cn, tme)
        gbig["conv_w"][l], gsm["conv_b"][l] = dcw, dcb[0]
        dQp, dKV = flash_bwd(sv["Qp"], sv["KV"], sv["oc"], sv["LSE"], dO, Cn, tq, tk)
        dP, dqn = prep_q_bwd(P, dQp, dP, cos, sin, sv["qn"], bd, Cn, tm)
        dP, dkn = prep_kv_bwd(P, dKV, dP, cos, sin, sv["kn"], bd, Cn, tm)
        gsm["q_norm"][l] = dqn[0, :64] + dqn[0, 64:]
        gsm["k_norm"][l] = dkn[0, :64] + dkn[0, 64:]
        dP, dpw, dps = pool_bwd(P, pool_w[l], row2(pool_scale[l]), dO, dP, Cn)
        gsm["pool_w"][l], gsm["pool_scale"][l] = dpw, dps[0]
        dQa, dKa, dVa, dbt = na_bwd(P, sv["bt"], dO, tb, Cn)
        dP = assemble_na(dKa, dVa, dQa, dP, Cn, tme)
        gsm["na_rpb"][l] = bias_table_grad(dbt, tb)
        dH = matmul("proj_in_dh", dP, Win[l], cb=1, out_dtype=F32, tm=tmm, tn=D, tk=_pick(IN, (1280, 1152, 768, 384, 128)))
        dWin = matmul("proj_in_dw", sv["H"], dP, ca=0, cb=0, tm=_pick(D, (1024, 512, 256)), tn=_pick(IN, (1280, 1152, 768, 384, 128)),
                      tk=_pick(T, (1408, 768, 384, 128)))
        g_in = cols_to_devs("w_in_devs", dWin, l, L, g_in)
        dX, dgpre, dsh, dsc = modulate_bwd(sv["X"], dH, dX, row2(g_pre[l]), shift[l], scale[l], Cn, tm)
        gsm["g_pre"][l] = dgpre[0]
        dmod[l] = jnp.concatenate([dsh[:, 0], dsc[:, 0], dgate[:, 0]], axis=1)
    grad_x = dX[Cn:][None]

    idx = jnp.stack([4 * _flip(mx, fx) + 2 * _flip(my, fy) + mc for fx, fy in _REL]).astype(jnp.int32)
    g_out = jnp.transpose(jnp.stack(gbig["w_out"]).reshape(L, N_DEV, D // N_DEV, D), (1, 0, 2, 3)).reshape(N_DEV, L * (D // N_DEV), D)
    g_pw = jnp.transpose(jnp.stack(gbig["conv_pw"]).reshape(L, N_DEV, -1), (1, 0, 2)).reshape(N_DEV, -1)
    g_cw = jnp.transpose(jnp.stack(gbig["conv_w"]).reshape(L, CONV_PAD, N_DEV, BRANCH_W // N_DEV), (2, 0, 1, 3)).reshape(N_DEV, -1)
    g_conv = jnp.concatenate([g_pw, g_cw], axis=1)
    g_conv = jnp.pad(g_conv, ((0, 0), (0, own2["conv"].size - g_conv.shape[1]))).reshape(N_DEV, -1, LANES)
    gbufs = dict(w_in=g_in, w_branch=g_br, w_out=g_out, conv=g_conv)
    mom2 = _own_2d(Mo["w_in"], Mo["w_branch"], Mo["w_out"], Mo["conv_pw"], Mo["conv_w"])
    var2 = _own_2d(Vo["w_in"], Vo["w_branch"], Vo["w_out"], Vo["conv_pw"], Vo["conv_w"])
    big2 = {}
    for tag in ("w_in", "w_branch", "w_out", "conv"):
        R2, C2 = own2[tag].shape
        tr2 = _row_tile(R2, C2)
        recv1 = rs_exchange_sibling(tag, gbufs[tag])
        own, sb = rs_chip_sum(tag, gbufs[tag], recv1, idx, tr2)
        recv2 = rs_exchange_chips(tag, sb)
        big2[tag] = adamw("adamw_" + tag, [(own, None), (recv2, 1), (recv2, 2), (recv2, 3)], own2[tag], mom2[tag], var2[tag], tr2)
    cshapes = [(L, BRANCH_W // N_DEV, BRANCH_W), (L, CONV_PAD, BRANCH_W // N_DEV)]
    conv_unp = [_unpack(r, cshapes) for r in big2["conv"]]
    big_out = dict(w_in=[r.reshape(w_in.shape) for r in big2["w_in"]], w_branch=[r.reshape(w_branch.shape) for r in big2["w_branch"]],
                   w_out=[r.reshape(w_out.shape) for r in big2["w_out"]], conv_pw=[u[0] for u in conv_unp],
                   conv_w=[u[1][:, :CONV_WIDTH] for u in conv_unp])

    small_own = [jnp.stack(gsm[n]) for n in SMALL]
    small_shapes = [a.shape for a in small_own]
    dmod_all = jnp.stack(dmod)
    spack = _pack(small_own + [dmod_all])
    sg = allgather("gather_small", spack)
    Rs = spack.shape[0]
    nsmall = sum(int(np.prod(s)) for s in small_shapes)
    dmod_g = sg.reshape(N_DEV, -1)[:, nsmall:nsmall + dmod_all.size].reshape(N_DEV, L, 2, N_DEV, wc)
    dm_ctx = dmod_g[0, :, 0, :, :]
    for k in range(1, N_DEV):
        dm_ctx = dm_ctx + dmod_g[k, :, 0, :, :]
    dm_ctx = lax.dynamic_index_in_dim(dm_ctx, me, axis=1, keepdims=False)
    dm_b = jnp.transpose(lax.dynamic_index_in_dim(dmod_g[:, :, 1], me, axis=2, keepdims=False), (1, 0, 2))
    dmp = jnp.zeros((L, 16, wc), F32).at[:, 0].set(dm_ctx).at[:, 1:1 + N_DEV].set(dm_b)
    g_wada, dsilu = ada_bwd(cvec, w_ada, dmp)
    cpart = allgather("gather_cctx", dsilu[0:1])
    g_cctx = silu_grad(c_ctx[None], cpart)[0]

    smallw = _pack([W[n] for n in SMALL])
    smallm = _pack([Mo[n] for n in SMALL])
    smallv = _pack([Vo[n] for n in SMALL])
    Rsm = smallw.shape[0]
    small_res = adamw("adamw_small", [(sg[:, :Rsm], k) for k in range(N_DEV)], smallw, smallm, smallv, _row_tile(Rsm))
    small_unp = [_unpack(r, small_shapes) for r in small_res]

    db_parts = dmod_g.reshape(N_DEV, L, 2, N_DEV * wc)
    bshape = (L * N_DEV * wc // LANES, LANES)
    bparts = [(db_parts[k, :, r].reshape(bshape), None) for k in range(N_DEV) for r in range(2)]
    bada_res = adamw("adamw_bada", bparts, b_ada.reshape(bshape), Mo["b_ada"].reshape(bshape), Vo["b_ada"].reshape(bshape), _row_tile(bshape[0]))
    wada_shape = (w_ada.size // LANES, LANES)
    wada_res = adamw("adamw_wada", [(g_wada.reshape(wada_shape), None)], w_ada.reshape(wada_shape), Mo["w_ada"].reshape(wada_shape),
                     Vo["w_ada"].reshape(wada_shape), _row_tile(wada_shape[0]))
    cshape = (D // LANES, LANES)
    cctx_res = adamw("adamw_cctx", [(g_cctx.reshape(cshape), None)], c_ctx.reshape(cshape), Mo["c_ctx"].reshape(cshape), Vo["c_ctx"].reshape(cshape),
                     _row_tile(cshape[0]) if cshape[0] % 8 == 0 else cshape[0])

    res = {}
    for n in BIG:
        res[n] = big_out[n]
    for k, n in enumerate(SMALL):
        res[n] = [u[k] for u in small_unp]
    res["b_ada"] = [r.reshape(b_ada.shape) for r in bada_res]
    res["w_ada"] = [r.reshape(w_ada.shape) for r in wada_res]
    res["c_ctx"] = [r.reshape(c_ctx.shape) for r in cctx_res]
    outs = [loss, grad_x]
    for k in range(4):
        outs += [res[n][k] for n in WEIGHTS]
    return tuple(outs)
```

```python
import functools

import numpy as np
import jax
import jax.numpy as jnp
from jax import lax
from jax.experimental import pallas as pl
from jax.experimental.pallas import tpu as pltpu

F32 = jnp.float32
BF = jnp.bfloat16
HI = lax.Precision.HIGHEST

GRID_W = 64
BRANCH_W = 512
HEAD_DIM = 64
NA_WIN_ROWS = 8
NA_WIN_COLS = 16
NA_TILE_ROWS = 2
NA_BAND = NA_WIN_ROWS + NA_TILE_ROWS - 1
CONV_WIDTH = 31
CONV_PAD = 32
EPS = 1e-6
ROPE_THETA = 10000.0
NEG = -1e30
N_DEV = 8
LANES = 128
VMEM_LIMIT_BYTES = 56 * 1024 * 1024

ADAM_LR, ADAM_B1, ADAM_B2, ADAM_EPS, ADAM_WD, ADAM_STEP = 0.001, 0.9, 0.999, 1e-08, 0.01, 10

CB = dict(a_k=0, a_v=4, c_k=8, c_v=9, a_q=10, c_q=14, a_gate=18, b_in=22, b_gate=26, c_gate=30, d_glu=34, d_gate=42, merge=46)
KV_COLS = 1280
MESH = pl.DeviceIdType.MESH


def _pick(n, cands):
    for c in cands:
        if n % c == 0:
            return c
    raise ValueError(f"no tile for {n} among {cands}")


def _cp(*sem):
    return pltpu.CompilerParams(dimension_semantics=sem if sem else None, vmem_limit_bytes=VMEM_LIMIT_BYTES)


def _dg(x, y, cx, cy):
    return lax.dot_general(x.astype(BF), y.astype(BF), (((cx,), (cy,)), ((), ())), preferred_element_type=F32)


@functools.partial(jax.custom_vjp, nondiff_argnums=(2, 3))
def bdot(a, b, ca=1, cb=0):
    return _dg(a, b, ca, cb)


def _bdot_fwd(a, b, ca, cb):
    return _dg(a, b, ca, cb), (a, b)


def _bdot_bwd(ca, cb, res, g):
    a, b = res
    jb = 1 if cb == 0 else 0
    ia = 0 if ca == 1 else 1
    da = _dg(g, b, 1, jb) if ca == 1 else _dg(b, g, jb, 1)
    db = _dg(a, g, ia, 0) if cb == 0 else _dg(g, a, 0, ia)
    return da.astype(a.dtype), db.astype(b.dtype)


bdot.defvjp(_bdot_fwd, _bdot_bwd)


def hdot(a, b):
    return jnp.dot(a, b, precision=HI, preferred_element_type=F32)


@jax.custom_vjp
def swap64(x):
    return pltpu.roll(x, 64, 1)


swap64.defvjp(lambda x: (swap64(x), None), lambda _, g: (swap64(g),))


@jax.custom_vjp
def partner(x):
    w = x.shape[-1]
    lane = lax.broadcasted_iota(jnp.int32, x.shape, 1)
    return jnp.where((lane % 32) < 16, pltpu.roll(x, w - 16, 1), pltpu.roll(x, 16, 1))


partner.defvjp(lambda x: (partner(x), None), lambda _, g: (partner(g),))


@functools.partial(jax.custom_vjp, nondiff_argnums=(1,))
def split_rows(x, n):
    k = x.shape[0] // n
    return tuple(x[i * k:(i + 1) * k] for i in range(n))


split_rows.defvjp(lambda x, n: (split_rows(x, n), None), lambda n, _, gs: (jnp.concatenate(gs, axis=0),))


def _lane(shape):
    return lax.broadcasted_iota(jnp.int32, shape, len(shape) - 1)


def _silu(x):
    return x * jax.nn.sigmoid(x)


def _normrope(x, w, cos, sin, bd):
    ms = hdot(x * x, bd)
    y = x * lax.rsqrt(ms + EPS) * w
    return y * cos + partner(y) * sin


def matmul(name, a, b, *, ca=1, cb=0, out_dtype=F32, tm, tn, tk):
    M = a.shape[1 - ca]
    K = a.shape[ca]
    N = b.shape[1 - cb]
    assert b.shape[cb] == K and M % tm == 0 and N % tn == 0 and K % tk == 0, (name, a.shape, b.shape)
    nk = K // tk
    a_spec = pl.BlockSpec((tm, tk), lambda i, j, k: (i, k)) if ca == 1 else pl.BlockSpec((tk, tm), lambda i, j, k: (k, i))
    b_spec = pl.BlockSpec((tk, tn), lambda i, j, k: (k, j)) if cb == 0 else pl.BlockSpec((tn, tk), lambda i, j, k: (j, k))

    def kern(a_ref, b_ref, o_ref, *scr):
        part = _dg(a_ref[...], b_ref[...], ca, cb)
        if nk == 1:
            o_ref[...] = part.astype(out_dtype)
            return
        acc_ref, = scr
        k = pl.program_id(2)

        @pl.when(k == 0)
        def _():
            acc_ref[...] = part

        @pl.when(k > 0)
        def _():
            acc_ref[...] += part

        @pl.when(k == nk - 1)
        def _():
            o_ref[...] = acc_ref[...].astype(out_dtype)

    return pl.pallas_call(
        kern, grid=(M // tm, N // tn, nk), in_specs=[a_spec, b_spec],
        out_specs=pl.BlockSpec((tm, tn), lambda i, j, k: (i, j)),
        out_shape=jax.ShapeDtypeStruct((M, N), out_dtype),
        scratch_shapes=[] if nk == 1 else [pltpu.VMEM((tm, tn), F32)],
        compiler_params=_cp("parallel", "parallel", "arbitrary"), name=name,
    )(a, b)


def rowcall(name, body, T, tm, ncol, nctx, rows, consts=(), segs=(), outs=(), accs=(), saccs=()):
    nctxb = nctx // tm
    assert T % tm == 0 and (nctx % tm == 0 or not (segs or saccs))
    nr, nc, ns = len(rows), len(consts), len(segs)

    def seg_map(i, j):
        return (jnp.where(i < nctxb, 0, 1), 0, 0)

    def col_map(cf):
        return lambda i, j: (i, cf(j))

    in_specs, args = [], []
    for arr, bw, cf in rows:
        in_specs.append(pl.BlockSpec((tm, bw), col_map(cf)))
        args.append(arr)
    for arr in consts:
        in_specs.append(pl.BlockSpec(arr.shape, lambda i, j, nd=arr.ndim: (0,) * nd))
        args.append(arr)
    for arr in segs:
        in_specs.append(pl.BlockSpec((None, 1, arr.shape[-1]), seg_map))
        args.append(arr)
    out_shape, out_specs, aliases, out_dtypes = [], [], {}, []
    for o in outs:
        if o[0] == "into":
            _, arr, bw, cf = o
            aliases[len(args)] = len(out_shape)
            in_specs.append(pl.BlockSpec(memory_space=pl.ANY))
            args.append(arr)
            out_shape.append(jax.ShapeDtypeStruct(arr.shape, arr.dtype))
            out_dtypes.append(arr.dtype)
        else:
            cols, dt, bw, cf = o
            out_shape.append(jax.ShapeDtypeStruct((T, cols), dt))
            out_dtypes.append(dt)
        out_specs.append(pl.BlockSpec((tm, bw), col_map(cf)))
    for shp in accs:
        out_shape.append(jax.ShapeDtypeStruct(shp, F32))
        out_specs.append(pl.BlockSpec(shp, lambda i, j, nd=len(shp): (0,) * nd))
    for w in saccs:
        out_shape.append(jax.ShapeDtypeStruct((2, 1, w), F32))
        out_specs.append(pl.BlockSpec((None, 1, w), seg_map))
    n_in = len(args)
    no, na, nsa = len(outs), len(accs), len(saccs)

    def kern(*refs):
        i, j = pl.program_id(0), pl.program_id(1)
        ins = refs[:nr + nc + ns]
        orefs = refs[n_in:n_in + no]
        arefs = refs[n_in + no:n_in + no + na]
        srefs = refs[n_in + no + na:n_in + no + na + nsa]
        ov, av, sv = body(i, j, [r[...] for r in ins[:nr]], [r[...] for r in ins[nr:nr + nc]], [r[...] for r in ins[nr + nc:]])
        for r, v, dt in zip(orefs, ov, out_dtypes):
            r[...] = v.astype(dt)
        if na:
            @pl.when((i == 0) & (j == 0))
            def _():
                for r in arefs:
                    r[...] = jnp.zeros_like(r)
            for r, v in zip(arefs, av):
                r[...] += v
        if nsa:
            @pl.when(((i == 0) | (i == nctxb)) & (j == 0))
            def _():
                for r in srefs:
                    r[...] = jnp.zeros_like(r)
            for r, v in zip(srefs, sv):
                r[...] += v

    res = pl.pallas_call(
        kern, grid=(T // tm, ncol), in_specs=in_specs, out_specs=out_specs, out_shape=out_shape,
        input_output_aliases=aliases, compiler_params=_cp("arbitrary", "arbitrary"), name=name,
    )(*args)
    return res


def _c(k):
    return lambda j: k


def f_modulate(x, g, sh, sc):
    y = x * lax.rsqrt(jnp.mean(x * x, axis=-1, keepdims=True) + EPS) * g
    return y * (1.0 + sc) + sh


def modulate_fwd(X, g_pre, shift, scale, Cn, tm):
    T, D = X.shape

    def body(i, j, r, c, s):
        return [f_modulate(r[0], c[0], s[0], s[1])], [], []

    return rowcall("modulate_fwd", body, T, tm, 1, Cn, [(X, D, _c(0))], [g_pre], [shift, scale], [(D, BF, D, _c(0))])[0]


def modulate_bwd(X, dH, dXn, g_pre, shift, scale, Cn, tm):
    T, D = X.shape

    def body(i, j, r, c, s):
        _, vjp = jax.vjp(f_modulate, r[0], c[0], s[0], s[1])
        dx, dg, dsh, dsc = vjp(r[1])
        return [dx + r[2]], [dg], [dsh, dsc]

    return rowcall("modulate_bwd", body, T, tm, 1, Cn, [(X, D, _c(0)), (dH, D, _c(0)), (dXn, D, _c(0))], [g_pre], [shift, scale],
                   [(D, F32, D, _c(0))], [(1, D)], [D, D])


def f_post(z, gp, gate):
    return gate * (z * lax.rsqrt(jnp.mean(z * z, axis=-1, keepdims=True) + EPS) * gp)


def post_fwd(X, Z, g_post, gate, Cn, tm):
    T, D = X.shape

    def body(i, j, r, c, s):
        return [r[0] + f_post(r[1], c[0], s[0])], [], []

    return rowcall("post_fwd", body, T, tm, 1, Cn, [(X, D, _c(0)), (Z, D, _c(0))], [g_post], [gate], [(D, F32, D, _c(0))])[0]


def post_bwd(Z, dXn, g_post, gate, Cn, tm):
    T, D = Z.shape

    def body(i, j, r, c, s):
        _, vjp = jax.vjp(f_post, r[0], c[0], s[0])
        dz, dgp, dgate = vjp(r[1])
        return [dz], [dgp], [dgate]

    return rowcall("post_bwd", body, T, tm, 1, Cn, [(Z, D, _c(0)), (dXn, D, _c(0))], [g_post], [gate],
                   [(D, BF, D, _c(0))], [(1, D)], [D])


def loss_call(XL, tgt, Cn, tm):
    T, D = XL.shape
    nctxb = Cn // tm

    def body(i, j, r, c, s):
        diff = jnp.where(i >= nctxb, r[0] - r[1], 0.0)
        per_row = jnp.mean(diff * diff, axis=-1, keepdims=True)
        tot = 0.5 * jnp.sum(per_row, axis=0, keepdims=True)
        return [diff / D], [jnp.broadcast_to(tot, (1, LANES))], []

    return rowcall("loss", body, T, tm, 1, Cn, [(XL, D, _c(0)), (tgt, D, _c(0))], [], [], [(D, F32, D, _c(0))], [(1, LANES)])


def f_prep_q(x, qn, cos, sin, bd, jj):
    y = _normrope(x, qn, cos, sin, bd) * 0.125
    lane = _lane(y.shape)
    a = jnp.where(lane < 64, y, 0.0)
    b = jnp.where(lane >= 64, y, 0.0)
    g0 = jj < 2
    return jnp.concatenate([jnp.where(g0, a, swap64(a)), jnp.where(g0, swap64(b), b)], axis=1)


def prep_q_fwd(P, cos, sin, qn, bd, Cn, tm):
    T = P.shape[0]

    def body(i, j, r, c, s):
        return [f_prep_q(r[0], c[0], r[1], r[2], c[1], j)], [], []

    return rowcall("prep_q_fwd", body, T, tm, 4, Cn, [(P, 128, lambda j: CB["c_q"] + j), (cos, 128, _c(0)), (sin, 128, _c(0))], [qn, bd], [],
                   [(1024, BF, 256, lambda j: j)])[0]


def prep_q_bwd(P, dQ, dP, cos, sin, qn, bd, Cn, tm):
    T = P.shape[0]

    def body(i, j, r, c, s):
        _, vjp = jax.vjp(lambda x, w: f_prep_q(x, w, r[1], r[2], c[1], j), r[0], c[0])
        dx, dw = vjp(r[3])
        return [dx], [dw], []

    return rowcall("prep_q_bwd", body, T, tm, 4, Cn,
                   [(P, 128, lambda j: CB["c_q"] + j), (cos, 128, _c(0)), (sin, 128, _c(0)), (dQ, 256, lambda j: j)], [qn, bd], [],
                   [("into", dP, 128, lambda j: CB["c_q"] + j)], [(1, 128)])


def f_prep_kv(x, kn, cos, sin, bd, jj):
    return jnp.where(jj == 0, _normrope(x, kn, cos, sin, bd), x)


def prep_kv_fwd(P, cos, sin, kn, bd, Cn, tm):
    T = P.shape[0]

    def body(i, j, r, c, s):
        return [f_prep_kv(r[0], c[0], r[1], r[2], c[1], j)], [], []

    return rowcall("prep_kv_fwd", body, T, tm, 2, Cn, [(P, 128, lambda j: CB["c_k"] + j), (cos, 128, _c(0)), (sin, 128, _c(0))], [kn, bd], [],
                   [(256, BF, 128, lambda j: j)])[0]


def prep_kv_bwd(P, dKV, dP, cos, sin, kn, bd, Cn, tm):
    T = P.shape[0]

    def body(i, j, r, c, s):
        _, vjp = jax.vjp(lambda x, w: f_prep_kv(x, w, r[1], r[2], c[1], j), r[0], c[0])
        dx, dw = vjp(r[3])
        return [dx], [dw], []

    return rowcall("prep_kv_bwd", body, T, tm, 2, Cn,
                   [(P, 128, lambda j: CB["c_k"] + j), (cos, 128, _c(0)), (sin, 128, _c(0)), (dKV, 128, lambda j: j)], [kn, bd], [],
                   [("into", dP, 128, lambda j: CB["c_k"] + j)], [(1, 128)])


def f_conv_post(y, lg, lb, w):
    mu = jnp.mean(y, axis=-1, keepdims=True)
    var = jnp.mean(jnp.square(y - mu), axis=-1, keepdims=True)
    h = (y - mu) * lax.rsqrt(var + EPS) * lg + lb
    return bdot(_silu(h), w)


def conv_post_fwd(Yc, lg, lb, w, Cn, tm):
    T = Yc.shape[0]

    def body(i, j, r, c, s):
        return [f_conv_post(r[0], c[0], c[1], c[2])], [], []

    return rowcall("conv_post_fwd", body, T, tm, 1, Cn, [(Yc, 512, _c(0))], [lg, lb, w], [], [(512, F32, 512, _c(0))])[0]


def conv_post_bwd(Yc, dO, lg, lb, w, Cn, tm):
    T = Yc.shape[0]

    def body(i, j, r, c, s):
        _, vjp = jax.vjp(f_conv_post, r[0], c[0], c[1], c[2])
        dy, dlg, dlb, dw = vjp(r[1])
        return [dy], [dlg, dlb, dw], []

    return rowcall("conv_post_bwd", body, T, tm, 1, Cn, [(Yc, 512, _c(0)), (dO, 512, _c(3))], [lg, lb, w], [],
                   [(512, F32, 512, _c(0))], [(1, 512), (1, 512), (512, 512)])


def _gate_blk(j):
    b = j // 2
    base = jnp.where(b == 0, CB["a_gate"] // 2, jnp.where(b == 1, CB["b_gate"] // 2, jnp.where(b == 2, CB["c_gate"] // 2, CB["d_gate"] // 2)))
    return base + j % 2


def _sel4(j, vals):
    b = j // 2
    return jnp.where(b == 0, vals[0], jnp.where(b == 1, vals[1], jnp.where(b == 2, vals[2], vals[3])))


def gate_outs_fwd(P, oa, ob, oc, od, Cn, tm):
    T = P.shape[0]

    def body(i, j, r, c, s):
        return [_sel4(j, r[:4]) * _silu(r[4])], [], []

    half = lambda j: j % 2
    return rowcall("gate_outs_fwd", body, T, tm, 8, Cn,
                   [(oa, 256, half), (ob, 256, half), (oc, 256, half), (od, 256, half), (P, 256, _gate_blk)], [], [],
                   [(2048, BF, 256, lambda j: j)])[0]


def gate_outs_bwd(P, oa, ob, oc, od, dG, dP, Cn, tm):
    T = P.shape[0]

    def body(i, j, r, c, s):
        o = _sel4(j, r[:4])
        _, vjp = jax.vjp(lambda oo, gg: oo * _silu(gg), o, r[4])
        do, dg = vjp(r[5])
        return [do, dg], [], []

    half = lambda j: j % 2
    return rowcall("gate_outs_bwd", body, T, tm, 8, Cn,
                   [(oa, 256, half), (ob, 256, half), (oc, 256, half), (od, 256, half), (P, 256, _gate_blk), (dG, 256, lambda j: j)], [], [],
                   [(2048, F32, 256, lambda j: j), ("into", dP, 256, _gate_blk)])


def merge_fwd(P, U, Cn, tm):
    T = P.shape[0]
    D4 = U.shape[1]
    nb = D4 // 4 // 256
    mb = CB["merge"] // 2

    def body(i, j, r, c, s):
        y = 0.0
        for b in range(4):
            y = y + jax.nn.sigmoid(r[b]) * r[4 + b]
        return [y], [], []

    rows = [(P, 256, (lambda j, b=b: mb + b * nb + j)) for b in range(4)] + [(U, 256, (lambda j, b=b: b * nb + j)) for b in range(4)]
    return rowcall("merge_fwd", body, T, tm, nb, Cn, rows, [], [], [(D4 // 4, BF, 256, lambda j: j)])[0]


def merge_bwd(P, U, dY, Cn, tm):
    T, IN = P.shape
    D4 = U.shape[1]
    nb = D4 // 4 // 256
    mb = CB["merge"] // 2

    def body(i, j, r, c, s):
        _, vjp = jax.vjp(lambda m, u: jax.nn.sigmoid(m) * u, r[0], r[1])
        dm, du = vjp(r[2])
        return [du, dm], [], []

    return rowcall("merge_bwd", body, T, tm, 4 * nb, Cn,
                   [(P, 256, lambda j: mb + j), (U, 256, lambda j: j), (dY, 256, lambda j: j % nb)], [], [],
                   [(D4, BF, 256, lambda j: j), (IN, F32, 256, lambda j: mb + j)])


def assemble_na(dK, dV, dQ, dP, Cn, tm):
    T = dP.shape[0]

    def body(i, j, r, c, s):
        return [jnp.where(j < 4, r[0], jnp.where(j < 8, r[1], r[2]))], [], []

    return rowcall("assemble_na", body, T, tm, 12, Cn,
                   [(dK, 128, lambda j: jnp.minimum(j, 3)), (dV, 128, lambda j: jnp.clip(j - 4, 0, 3)), (dQ, 128, lambda j: jnp.clip(j - 8, 0, 3))],
                   [], [], [("into", dP, 128, lambda j: jnp.where(j < 8, j, j + 2))])[0]


def branch_fwd(G, Wb, tm):
    T = G.shape[0]
    D = Wb.shape[2]

    def kern(g_ref, w_ref, o_ref):
        o_ref[...] = _dg(g_ref[...], w_ref[...], 1, 0)

    return pl.pallas_call(
        kern, grid=(T // tm, 4),
        in_specs=[pl.BlockSpec((tm, 512), lambda i, b: (i, b)), pl.BlockSpec((None, 512, D), lambda i, b: (b, 0, 0))],
        out_specs=pl.BlockSpec((tm, D), lambda i, b: (i, b)), out_shape=jax.ShapeDtypeStruct((T, 4 * D), F32),
        compiler_params=_cp("parallel", "arbitrary"), name="branch_fwd")(G, Wb)


def branch_dg(dU, Wb, tm):
    T = dU.shape[0]
    D = Wb.shape[2]

    def kern(u_ref, w_ref, o_ref):
        o_ref[...] = _dg(u_ref[...], w_ref[...], 1, 1)

    return pl.pallas_call(
        kern, grid=(T // tm, 4),
        in_specs=[pl.BlockSpec((tm, D), lambda i, b: (i, b)), pl.BlockSpec((None, 512, D), lambda i, b: (b, 0, 0))],
        out_specs=pl.BlockSpec((tm, 512), lambda i, b: (i, b)), out_shape=jax.ShapeDtypeStruct((T, 2048), F32),
        compiler_params=_cp("parallel", "arbitrary"), name="branch_dg")(dU, Wb)


def branch_merge_fwd(P, G, Wb, tm):
    T = P.shape[0]
    D = Wb.shape[2]
    nj = D // 256
    mb = CB["merge"] // 2

    def kern(g_ref, w_ref, m0, m1, m2, m3, y_ref):
        y = None
        for b, m_ref in enumerate((m0, m1, m2, m3)):
            term = jax.nn.sigmoid(m_ref[...]) * _dg(g_ref[:, 512 * b:512 * (b + 1)], w_ref[b], 1, 0)
            y = term if y is None else y + term
        y_ref[...] = y.astype(BF)

    m_specs = [pl.BlockSpec((tm, 256), (lambda i, j, b=b: (i, mb + b * nj + j))) for b in range(4)]
    return pl.pallas_call(
        kern, grid=(T // tm, nj),
        in_specs=[pl.BlockSpec((tm, 2048), lambda i, j: (i, 0)), pl.BlockSpec((4, 512, 256), lambda i, j: (0, 0, j))] + m_specs,
        out_specs=pl.BlockSpec((tm, 256), lambda i, j: (i, j)), out_shape=jax.ShapeDtypeStruct((T, D), BF),
        compiler_params=_cp("parallel", "arbitrary"), name="branch_merge_fwd")(G, Wb, P, P, P, P)


def branch_merge_bwd(P, G, Wb, dY, tm):
    T, IN = P.shape
    D = Wb.shape[2]
    nj = D // 256
    mb = CB["merge"] // 2

    def kern(g_ref, w_ref, m_ref, dy_ref, du_ref, dg_ref, dm_ref, acc_s):
        j = pl.program_id(2)
        w = w_ref[...]
        u = _dg(g_ref[...], w, 1, 0)
        sig = jax.nn.sigmoid(m_ref[...])
        dy = dy_ref[...]
        dm_ref[...] = (dy * u * sig * (1.0 - sig)).astype(BF)
        du = dy * sig
        du_ref[...] = du.astype(BF)
        part = _dg(du, w, 1, 1)

        @pl.when(j == 0)
        def _():
            acc_s[...] = part

        @pl.when(j > 0)
        def _():
            acc_s[...] += part

        @pl.when(j == nj - 1)
        def _():
            dg_ref[...] = acc_s[...]

    return pl.pallas_call(
        kern, grid=(T // tm, 4, nj),
        in_specs=[pl.BlockSpec((tm, 512), lambda i, b, j: (i, b)), pl.BlockSpec((None, 512, 256), lambda i, b, j: (b, 0, j)),
                  pl.BlockSpec((tm, 256), lambda i, b, j: (i, mb + b * nj + j)), pl.BlockSpec((tm, 256), lambda i, b, j: (i, j))],
        out_specs=[pl.BlockSpec((tm, 256), lambda i, b, j: (i, b * nj + j)), pl.BlockSpec((tm, 512), lambda i, b, j: (i, b)),
                   pl.BlockSpec((tm, 256), lambda i, b, j: (i, mb + b * nj + j))],
        out_shape=[jax.ShapeDtypeStruct((T, 4 * D), BF), jax.ShapeDtypeStruct((T, 2048), F32), jax.ShapeDtypeStruct((T, IN), BF)],
        scratch_shapes=[pltpu.VMEM((tm, 512), F32)],
        compiler_params=_cp("parallel", "arbitrary", "arbitrary"), name="branch_merge_bwd")(G, Wb, P, dY)


def rows_from_devs(name, xg, l, L):
    _, Lr, C = xg.shape
    r = Lr // L

    def body(x_ref, o_ref):
        o_ref[...] = x_ref[...]

    return pl.pallas_call(
        body, grid=(N_DEV,), in_specs=[pl.BlockSpec((None, r, C), lambda d: (d, l, 0))], out_specs=pl.BlockSpec((r, C), lambda d: (d, 0)),
        out_shape=jax.ShapeDtypeStruct((N_DEV * r, C), xg.dtype), compiler_params=_cp("parallel"), name=name)(xg)


def rows_to_devs(name, g, l, L, prev):
    R8, C = g.shape
    r = R8 // N_DEV

    def body(x_ref, *rest):
        rest[-1][...] = x_ref[...]

    in_specs, args, aliases = [pl.BlockSpec((r, C), lambda d: (d, 0))], [g], {}
    if prev is not None:
        in_specs.append(pl.BlockSpec(memory_space=pl.ANY))
        args.append(prev)
        aliases = {1: 0}
    return pl.pallas_call(
        body, grid=(N_DEV,), in_specs=in_specs, out_specs=pl.BlockSpec((None, r, C), lambda d: (d, l, 0)),
        out_shape=jax.ShapeDtypeStruct((N_DEV, L * r, C), g.dtype), input_output_aliases=aliases,
        compiler_params=_cp("parallel"), name=name)(*args)


def branch_dw(G, dU, tk):
    T = G.shape[0]
    D = dU.shape[1] // 4
    nk = T // tk

    def kern(g_ref, u_ref, o_ref):
        k = pl.program_id(1)
        part = _dg(g_ref[...], u_ref[...], 0, 0)

        @pl.when(k == 0)
        def _():
            o_ref[...] = part

        @pl.when(k > 0)
        def _():
            o_ref[...] += part

    return pl.pallas_call(
        kern, grid=(4, nk),
        in_specs=[pl.BlockSpec((tk, 512), lambda b, k: (k, b)), pl.BlockSpec((tk, D), lambda b, k: (k, b))],
        out_specs=pl.BlockSpec((None, 512, D), lambda b, k: (b, 0, 0)), out_shape=jax.ShapeDtypeStruct((4, 512, D), F32),
        compiler_params=_cp("parallel", "arbitrary"), name="branch_dw")(G, dU)


def na_tables(S, Cn):
    R = S // GRID_W
    ntile = (Cn + S) // 128
    nct = Cn // 128
    rs = np.clip(np.arange(R) - NA_WIN_ROWS // 2, 0, R - NA_WIN_ROWS)
    seen, pats, cls, ws = {}, [], [], []
    for t in range(ntile):
        pat = -np.ones((NA_TILE_ROWS, NA_BAND), np.int64)
        w0 = 0
        if t >= nct:
            r0 = NA_TILE_ROWS * (t - nct)
            w0 = min(rs[r0], R - NA_BAND)
            for a in range(NA_TILE_ROWS):
                for j in range(NA_BAND):
                    kr = w0 + j
                    if rs[r0 + a] <= kr < rs[r0 + a] + NA_WIN_ROWS:
                        pat[a, j] = kr - (r0 + a) + NA_WIN_ROWS - 1
        key = pat.tobytes()
        if key not in seen:
            seen[key] = len(pats)
            pats.append(pat)
        cls.append(seen[key])
        ws.append(Cn + GRID_W * int(w0))
    cls = np.asarray(cls, np.int32)
    first = np.asarray([1 if t == 0 or cls[t] != cls[t - 1] else 0 for t in range(ntile)], np.int32)
    assert len(set(cls[first == 1].tolist())) == int(first.sum())
    pats = np.stack(pats)
    ncls = pats.shape[0]
    nrow = -(-(ncls * NA_TILE_ROWS * NA_BAND) // 128) * 128
    m1 = np.zeros((nrow, 128), np.float32)
    flat = pats.reshape(-1)
    for k, dr in enumerate(flat):
        if dr >= 0:
            m1[k, dr] = 1.0
    qc = np.arange(GRID_W)
    col_start = np.clip(qc - NA_WIN_COLS // 2, 0, GRID_W - NA_WIN_COLS)
    kc = np.arange(GRID_W)
    col_ok = (kc[None, :] >= col_start[:, None]) & (kc[None, :] < col_start[:, None] + NA_WIN_COLS)
    m2 = np.zeros((128, GRID_W * GRID_W), np.float32)
    for q in range(GRID_W):
        for k in range(GRID_W):
            if col_ok[q, k]:
                m2[k - q + NA_WIN_COLS - 1, q * GRID_W + k] = 1.0
    valid = (pats >= 0)[:, :, :, None, None] & col_ok[None, None, None]
    vmask = np.transpose(valid, (0, 2, 4, 1, 3)).reshape(ncls, 1, NA_BAND * GRID_W, 1, NA_TILE_ROWS * GRID_W)
    vmask = np.broadcast_to(vmask, (ncls, 1, NA_BAND * GRID_W, 2, NA_TILE_ROWS * GRID_W)).reshape(ncls, 1, NA_BAND * GRID_W, 256)
    return dict(cls=cls, ws=np.asarray(ws, np.int32), first=first, m1=m1, m2=m2, vmask=vmask, ncls=ncls, nrow=nrow)


def rpb_map(name, x, left, right):
    H = x.shape[0]

    def kern(x_ref, l_ref, r_ref, o_ref):
        o_ref[...] = hdot(hdot(l_ref[...], x_ref[...]), r_ref[...])

    return pl.pallas_call(
        kern, grid=(H,),
        in_specs=[pl.BlockSpec((None,) + x.shape[1:], lambda h: (h, 0, 0)), pl.BlockSpec(left.shape, lambda h: (0, 0)),
                  pl.BlockSpec(right.shape, lambda h: (0, 0))],
        out_specs=pl.BlockSpec((None, left.shape[0], right.shape[1]), lambda h: (h, 0, 0)),
        out_shape=jax.ShapeDtypeStruct((H, left.shape[0], right.shape[1]), F32), compiler_params=_cp("parallel"), name=name)(x, left, right)


def bias_table(rpb, tb):
    H = rpb.shape[0]
    xp = jnp.zeros((H, 128, 128), F32).at[:, :rpb.shape[1], :rpb.shape[2]].set(rpb)
    a = rpb_map("rpb_expand", xp, jnp.asarray(tb["m1"]), jnp.asarray(tb["m2"]))
    ncls = tb["ncls"]
    a = a[:, :ncls * NA_TILE_ROWS * NA_BAND].reshape(H // 2, 2, ncls, NA_TILE_ROWS, NA_BAND, GRID_W, GRID_W)
    a = jnp.transpose(a, (2, 0, 4, 6, 1, 3, 5)).reshape(ncls, H // 2, NA_BAND * GRID_W, 256)
    return jnp.where(jnp.asarray(tb["vmask"]), a, NEG)


def bias_table_grad(dbt, tb):
    ncls = tb["ncls"]
    H2 = dbt.shape[1]
    d = dbt.reshape(ncls, H2, NA_BAND, GRID_W, 2, NA_TILE_ROWS, GRID_W)
    d = jnp.transpose(d, (1, 4, 0, 5, 2, 6, 3)).reshape(H2 * 2, ncls * NA_TILE_ROWS * NA_BAND, GRID_W * GRID_W)
    d = jnp.pad(d, ((0, 0), (0, tb["nrow"] - d.shape[1]), (0, 0)))
    g = rpb_map("rpb_reduce", d, jnp.asarray(tb["m1"].T.copy()), jnp.asarray(tb["m2"].T.copy()))
    return g[:, :2 * NA_WIN_ROWS - 1, :2 * NA_WIN_COLS - 1]


def f_na(q, kb, vb, kc, vc, bias_t):
    d = lax.broadcasted_iota(jnp.int32, (128, 1), 0)
    q_t = q.T
    qbd = jnp.concatenate([jnp.where(d < 64, q_t, 0.0), jnp.where(d >= 64, q_t, 0.0)], axis=1)
    sb = bdot(kb, qbd) * 0.125 + bias_t
    sc = bdot(kc, qbd) * 0.125
    m = jnp.maximum(jnp.max(sb, axis=0, keepdims=True), jnp.max(sc, axis=0, keepdims=True))
    eb = jnp.exp(sb - m)
    ec = jnp.exp(sc - m)
    den = jnp.sum(eb, axis=0, keepdims=True) + jnp.sum(ec, axis=0, keepdims=True)
    of = bdot(eb / den, vb, 0, 0) + bdot(ec / den, vc, 0, 0)
    o0, o1 = split_rows(of, 2)
    return jnp.where(_lane(o0.shape) < 64, o0, o1)


def _na_specs(T, Cn, nband):
    qs = pl.BlockSpec((128, 128), lambda hp, t, *_: (t, CB["a_q"] + hp))
    ks = pl.BlockSpec((T, 128), lambda hp, t, *_: (0, CB["a_k"] + hp))
    vs = pl.BlockSpec((T, 128), lambda hp, t, *_: (0, CB["a_v"] + hp))
    bs = pl.BlockSpec((None, None, nband, 256), lambda hp, t, cls, ws, first: (cls[t], hp, 0, 0))
    return qs, ks, vs, bs


def na_fwd(P, bt, tb, Cn):
    T = P.shape[0]
    nband = NA_BAND * GRID_W
    qs, ks, vs, bs = _na_specs(T, Cn, nband)

    def kern(cls, ws, first, q_ref, k_ref, v_ref, b_ref, o_ref):
        w0 = pl.multiple_of(ws[pl.program_id(1)], 64)
        o_ref[...] = f_na(q_ref[...], k_ref[pl.ds(w0, nband), :], v_ref[pl.ds(w0, nband), :], k_ref[0:Cn, :], v_ref[0:Cn, :], b_ref[...])

    return pl.pallas_call(
        kern, grid_spec=pltpu.PrefetchScalarGridSpec(
            num_scalar_prefetch=3, grid=(4, T // 128), in_specs=[qs, ks, vs, bs],
            out_specs=pl.BlockSpec((128, 128), lambda hp, t, *_: (t, hp))),
        out_shape=jax.ShapeDtypeStruct((T, 512), F32), compiler_params=_cp("arbitrary", "arbitrary"), name="na_fwd",
    )(jnp.asarray(tb["cls"]), jnp.asarray(tb["ws"]), jnp.asarray(tb["first"]), P, P, P, bt)


def na_bwd(P, bt, dO, tb, Cn):
    T = P.shape[0]
    nband = NA_BAND * GRID_W
    qs, ks, vs, bs = _na_specs(T, Cn, nband)
    dos = pl.BlockSpec((128, 128), lambda hp, t, *_: (t, hp))

    def kern(cls, ws, first, q_ref, k_ref, v_ref, b_ref, do_ref, dq_ref, dk_ref, dv_ref, db_ref):
        t = pl.program_id(1)
        w0 = pl.multiple_of(ws[t], 64)
        band = pl.ds(w0, nband)
        _, vjp = jax.vjp(f_na, q_ref[...], k_ref[band, :], v_ref[band, :], k_ref[0:Cn, :], v_ref[0:Cn, :], b_ref[...])
        dq, dkb, dvb, dkc, dvc, db = vjp(do_ref[...])
        dq_ref[...] = dq

        @pl.when(t == 0)
        def _():
            dk_ref[...] = jnp.zeros_like(dk_ref)
            dv_ref[...] = jnp.zeros_like(dv_ref)

        dk_ref[band, :] += dkb
        dv_ref[band, :] += dvb
        dk_ref[0:Cn, :] += dkc
        dv_ref[0:Cn, :] += dvc

        @pl.when(first[t] == 1)
        def _():
            db_ref[...] = db

        @pl.when(first[t] == 0)
        def _():
            db_ref[...] += db

    full = lambda hp, t, *_: (0, hp)
    return pl.pallas_call(
        kern, grid_spec=pltpu.PrefetchScalarGridSpec(
            num_scalar_prefetch=3, grid=(4, T // 128), in_specs=[qs, ks, vs, bs, dos],
            out_specs=[pl.BlockSpec((128, 128), lambda hp, t, *_: (t, hp)), pl.BlockSpec((T, 128), full), pl.BlockSpec((T, 128), full),
                       pl.BlockSpec((None, None, nband, 256), lambda hp, t, cls, ws, first: (cls[t], hp, 0, 0))]),
        out_shape=[jax.ShapeDtypeStruct((T, 512), F32), jax.ShapeDtypeStruct((T, 512), F32), jax.ShapeDtypeStruct((T, 512), F32),
                   jax.ShapeDtypeStruct(bt.shape, F32)],
        compiler_params=_cp("arbitrary", "arbitrary"), name="na_bwd",
    )(jnp.asarray(tb["cls"]), jnp.asarray(tb["ws"]), jnp.asarray(tb["first"]), P, P, P, bt, dO)


def _expand_heads(blk, g):
    out = []
    for p in range(2):
        pair = blk[:, 128 * p:128 * (p + 1)]
        lane = _lane(pair.shape)
        a = jnp.where(lane < 64, pair, 0.0)
        b = jnp.where(lane >= 64, pair, 0.0)
        out.append(jnp.where(g == 0, a, pltpu.roll(a, 64, 1)))
        out.append(jnp.where(g == 0, pltpu.roll(b, 64, 1), b))
    return out


def _compact_heads(hs, g):
    out = []
    for p in range(2):
        e, o = hs[2 * p], hs[2 * p + 1]
        lane = _lane(e.shape)
        e0 = jnp.where(g == 0, e, pltpu.roll(e, 64, 1))
        o1 = jnp.where(g == 0, pltpu.roll(o, 64, 1), o)
        out.append(jnp.where(lane < 64, e0, o1))
    return jnp.concatenate(out, axis=1)


def _kv_map(Cn, tq, tk, col):
    nq_ctx = Cn // tq
    last_ctx = (Cn - 1) // tk

    def f(g, qi, kj):
        return (jnp.where(qi < nq_ctx, jnp.minimum(kj, last_ctx), kj), col)
    return f


def flash_fwd(Qp, KV, Cn, tq, tk):
    T = Qp.shape[0]
    nq, nk = T // tq, T // tk
    nq_ctx = Cn // tq

    def kern(q_ref, k_ref, v_ref, o_ref, lse_ref, m_s, l_s, acc_s):
        g, qi, kj = pl.program_id(0), pl.program_id(1), pl.program_id(2)

        @pl.when(kj == 0)
        def _():
            m_s[...] = jnp.full_like(m_s, NEG)
            l_s[...] = jnp.zeros_like(l_s)
            acc_s[...] = jnp.zeros_like(acc_s)

        def step(masked):
            k = k_ref[...]
            v = v_ref[...]
            if masked:
                valid = kj * tk + lax.broadcasted_iota(jnp.int32, (1, tk), 1) < Cn
            for h in range(4):
                rows = pl.ds(tq * h, tq)
                s = _dg(q_ref[:, 128 * h:128 * (h + 1)], k, 1, 1)
                if masked:
                    s = jnp.where(valid, s, NEG)
                m_old = m_s[rows, :]
                m_new = jnp.maximum(m_old, jnp.max(s, axis=1, keepdims=True))
                alpha = jnp.exp(m_old - m_new)
                p = jnp.exp(s - m_new)
                l_s[rows, :] = alpha * l_s[rows, :] + jnp.sum(p, axis=1, keepdims=True)
                acc_s[rows, :] = alpha * acc_s[rows, :] + _dg(p, v, 1, 0)
                m_s[rows, :] = m_new

        @pl.when(qi >= nq_ctx)
        def _():
            step(False)

        @pl.when((qi < nq_ctx) & (kj * tk < Cn))
        def _():
            step(True)

        @pl.when(kj == nk - 1)
        def _():
            o4 = acc_s[...] / l_s[...]
            o_ref[...] = _compact_heads([o4[tq * h:tq * (h + 1)] for h in range(4)], g)
            lse = m_s[...] + jnp.log(l_s[...])
            lse_ref[...] = jnp.concatenate([jnp.broadcast_to(lse[tq * h:tq * (h + 1)], (tq, 128)) for h in range(4)], axis=1)

    return pl.pallas_call(
        kern, grid=(2, nq, nk),
        in_specs=[pl.BlockSpec((tq, 512), lambda g, qi, kj: (qi, g)), pl.BlockSpec((tk, 128), _kv_map(Cn, tq, tk, 0)),
                  pl.BlockSpec((tk, 128), _kv_map(Cn, tq, tk, 1))],
        out_specs=[pl.BlockSpec((tq, 256), lambda g, qi, kj: (qi, g)), pl.BlockSpec((tq, 512), lambda g, qi, kj: (qi, g))],
        out_shape=[jax.ShapeDtypeStruct((T, 512), F32), jax.ShapeDtypeStruct((T, 1024), F32)],
        scratch_shapes=[pltpu.VMEM((4 * tq, 1), F32), pltpu.VMEM((4 * tq, 1), F32), pltpu.VMEM((4 * tq, 128), F32)],
        compiler_params=_cp("arbitrary", "arbitrary", "arbitrary"), name="flash_fwd")(Qp, KV, KV)


def flash_bwd(Qp, KV, O, LSE, dOall, Cn, tq, tk):
    T = Qp.shape[0]
    nq, nk = T // tq, T // tk
    nq_ctx = Cn // tq

    def kern(q_ref, k_ref, v_ref, o_ref, lse_ref, do_ref, dq_ref, dkv_ref, dq_s, do_s, dl_s, ls_s):
        g, qi, kj = pl.program_id(0), pl.program_id(1), pl.program_id(2)

        @pl.when((g == 0) & (qi == 0) & (kj == 0))
        def _():
            dkv_ref[...] = jnp.zeros_like(dkv_ref)

        @pl.when(kj == 0)
        def _():
            do4 = jnp.concatenate(_expand_heads(do_ref[...], g), axis=0)
            o4 = jnp.concatenate(_expand_heads(o_ref[...], g), axis=0)
            do_s[...] = do4.astype(BF)
            dl_s[...] = jnp.sum(do4 * o4, axis=1, keepdims=True)
            ls_s[...] = jnp.concatenate([jnp.max(lse_ref[:, 128 * h:128 * (h + 1)], axis=1, keepdims=True) for h in range(4)], axis=0)
            dq_s[...] = jnp.zeros_like(dq_s)

        def step(masked):
            k = k_ref[...]
            v = v_ref[...]
            if masked:
                valid = kj * tk + lax.broadcasted_iota(jnp.int32, (1, tk), 1) < Cn
            dk = dv = None
            for h in range(4):
                rows = pl.ds(tq * h, tq)
                qh = q_ref[:, 128 * h:128 * (h + 1)]
                s = _dg(qh, k, 1, 1)
                if masked:
                    s = jnp.where(valid, s, NEG)
                p = jnp.exp(s - ls_s[rows, :])
                doh = do_s[rows, :]
                dvh = _dg(p, doh, 0, 0)
                dp = _dg(doh, v, 1, 1)
                ds = p * (dp - dl_s[rows, :])
                dq_s[rows, :] += _dg(ds, k, 1, 0)
                dkh = _dg(ds, qh, 0, 0)
                dk = dkh if dk is None else dk + dkh
                dv = dvh if dv is None else dv + dvh
            krows = pl.ds(pl.multiple_of(kj * tk, tk), tk)
            dkv_ref[krows, 0:128] += dk
            dkv_ref[krows, 128:256] += dv

        @pl.when(qi >= nq_ctx)
        def _():
            step(False)

        @pl.when((qi < nq_ctx) & (kj * tk < Cn))
        def _():
            step(True)

        @pl.when(kj == nk - 1)
        def _():
            dq = dq_s[...]
            dq_ref[...] = jnp.concatenate([dq[tq * h:tq * (h + 1)] for h in range(4)], axis=1)

    return pl.pallas_call(
        kern, grid=(2, nq, nk),
        in_specs=[pl.BlockSpec((tq, 512), lambda g, qi, kj: (qi, g)), pl.BlockSpec((tk, 128), _kv_map(Cn, tq, tk, 0)),
                  pl.BlockSpec((tk, 128), _kv_map(Cn, tq, tk, 1)), pl.BlockSpec((tq, 256), lambda g, qi, kj: (qi, g)),
                  pl.BlockSpec((tq, 512), lambda g, qi, kj: (qi, g)), pl.BlockSpec((tq, 256), lambda g, qi, kj: (qi, 4 + g))],
        out_specs=[pl.BlockSpec((tq, 512), lambda g, qi, kj: (qi, g)), pl.BlockSpec((T, 256), lambda g, qi, kj: (0, 0))],
        out_shape=[jax.ShapeDtypeStruct((T, 1024), F32), jax.ShapeDtypeStruct((T, 256), F32)],
        scratch_shapes=[pltpu.VMEM((4 * tq, 128), F32), pltpu.VMEM((4 * tq, 128), BF), pltpu.VMEM((4 * tq, 1), F32), pltpu.VMEM((4 * tq, 1), F32)],
        compiler_params=_cp("arbitrary", "arbitrary", "arbitrary"), name="flash_bwd")(Qp, KV, KV, O, LSE, dOall)


def _pool_band(t0, w0, n, win, gi, Cn, T):
    i = lax.broadcasted_iota(jnp.int32, (n, win), 0)
    jx = lax.broadcasted_iota(jnp.int32, (n, win), 1)
    t = t0 + i
    tp = w0 + jx
    half = lax.shift_left(jnp.int32(1), gi)
    lo = jnp.maximum(t - half, jnp.where(t < Cn, 0, Cn))
    hi = jnp.minimum(t + half - 1, jnp.where(t < Cn, Cn, T) - 1)
    cnt = (hi - lo + 1).astype(F32)
    return jnp.where((tp >= lo) & (tp <= hi), 1.0 / cnt, 0.0) - jnp.where(tp == t, 1.0, 0.0)


def _pool_geom(T):
    n = _pick(T, (256, 128))
    return n, n + 128


def pool_fwd(P, pw, ps, Cn):
    T = P.shape[0]
    n, win = _pool_geom(T)

    def kern(u_ref, w_ref, s_ref, o_ref):
        gi = pl.program_id(0)

        def blk(b, carry):
            t0 = pl.multiple_of(b * n, n)
            w0 = pl.multiple_of(jnp.clip(t0 - 64, 0, T - win), 64)
            d = hdot(_pool_band(t0, w0, n, win, gi, Cn, T), u_ref[pl.ds(w0, win), :])
            o_ref[pl.ds(t0, n), :] = bdot(d, w_ref[...]) * s_ref[...]
            return carry

        lax.fori_loop(0, T // n, blk, 0)

    return pl.pallas_call(
        kern, grid=(4,),
        in_specs=[pl.BlockSpec((T, 128), lambda g: (0, CB["b_in"] + g)), pl.BlockSpec((None, 128, 128), lambda g: (g, 0, 0)),
                  pl.BlockSpec((1, 128), lambda g: (0, g))],
        out_specs=pl.BlockSpec((T, 128), lambda g: (0, g)), out_shape=jax.ShapeDtypeStruct((T, 512), F32),
        compiler_params=_cp("arbitrary"), name="pool_fwd")(P, pw, ps)


def pool_bwd(P, pw, ps, dOall, dP, Cn):
    T = P.shape[0]
    n, win = _pool_geom(T)

    def kern(u_ref, w_ref, s_ref, do_ref, dp_in, du_out, dw_ref, ds_ref, du_ref):
        gi = pl.program_id(0)
        du_ref[...] = jnp.zeros_like(du_ref)
        dw_ref[...] = jnp.zeros_like(dw_ref)
        ds_ref[...] = jnp.zeros_like(ds_ref)

        def blk(b, carry):
            t0 = pl.multiple_of(b * n, n)
            w0 = pl.multiple_of(jnp.clip(t0 - 64, 0, T - win), 64)
            band = _pool_band(t0, w0, n, win, gi, Cn, T)
            _, vjp = jax.vjp(lambda uw, w, s: bdot(hdot(band, uw), w) * s, u_ref[pl.ds(w0, win), :], w_ref[...], s_ref[...])
            duw, dw, ds = vjp(do_ref[pl.ds(t0, n), :])
            du_ref[pl.ds(w0, win), :] += duw
            dw_ref[...] += dw
            ds_ref[...] += ds
            return carry

        lax.fori_loop(0, T // n, blk, 0)
        du_out[...] = du_ref[...].astype(du_out.dtype)

    return pl.pallas_call(
        kern, grid=(4,),
        in_specs=[pl.BlockSpec((T, 128), lambda g: (0, CB["b_in"] + g)), pl.BlockSpec((None, 128, 128), lambda g: (g, 0, 0)),
                  pl.BlockSpec((1, 128), lambda g: (0, g)), pl.BlockSpec((T, 128), lambda g: (0, 4 + g)), pl.BlockSpec(memory_space=pl.ANY)],
        out_specs=[pl.BlockSpec((T, 128), lambda g: (0, CB["b_in"] + g)), pl.BlockSpec((None, 128, 128), lambda g: (g, 0, 0)),
                   pl.BlockSpec((1, 128), lambda g: (0, g))],
        out_shape=[jax.ShapeDtypeStruct(dP.shape, dP.dtype), jax.ShapeDtypeStruct((4, 128, 128), F32), jax.ShapeDtypeStruct((1, 512), F32)],
        scratch_shapes=[pltpu.VMEM((T, 128), F32)],
        input_output_aliases={4: 0}, compiler_params=_cp("arbitrary"), name="pool_bwd")(P, pw, ps, dOall, dP)


CONV_HALO = 16


def _conv_block(T):
    return _pick(T, (256, 128))


def _stage(dst_ref, src, t0, n, T, Cn):
    h = CONV_HALO
    dst_ref[h:h + n, :] = src(pl.ds(t0, n))
    left_ok = (t0 != 0) & (t0 != Cn)
    right_ok = (t0 + n != Cn) & (t0 + n != T)
    lo = pl.multiple_of(jnp.maximum(t0 - h, 0), 8)
    hi = pl.multiple_of(jnp.minimum(t0 + n, T - h), 8)
    dst_ref[0:h, :] = jnp.where(left_ok, src(pl.ds(lo, h)), 0.0)
    dst_ref[h + n:2 * h + n, :] = jnp.where(right_ok, src(pl.ds(hi, h)), 0.0)


def conv_fwd(P, cw, cb, Cn):
    T = P.shape[0]
    n = _conv_block(T)
    off = CONV_HALO - CONV_WIDTH // 2

    def kern(a_ref, g_ref, w_ref, b_ref, y_ref, us):
        def blk(b, carry):
            t0 = pl.multiple_of(b * n, n)
            _stage(us, lambda r: a_ref[r, :] * jax.nn.sigmoid(g_ref[r, :]), t0, n, T, Cn)
            acc = jnp.zeros((n, 128), F32)
            for k in range(CONV_WIDTH):
                acc = acc + us[k + off:k + off + n, :] * w_ref[k:k + 1, :]
            y_ref[pl.ds(t0, n), :] = acc + b_ref[...]
            return carry

        lax.fori_loop(0, T // n, blk, 0)

    slab = lambda o: pl.BlockSpec((T, 128), lambda c: (0, o + c))
    return pl.pallas_call(
        kern, grid=(4,),
        in_specs=[slab(CB["d_glu"]), slab(CB["d_glu"] + 4), pl.BlockSpec((CONV_PAD, 128), lambda c: (0, c)), pl.BlockSpec((1, 128), lambda c: (0, c))],
        out_specs=pl.BlockSpec((T, 128), lambda c: (0, c)), out_shape=jax.ShapeDtypeStruct((T, 512), F32),
        scratch_shapes=[pltpu.VMEM((n + 2 * CONV_HALO, 128), F32)],
        compiler_params=_cp("arbitrary"), name="conv_fwd")(P, P, cw, cb)


def conv_bwd(P, cw, dY, dP, Cn):
    T = P.shape[0]
    n = _conv_block(T)
    off = CONV_HALO - CONV_WIDTH // 2
    back = CONV_HALO + CONV_WIDTH // 2

    def kern(a_ref, g_ref, w_ref, dy_ref, dp_in, da_ref, dg_ref, dw_ref, db_ref, us, dys):
        dw_ref[...] = jnp.zeros_like(dw_ref)
        db_ref[...] = jnp.zeros_like(db_ref)

        def blk(b, carry):
            t0 = pl.multiple_of(b * n, n)
            cur = pl.ds(t0, n)
            _stage(us, lambda r: a_ref[r, :] * jax.nn.sigmoid(g_ref[r, :]), t0, n, T, Cn)
            _stage(dys, lambda r: dy_ref[r, :], t0, n, T, Cn)
            dyc = dy_ref[cur, :]
            du = jnp.zeros((n, 128), F32)
            for k in range(CONV_WIDTH):
                du = du + dys[back - k:back - k + n, :] * w_ref[k:k + 1, :]
                dw_ref[k:k + 1, :] += jnp.sum(us[k + off:k + off + n, :] * dyc, axis=0, keepdims=True)
            a = a_ref[cur, :]
            sig = jax.nn.sigmoid(g_ref[cur, :])
            da_ref[cur, :] = (du * sig).astype(da_ref.dtype)
            dg_ref[cur, :] = du * a * sig * (1.0 - sig)
            db_ref[...] += jnp.sum(dyc, axis=0, keepdims=True)
            return carry

        lax.fori_loop(0, T // n, blk, 0)

    slab = lambda o: pl.BlockSpec((T, 128), lambda c: (0, o + c))
    return pl.pallas_call(
        kern, grid=(4,),
        in_specs=[slab(CB["d_glu"]), slab(CB["d_glu"] + 4), pl.BlockSpec((CONV_PAD, 128), lambda c: (0, c)), slab(0),
                  pl.BlockSpec(memory_space=pl.ANY)],
        out_specs=[slab(CB["d_glu"]), slab(0), pl.BlockSpec((CONV_PAD, 128), lambda c: (0, c)), pl.BlockSpec((1, 128), lambda c: (0, c))],
        out_shape=[jax.ShapeDtypeStruct(dP.shape, dP.dtype), jax.ShapeDtypeStruct((T, 512), F32), jax.ShapeDtypeStruct((CONV_PAD, 512), F32),
                   jax.ShapeDtypeStruct((1, 512), F32)],
        scratch_shapes=[pltpu.VMEM((n + 2 * CONV_HALO, 128), F32), pltpu.VMEM((n + 2 * CONV_HALO, 128), F32)],
        input_output_aliases={4: 0}, compiler_params=_cp("arbitrary"), name="conv_bwd")(P, P, cw, dY, dP)


def copy_cols(name, src, dP, col0, Cn, tm):
    T = dP.shape[0]

    def body(i, j, r, c, s):
        return [r[0]], [], []

    return rowcall(name, body, T, tm, src.shape[1] // 128, Cn, [(src, 128, lambda j: j)], [], [], [("into", dP, 128, lambda j: col0 + j)])[0]


def ada_fwd(cvec, w, b):
    L, D, wc = w.shape

    def kern(c_ref, w_ref, b_ref, o_ref):
        o_ref[...] = _dg(_silu(c_ref[...]), w_ref[...], 1, 0) + b_ref[...]

    return pl.pallas_call(
        kern, grid=(L,),
        in_specs=[pl.BlockSpec((16, D), lambda l: (0, 0)), pl.BlockSpec((None, D, wc), lambda l: (l, 0, 0)), pl.BlockSpec((None, 1, wc), lambda l: (l, 0, 0))],
        out_specs=pl.BlockSpec((None, 16, wc), lambda l: (l, 0, 0)), out_shape=jax.ShapeDtypeStruct((L, 16, wc), F32),
        compiler_params=_cp("arbitrary"), name="ada_fwd")(cvec, w, b)


def ada_bwd(cvec, w, dm):
    L, D, wc = w.shape

    def kern(c_ref, w_ref, d_ref, gw_ref, ds_ref):
        l = pl.program_id(0)
        d = d_ref[...]
        gw_ref[...] = _dg(_silu(c_ref[...]), d, 0, 0)
        part = _dg(d, w_ref[...], 1, 1)

        @pl.when(l == 0)
        def _():
            ds_ref[...] = part

        @pl.when(l > 0)
        def _():
            ds_ref[...] += part

    return pl.pallas_call(
        kern, grid=(L,),
        in_specs=[pl.BlockSpec((16, D), lambda l: (0, 0)), pl.BlockSpec((None, D, wc), lambda l: (l, 0, 0)), pl.BlockSpec((None, 16, wc), lambda l: (l, 0, 0))],
        out_specs=[pl.BlockSpec((None, D, wc), lambda l: (l, 0, 0)), pl.BlockSpec((16, D), lambda l: (0, 0))],
        out_shape=[jax.ShapeDtypeStruct((L, D, wc), F32), jax.ShapeDtypeStruct((16, D), F32)],
        compiler_params=_cp("arbitrary"), name="ada_bwd")(cvec, w, dm)


def silu_grad(cc, parts):
    D = cc.shape[1]

    def kern(c_ref, p_ref, o_ref):
        tot = p_ref[0]
        for k in range(1, N_DEV):
            tot = tot + p_ref[k]
        _, vjp = jax.vjp(_silu, c_ref[...])
        o_ref[...] = vjp(tot)[0]

    return pl.pallas_call(kern, out_shape=jax.ShapeDtypeStruct((1, D), F32), name="silu_grad")(cc, parts)


def _flip(v, f):
    return 1 - v if f else v


_REL = ((0, 0), (1, 0), (0, 1), (1, 1))


def allgather(name, x):
    def body(x_ref, out_ref, send_sems, recv_sems, local_sem):
        mx, my, mc = lax.axis_index("x"), lax.axis_index("y"), lax.axis_index("c")
        me, sibling = (mx, my, mc), (mx, my, 1 - mc)
        chips = [(_flip(mx, fx), _flip(my, fy)) for fx, fy in _REL[1:]]

        def slot(px, py, pc):
            return out_ref.at[4 * px + 2 * py + pc]

        def copy(k, block, to, src=None):
            return pltpu.make_async_remote_copy(
                src_ref=slot(*block) if src is None else src, dst_ref=slot(*block),
                send_sem=send_sems.at[k], recv_sem=recv_sems.at[k], device_id=to, device_id_type=MESH)

        mine = pltpu.make_async_copy(x_ref, slot(*me), local_sem)
        mine.start()
        first = [copy(0, me, sibling, src=x_ref)] + [copy(1 + j, me, (*chip, mc), src=x_ref) for j, chip in enumerate(chips)]
        for cp in first:
            cp.start()
        passed = [copy(4 + j, (*chip, mc), sibling) for j, chip in enumerate(chips)]
        for j, chip in enumerate(chips):
            copy(1 + j, (*chip, mc), me).wait_recv()
            passed[j].start()
        copy(0, sibling, me).wait_recv()
        for j, chip in enumerate(chips):
            copy(4 + j, (*chip, 1 - mc), me).wait_recv()
        for cp in first + passed:
            cp.wait_send()
        mine.wait()

    return pl.pallas_call(
        body, out_shape=jax.ShapeDtypeStruct((N_DEV,) + x.shape, x.dtype),
        in_specs=[pl.BlockSpec(memory_space=pl.ANY)], out_specs=pl.BlockSpec(memory_space=pl.ANY),
        scratch_shapes=[pltpu.SemaphoreType.DMA((7,)), pltpu.SemaphoreType.DMA((7,)), pltpu.SemaphoreType.DMA(())], name=name)(x)


def cols_from_devs(name, xg, l, nrow):
    _, _, C = xg.shape
    tr = _row_tile(nrow, N_DEV * C, xg.dtype.itemsize, 16, 8 << 20)
    nblk = nrow // tr

    def body(x_ref, o_ref):
        for d in range(N_DEV):
            o_ref[:, d * C:(d + 1) * C] = x_ref[d]

    return pl.pallas_call(
        body, grid=(nblk,), in_specs=[pl.BlockSpec((N_DEV, tr, C), lambda i: (0, l * nblk + i, 0))],
        out_specs=pl.BlockSpec((tr, N_DEV * C), lambda i: (i, 0)), out_shape=jax.ShapeDtypeStruct((nrow, N_DEV * C), xg.dtype),
        compiler_params=_cp("parallel"), name=name)(xg)


def cols_to_devs(name, g, l, L, prev):
    nrow, W = g.shape
    C = W // N_DEV
    tr = _row_tile(nrow, W, 4, 8, 8 << 20)
    nblk = nrow // tr

    def body(x_ref, *rest):
        o_ref = rest[-1]
        for d in range(N_DEV):
            o_ref[d] = x_ref[:, d * C:(d + 1) * C]

    in_specs, args, aliases = [pl.BlockSpec((tr, W), lambda i: (i, 0))], [g], {}
    if prev is not None:
        in_specs.append(pl.BlockSpec(memory_space=pl.ANY))
        args.append(prev)
        aliases = {1: 0}
    return pl.pallas_call(
        body, grid=(nblk,), in_specs=in_specs, out_specs=pl.BlockSpec((N_DEV, tr, C), lambda i: (0, l * nblk + i, 0)),
        out_shape=jax.ShapeDtypeStruct((N_DEV, L * nrow, C), F32), input_output_aliases=aliases,
        compiler_params=_cp("parallel"), name=name)(*args)


def rs_exchange_sibling(tag, buf):
    _, R, C = buf.shape

    def body(buf_ref, out_ref, send_sems, recv_sems):
        mx, my, mc = lax.axis_index("x"), lax.axis_index("y"), lax.axis_index("c")
        sibling = (mx, my, 1 - mc)
        cps = []
        for j, (fx, fy) in enumerate(_REL):
            d = 4 * _flip(mx, fx) + 2 * _flip(my, fy) + (1 - mc)
            cps.append(pltpu.make_async_remote_copy(src_ref=buf_ref.at[d], dst_ref=out_ref.at[j], send_sem=send_sems.at[j],
                                                    recv_sem=recv_sems.at[j], device_id=sibling, device_id_type=MESH))
        for cp in cps:
            cp.start()
        for cp in cps:
            cp.wait_recv()
        for cp in cps:
            cp.wait_send()

    return pl.pallas_call(
        body, out_shape=jax.ShapeDtypeStruct((4, R, C), buf.dtype),
        in_specs=[pl.BlockSpec(memory_space=pl.ANY)], out_specs=pl.BlockSpec(memory_space=pl.ANY),
        scratch_shapes=[pltpu.SemaphoreType.DMA((4,)), pltpu.SemaphoreType.DMA((4,))], name="rs_sibling_" + tag)(buf)


def rs_chip_sum(tag, buf, recv, idx, tr):
    _, R, C = buf.shape

    def kern(idx_ref, b_ref, r_ref, own_ref, sb_ref):
        j = pl.program_id(1)
        s = b_ref[...] + r_ref[...]
        sb_ref[...] = s.astype(BF)

        @pl.when(j == 0)
        def _():
            own_ref[...] = s

    return pl.pallas_call(
        kern, grid_spec=pltpu.PrefetchScalarGridSpec(
            num_scalar_prefetch=1, grid=(R // tr, 4),
            in_specs=[pl.BlockSpec((None, tr, C), lambda r, j, idx: (idx[j], r, 0)), pl.BlockSpec((None, tr, C), lambda r, j, idx: (j, r, 0))],
            out_specs=[pl.BlockSpec((tr, C), lambda r, j, idx: (r, 0)), pl.BlockSpec((None, tr, C), lambda r, j, idx: (j, r, 0))]),
        out_shape=[jax.ShapeDtypeStruct((R, C), F32), jax.ShapeDtypeStruct((4, R, C), BF)],
        compiler_params=_cp("arbitrary", "arbitrary"), name="rs_chip_sum_" + tag)(idx, buf, recv)


def rs_exchange_chips(tag, sb):
    _, R, C = sb.shape

    def body(sb_ref, out_ref, send_sems, recv_sems):
        mx, my, mc = lax.axis_index("x"), lax.axis_index("y"), lax.axis_index("c")
        cps = []
        for j, (fx, fy) in enumerate(_REL):
            if j == 0:
                continue
            cps.append(pltpu.make_async_remote_copy(src_ref=sb_ref.at[j], dst_ref=out_ref.at[j], send_sem=send_sems.at[j - 1],
                                                    recv_sem=recv_sems.at[j - 1], device_id=(_flip(mx, fx), _flip(my, fy), mc),
                                                    device_id_type=MESH))
        for cp in cps:
            cp.start()
        for cp in cps:
            cp.wait_recv()
        for cp in cps:
            cp.wait_send()

    return pl.pallas_call(
        body, out_shape=jax.ShapeDtypeStruct((4, R, C), sb.dtype),
        in_specs=[pl.BlockSpec(memory_space=pl.ANY)], out_specs=pl.BlockSpec(memory_space=pl.ANY),
        scratch_shapes=[pltpu.SemaphoreType.DMA((3,)), pltpu.SemaphoreType.DMA((3,))], name="rs_chips_" + tag)(sb)


def adamw(name, parts, w, m, v, tr):
    R, C = w.shape
    in_specs, args = [], []
    for arr, lead in parts:
        if lead is None:
            in_specs.append(pl.BlockSpec((tr, C), lambda r: (r, 0)))
        else:
            in_specs.append(pl.BlockSpec((None, tr, C), lambda r, k=lead: (k, r, 0)))
        args.append(arr)
    npart = len(parts)
    blk = pl.BlockSpec((tr, C), lambda r: (r, 0))

    def kern(*refs):
        g = refs[0][...].astype(F32)
        for r in refs[1:npart]:
            g = g + r[...].astype(F32)
        w_ref, m_ref, v_ref, g_out, d_out, m_out, v_out = refs[npart:]
        mn = ADAM_B1 * m_ref[...] + (1.0 - ADAM_B1) * g
        vn = ADAM_B2 * v_ref[...] + (1.0 - ADAM_B2) * jnp.square(g)
        m_hat = mn / (1.0 - ADAM_B1 ** ADAM_STEP)
        v_hat = vn / (1.0 - ADAM_B2 ** ADAM_STEP)
        g_out[...] = g
        d_out[...] = -ADAM_LR * (m_hat / (jnp.sqrt(v_hat) + ADAM_EPS) + ADAM_WD * w_ref[...])
        m_out[...] = mn
        v_out[...] = vn

    return pl.pallas_call(
        kern, grid=(R // tr,), in_specs=in_specs + [blk, blk, blk], out_specs=[blk] * 4,
        out_shape=[jax.ShapeDtypeStruct((R, C), F32)] * 4, compiler_params=_cp("parallel"), name=name)(*args, w, m, v)


def _pack(arrs):
    flat = [a.reshape(-1) for a in arrs]
    n = sum(f.shape[0] for f in flat)
    pad = (-n) % (8 * LANES)
    if pad:
        flat.append(jnp.zeros((pad,), flat[0].dtype))
    return jnp.concatenate(flat).reshape(-1, LANES)


def _unpack(packed, shapes):
    flat = packed.reshape(-1)
    out, off = [], 0
    for s in shapes:
        n = int(np.prod(s))
        out.append(flat[off:off + n].reshape(s))
        off += n
    return out


def _row_tile(R, C=LANES, itemsize=4, mult=16, target=1 << 20):
    best = None
    for t in range(mult, R + 1, mult):
        if R % t == 0 and t * C * itemsize <= target:
            best = t
    return best or R


BIG = ("w_in", "w_branch", "w_out", "conv_pw", "conv_w")
SMALL = ("g_pre", "g_post", "na_rpb", "pool_w", "pool_scale", "q_norm", "k_norm", "conv_b", "conv_ln_g", "conv_ln_b")
WEIGHTS = ['c_ctx', 'w_ada', 'b_ada', 'g_pre', 'g_post', 'w_in', 'na_rpb', 'pool_w', 'pool_scale', 'q_norm', 'k_norm', 'conv_w', 'conv_b',
           'conv_ln_g', 'conv_ln_b', 'conv_pw', 'w_branch', 'w_out']


def _rope_tables(S, Cn):
    t = np.arange(S)
    pos = np.stack([t // GRID_W, t % GRID_W], 1).astype(np.float64)
    lane = np.arange(128)
    within = lane % 64
    axis = within // 32
    f = (within % 32) % 16
    freqs = ROPE_THETA ** (-np.arange(16, dtype=np.float32) / 16)
    ang = pos[:, axis].astype(np.float32) * freqs[f][None, :]
    cos = np.cos(ang).astype(np.float32)
    sin = np.sin(ang).astype(np.float32) * np.where((within % 32) < 16, -1.0, 1.0).astype(np.float32)[None, :]
    cos = np.concatenate([np.ones((Cn, 128), np.float32), cos])
    sin = np.concatenate([np.zeros((Cn, 128), np.float32), sin])
    return jnp.asarray(cos), jnp.asarray(sin)


def _own_2d(w_in, w_branch, w_out, conv_pw, conv_w):
    L = w_in.shape[0]
    cwp = jnp.zeros((L, CONV_PAD, conv_w.shape[2]), conv_w.dtype).at[:, :CONV_WIDTH].set(conv_w)
    return dict(w_in=w_in.reshape(-1, w_in.shape[-1]), w_branch=w_branch.reshape(-1, w_branch.shape[-1]),
                w_out=w_out.reshape(-1, w_out.shape[-1]), conv=_pack([conv_pw, cwp]))


def kernel(x, c, ctx, c_ctx, w_ada, b_ada, g_pre, g_post, w_in, na_rpb, pool_w, pool_scale, q_norm, k_norm, conv_w, conv_b, conv_ln_g, conv_ln_b, conv_pw, w_branch, w_out, loss_target, m_c_ctx, m_w_ada, m_b_ada, m_g_pre, m_g_post, m_w_in, m_na_rpb, m_pool_w, m_pool_scale, m_q_norm, m_k_norm, m_conv_w, m_conv_b, m_conv_ln_g, m_conv_ln_b, m_conv_pw, m_w_branch, m_w_out, v_c_ctx, v_w_ada, v_b_ada, v_g_pre, v_g_post, v_w_in, v_na_rpb, v_pool_w, v_pool_scale, v_q_norm, v_k_norm, v_conv_w, v_conv_b, v_conv_ln_g, v_conv_ln_b, v_conv_pw, v_w_branch, v_w_out):
    W = dict(c_ctx=c_ctx, w_ada=w_ada, b_ada=b_ada, g_pre=g_pre, g_post=g_post, w_in=w_in, na_rpb=na_rpb, pool_w=pool_w, pool_scale=pool_scale,
             q_norm=q_norm, k_norm=k_norm, conv_w=conv_w, conv_b=conv_b, conv_ln_g=conv_ln_g, conv_ln_b=conv_ln_b, conv_pw=conv_pw,
             w_branch=w_branch, w_out=w_out)
    Mo = dict(c_ctx=m_c_ctx, w_ada=m_w_ada, b_ada=m_b_ada, g_pre=m_g_pre, g_post=m_g_post, w_in=m_w_in, na_rpb=m_na_rpb, pool_w=m_pool_w,
              pool_scale=m_pool_scale, q_norm=m_q_norm, k_norm=m_k_norm, conv_w=m_conv_w, conv_b=m_conv_b, conv_ln_g=m_conv_ln_g,
              conv_ln_b=m_conv_ln_b, conv_pw=m_conv_pw, w_branch=m_w_branch, w_out=m_w_out)
    Vo = dict(c_ctx=v_c_ctx, w_ada=v_w_ada, b_ada=v_b_ada, g_pre=v_g_pre, g_post=v_g_post, w_in=v_w_in, na_rpb=v_na_rpb, pool_w=v_pool_w,
              pool_scale=v_pool_scale, q_norm=v_q_norm, k_norm=v_k_norm, conv_w=v_conv_w, conv_b=v_conv_b, conv_ln_g=v_conv_ln_g,
              conv_ln_b=v_conv_ln_b, conv_pw=v_conv_pw, w_branch=v_w_branch, w_out=v_w_out)

    S, D = x.shape[1], x.shape[2]
    Cn = ctx.shape[1]
    T = Cn + S
    L = w_in.shape[0]
    IN = w_in.shape[2] * N_DEV
    mx, my, mc = lax.axis_index("x"), lax.axis_index("y"), lax.axis_index("c")
    me = 4 * mx + 2 * my + mc
    tm = _pick(Cn, (256, 128))
    tme = _pick(T, (768, 256, 128))
    tmm = _pick(T, (768, 256, 128))
    tq = _pick(Cn, (256, 128))
    tk = _pick(T, (4224, 768, 384, 128))

    cg = allgather("gather_c", c)
    cvec = jnp.zeros((16, D), F32).at[0].set(c_ctx).at[1:1 + N_DEV].set(cg[:, 0])
    wc = w_ada.shape[2]
    b_sh = lax.dynamic_slice_in_dim(b_ada, me * wc, wc, axis=1)[:, None, :]
    modp = ada_fwd(cvec, w_ada, b_sh)
    modg = allgather("gather_mod", modp.reshape(L * 16, wc)).reshape(N_DEV, L, 16, wc)
    mod_full = jnp.transpose(modg, (1, 2, 0, 3)).reshape(L, 16, N_DEV * wc)
    mod2 = jnp.stack([mod_full[:, 0], lax.dynamic_index_in_dim(mod_full, 1 + me, axis=1, keepdims=False)], axis=1)
    shift, scale, gate = [mod2[:, :, None, k * D:(k + 1) * D] for k in range(3)]

    own2 = _own_2d(w_in, w_branch, w_out, conv_pw, conv_w)
    win_g = allgather("gather_w_in", own2["w_in"].astype(BF))
    wbr_g = allgather("gather_w_branch", own2["w_branch"].astype(BF))
    wout_g = allgather("gather_w_out", own2["w_out"].astype(BF))
    conv_g = allgather("gather_conv", own2["conv"].astype(BF)).reshape(N_DEV, -1)
    npw = L * (BRANCH_W // N_DEV) * BRANCH_W
    cpw_g = conv_g[:, :npw].reshape(N_DEV, L, BRANCH_W // N_DEV, BRANCH_W)
    cw_g = conv_g[:, npw:npw + L * CONV_PAD * (BRANCH_W // N_DEV)].reshape(N_DEV, L, CONV_PAD, BRANCH_W // N_DEV)
    Win = [cols_from_devs("w_in_cols", win_g, l, D) for l in range(L)]
    Wb = [cols_from_devs("w_branch_cols", wbr_g, l, 4 * BRANCH_W).reshape(4, BRANCH_W, D) for l in range(L)]
    Wo = [rows_from_devs("w_out_rows", wout_g, l, L) for l in range(L)]
    Cpw = [cpw_g[:, l].reshape(BRANCH_W, BRANCH_W).astype(F32) for l in range(L)]
    Cw = [jnp.transpose(cw_g[:, l], (1, 0, 2)).reshape(CONV_PAD, BRANCH_W).astype(F32) for l in range(L)]

    cos, sin = _rope_tables(S, Cn)
    tb = na_tables(S, Cn)
    bd = jnp.asarray(np.kron(np.eye(2, dtype=np.float32), np.full((64, 64), 1.0 / 64, np.float32)))
    row2 = lambda a: a.reshape(1, -1)

    X = jnp.concatenate([ctx[0], x[0]], axis=0)
    saved = []
    for l in range(L):
        gp = row2(g_pre[l])
        H = modulate_fwd(X, gp, shift[l], scale[l], Cn, tm)
        P = matmul("proj_in", H, Win[l], out_dtype=F32, tm=tmm, tn=_pick(IN, (1280, 1152, 768, 384, 128)), tk=D)
        bt = bias_table(na_rpb[l], tb)
        oa = na_fwd(P, bt, tb, Cn)
        ob = pool_fwd(P, pool_w[l], row2(pool_scale[l]), Cn)
        qn = row2(jnp.tile(q_norm[l], 2))
        kn = row2(jnp.tile(k_norm[l], 2))
        Qp = prep_q_fwd(P, cos, sin, qn, bd, Cn, tm)
        KV = prep_kv_fwd(P, cos, sin, kn, bd, Cn, tm)
        oc, LSE = flash_fwd(Qp, KV, Cn, tq, tk)
        Yc = conv_fwd(P, Cw[l], row2(conv_b[l]), Cn)
        od = conv_post_fwd(Yc, row2(conv_ln_g[l]), row2(conv_ln_b[l]), Cpw[l], Cn, tm)
        G = gate_outs_fwd(P, oa, ob, oc, od, Cn, tme)
        Y = branch_merge_fwd(P, G, Wb[l], tmm)
        Z = matmul("proj_out", Y, Wo[l], out_dtype=F32, tm=tmm, tn=D, tk=D)
        Xn = post_fwd(X, Z, row2(g_post[l]), gate[l], Cn, tm)
        saved.append(dict(X=X, H=H, P=P, bt=bt, oa=oa, ob=ob, oc=oc, od=od, Qp=Qp, KV=KV, LSE=LSE, Yc=Yc, G=G, Y=Y, Z=Z, qn=qn, kn=kn))
        X = Xn

    tgt = jnp.concatenate([jnp.zeros((Cn, D), F32), loss_target[0]], axis=0)
    dX, loss_acc = loss_call(X, tgt, Cn, tm)
    loss = lax.psum(loss_acc[0, 0], ("x", "y", "c"))

    gsm = {n: [None] * L for n in SMALL}
    gbig = {n: [None] * L for n in BIG}
    g_in = g_br = g_out = None
    dmod = [None] * L
    for l in reversed(range(L)):
        sv = saved[l]
        P = sv["P"]
        dZ, dgp, dgate = post_bwd(sv["Z"], dX, row2(g_post[l]), gate[l], Cn, tm)
        gsm["g_post"][l] = dgp[0]
        dY = matmul("proj_out_dy", dZ, Wo[l], cb=1, out_dtype=F32, tm=tmm, tn=D, tk=D)
        dWo = matmul("proj_out_dw", sv["Y"], dZ, ca=0, cb=0, tm=_pick(D, (1024, 512, 256)), tn=D, tk=tmm)
        g_out = rows_to_devs("w_out_devs", dWo, l, L, g_out)
        dU, dG, dP = branch_merge_bwd(P, sv["G"], Wb[l], dY, tmm)
        g_br = cols_to_devs("w_branch_devs", branch_dw(sv["G"], dU, tmm).reshape(4 * BRANCH_W, D), l, L, g_br)
        dO, dP = gate_outs_bwd(P, sv["oa"], sv["ob"], sv["oc"], sv["od"], dG, dP, Cn, tme)
        dYc, dlg, dlb, dcpw = conv_post_bwd(sv["Yc"], dO, row2(conv_ln_g[l]), row2(conv_ln_b[l]), Cpw[l], Cn, tm)
        gsm["conv_ln_g"][l], gsm["conv_ln_b"][l], gbig["conv_pw"][l] = dlg[0], dlb[0], dcpw
        dP, dGg, dcw, dcb = conv_bwd(P, Cw[l], dYc, dP, Cn)
        dP = copy_cols("copy_glu_gate", dGg, dP, CB["d_glu"] + 4, Cn, tme)
        gbig["conv_w"][l], gsm["conv_b"][l] = dcw, dcb[0]
        dQp, dKV = flash_bwd(sv["Qp"], sv["KV"], sv["oc"], sv["LSE"], dO, Cn, tq, tk)
        dP, dqn = prep_q_bwd(P, dQp, dP, cos, sin, sv["qn"], bd, Cn, tm)
        dP, dkn = prep_kv_bwd(P, dKV, dP, cos, sin, sv["kn"], bd, Cn, tm)
        gsm["q_norm"][l] = dqn[0, :64] + dqn[0, 64:]
        gsm["k_norm"][l] = dkn[0, :64] + dkn[0, 64:]
        dP, dpw, dps = pool_bwd(P, pool_w[l], row2(pool_scale[l]), dO, dP, Cn)
        gsm["pool_w"][l], gsm["pool_scale"][l] = dpw, dps[0]
        dQa, dKa, dVa, dbt = na_bwd(P, sv["bt"], dO, tb, Cn)
        dP = assemble_na(dKa, dVa, dQa, dP, Cn, tme)
        gsm["na_rpb"][l] = bias_table_grad(dbt, tb)
        dH = matmul("proj_in_dh", dP, Win[l], cb=1, out_dtype=F32, tm=tmm, tn=D, tk=_pick(IN, (1280, 1152, 768, 384, 128)))
        dWin = matmul("proj_in_dw", sv["H"], dP, ca=0, cb=0, tm=_pick(D, (1024, 512, 256)), tn=_pick(IN, (1280, 1152, 768, 384, 128)),
                      tk=_pick(T, (1408, 768, 384, 128)))
        g_in = cols_to_devs("w_in_devs", dWin, l, L, g_in)
        dX, dgpre, dsh, dsc = modulate_bwd(sv["X"], dH, dX, row2(g_pre[l]), shift[l], scale[l], Cn, tm)
        gsm["g_pre"][l] = dgpre[0]
        dmod[l] = jnp.concatenate([dsh[:, 0], dsc[:, 0], dgate[:, 0]], axis=1)
    grad_x = dX[Cn:][None]

    idx = jnp.stack([4 * _flip(mx, fx) + 2 * _flip(my, fy) + mc for fx, fy in _REL]).astype(jnp.int32)
    g_pw = jnp.transpose(jnp.stack(gbig["conv_pw"]).reshape(L, N_DEV, -1), (1, 0, 2)).reshape(N_DEV, -1)
    g_cw = jnp.transpose(jnp.stack(gbig["conv_w"]).reshape(L, CONV_PAD, N_DEV, BRANCH_W // N_DEV), (2, 0, 1, 3)).reshape(N_DEV, -1)
    g_conv = jnp.concatenate([g_pw, g_cw], axis=1)
    g_conv = jnp.pad(g_conv, ((0, 0), (0, own2["conv"].size - g_conv.shape[1]))).reshape(N_DEV, -1, LANES)
    gbufs = dict(w_in=g_in, w_branch=g_br, w_out=g_out, conv=g_conv)
    mom2 = _own_2d(Mo["w_in"], Mo["w_branch"], Mo["w_out"], Mo["conv_pw"], Mo["conv_w"])
    var2 = _own_2d(Vo["w_in"], Vo["w_branch"], Vo["w_out"], Vo["conv_pw"], Vo["conv_w"])
    big2 = {}
    for tag in ("w_in", "w_branch", "w_out", "conv"):
        R2, C2 = own2[tag].shape
        tr2 = _row_tile(R2, C2)
        recv1 = rs_exchange_sibling(tag, gbufs[tag])
        own, sb = rs_chip_sum(tag, gbufs[tag], recv1, idx, tr2)
        recv2 = rs_exchange_chips(tag, sb)
        big2[tag] = adamw("adamw_" + tag, [(own, None), (recv2, 1), (recv2, 2), (recv2, 3)], own2[tag], mom2[tag], var2[tag], tr2)
    cshapes = [(L, BRANCH_W // N_DEV, BRANCH_W), (L, CONV_PAD, BRANCH_W // N_DEV)]
    conv_unp = [_unpack(r, cshapes) for r in big2["conv"]]
    big_out = dict(w_in=[r.reshape(w_in.shape) for r in big2["w_in"]], w_branch=[r.reshape(w_branch.shape) for r in big2["w_branch"]],
                   w_out=[r.reshape(w_out.shape) for r in big2["w_out"]], conv_pw=[u[0] for u in conv_unp],
                   conv_w=[u[1][:, :CONV_WIDTH] for u in conv_unp])

    small_own = [jnp.stack(gsm[n]) for n in SMALL]
    small_shapes = [a.shape for a in small_own]
    dmod_all = jnp.stack(dmod)
    spack = _pack(small_own + [dmod_all])
    sg = allgather("gather_small", spack)
    Rs = spack.shape[0]
    nsmall = sum(int(np.prod(s)) for s in small_shapes)
    dmod_g = sg.reshape(N_DEV, -1)[:, nsmall:nsmall + dmod_all.size].reshape(N_DEV, L, 2, N_DEV, wc)
    dm_ctx = dmod_g[0, :, 0, :, :]
    for k in range(1, N_DEV):
        dm_ctx = dm_ctx + dmod_g[k, :, 0, :, :]
    dm_ctx = lax.dynamic_index_in_dim(dm_ctx, me, axis=1, keepdims=False)
    dm_b = jnp.transpose(lax.dynamic_index_in_dim(dmod_g[:, :, 1], me, axis=2, keepdims=False), (1, 0, 2))
    dmp = jnp.zeros((L, 16, wc), F32).at[:, 0].set(dm_ctx).at[:, 1:1 + N_DEV].set(dm_b)
    g_wada, dsilu = ada_bwd(cvec, w_ada, dmp)
    cpart = allgather("gather_cctx", dsilu[0:1])
    g_cctx = silu_grad(c_ctx[None], cpart)[0]

    smallw = _pack([W[n] for n in SMALL])
    smallm = _pack([Mo[n] for n in SMALL])
    smallv = _pack([Vo[n] for n in SMALL])
    Rsm = smallw.shape[0]
    small_res = adamw("adamw_small", [(sg[:, :Rsm], k) for k in range(N_DEV)], smallw, smallm, smallv, _row_tile(Rsm))
    small_unp = [_unpack(r, small_shapes) for r in small_res]

    db_parts = dmod_g.reshape(N_DEV, L, 2, N_DEV * wc)
    bshape = (L * N_DEV * wc // LANES, LANES)
    bparts = [(db_parts[k, :, r].reshape(bshape), None) for k in range(N_DEV) for r in range(2)]
    bada_res = adamw("adamw_bada", bparts, b_ada.reshape(bshape), Mo["b_ada"].reshape(bshape), Vo["b_ada"].reshape(bshape), _row_tile(bshape[0]))
    wada_shape = (w_ada.size // LANES, LANES)
    wada_res = adamw("adamw_wada", [(g_wada.reshape(wada_shape), None)], w_ada.reshape(wada_shape), Mo["w_ada"].reshape(wada_shape),
                     Vo["w_ada"].reshape(wada_shape), _row_tile(wada_shape[0]))
    cshape = (D // LANES, LANES)
    cctx_res = adamw("adamw_cctx", [(g_cctx.reshape(cshape), None)], c_ctx.reshape(cshape), Mo["c_ctx"].reshape(cshape), Vo["c_ctx"].reshape(cshape),
                     _row_tile(cshape[0]) if cshape[0] % 8 == 0 else cshape[0])

    res = {}
    for n in BIG:
        res[n] = big_out[n]
    for k, n in enumerate(SMALL):
        res[n] = [u[k] for u in small_unp]
    res["b_ada"] = [r.reshape(b_ada.shape) for r in bada_res]
    res["w_ada"] = [r.reshape(w_ada.shape) for r in wada_res]
    res["c_ctx"] = [r.reshape(c_ctx.shape) for r in cctx_res]
    outs = [loss, grad_x]
    for k in range(4):
        outs += [res[n][k] for n in WEIGHTS]
    return tuple(outs)
```

```python
import functools

import numpy as np
import jax
import jax.numpy as jnp
from jax import lax
from jax.experimental import pallas as pl
from jax.experimental.pallas import tpu as pltpu

F32 = jnp.float32
BF = jnp.bfloat16
HI = lax.Precision.HIGHEST

GRID_W = 64
BRANCH_W = 512
HEAD_DIM = 64
NA_WIN_ROWS = 8
NA_WIN_COLS = 16
NA_TILE_ROWS = 2
NA_BAND = NA_WIN_ROWS + NA_TILE_ROWS - 1
CONV_WIDTH = 31
CONV_PAD = 32
EPS = 1e-6
ROPE_THETA = 10000.0
NEG = -1e30
N_DEV = 8
LANES = 128
VMEM_LIMIT_BYTES = 56 * 1024 * 1024

ADAM_LR, ADAM_B1, ADAM_B2, ADAM_EPS, ADAM_WD, ADAM_STEP = 0.001, 0.9, 0.999, 1e-08, 0.01, 10

CB = dict(a_k=0, a_v=4, c_k=8, c_v=9, a_q=10, c_q=14, a_gate=18, b_in=22, b_gate=26, c_gate=30, d_glu=34, d_gate=42, merge=46)
KV_COLS = 1280
MESH = pl.DeviceIdType.MESH


def _pick(n, cands):
    for c in cands:
        if n % c == 0:
            return c
    raise ValueError(f"no tile for {n} among {cands}")


def _cp(*sem):
    return pltpu.CompilerParams(dimension_semantics=sem if sem else None, vmem_limit_bytes=VMEM_LIMIT_BYTES)


def _dg(x, y, cx, cy):
    return lax.dot_general(x.astype(BF), y.astype(BF), (((cx,), (cy,)), ((), ())), preferred_element_type=F32)


@functools.partial(jax.custom_vjp, nondiff_argnums=(2, 3))
def bdot(a, b, ca=1, cb=0):
    return _dg(a, b, ca, cb)


def _bdot_fwd(a, b, ca, cb):
    return _dg(a, b, ca, cb), (a, b)


def _bdot_bwd(ca, cb, res, g):
    a, b = res
    jb = 1 if cb == 0 else 0
    ia = 0 if ca == 1 else 1
    da = _dg(g, b, 1, jb) if ca == 1 else _dg(b, g, jb, 1)
    db = _dg(a, g, ia, 0) if cb == 0 else _dg(g, a, 0, ia)
    return da.astype(a.dtype), db.astype(b.dtype)


bdot.defvjp(_bdot_fwd, _bdot_bwd)


def hdot(a, b):
    return jnp.dot(a, b, precision=HI, preferred_element_type=F32)


@jax.custom_vjp
def swap64(x):
    return pltpu.roll(x, 64, 1)


swap64.defvjp(lambda x: (swap64(x), None), lambda _, g: (swap64(g),))


@jax.custom_vjp
def partner(x):
    w = x.shape[-1]
    lane = lax.broadcasted_iota(jnp.int32, x.shape, 1)
    return jnp.where((lane % 32) < 16, pltpu.roll(x, w - 16, 1), pltpu.roll(x, 16, 1))


partner.defvjp(lambda x: (partner(x), None), lambda _, g: (partner(g),))


@functools.partial(jax.custom_vjp, nondiff_argnums=(1,))
def split_rows(x, n):
    k = x.shape[0] // n
    return tuple(x[i * k:(i + 1) * k] for i in range(n))


split_rows.defvjp(lambda x, n: (split_rows(x, n), None), lambda n, _, gs: (jnp.concatenate(gs, axis=0),))


def _lane(shape):
    return lax.broadcasted_iota(jnp.int32, shape, len(shape) - 1)


def _silu(x):
    return x * jax.nn.sigmoid(x)


def _normrope(x, w, cos, sin, bd):
    ms = hdot(x * x, bd)
    y = x * lax.rsqrt(ms + EPS) * w
    return y * cos + partner(y) * sin


def matmul(name, a, b, *, ca=1, cb=0, out_dtype=F32, tm, tn, tk):
    M = a.shape[1 - ca]
    K = a.shape[ca]
    N = b.shape[1 - cb]
    assert b.shape[cb] == K and M % tm == 0 and N % tn == 0 and K % tk == 0, (name, a.shape, b.shape)
    nk = K // tk
    a_spec = pl.BlockSpec((tm, tk), lambda i, j, k: (i, k)) if ca == 1 else pl.BlockSpec((tk, tm), lambda i, j, k: (k, i))
    b_spec = pl.BlockSpec((tk, tn), lambda i, j, k: (k, j)) if cb == 0 else pl.BlockSpec((tn, tk), lambda i, j, k: (j, k))

    def kern(a_ref, b_ref, o_ref, *scr):
        part = _dg(a_ref[...], b_ref[...], ca, cb)
        if nk == 1:
            o_ref[...] = part.astype(out_dtype)
            return
        acc_ref, = scr
        k = pl.program_id(2)

        @pl.when(k == 0)
        def _():
            acc_ref[...] = part

        @pl.when(k > 0)
        def _():
            acc_ref[...] += part

        @pl.when(k == nk - 1)
        def _():
            o_ref[...] = acc_ref[...].astype(out_dtype)

    return pl.pallas_call(
        kern, grid=(M // tm, N // tn, nk), in_specs=[a_spec, b_spec],
        out_specs=pl.BlockSpec((tm, tn), lambda i, j, k: (i, j)),
        out_shape=jax.ShapeDtypeStruct((M, N), out_dtype),
        scratch_shapes=[] if nk == 1 else [pltpu.VMEM((tm, tn), F32)],
        compiler_params=_cp("parallel", "parallel", "arbitrary"), name=name,
    )(a, b)


def rowcall(name, body, T, tm, ncol, nctx, rows, consts=(), segs=(), outs=(), accs=(), saccs=()):
    nctxb = nctx // tm
    assert T % tm == 0 and (nctx % tm == 0 or not (segs or saccs))
    nr, nc, ns = len(rows), len(consts), len(segs)

    def seg_map(i, j):
        return (jnp.where(i < nctxb, 0, 1), 0, 0)

    def col_map(cf):
        return lambda i, j: (i, cf(j))

    in_specs, args = [], []
    for arr, bw, cf in rows:
        in_specs.append(pl.BlockSpec((tm, bw), col_map(cf)))
        args.append(arr)
    for arr in consts:
        in_specs.append(pl.BlockSpec(arr.shape, lambda i, j, nd=arr.ndim: (0,) * nd))
        args.append(arr)
    for arr in segs:
        in_specs.append(pl.BlockSpec((None, 1, arr.shape[-1]), seg_map))
        args.append(arr)
    out_shape, out_specs, aliases, out_dtypes = [], [], {}, []
    for o in outs:
        if o[0] == "into":
            _, arr, bw, cf = o
            aliases[len(args)] = len(out_shape)
            in_specs.append(pl.BlockSpec(memory_space=pl.ANY))
            args.append(arr)
            out_shape.append(jax.ShapeDtypeStruct(arr.shape, arr.dtype))
            out_dtypes.append(arr.dtype)
        else:
            cols, dt, bw, cf = o
            out_shape.append(jax.ShapeDtypeStruct((T, cols), dt))
            out_dtypes.append(dt)
        out_specs.append(pl.BlockSpec((tm, bw), col_map(cf)))
    for shp in accs:
        out_shape.append(jax.ShapeDtypeStruct(shp, F32))
        out_specs.append(pl.BlockSpec(shp, lambda i, j, nd=len(shp): (0,) * nd))
    for w in saccs:
        out_shape.append(jax.ShapeDtypeStruct((2, 1, w), F32))
        out_specs.append(pl.BlockSpec((None, 1, w), seg_map))
    n_in = len(args)
    no, na, nsa = len(outs), len(accs), len(saccs)

    def kern(*refs):
        i, j = pl.program_id(0), pl.program_id(1)
        ins = refs[:nr + nc + ns]
        orefs = refs[n_in:n_in + no]
        arefs = refs[n_in + no:n_in + no + na]
        srefs = refs[n_in + no + na:n_in + no + na + nsa]
        ov, av, sv = body(i, j, [r[...] for r in ins[:nr]], [r[...] for r in ins[nr:nr + nc]], [r[...] for r in ins[nr + nc:]])
        for r, v, dt in zip(orefs, ov, out_dtypes):
            r[...] = v.astype(dt)
        if na:
            @pl.when((i == 0) & (j == 0))
            def _():
                for r in arefs:
                    r[...] = jnp.zeros_like(r)
            for r, v in zip(arefs, av):
                r[...] += v
        if nsa:
            @pl.when(((i == 0) | (i == nctxb)) & (j == 0))
            def _():
                for r in srefs:
                    r[...] = jnp.zeros_like(r)
            for r, v in zip(srefs, sv):
                r[...] += v

    res = pl.pallas_call(
        kern, grid=(T // tm, ncol), in_specs=in_specs, out_specs=out_specs, out_shape=out_shape,
        input_output_aliases=aliases, compiler_params=_cp("arbitrary", "arbitrary"), name=name,
    )(*args)
    return res


def _c(k):
    return lambda j: k


def f_modulate(x, g, sh, sc):
    y = x * lax.rsqrt(jnp.mean(x * x, axis=-1, keepdims=True) + EPS) * g
    return y * (1.0 + sc) + sh


def modulate_fwd(X, g_pre, shift, scale, Cn, tm):
    T, D = X.shape

    def body(i, j, r, c, s):
        return [f_modulate(r[0], c[0], s[0], s[1])], [], []

    return rowcall("modulate_fwd", body, T, tm, 1, Cn, [(X, D, _c(0))], [g_pre], [shift, scale], [(D, BF, D, _c(0))])[0]


def modulate_bwd(X, dH, dXn, g_pre, shift, scale, Cn, tm):
    T, D = X.shape

    def body(i, j, r, c, s):
        _, vjp = jax.vjp(f_modulate, r[0], c[0], s[0], s[1])
        dx, dg, dsh, dsc = vjp(r[1])
        return [dx + r[2]], [dg], [dsh, dsc]

    return rowcall("modulate_bwd", body, T, tm, 1, Cn, [(X, D, _c(0)), (dH, D, _c(0)), (dXn, D, _c(0))], [g_pre], [shift, scale],
                   [(D, F32, D, _c(0))], [(1, D)], [D, D])


def f_post(z, gp, gate):
    return gate * (z * lax.rsqrt(jnp.mean(z * z, axis=-1, keepdims=True) + EPS) * gp)


def post_fwd(X, Z, g_post, gate, Cn, tm):
    T, D = X.shape

    def body(i, j, r, c, s):
        return [r[0] + f_post(r[1], c[0], s[0])], [], []

    return rowcall("post_fwd", body, T, tm, 1, Cn, [(X, D, _c(0)), (Z, D, _c(0))], [g_post], [gate], [(D, F32, D, _c(0))])[0]


def post_bwd(Z, dXn, g_post, gate, Cn, tm):
    T, D = Z.shape

    def body(i, j, r, c, s):
        _, vjp = jax.vjp(f_post, r[0], c[0], s[0])
        dz, dgp, dgate = vjp(r[1])
        return [dz], [dgp], [dgate]

    return rowcall("post_bwd", body, T, tm, 1, Cn, [(Z, D, _c(0)), (dXn, D, _c(0))], [g_post], [gate],
                   [(D, BF, D, _c(0))], [(1, D)], [D])


def loss_call(XL, tgt, Cn, tm):
    T, D = XL.shape
    nctxb = Cn // tm

    def body(i, j, r, c, s):
        diff = jnp.where(i >= nctxb, r[0] - r[1], 0.0)
        per_row = jnp.mean(diff * diff, axis=-1, keepdims=True)
        tot = 0.5 * jnp.sum(per_row, axis=0, keepdims=True)
        return [diff / D], [jnp.broadcast_to(tot, (1, LANES))], []

    return rowcall("loss", body, T, tm, 1, Cn, [(XL, D, _c(0)), (tgt, D, _c(0))], [], [], [(D, F32, D, _c(0))], [(1, LANES)])


def f_prep_q(x, qn, cos, sin, bd, jj):
    y = _normrope(x, qn, cos, sin, bd) * 0.125
    lane = _lane(y.shape)
    a = jnp.where(lane < 64, y, 0.0)
    b = jnp.where(lane >= 64, y, 0.0)
    g0 = jj < 2
    return jnp.concatenate([jnp.where(g0, a, swap64(a)), jnp.where(g0, swap64(b), b)], axis=1)


def prep_q_fwd(P, cos, sin, qn, bd, Cn, tm):
    T = P.shape[0]

    def body(i, j, r, c, s):
        return [f_prep_q(r[0], c[0], r[1], r[2], c[1], j)], [], []

    return rowcall("prep_q_fwd", body, T, tm, 4, Cn, [(P, 128, lambda j: CB["c_q"] + j), (cos, 128, _c(0)), (sin, 128, _c(0))], [qn, bd], [],
                   [(1024, BF, 256, lambda j: j)])[0]


def prep_q_bwd(P, dQ, dP, cos, sin, qn, bd, Cn, tm):
    T = P.shape[0]

    def body(i, j, r, c, s):
        _, vjp = jax.vjp(lambda x, w: f_prep_q(x, w, r[1], r[2], c[1], j), r[0], c[0])
        dx, dw = vjp(r[3])
        return [dx], [dw], []

    return rowcall("prep_q_bwd", body, T, tm, 4, Cn,
                   [(P, 128, lambda j: CB["c_q"] + j), (cos, 128, _c(0)), (sin, 128, _c(0)), (dQ, 256, lambda j: j)], [qn, bd], [],
                   [("into", dP, 128, lambda j: CB["c_q"] + j)], [(1, 128)])


def f_prep_kv(x, kn, cos, sin, bd, jj):
    return jnp.where(jj == 0, _normrope(x, kn, cos, sin, bd), x)


def prep_kv_fwd(P, cos, sin, kn, bd, Cn, tm):
    T = P.shape[0]

    def body(i, j, r, c, s):
        return [f_prep_kv(r[0], c[0], r[1], r[2], c[1], j)], [], []

    return rowcall("prep_kv_fwd", body, T, tm, 2, Cn, [(P, 128, lambda j: CB["c_k"] + j), (cos, 128, _c(0)), (sin, 128, _c(0))], [kn, bd], [],
                   [(256, BF, 128, lambda j: j)])[0]


def prep_kv_bwd(P, dKV, dP, cos, sin, kn, bd, Cn, tm):
    T = P.shape[0]

    def body(i, j, r, c, s):
        _, vjp = jax.vjp(lambda x, w: f_prep_kv(x, w, r[1], r[2], c[1], j), r[0], c[0])
        dx, dw = vjp(r[3])
        return [dx], [dw], []

    return rowcall("prep_kv_bwd", body, T, tm, 2, Cn,
                   [(P, 128, lambda j: CB["c_k"] + j), (cos, 128, _c(0)), (sin, 128, _c(0)), (dKV, 128, lambda j: j)], [kn, bd], [],
                   [("into", dP, 128, lambda j: CB["c_k"] + j)], [(1, 128)])


def f_conv_post(y, lg, lb, w):
    mu = jnp.mean(y, axis=-1, keepdims=True)
    var = jnp.mean(jnp.square(y - mu), axis=-1, keepdims=True)
    h = (y - mu) * lax.rsqrt(var + EPS) * lg + lb
    return bdot(_silu(h), w)


def conv_post_fwd(Yc, lg, lb, w, Cn, tm):
    T = Yc.shape[0]

    def body(i, j, r, c, s):
        return [f_conv_post(r[0], c[0], c[1], c[2])], [], []

    return rowcall("conv_post_fwd", body, T, tm, 1, Cn, [(Yc, 512, _c(0))], [lg, lb, w], [], [(512, F32, 512, _c(0))])[0]


def conv_post_bwd(Yc, dO, lg, lb, w, Cn, tm):
    T = Yc.shape[0]

    def body(i, j, r, c, s):
        _, vjp = jax.vjp(f_conv_post, r[0], c[0], c[1], c[2])
        dy, dlg, dlb, dw = vjp(r[1])
        return [dy], [dlg, dlb, dw], []

    return rowcall("conv_post_bwd", body, T, tm, 1, Cn, [(Yc, 512, _c(0)), (dO, 512, _c(3))], [lg, lb, w], [],
                   [(512, F32, 512, _c(0))], [(1, 512), (1, 512), (512, 512)])


def _gate_blk(j):
    b = j // 2
    base = jnp.where(b == 0, CB["a_gate"] // 2, jnp.where(b == 1, CB["b_gate"] // 2, jnp.where(b == 2, CB["c_gate"] // 2, CB["d_gate"] // 2)))
    return base + j % 2


def _sel4(j, vals):
    b = j // 2
    return jnp.where(b == 0, vals[0], jnp.where(b == 1, vals[1], jnp.where(b == 2, vals[2], vals[3])))


def gate_outs_fwd(P, oa, ob, oc, od, Cn, tm):
    T = P.shape[0]

    def body(i, j, r, c, s):
        return [_sel4(j, r[:4]) * _silu(r[4])], [], []

    half = lambda j: j % 2
    return rowcall("gate_outs_fwd", body, T, tm, 8, Cn,
                   [(oa, 256, half), (ob, 256, half), (oc, 256, half), (od, 256, half), (P, 256, _gate_blk)], [], [],
                   [(2048, BF, 256, lambda j: j)])[0]


def gate_outs_bwd(P, oa, ob, oc, od, dG, dP, Cn, tm):
    T = P.shape[0]

    def body(i, j, r, c, s):
        o = _sel4(j, r[:4])
        _, vjp = jax.vjp(lambda oo, gg: oo * _silu(gg), o, r[4])
        do, dg = vjp(r[5])
        return [do, dg], [], []

    half = lambda j: j % 2
    return rowcall("gate_outs_bwd", body, T, tm, 8, Cn,
                   [(oa, 256, half), (ob, 256, half), (oc, 256, half), (od, 256, half), (P, 256, _gate_blk), (dG, 256, lambda j: j)], [], [],
                   [(2048, F32, 256, lambda j: j), ("into", dP, 256, _gate_blk)])


def merge_fwd(P, U, Cn, tm):
    T = P.shape[0]
    D4 = U.shape[1]
    nb = D4 // 4 // 256
    mb = CB["merge"] // 2

    def body(i, j, r, c, s):
        y = 0.0
        for b in range(4):
            y = y + jax.nn.sigmoid(r[b]) * r[4 + b]
        return [y], [], []

    rows = [(P, 256, (lambda j, b=b: mb + b * nb + j)) for b in range(4)] + [(U, 256, (lambda j, b=b: b * nb + j)) for b in range(4)]
    return rowcall("merge_fwd", body, T, tm, nb, Cn, rows, [], [], [(D4 // 4, BF, 256, lambda j: j)])[0]


def merge_bwd(P, U, dY, Cn, tm):
    T, IN = P.shape
    D4 = U.shape[1]
    nb = D4 // 4 // 256
    mb = CB["merge"] // 2

    def body(i, j, r, c, s):
        _, vjp = jax.vjp(lambda m, u: jax.nn.sigmoid(m) * u, r[0], r[1])
        dm, du = vjp(r[2])
        return [du, dm], [], []

    return rowcall("merge_bwd", body, T, tm, 4 * nb, Cn,
                   [(P, 256, lambda j: mb + j), (U, 256, lambda j: j), (dY, 256, lambda j: j % nb)], [], [],
                   [(D4, BF, 256, lambda j: j), (IN, F32, 256, lambda j: mb + j)])


def assemble_na(dK, dV, dQ, dP, Cn, tm):
    T = dP.shape[0]

    def body(i, j, r, c, s):
        return [jnp.where(j < 4, r[0], jnp.where(j < 8, r[1], r[2]))], [], []

    return rowcall("assemble_na", body, T, tm, 12, Cn,
                   [(dK, 128, lambda j: jnp.minimum(j, 3)), (dV, 128, lambda j: jnp.clip(j - 4, 0, 3)), (dQ, 128, lambda j: jnp.clip(j - 8, 0, 3))],
                   [], [], [("into", dP, 128, lambda j: jnp.where(j < 8, j, j + 2))])[0]


def branch_fwd(G, Wb, tm):
    T = G.shape[0]
    D = Wb.shape[2]

    def kern(g_ref, w_ref, o_ref):
        o_ref[...] = _dg(g_ref[...], w_ref[...], 1, 0)

    return pl.pallas_call(
        kern, grid=(T // tm, 4),
        in_specs=[pl.BlockSpec((tm, 512), lambda i, b: (i, b)), pl.BlockSpec((None, 512, D), lambda i, b: (b, 0, 0))],
        out_specs=pl.BlockSpec((tm, D), lambda i, b: (i, b)), out_shape=jax.ShapeDtypeStruct((T, 4 * D), F32),
        compiler_params=_cp("parallel", "arbitrary"), name="branch_fwd")(G, Wb)


def branch_dg(dU, Wb, tm):
    T = dU.shape[0]
    D = Wb.shape[2]

    def kern(u_ref, w_ref, o_ref):
        o_ref[...] = _dg(u_ref[...], w_ref[...], 1, 1)

    return pl.pallas_call(
        kern, grid=(T // tm, 4),
        in_specs=[pl.BlockSpec((tm, D), lambda i, b: (i, b)), pl.BlockSpec((None, 512, D), lambda i, b: (b, 0, 0))],
        out_specs=pl.BlockSpec((tm, 512), lambda i, b: (i, b)), out_shape=jax.ShapeDtypeStruct((T, 2048), F32),
        compiler_params=_cp("parallel", "arbitrary"), name="branch_dg")(dU, Wb)


def branch_merge_fwd(P, G, Wb, tm):
    T = P.shape[0]
    D = Wb.shape[2]
    nj = D // 256
    mb = CB["merge"] // 2

    def kern(g_ref, w_ref, m0, m1, m2, m3, y_ref):
        y = None
        for b, m_ref in enumerate((m0, m1, m2, m3)):
            term = jax.nn.sigmoid(m_ref[...]) * _dg(g_ref[:, 512 * b:512 * (b + 1)], w_ref[b], 1, 0)
            y = term if y is None else y + term
        y_ref[...] = y.astype(BF)

    m_specs = [pl.BlockSpec((tm, 256), (lambda i, j, b=b: (i, mb + b * nj + j))) for b in range(4)]
    return pl.pallas_call(
        kern, grid=(T // tm, nj),
        in_specs=[pl.BlockSpec((tm, 2048), lambda i, j: (i, 0)), pl.BlockSpec((4, 512, 256), lambda i, j: (0, 0, j))] + m_specs,
        out_specs=pl.BlockSpec((tm, 256), lambda i, j: (i, j)), out_shape=jax.ShapeDtypeStruct((T, D), BF),
        compiler_params=_cp("parallel", "arbitrary"), name="branch_merge_fwd")(G, Wb, P, P, P, P)


def branch_merge_bwd(P, G, Wb, dY, tm):
    T, IN = P.shape
    D = Wb.shape[2]
    nj = D // 256
    mb = CB["merge"] // 2

    def kern(g_ref, w_ref, m_ref, dy_ref, du_ref, dg_ref, dm_ref, acc_s):
        j = pl.program_id(2)
        w = w_ref[...]
        u = _dg(g_ref[...], w, 1, 0)
        sig = jax.nn.sigmoid(m_ref[...])
        dy = dy_ref[...]
        dm_ref[...] = (dy * u * sig * (1.0 - sig)).astype(BF)
        du = dy * sig
        du_ref[...] = du.astype(BF)
        part = _dg(du, w, 1, 1)

        @pl.when(j == 0)
        def _():
            acc_s[...] = part

        @pl.when(j > 0)
        def _():
            acc_s[...] += part

        @pl.when(j == nj - 1)
        def _():
            dg_ref[...] = acc_s[...]

    return pl.pallas_call(
        kern, grid=(T // tm, 4, nj),
        in_specs=[pl.BlockSpec((tm, 512), lambda i, b, j: (i, b)), pl.BlockSpec((None, 512, 256), lambda i, b, j: (b, 0, j)),
                  pl.BlockSpec((tm, 256), lambda i, b, j: (i, mb + b * nj + j)), pl.BlockSpec((tm, 256), lambda i, b, j: (i, j))],
        out_specs=[pl.BlockSpec((tm, 256), lambda i, b, j: (i, b * nj + j)), pl.BlockSpec((tm, 512), lambda i, b, j: (i, b)),
                   pl.BlockSpec((tm, 256), lambda i, b, j: (i, mb + b * nj + j))],
        out_shape=[jax.ShapeDtypeStruct((T, 4 * D), BF), jax.ShapeDtypeStruct((T, 2048), F32), jax.ShapeDtypeStruct((T, IN), BF)],
        scratch_shapes=[pltpu.VMEM((tm, 512), F32)],
        compiler_params=_cp("parallel", "arbitrary", "arbitrary"), name="branch_merge_bwd")(G, Wb, P, dY)


def rows_from_devs(name, xg, l, L):
    _, Lr, C = xg.shape
    r = Lr // L

    def body(x_ref, o_ref):
        o_ref[...] = x_ref[...]

    return pl.pallas_call(
        body, grid=(N_DEV,), in_specs=[pl.BlockSpec((None, r, C), lambda d: (d, l, 0))], out_specs=pl.BlockSpec((r, C), lambda d: (d, 0)),
        out_shape=jax.ShapeDtypeStruct((N_DEV * r, C), xg.dtype), compiler_params=_cp("parallel"), name=name)(xg)


def rows_to_devs(name, g, l, L, prev):
    R8, C = g.shape
    r = R8 // N_DEV

    def body(x_ref, *rest):
        rest[-1][...] = x_ref[...]

    in_specs, args, aliases = [pl.BlockSpec((r, C), lambda d: (d, 0))], [g], {}
    if prev is not None:
        in_specs.append(pl.BlockSpec(memory_space=pl.ANY))
        args.append(prev)
        aliases = {1: 0}
    return pl.pallas_call(
        body, grid=(N_DEV,), in_specs=in_specs, out_specs=pl.BlockSpec((None, r, C), lambda d: (d, l, 0)),
        out_shape=jax.ShapeDtypeStruct((N_DEV, L * r, C), g.dtype), input_output_aliases=aliases,
        compiler_params=_cp("parallel"), name=name)(*args)


def branch_dw(G, dU, tk):
    T = G.shape[0]
    D = dU.shape[1] // 4
    nk = T // tk

    def kern(g_ref, u_ref, o_ref):
        k = pl.program_id(1)
        part = _dg(g_ref[...], u_ref[...], 0, 0)

        @pl.when(k == 0)
        def _():
            o_ref[...] = part

        @pl.when(k > 0)
        def _():
            o_ref[...] += part

    return pl.pallas_call(
        kern, grid=(4, nk),
        in_specs=[pl.BlockSpec((tk, 512), lambda b, k: (k, b)), pl.BlockSpec((tk, D), lambda b, k: (k, b))],
        out_specs=pl.BlockSpec((None, 512, D), lambda b, k: (b, 0, 0)), out_shape=jax.ShapeDtypeStruct((4, 512, D), F32),
        compiler_params=_cp("parallel", "arbitrary"), name="branch_dw")(G, dU)


def na_tables(S, Cn):
    R = S // GRID_W
    ntile = (Cn + S) // 128
    nct = Cn // 128
    rs = np.clip(np.arange(R) - NA_WIN_ROWS // 2, 0, R - NA_WIN_ROWS)
    seen, pats, cls, ws = {}, [], [], []
    for t in range(ntile):
        pat = -np.ones((NA_TILE_ROWS, NA_BAND), np.int64)
        w0 = 0
        if t >= nct:
            r0 = NA_TILE_ROWS * (t - nct)
            w0 = min(rs[r0], R - NA_BAND)
            for a in range(NA_TILE_ROWS):
                for j in range(NA_BAND):
                    kr = w0 + j
                    if rs[r0 + a] <= kr < rs[r0 + a] + NA_WIN_ROWS:
                        pat[a, j] = kr - (r0 + a) + NA_WIN_ROWS - 1
        key = pat.tobytes()
        if key not in seen:
            seen[key] = len(pats)
            pats.append(pat)
        cls.append(seen[key])
        ws.append(Cn + GRID_W * int(w0))
    cls = np.asarray(cls, np.int32)
    first = np.asarray([1 if t == 0 or cls[t] != cls[t - 1] else 0 for t in range(ntile)], np.int32)
    assert len(set(cls[first == 1].tolist())) == int(first.sum())
    pats = np.stack(pats)
    ncls = pats.shape[0]
    nrow = -(-(ncls * NA_TILE_ROWS * NA_BAND) // 128) * 128
    m1 = np.zeros((nrow, 128), np.float32)
    flat = pats.reshape(-1)
    for k, dr in enumerate(flat):
        if dr >= 0:
            m1[k, dr] = 1.0
    qc = np.arange(GRID_W)
    col_start = np.clip(qc - NA_WIN_COLS // 2, 0, GRID_W - NA_WIN_COLS)
    kc = np.arange(GRID_W)
    col_ok = (kc[None, :] >= col_start[:, None]) & (kc[None, :] < col_start[:, None] + NA_WIN_COLS)
    m2 = np.zeros((128, GRID_W * GRID_W), np.float32)
    for q in range(GRID_W):
        for k in range(GRID_W):
            if col_ok[q, k]:
                m2[k - q + NA_WIN_COLS - 1, q * GRID_W + k] = 1.0
    valid = (pats >= 0)[:, :, :, None, None] & col_ok[None, None, None]
    vmask = np.transpose(valid, (0, 2, 4, 1, 3)).reshape(ncls, 1, NA_BAND * GRID_W, 1, NA_TILE_ROWS * GRID_W)
    vmask = np.broadcast_to(vmask, (ncls, 1, NA_BAND * GRID_W, 2, NA_TILE_ROWS * GRID_W)).reshape(ncls, 1, NA_BAND * GRID_W, 256)
    return dict(cls=cls, ws=np.asarray(ws, np.int32), first=first, m1=m1, m2=m2, vmask=vmask, ncls=ncls, nrow=nrow)


def rpb_map(name, x, left, right):
    H = x.shape[0]

    def kern(x_ref, l_ref, r_ref, o_ref):
        o_ref[...] = hdot(hdot(l_ref[...], x_ref[...]), r_ref[...])

    return pl.pallas_call(
        kern, grid=(H,),
        in_specs=[pl.BlockSpec((None,) + x.shape[1:], lambda h: (h, 0, 0)), pl.BlockSpec(left.shape, lambda h: (0, 0)),
                  pl.BlockSpec(right.shape, lambda h: (0, 0))],
        out_specs=pl.BlockSpec((None, left.shape[0], right.shape[1]), lambda h: (h, 0, 0)),
        out_shape=jax.ShapeDtypeStruct((H, left.shape[0], right.shape[1]), F32), compiler_params=_cp("parallel"), name=name)(x, left, right)


def bias_table(rpb, tb):
    H = rpb.shape[0]
    xp = jnp.zeros((H, 128, 128), F32).at[:, :rpb.shape[1], :rpb.shape[2]].set(rpb)
    a = rpb_map("rpb_expand", xp, jnp.asarray(tb["m1"]), jnp.asarray(tb["m2"]))
    ncls = tb["ncls"]
    a = a[:, :ncls * NA_TILE_ROWS * NA_BAND].reshape(H // 2, 2, ncls, NA_TILE_ROWS, NA_BAND, GRID_W, GRID_W)
    a = jnp.transpose(a, (2, 0, 4, 6, 1, 3, 5)).reshape(ncls, H // 2, NA_BAND * GRID_W, 256)
    return jnp.where(jnp.asarray(tb["vmask"]), a, NEG)


def bias_table_grad(dbt, tb):
    ncls = tb["ncls"]
    H2 = dbt.shape[1]
    d = dbt.reshape(ncls, H2, NA_BAND, GRID_W, 2, NA_TILE_ROWS, GRID_W)
    d = jnp.transpose(d, (1, 4, 0, 5, 2, 6, 3)).reshape(H2 * 2, ncls * NA_TILE_ROWS * NA_BAND, GRID_W * GRID_W)
    d = jnp.pad(d, ((0, 0), (0, tb["nrow"] - d.shape[1]), (0, 0)))
    g = rpb_map("rpb_reduce", d, jnp.asarray(tb["m1"].T.copy()), jnp.asarray(tb["m2"].T.copy()))
    return g[:, :2 * NA_WIN_ROWS - 1, :2 * NA_WIN_COLS - 1]


def f_na(q, kb, vb, kc, vc, bias_t):
    d = lax.broadcasted_iota(jnp.int32, (128, 1), 0)
    q_t = q.T
    qbd = jnp.concatenate([jnp.where(d < 64, q_t, 0.0), jnp.where(d >= 64, q_t, 0.0)], axis=1)
    sb = bdot(kb, qbd) * 0.125 + bias_t
    sc = bdot(kc, qbd) * 0.125
    m = jnp.maximum(jnp.max(sb, axis=0, keepdims=True), jnp.max(sc, axis=0, keepdims=True))
    eb = jnp.exp(sb - m)
    ec = jnp.exp(sc - m)
    den = jnp.sum(eb, axis=0, keepdims=True) + jnp.sum(ec, axis=0, keepdims=True)
    of = bdot(eb / den, vb, 0, 0) + bdot(ec / den, vc, 0, 0)
    o0, o1 = split_rows(of, 2)
    return jnp.where(_lane(o0.shape) < 64, o0, o1)


def _na_specs(T, Cn, nband):
    qs = pl.BlockSpec((128, 128), lambda hp, t, *_: (t, CB["a_q"] + hp))
    ks = pl.BlockSpec((T, 128), lambda hp, t, *_: (0, CB["a_k"] + hp))
    vs = pl.BlockSpec((T, 128), lambda hp, t, *_: (0, CB["a_v"] + hp))
    bs = pl.BlockSpec((None, None, nband, 256), lambda hp, t, cls, ws, first: (cls[t], hp, 0, 0))
    return qs, ks, vs, bs


def na_fwd(P, bt, tb, Cn):
    T = P.shape[0]
    nband = NA_BAND * GRID_W
    qs, ks, vs, bs = _na_specs(T, Cn, nband)

    def kern(cls, ws, first, q_ref, k_ref, v_ref, b_ref, o_ref):
        w0 = pl.multiple_of(ws[pl.program_id(1)], 64)
        o_ref[...] = f_na(q_ref[...], k_ref[pl.ds(w0, nband), :], v_ref[pl.ds(w0, nband), :], k_ref[0:Cn, :], v_ref[0:Cn, :], b_ref[...])

    return pl.pallas_call(
        kern, grid_spec=pltpu.PrefetchScalarGridSpec(
            num_scalar_prefetch=3, grid=(4, T // 128), in_specs=[qs, ks, vs, bs],
            out_specs=pl.BlockSpec((128, 128), lambda hp, t, *_: (t, hp))),
        out_shape=jax.ShapeDtypeStruct((T, 512), F32), compiler_params=_cp("arbitrary", "arbitrary"), name="na_fwd",
    )(jnp.asarray(tb["cls"]), jnp.asarray(tb["ws"]), jnp.asarray(tb["first"]), P, P, P, bt)


def na_bwd(P, bt, dO, tb, Cn):
    T = P.shape[0]
    nband = NA_BAND * GRID_W
    qs, ks, vs, bs = _na_specs(T, Cn, nband)
    dos = pl.BlockSpec((128, 128), lambda hp, t, *_: (t, hp))

    def kern(cls, ws, first, q_ref, k_ref, v_ref, b_ref, do_ref, dq_ref, dk_ref, dv_ref, db_ref):
        t = pl.program_id(1)
        w0 = pl.multiple_of(ws[t], 64)
        band = pl.ds(w0, nband)
        _, vjp = jax.vjp(f_na, q_ref[...], k_ref[band, :], v_ref[band, :], k_ref[0:Cn, :], v_ref[0:Cn, :], b_ref[...])
        dq, dkb, dvb, dkc, dvc, db = vjp(do_ref[...])
        dq_ref[...] = dq

        @pl.when(t == 0)
        def _():
            dk_ref[...] = jnp.zeros_like(dk_ref)
            dv_ref[...] = jnp.zeros_like(dv_ref)

        dk_ref[band, :] += dkb
        dv_ref[band, :] += dvb
        dk_ref[0:Cn, :] += dkc
        dv_ref[0:Cn, :] += dvc

        @pl.when(first[t] == 1)
        def _():
            db_ref[...] = db

        @pl.when(first[t] == 0)
        def _():
            db_ref[...] += db

    full = lambda hp, t, *_: (0, hp)
    return pl.pallas_call(
        kern, grid_spec=pltpu.PrefetchScalarGridSpec(
            num_scalar_prefetch=3, grid=(4, T // 128), in_specs=[qs, ks, vs, bs, dos],
            out_specs=[pl.BlockSpec((128, 128), lambda hp, t, *_: (t, hp)), pl.BlockSpec((T, 128), full), pl.BlockSpec((T, 128), full),
                       pl.BlockSpec((None, None, nband, 256), lambda hp, t, cls, ws, first: (cls[t], hp, 0, 0))]),
        out_shape=[jax.ShapeDtypeStruct((T, 512), F32), jax.ShapeDtypeStruct((T, 512), F32), jax.ShapeDtypeStruct((T, 512), F32),
                   jax.ShapeDtypeStruct(bt.shape, F32)],
        compiler_params=_cp("arbitrary", "arbitrary"), name="na_bwd",
    )(jnp.asarray(tb["cls"]), jnp.asarray(tb["ws"]), jnp.asarray(tb["first"]), P, P, P, bt, dO)


def _expand_heads(blk, g):
    out = []
    for p in range(2):
        pair = blk[:, 128 * p:128 * (p + 1)]
        lane = _lane(pair.shape)
        a = jnp.where(lane < 64, pair, 0.0)
        b = jnp.where(lane >= 64, pair, 0.0)
        out.append(jnp.where(g == 0, a, pltpu.roll(a, 64, 1)))
        out.append(jnp.where(g == 0, pltpu.roll(b, 64, 1), b))
    return out


def _compact_heads(hs, g):
    out = []
    for p in range(2):
        e, o = hs[2 * p], hs[2 * p + 1]
        lane = _lane(e.shape)
        e0 = jnp.where(g == 0, e, pltpu.roll(e, 64, 1))
        o1 = jnp.where(g == 0, pltpu.roll(o, 64, 1), o)
        out.append(jnp.where(lane < 64, e0, o1))
    return jnp.concatenate(out, axis=1)


def _kv_map(Cn, tq, tk, col):
    nq_ctx = Cn // tq
    last_ctx = (Cn - 1) // tk

    def f(g, qi, kj):
        return (jnp.where(qi < nq_ctx, jnp.minimum(kj, last_ctx), kj), col)
    return f


def flash_fwd(Qp, KV, Cn, tq, tk):
    T = Qp.shape[0]
    nq, nk = T // tq, T // tk
    nq_ctx = Cn // tq

    def kern(q_ref, k_ref, v_ref, o_ref, lse_ref, m_s, l_s, acc_s):
        g, qi, kj = pl.program_id(0), pl.program_id(1), pl.program_id(2)

        @pl.when(kj == 0)
        def _():
            m_s[...] = jnp.full_like(m_s, NEG)
            l_s[...] = jnp.zeros_like(l_s)
            acc_s[...] = jnp.zeros_like(acc_s)

        def step(masked):
            k = k_ref[...]
            v = v_ref[...]
            if masked:
                valid = kj * tk + lax.broadcasted_iota(jnp.int32, (1, tk), 1) < Cn
            for h in range(4):
                rows = pl.ds(tq * h, tq)
                s = _dg(q_ref[:, 128 * h:128 * (h + 1)], k, 1, 1)
                if masked:
                    s = jnp.where(valid, s, NEG)
                m_old = m_s[rows, :]
                m_new = jnp.maximum(m_old, jnp.max(s, axis=1, keepdims=True))
                alpha = jnp.exp(m_old - m_new)
                p = jnp.exp(s - m_new)
                l_s[rows, :] = alpha * l_s[rows, :] + jnp.sum(p, axis=1, keepdims=True)
                acc_s[rows, :] = alpha * acc_s[rows, :] + _dg(p, v, 1, 0)
                m_s[rows, :] = m_new

        @pl.when(qi >= nq_ctx)
        def _():
            step(False)

        @pl.when((qi < nq_ctx) & (kj * tk < Cn))
        def _():
            step(True)

        @pl.when(kj == nk - 1)
        def _():
            o4 = acc_s[...] / l_s[...]
            o_ref[...] = _compact_heads([o4[tq * h:tq * (h + 1)] for h in range(4)], g)
            lse = m_s[...] + jnp.log(l_s[...])
            lse_ref[...] = jnp.concatenate([jnp.broadcast_to(lse[tq * h:tq * (h + 1)], (tq, 128)) for h in range(4)], axis=1)

    return pl.pallas_call(
        kern, grid=(2, nq, nk),
        in_specs=[pl.BlockSpec((tq, 512), lambda g, qi, kj: (qi, g)), pl.BlockSpec((tk, 128), _kv_map(Cn, tq, tk, 0)),
                  pl.BlockSpec((tk, 128), _kv_map(Cn, tq, tk, 1))],
        out_specs=[pl.BlockSpec((tq, 256), lambda g, qi, kj: (qi, g)), pl.BlockSpec((tq, 512), lambda g, qi, kj: (qi, g))],
        out_shape=[jax.ShapeDtypeStruct((T, 512), F32), jax.ShapeDtypeStruct((T, 1024), F32)],
        scratch_shapes=[pltpu.VMEM((4 * tq, 1), F32), pltpu.VMEM((4 * tq, 1), F32), pltpu.VMEM((4 * tq, 128), F32)],
        compiler_params=_cp("arbitrary", "arbitrary", "arbitrary"), name="flash_fwd")(Qp, KV, KV)


def flash_bwd(Qp, KV, O, LSE, dOall, Cn, tq, tk):
    T = Qp.shape[0]
    nq, nk = T // tq, T // tk
    nq_ctx = Cn // tq

    def kern(q_ref, k_ref, v_ref, o_ref, lse_ref, do_ref, dq_ref, dkv_ref, dq_s, do_s, dl_s, ls_s):
        g, qi, kj = pl.program_id(0), pl.program_id(1), pl.program_id(2)

        @pl.when((g == 0) & (qi == 0) & (kj == 0))
        def _():
            dkv_ref[...] = jnp.zeros_like(dkv_ref)

        @pl.when(kj == 0)
        def _():
            do4 = jnp.concatenate(_expand_heads(do_ref[...], g), axis=0)
            o4 = jnp.concatenate(_expand_heads(o_ref[...], g), axis=0)
            do_s[...] = do4.astype(BF)
            dl_s[...] = jnp.sum(do4 * o4, axis=1, keepdims=True)
            ls_s[...] = jnp.concatenate([jnp.max(lse_ref[:, 128 * h:128 * (h + 1)], axis=1, keepdims=True) for h in range(4)], axis=0)
            dq_s[...] = jnp.zeros_like(dq_s)

        def step(masked):
            k = k_ref[...]
            v = v_ref[...]
            if masked:
                valid = kj * tk + lax.broadcasted_iota(jnp.int32, (1, tk), 1) < Cn
            dk = dv = None
            for h in range(4):
                rows = pl.ds(tq * h, tq)
                qh = q_ref[:, 128 * h:128 * (h + 1)]
                s = _dg(qh, k, 1, 1)
                if masked:
                    s = jnp.where(valid, s, NEG)
                p = jnp.exp(s - ls_s[rows, :])
                doh = do_s[rows, :]
                dvh = _dg(p, doh, 0, 0)
                dp = _dg(doh, v, 1, 1)
                ds = p * (dp - dl_s[rows, :])
                dq_s[rows, :] += _dg(ds, k, 1, 0)
                dkh = _dg(ds, qh, 0, 0)
                dk = dkh if dk is None else dk + dkh
                dv = dvh if dv is None else dv + dvh
            krows = pl.ds(pl.multiple_of(kj * tk, tk), tk)
            dkv_ref[krows, 0:128] += dk
            dkv_ref[krows, 128:256] += dv

        @pl.when(qi >= nq_ctx)
        def _():
            step(False)

        @pl.when((qi < nq_ctx) & (kj * tk < Cn))
        def _():
            step(True)

        @pl.when(kj == nk - 1)
        def _():
            dq = dq_s[...]
            dq_ref[...] = jnp.concatenate([dq[tq * h:tq * (h + 1)] for h in range(4)], axis=1)

    return pl.pallas_call(
        kern, grid=(2, nq, nk),
        in_specs=[pl.BlockSpec((tq, 512), lambda g, qi, kj: (qi, g)), pl.BlockSpec((tk, 128), _kv_map(Cn, tq, tk, 0)),
                  pl.BlockSpec((tk, 128), _kv_map(Cn, tq, tk, 1)), pl.BlockSpec((tq, 256), lambda g, qi, kj: (qi, g)),
                  pl.BlockSpec((tq, 512), lambda g, qi, kj: (qi, g)), pl.BlockSpec((tq, 256), lambda g, qi, kj: (qi, 4 + g))],
        out_specs=[pl.BlockSpec((tq, 512), lambda g, qi, kj: (qi, g)), pl.BlockSpec((T, 256), lambda g, qi, kj: (0, 0))],
        out_shape=[jax.ShapeDtypeStruct((T, 1024), F32), jax.ShapeDtypeStruct((T, 256), F32)],
        scratch_shapes=[pltpu.VMEM((4 * tq, 128), F32), pltpu.VMEM((4 * tq, 128), BF), pltpu.VMEM((4 * tq, 1), F32), pltpu.VMEM((4 * tq, 1), F32)],
        compiler_params=_cp("arbitrary", "arbitrary", "arbitrary"), name="flash_bwd")(Qp, KV, KV, O, LSE, dOall)


def _pool_band(t0, w0, n, win, gi, Cn, T):
    i = lax.broadcasted_iota(jnp.int32, (n, win), 0)
    jx = lax.broadcasted_iota(jnp.int32, (n, win), 1)
    t = t0 + i
    tp = w0 + jx
    half = lax.shift_left(jnp.int32(1), gi)
    lo = jnp.maximum(t - half, jnp.where(t < Cn, 0, Cn))
    hi = jnp.minimum(t + half - 1, jnp.where(t < Cn, Cn, T) - 1)
    cnt = (hi - lo + 1).astype(F32)
    return jnp.where((tp >= lo) & (tp <= hi), 1.0 / cnt, 0.0) - jnp.where(tp == t, 1.0, 0.0)


def _pool_geom(T):
    n = _pick(T, (256, 128))
    return n, n + 128


def pool_fwd(P, pw, ps, Cn):
    T = P.shape[0]
    n, win = _pool_geom(T)

    def kern(u_ref, w_ref, s_ref, o_ref):
        gi = pl.program_id(0)

        def blk(b, carry):
            t0 = pl.multiple_of(b * n, n)
            w0 = pl.multiple_of(jnp.clip(t0 - 64, 0, T - win), 64)
            d = hdot(_pool_band(t0, w0, n, win, gi, Cn, T), u_ref[pl.ds(w0, win), :])
            o_ref[pl.ds(t0, n), :] = bdot(d, w_ref[...]) * s_ref[...]
            return carry

        lax.fori_loop(0, T // n, blk, 0)

    return pl.pallas_call(
        kern, grid=(4,),
        in_specs=[pl.BlockSpec((T, 128), lambda g: (0, CB["b_in"] + g)), pl.BlockSpec((None, 128, 128), lambda g: (g, 0, 0)),
                  pl.BlockSpec((1, 128), lambda g: (0, g))],
        out_specs=pl.BlockSpec((T, 128), lambda g: (0, g)), out_shape=jax.ShapeDtypeStruct((T, 512), F32),
        compiler_params=_cp("arbitrary"), name="pool_fwd")(P, pw, ps)


def pool_bwd(P, pw, ps, dOall, dP, Cn):
    T = P.shape[0]
    n, win = _pool_geom(T)

    def kern(u_ref, w_ref, s_ref, do_ref, dp_in, du_out, dw_ref, ds_ref, du_ref):
        gi = pl.program_id(0)
        du_ref[...] = jnp.zeros_like(du_ref)
        dw_ref[...] = jnp.zeros_like(dw_ref)
        ds_ref[...] = jnp.zeros_like(ds_ref)

        def blk(b, carry):
            t0 = pl.multiple_of(b * n, n)
            w0 = pl.multiple_of(jnp.clip(t0 - 64, 0, T - win), 64)
            band = _pool_band(t0, w0, n, win, gi, Cn, T)
            _, vjp = jax.vjp(lambda uw, w, s: bdot(hdot(band, uw), w) * s, u_ref[pl.ds(w0, win), :], w_ref[...], s_ref[...])
            duw, dw, ds = vjp(do_ref[pl.ds(t0, n), :])
            du_ref[pl.ds(w0, win), :] += duw
            dw_ref[...] += dw
            ds_ref[...] += ds
            return carry

        lax.fori_loop(0, T // n, blk, 0)
        du_out[...] = du_ref[...].astype(du_out.dtype)

    return pl.pallas_call(
        kern, grid=(4,),
        in_specs=[pl.BlockSpec((T, 128), lambda g: (0, CB["b_in"] + g)), pl.BlockSpec((None, 128, 128), lambda g: (g, 0, 0)),
                  pl.BlockSpec((1, 128), lambda g: (0, g)), pl.BlockSpec((T, 128), lambda g: (0, 4 + g)), pl.BlockSpec(memory_space=pl.ANY)],
        out_specs=[pl.BlockSpec((T, 128), lambda g: (0, CB["b_in"] + g)), pl.BlockSpec((None, 128, 128), lambda g: (g, 0, 0)),
                   pl.BlockSpec((1, 128), lambda g: (0, g))],
        out_shape=[jax.ShapeDtypeStruct(dP.shape, dP.dtype), jax.ShapeDtypeStruct((4, 128, 128), F32), jax.ShapeDtypeStruct((1, 512), F32)],
        scratch_shapes=[pltpu.VMEM((T, 128), F32)],
        input_output_aliases={4: 0}, compiler_params=_cp("arbitrary"), name="pool_bwd")(P, pw, ps, dOall, dP)


CONV_HALO = 16


def _conv_block(T):
    return _pick(T, (256, 128))


def _stage(dst_ref, src, t0, n, T, Cn):
    h = CONV_HALO
    dst_ref[h:h + n, :] = src(pl.ds(t0, n))
    left_ok = (t0 != 0) & (t0 != Cn)
    right_ok = (t0 + n != Cn) & (t0 + n != T)
    lo = pl.multiple_of(jnp.maximum(t0 - h, 0), 8)
    hi = pl.multiple_of(jnp.minimum(t0 + n, T - h), 8)
    dst_ref[0:h, :] = jnp.where(left_ok, src(pl.ds(lo, h)), 0.0)
    dst_ref[h + n:2 * h + n, :] = jnp.where(right_ok, src(pl.ds(hi, h)), 0.0)


def conv_fwd(P, cw, cb, Cn):
    T = P.shape[0]
    n = _conv_block(T)
    off = CONV_HALO - CONV_WIDTH // 2

    def kern(a_ref, g_ref, w_ref, b_ref, y_ref, us):
        def blk(b, carry):
            t0 = pl.multiple_of(b * n, n)
            _stage(us, lambda r: a_ref[r, :] * jax.nn.sigmoid(g_ref[r, :]), t0, n, T, Cn)
            acc = jnp.zeros((n, 128), F32)
            for k in range(CONV_WIDTH):
                acc = acc + us[k + off:k + off + n, :] * w_ref[k:k + 1, :]
            y_ref[pl.ds(t0, n), :] = acc + b_ref[...]
            return carry

        lax.fori_loop(0, T // n, blk, 0)

    slab = lambda o: pl.BlockSpec((T, 128), lambda c: (0, o + c))
    return pl.pallas_call(
        kern, grid=(4,),
        in_specs=[slab(CB["d_glu"]), slab(CB["d_glu"] + 4), pl.BlockSpec((CONV_PAD, 128), lambda c: (0, c)), pl.BlockSpec((1, 128), lambda c: (0, c))],
        out_specs=pl.BlockSpec((T, 128), lambda c: (0, c)), out_shape=jax.ShapeDtypeStruct((T, 512), F32),
        scratch_shapes=[pltpu.VMEM((n + 2 * CONV_HALO, 128), F32)],
        compiler_params=_cp("arbitrary"), name="conv_fwd")(P, P, cw, cb)


def conv_bwd(P, cw, dY, dP, Cn):
    T = P.shape[0]
    n = _conv_block(T)
    off = CONV_HALO - CONV_WIDTH // 2
    back = CONV_HALO + CONV_WIDTH // 2

    def kern(a_ref, g_ref, w_ref, dy_ref, dp_in, da_ref, dg_ref, dw_ref, db_ref, us, dys):
        dw_ref[...] = jnp.zeros_like(dw_ref)
        db_ref[...] = jnp.zeros_like(db_ref)

        def blk(b, carry):
            t0 = pl.multiple_of(b * n, n)
            cur = pl.ds(t0, n)
            _stage(us, lambda r: a_ref[r, :] * jax.nn.sigmoid(g_ref[r, :]), t0, n, T, Cn)
            _stage(dys, lambda r: dy_ref[r, :], t0, n, T, Cn)
            dyc = dy_ref[cur, :]
            du = jnp.zeros((n, 128), F32)
            for k in range(CONV_WIDTH):
                du = du + dys[back - k:back - k + n, :] * w_ref[k:k + 1, :]
                dw_ref[k:k + 1, :] += jnp.sum(us[k + off:k + off + n, :] * dyc, axis=0, keepdims=True)
            a = a_ref[cur, :]
            sig = jax.nn.sigmoid(g_ref[cur, :])
            da_ref[cur, :] = (du * sig).astype(da_ref.dtype)
            dg_ref[cur, :] = du * a * sig * (1.0 - sig)
            db_ref[...] += jnp.sum(dyc, axis=0, keepdims=True)
            return carry

        lax.fori_loop(0, T // n, blk, 0)

    slab = lambda o: pl.BlockSpec((T, 128), lambda c: (0, o + c))
    return pl.pallas_call(
        kern, grid=(4,),
        in_specs=[slab(CB["d_glu"]), slab(CB["d_glu"] + 4), pl.BlockSpec((CONV_PAD, 128), lambda c: (0, c)), slab(0),
                  pl.BlockSpec(memory_space=pl.ANY)],
        out_specs=[slab(CB["d_glu"]), slab(0), pl.BlockSpec((CONV_PAD, 128), lambda c: (0, c)), pl.BlockSpec((1, 128), lambda c: (0, c))],
        out_shape=[jax.ShapeDtypeStruct(dP.shape, dP.dtype), jax.ShapeDtypeStruct((T, 512), F32), jax.ShapeDtypeStruct((CONV_PAD, 512), F32),
                   jax.ShapeDtypeStruct((1, 512), F32)],
        scratch_shapes=[pltpu.VMEM((n + 2 * CONV_HALO, 128), F32), pltpu.VMEM((n + 2 * CONV_HALO, 128), F32)],
        input_output_aliases={4: 0}, compiler_params=_cp("arbitrary"), name="conv_bwd")(P, P, cw, dY, dP)


def copy_cols(name, src, dP, col0, Cn, tm):
    T = dP.shape[0]

    def body(i, j, r, c, s):
        return [r[0]], [], []

    return rowcall(name, body, T, tm, src.shape[1] // 128, Cn, [(src, 128, lambda j: j)], [], [], [("into", dP, 128, lambda j: col0 + j)])[0]


def ada_fwd(cvec, w, b):
    L, D, wc = w.shape

    def kern(c_ref, w_ref, b_ref, o_ref):
        o_ref[...] = _dg(_silu(c_ref[...]), w_ref[...], 1, 0) + b_ref[...]

    return pl.pallas_call(
        kern, grid=(L,),
        in_specs=[pl.BlockSpec((16, D), lambda l: (0, 0)), pl.BlockSpec((None, D, wc), lambda l: (l, 0, 0)), pl.BlockSpec((None, 1, wc), lambda l: (l, 0, 0))],
        out_specs=pl.BlockSpec((None, 16, wc), lambda l: (l, 0, 0)), out_shape=jax.ShapeDtypeStruct((L, 16, wc), F32),
        compiler_params=_cp("arbitrary"), name="ada_fwd")(cvec, w, b)


def ada_bwd(cvec, w, dm):
    L, D, wc = w.shape

    def kern(c_ref, w_ref, d_ref, gw_ref, ds_ref):
        l = pl.program_id(0)
        d = d_ref[...]
        gw_ref[...] = _dg(_silu(c_ref[...]), d, 0, 0)
        part = _dg(d, w_ref[...], 1, 1)

        @pl.when(l == 0)
        def _():
            ds_ref[...] = part

        @pl.when(l > 0)
        def _():
            ds_ref[...] += part

    return pl.pallas_call(
        kern, grid=(L,),
        in_specs=[pl.BlockSpec((16, D), lambda l: (0, 0)), pl.BlockSpec((None, D, wc), lambda l: (l, 0, 0)), pl.BlockSpec((None, 16, wc), lambda l: (l, 0, 0))],
        out_specs=[pl.BlockSpec((None, D, wc), lambda l: (l, 0, 0)), pl.BlockSpec((16, D), lambda l: (0, 0))],
        out_shape=[jax.ShapeDtypeStruct((L, D, wc), F32), jax.ShapeDtypeStruct((16, D), F32)],
        compiler_params=_cp("arbitrary"), name="ada_bwd")(cvec, w, dm)


def silu_grad(cc, parts):
    D = cc.shape[1]

    def kern(c_ref, p_ref, o_ref):
        tot = p_ref[0]
        for k in range(1, N_DEV):
            tot = tot + p_ref[k]
        _, vjp = jax.vjp(_silu, c_ref[...])
        o_ref[...] = vjp(tot)[0]

    return pl.pallas_call(kern, out_shape=jax.ShapeDtypeStruct((1, D), F32), name="silu_grad")(cc, parts)


def _flip(v, f):
    return 1 - v if f else v


_REL = ((0, 0), (1, 0), (0, 1), (1, 1))


def allgather(name, x):
    def body(x_ref, out_ref, send_sems, recv_sems, local_sem):
        mx, my, mc = lax.axis_index("x"), lax.axis_index("y"), lax.axis_index("c")
        me, sibling = (mx, my, mc), (mx, my, 1 - mc)
        chips = [(_flip(mx, fx), _flip(my, fy)) for fx, fy in _REL[1:]]

        def slot(px, py, pc):
            return out_ref.at[4 * px + 2 * py + pc]

        def copy(k, block, to, src=None):
            return pltpu.make_async_remote_copy(
                src_ref=slot(*block) if src is None else src, dst_ref=slot(*block),
                send_sem=send_sems.at[k], recv_sem=recv_sems.at[k], device_id=to, device_id_type=MESH)

        mine = pltpu.make_async_copy(x_ref, slot(*me), local_sem)
        mine.start()
        first = [copy(0, me, sibling, src=x_ref)] + [copy(1 + j, me, (*chip, mc), src=x_ref) for j, chip in enumerate(chips)]
        for cp in first:
            cp.start()
        passed = [copy(4 + j, (*chip, mc), sibling) for j, chip in enumerate(chips)]
        for j, chip in enumerate(chips):
            copy(1 + j, (*chip, mc), me).wait_recv()
            passed[j].start()
        copy(0, sibling, me).wait_recv()
        for j, chip in enumerate(chips):
            copy(4 + j, (*chip, 1 - mc), me).wait_recv()
        for cp in first + passed:
            cp.wait_send()
        mine.wait()

    return pl.pallas_call(
        body, out_shape=jax.ShapeDtypeStruct((N_DEV,) + x.shape, x.dtype),
        in_specs=[pl.BlockSpec(memory_space=pl.ANY)], out_specs=pl.BlockSpec(memory_space=pl.ANY),
        scratch_shapes=[pltpu.SemaphoreType.DMA((7,)), pltpu.SemaphoreType.DMA((7,)), pltpu.SemaphoreType.DMA(())], name=name)(x)


def cols_from_devs(name, xg, l, nrow):
    _, _, C = xg.shape
    tr = _row_tile(nrow, N_DEV * C, xg.dtype.itemsize, 16, 8 << 20)
    nblk = nrow // tr

    def body(x_ref, o_ref):
        for d in range(N_DEV):
            o_ref[:, d * C:(d + 1) * C] = x_ref[d]

    return pl.pallas_call(
        body, grid=(nblk,), in_specs=[pl.BlockSpec((N_DEV, tr, C), lambda i: (0, l * nblk + i, 0))],
        out_specs=pl.BlockSpec((tr, N_DEV * C), lambda i: (i, 0)), out_shape=jax.ShapeDtypeStruct((nrow, N_DEV * C), xg.dtype),
        compiler_params=_cp("parallel"), name=name)(xg)


def cols_to_devs(name, g, l, L, prev):
    nrow, W = g.shape
    C = W // N_DEV
    tr = _row_tile(nrow, W, 4, 8, 8 << 20)
    nblk = nrow // tr

    def body(x_ref, *rest):
        o_ref = rest[-1]
        for d in range(N_DEV):
            o_ref[d] = x_ref[:, d * C:(d + 1) * C]

    in_specs, args, aliases = [pl.BlockSpec((tr, W), lambda i: (i, 0))], [g], {}
    if prev is not None:
        in_specs.append(pl.BlockSpec(memory_space=pl.ANY))
        args.append(prev)
        aliases = {1: 0}
    return pl.pallas_call(
        body, grid=(nblk,), in_specs=in_specs, out_specs=pl.BlockSpec((N_DEV, tr, C), lambda i: (0, l * nblk + i, 0)),
        out_shape=jax.ShapeDtypeStruct((N_DEV, L * nrow, C), F32), input_output_aliases=aliases,
        compiler_params=_cp("parallel"), name=name)(*args)


def proj_in_gather(H, W, shards, row0, tm, tn):
    T, D = H.shape
    IN = W.shape[1]
    C = shards.shape[1]
    ni, nj = T // tm, IN // tn

    def kern(a_ref, b_ref, x_all, o_ref, g_ref, send_sems, recv_sems, local_sem):
        i, j = pl.program_id(0), pl.program_id(1)
        mx, my, mc = lax.axis_index("x"), lax.axis_index("y"), lax.axis_index("c")
        me, sibling = (mx, my, mc), (mx, my, 1 - mc)
        chips = [(_flip(mx, fx), _flip(my, fy)) for fx, fy in _REL[1:]]
        x_ref = x_all.at[pl.ds(row0, D)]

        def slot(px, py, pc):
            return g_ref.at[4 * px + 2 * py + pc]

        def copy(k, block, to, src=None):
            return pltpu.make_async_remote_copy(
                src_ref=slot(*block) if src is None else src, dst_ref=slot(*block),
                send_sem=send_sems.at[k], recv_sem=recv_sems.at[k], device_id=to, device_id_type=MESH)

        mine = pltpu.make_async_copy(x_ref, slot(*me), local_sem)
        first = [copy(0, me, sibling, src=x_ref)] + [copy(1 + q, me, (*chip, mc), src=x_ref) for q, chip in enumerate(chips)]
        passed = [copy(4 + q, (*chip, mc), sibling) for q, chip in enumerate(chips)]

        @pl.when((i == 0) & (j == 0))
        def _():
            mine.start()
            for cp in first:
                cp.start()

        o_ref[...] = _dg(a_ref[...], b_ref[...], 1, 0)

        @pl.when((i == ni - 2) & (j == 0))
        def _():
            for q, chip in enumerate(chips):
                copy(1 + q, (*chip, mc), me).wait_recv()
                passed[q].start()

        @pl.when((i == ni - 1) & (j == nj - 1))
        def _():
            copy(0, sibling, me).wait_recv()
            for q, chip in enumerate(chips):
                copy(4 + q, (*chip, 1 - mc), me).wait_recv()
            for cp in first + passed:
                cp.wait_send()
            mine.wait()

    assert ni >= 3
    return pl.pallas_call(
        kern, grid=(ni, nj),
        in_specs=[pl.BlockSpec((tm, D), lambda i, j: (i, 0)), pl.BlockSpec((D, tn), lambda i, j: (0, j)), pl.BlockSpec(memory_space=pl.ANY)],
        out_specs=[pl.BlockSpec((tm, tn), lambda i, j: (i, j)), pl.BlockSpec(memory_space=pl.ANY)],
        out_shape=[jax.ShapeDtypeStruct((T, IN), F32), jax.ShapeDtypeStruct((N_DEV, D, C), shards.dtype)],
        scratch_shapes=[pltpu.SemaphoreType.DMA((7,)), pltpu.SemaphoreType.DMA((7,)), pltpu.SemaphoreType.DMA(())],
        compiler_params=_cp("arbitrary", "arbitrary"), name="proj_in_gather")(H, W, shards)


def rs_exchange_sibling(tag, buf):
    _, R, C = buf.shape

    def body(buf_ref, out_ref, send_sems, recv_sems):
        mx, my, mc = lax.axis_index("x"), lax.axis_index("y"), lax.axis_index("c")
        sibling = (mx, my, 1 - mc)
        cps = []
        for j, (fx, fy) in enumerate(_REL):
            d = 4 * _flip(mx, fx) + 2 * _flip(my, fy) + (1 - mc)
            cps.append(pltpu.make_async_remote_copy(src_ref=buf_ref.at[d], dst_ref=out_ref.at[j], send_sem=send_sems.at[j],
                                                    recv_sem=recv_sems.at[j], device_id=sibling, device_id_type=MESH))
        for cp in cps:
            cp.start()
        for cp in cps:
            cp.wait_recv()
        for cp in cps:
            cp.wait_send()

    return pl.pallas_call(
        body, out_shape=jax.ShapeDtypeStruct((4, R, C), buf.dtype),
        in_specs=[pl.BlockSpec(memory_space=pl.ANY)], out_specs=pl.BlockSpec(memory_space=pl.ANY),
        scratch_shapes=[pltpu.SemaphoreType.DMA((4,)), pltpu.SemaphoreType.DMA((4,))], name="rs_sibling_" + tag)(buf)


def rs_chip_sum(tag, buf, recv, idx, tr):
    _, R, C = buf.shape

    def kern(idx_ref, b_ref, r_ref, own_ref, sb_ref):
        j = pl.program_id(1)
        s = b_ref[...] + r_ref[...]
        sb_ref[...] = s.astype(BF)

        @pl.when(j == 0)
        def _():
            own_ref[...] = s

    return pl.pallas_call(
        kern, grid_spec=pltpu.PrefetchScalarGridSpec(
            num_scalar_prefetch=1, grid=(R // tr, 4),
            in_specs=[pl.BlockSpec((None, tr, C), lambda r, j, idx: (idx[j], r, 0)), pl.BlockSpec((None, tr, C), lambda r, j, idx: (j, r, 0))],
            out_specs=[pl.BlockSpec((tr, C), lambda r, j, idx: (r, 0)), pl.BlockSpec((None, tr, C), lambda r, j, idx: (j, r, 0))]),
        out_shape=[jax.ShapeDtypeStruct((R, C), F32), jax.ShapeDtypeStruct((4, R, C), BF)],
        compiler_params=_cp("arbitrary", "arbitrary"), name="rs_chip_sum_" + tag)(idx, buf, recv)


def rs_exchange_chips(tag, sb):
    _, R, C = sb.shape

    def body(sb_ref, out_ref, send_sems, recv_sems):
        mx, my, mc = lax.axis_index("x"), lax.axis_index("y"), lax.axis_index("c")
        cps = []
        for j, (fx, fy) in enumerate(_REL):
            if j == 0:
                continue
            cps.append(pltpu.make_async_remote_copy(src_ref=sb_ref.at[j], dst_ref=out_ref.at[j], send_sem=send_sems.at[j - 1],
                                                    recv_sem=recv_sems.at[j - 1], device_id=(_flip(mx, fx), _flip(my, fy), mc),
                                                    device_id_type=MESH))
        for cp in cps:
            cp.start()
        for cp in cps:
            cp.wait_recv()
        for cp in cps:
            cp.wait_send()

    return pl.pallas_call(
        body, out_shape=jax.ShapeDtypeStruct((4, R, C), sb.dtype),
        in_specs=[pl.BlockSpec(memory_space=pl.ANY)], out_specs=pl.BlockSpec(memory_space=pl.ANY),
        scratch_shapes=[pltpu.SemaphoreType.DMA((3,)), pltpu.SemaphoreType.DMA((3,))], name="rs_chips_" + tag)(sb)


def adamw(name, parts, w, m, v, tr):
    R, C = w.shape
    in_specs, args = [], []
    for arr, lead in parts:
        if lead is None:
            in_specs.append(pl.BlockSpec((tr, C), lambda r: (r, 0)))
        else:
            in_specs.append(pl.BlockSpec((None, tr, C), lambda r, k=lead: (k, r, 0)))
        args.append(arr)
    npart = len(parts)
    blk = pl.BlockSpec((tr, C), lambda r: (r, 0))

    def kern(*refs):
        g = refs[0][...].astype(F32)
        for r in refs[1:npart]:
            g = g + r[...].astype(F32)
        w_ref, m_ref, v_ref, g_out, d_out, m_out, v_out = refs[npart:]
        mn = ADAM_B1 * m_ref[...] + (1.0 - ADAM_B1) * g
        vn = ADAM_B2 * v_ref[...] + (1.0 - ADAM_B2) * jnp.square(g)
        m_hat = mn / (1.0 - ADAM_B1 ** ADAM_STEP)
        v_hat = vn / (1.0 - ADAM_B2 ** ADAM_STEP)
        g_out[...] = g
        d_out[...] = -ADAM_LR * (m_hat / (jnp.sqrt(v_hat) + ADAM_EPS) + ADAM_WD * w_ref[...])
        m_out[...] = mn
        v_out[...] = vn

    return pl.pallas_call(
        kern, grid=(R // tr,), in_specs=in_specs + [blk, blk, blk], out_specs=[blk] * 4,
        out_shape=[jax.ShapeDtypeStruct((R, C), F32)] * 4, compiler_params=_cp("parallel"), name=name)(*args, w, m, v)


def _pack(arrs):
    flat = [a.reshape(-1) for a in arrs]
    n = sum(f.shape[0] for f in flat)
    pad = (-n) % (8 * LANES)
    if pad:
        flat.append(jnp.zeros((pad,), flat[0].dtype))
    return jnp.concatenate(flat).reshape(-1, LANES)


def _unpack(packed, shapes):
    flat = packed.reshape(-1)
    out, off = [], 0
    for s in shapes:
        n = int(np.prod(s))
        out.append(flat[off:off + n].reshape(s))
        off += n
    return out


def _row_tile(R, C=LANES, itemsize=4, mult=16, target=1 << 20):
    best = None
    for t in range(mult, R + 1, mult):
        if R % t == 0 and t * C * itemsize <= target:
            best = t
    return best or R


BIG = ("w_in", "w_branch", "w_out", "conv_pw", "conv_w")
SMALL = ("g_pre", "g_post", "na_rpb", "pool_w", "pool_scale", "q_norm", "k_norm", "conv_b", "conv_ln_g", "conv_ln_b")
WEIGHTS = ['c_ctx', 'w_ada', 'b_ada', 'g_pre', 'g_post', 'w_in', 'na_rpb', 'pool_w', 'pool_scale', 'q_norm', 'k_norm', 'conv_w', 'conv_b',
           'conv_ln_g', 'conv_ln_b', 'conv_pw', 'w_branch', 'w_out']


def _rope_tables(S, Cn):
    t = np.arange(S)
    pos = np.stack([t // GRID_W, t % GRID_W], 1).astype(np.float64)
    lane = np.arange(128)
    within = lane % 64
    axis = within // 32
    f = (within % 32) % 16
    freqs = ROPE_THETA ** (-np.arange(16, dtype=np.float32) / 16)
    ang = pos[:, axis].astype(np.float32) * freqs[f][None, :]
    cos = np.cos(ang).astype(np.float32)
    sin = np.sin(ang).astype(np.float32) * np.where((within % 32) < 16, -1.0, 1.0).astype(np.float32)[None, :]
    cos = np.concatenate([np.ones((Cn, 128), np.float32), cos])
    sin = np.concatenate([np.zeros((Cn, 128), np.float32), sin])
    return jnp.asarray(cos), jnp.asarray(sin)


def _own_2d(w_in, w_branch, w_out, conv_pw, conv_w):
    L = w_in.shape[0]
    cwp = jnp.zeros((L, CONV_PAD, conv_w.shape[2]), conv_w.dtype).at[:, :CONV_WIDTH].set(conv_w)
    return dict(w_in=w_in.reshape(-1, w_in.shape[-1]), w_branch=w_branch.reshape(-1, w_branch.shape[-1]),
                w_out=w_out.reshape(-1, w_out.shape[-1]), conv=_pack([conv_pw, cwp]))


def kernel(x, c, ctx, c_ctx, w_ada, b_ada, g_pre, g_post, w_in, na_rpb, pool_w, pool_scale, q_norm, k_norm, conv_w, conv_b, conv_ln_g, conv_ln_b, conv_pw, w_branch, w_out, loss_target, m_c_ctx, m_w_ada, m_b_ada, m_g_pre, m_g_post, m_w_in, m_na_rpb, m_pool_w, m_pool_scale, m_q_norm, m_k_norm, m_conv_w, m_conv_b, m_conv_ln_g, m_conv_ln_b, m_conv_pw, m_w_branch, m_w_out, v_c_ctx, v_w_ada, v_b_ada, v_g_pre, v_g_post, v_w_in, v_na_rpb, v_pool_w, v_pool_scale, v_q_norm, v_k_norm, v_conv_w, v_conv_b, v_conv_ln_g, v_conv_ln_b, v_conv_pw, v_w_branch, v_w_out):
    W = dict(c_ctx=c_ctx, w_ada=w_ada, b_ada=b_ada, g_pre=g_pre, g_post=g_post, w_in=w_in, na_rpb=na_rpb, pool_w=pool_w, pool_scale=pool_scale,
             q_norm=q_norm, k_norm=k_norm, conv_w=conv_w, conv_b=conv_b, conv_ln_g=conv_ln_g, conv_ln_b=conv_ln_b, conv_pw=conv_pw,
             w_branch=w_branch, w_out=w_out)
    Mo = dict(c_ctx=m_c_ctx, w_ada=m_w_ada, b_ada=m_b_ada, g_pre=m_g_pre, g_post=m_g_post, w_in=m_w_in, na_rpb=m_na_rpb, pool_w=m_pool_w,
              pool_scale=m_pool_scale, q_norm=m_q_norm, k_norm=m_k_norm, conv_w=m_conv_w, conv_b=m_conv_b, conv_ln_g=m_conv_ln_g,
              conv_ln_b=m_conv_ln_b, conv_pw=m_conv_pw, w_branch=m_w_branch, w_out=m_w_out)
    Vo = dict(c_ctx=v_c_ctx, w_ada=v_w_ada, b_ada=v_b_ada, g_pre=v_g_pre, g_post=v_g_post, w_in=v_w_in, na_rpb=v_na_rpb, pool_w=v_pool_w,
              pool_scale=v_pool_scale, q_norm=v_q_norm, k_norm=v_k_norm, conv_w=v_conv_w, conv_b=v_conv_b, conv_ln_g=v_conv_ln_g,
              conv_ln_b=v_conv_ln_b, conv_pw=v_conv_pw, w_branch=v_w_branch, w_out=v_w_out)

    S, D = x.shape[1], x.shape[2]
    Cn = ctx.shape[1]
    T = Cn + S
    L = w_in.shape[0]
    IN = w_in.shape[2] * N_DEV
    mx, my, mc = lax.axis_index("x"), lax.axis_index("y"), lax.axis_index("c")
    me = 4 * mx + 2 * my + mc
    tm = _pick(Cn, (256, 128))
    tme = _pick(T, (768, 256, 128))
    tmb = _pick(T, (1408, 768, 256, 128))
    tmm = _pick(T, (768, 256, 128))
    tq = _pick(Cn, (256, 128))
    tk = _pick(T, (4224, 768, 384, 128))

    cg = allgather("gather_c", c)
    cvec = jnp.zeros((16, D), F32).at[0].set(c_ctx).at[1:1 + N_DEV].set(cg[:, 0])
    wc = w_ada.shape[2]
    b_sh = lax.dynamic_slice_in_dim(b_ada, me * wc, wc, axis=1)[:, None, :]
    modp = ada_fwd(cvec, w_ada, b_sh)
    modg = allgather("gather_mod", modp.reshape(L * 16, wc)).reshape(N_DEV, L, 16, wc)
    mod_full = jnp.transpose(modg, (1, 2, 0, 3)).reshape(L, 16, N_DEV * wc)
    mod2 = jnp.stack([mod_full[:, 0], lax.dynamic_index_in_dim(mod_full, 1 + me, axis=1, keepdims=False)], axis=1)
    shift, scale, gate = [mod2[:, :, None, k * D:(k + 1) * D] for k in range(3)]

    own2 = _own_2d(w_in, w_branch, w_out, conv_pw, conv_w)
    w_in_bf = own2["w_in"].astype(BF)
    win_g = allgather("gather_w_in0", w_in_bf[:D])
    wbr_g = allgather("gather_w_branch", own2["w_branch"].astype(BF))
    wout_g = allgather("gather_w_out", own2["w_out"].astype(BF))
    conv_g = allgather("gather_conv", own2["conv"].astype(BF)).reshape(N_DEV, -1)
    npw = L * (BRANCH_W // N_DEV) * BRANCH_W
    cpw_g = conv_g[:, :npw].reshape(N_DEV, L, BRANCH_W // N_DEV, BRANCH_W)
    cw_g = conv_g[:, npw:npw + L * CONV_PAD * (BRANCH_W // N_DEV)].reshape(N_DEV, L, CONV_PAD, BRANCH_W // N_DEV)
    Win = []
    Wb = [cols_from_devs("w_branch_cols", wbr_g, l, 4 * BRANCH_W).reshape(4, BRANCH_W, D) for l in range(L)]
    Wo = [rows_from_devs("w_out_rows", wout_g, l, L) for l in range(L)]
    Cpw = [cpw_g[:, l].reshape(BRANCH_W, BRANCH_W).astype(F32) for l in range(L)]
    Cw = [jnp.transpose(cw_g[:, l], (1, 0, 2)).reshape(CONV_PAD, BRANCH_W).astype(F32) for l in range(L)]

    cos, sin = _rope_tables(S, Cn)
    tb = na_tables(S, Cn)
    bd = jnp.asarray(np.kron(np.eye(2, dtype=np.float32), np.full((64, 64), 1.0 / 64, np.float32)))
    row2 = lambda a: a.reshape(1, -1)

    X = jnp.concatenate([ctx[0], x[0]], axis=0)
    saved = []
    for l in range(L):
        gp = row2(g_pre[l])
        H = modulate_fwd(X, gp, shift[l], scale[l], Cn, tm)
        Win.append(cols_from_devs("w_in_cols", win_g, 0, D))
        tn_in = _pick(IN, (1280, 1152, 768, 384, 128))
        if l + 1 < L:
            P, win_g = proj_in_gather(H, Win[l], w_in_bf, (l + 1) * D, tmm, tn_in)
        else:
            P = matmul("proj_in", H, Win[l], out_dtype=F32, tm=tmm, tn=tn_in, tk=D)
        bt = bias_table(na_rpb[l], tb)
        oa = na_fwd(P, bt, tb, Cn)
        ob = pool_fwd(P, pool_w[l], row2(pool_scale[l]), Cn)
        qn = row2(jnp.tile(q_norm[l], 2))
        kn = row2(jnp.tile(k_norm[l], 2))
        Qp = prep_q_fwd(P, cos, sin, qn, bd, Cn, tm)
        KV = prep_kv_fwd(P, cos, sin, kn, bd, Cn, tm)
        oc, LSE = flash_fwd(Qp, KV, Cn, tq, tk)
        Yc = conv_fwd(P, Cw[l], row2(conv_b[l]), Cn)
        od = conv_post_fwd(Yc, row2(conv_ln_g[l]), row2(conv_ln_b[l]), Cpw[l], Cn, tm)
        G = gate_outs_fwd(P, oa, ob, oc, od, Cn, tme)
        Y = branch_merge_fwd(P, G, Wb[l], tmb)
        Z = matmul("proj_out", Y, Wo[l], out_dtype=F32, tm=tmm, tn=D, tk=D)
        Xn = post_fwd(X, Z, row2(g_post[l]), gate[l], Cn, tm)
        saved.append(dict(X=X, H=H, P=P, bt=bt, oa=oa, ob=ob, oc=oc, od=od, Qp=Qp, KV=KV, LSE=LSE, Yc=Yc, G=G, Y=Y, Z=Z, qn=qn, kn=kn))
        X = Xn

    tgt = jnp.concatenate([jnp.zeros((Cn, D), F32), loss_target[0]], axis=0)
    dX, loss_acc = loss_call(X, tgt, Cn, tm)
    loss = lax.psum(loss_acc[0, 0], ("x", "y", "c"))

    gsm = {n: [None] * L for n in SMALL}
    gbig = {n: [None] * L for n in BIG}
    g_in = g_br = g_out = None
    dmod = [None] * L
    for l in reversed(range(L)):
        sv = saved[l]
        P = sv["P"]
        dZ, dgp, dgate = post_bwd(sv["Z"], dX, row2(g_post[l]), gate[l], Cn, tm)
        gsm["g_post"][l] = dgp[0]
        dY = matmul("proj_out_dy", dZ, Wo[l], cb=1, out_dtype=F32, tm=tmm, tn=D, tk=D)
        dWo = matmul("proj_out_dw", sv["Y"], dZ, ca=0, cb=0, tm=_pick(D, (1024, 512, 256)), tn=D, tk=tmm)
        g_out = rows_to_devs("w_out_devs", dWo, l, L, g_out)
        dU, dG, dP = branch_merge_bwd(P, sv["G"], Wb[l], dY, tmb)
        g_br = cols_to_devs("w_branch_devs", branch_dw(sv["G"], dU, tmm).reshape(4 * BRANCH_W, D), l, L, g_br)
        dO, dP = gate_outs_bwd(P, sv["oa"], sv["ob"], sv["oc"], sv["od"], dG, dP, Cn, tme)
        dYc, dlg, dlb, dcpw = conv_post_bwd(sv["Yc"], dO, row2(conv_ln_g[l]), row2(conv_ln_b[l]), Cpw[l], Cn, tm)
        gsm["conv_ln_g"][l], gsm["conv_ln_b"][l], gbig["conv_pw"][l] = dlg[0], dlb[0], dcpw
        dP, dGg, dcw, dcb = conv_bwd(P, Cw[l], dYc, dP, Cn)
        dP = copy_cols("copy_glu_gate", dGg, dP, CB["d_glu"] + 4, Cn, tme)
        gbig["conv_w"][l], gsm["conv_b"][l] = dcw, dcb[0]
        dQp, dKV = flash_bwd(sv["Qp"], sv["KV"], sv["oc"], sv["LSE"], dO, Cn, tq, tk)
        dP, dqn = prep_q_bwd(P, dQp, dP, cos, sin, sv["qn"], bd, Cn, tm)
        dP, dkn = prep_kv_bwd(P, dKV, dP, cos, sin, sv["kn"], bd, Cn, tm)
        gsm["q_norm"][l] = dqn[0, :64] + dqn[0, 64:]
        gsm["k_norm"][l] = dkn[0, :64] + dkn[0, 64:]
        dP, dpw, dps = pool_bwd(P, pool_w[l], row2(pool_scale[l]), dO, dP, Cn)
        gsm["pool_w"][l], gsm["pool_scale"][l] = dpw, dps[0]
        dQa, dKa, dVa, dbt = na_bwd(P, sv["bt"], dO, tb, Cn)
        dP = assemble_na(dKa, dVa, dQa, dP, Cn, tme)
        gsm["na_rpb"][l] = bias_table_grad(dbt, tb)
        dH = matmul("proj_in_dh", dP, Win[l], cb=1, out_dtype=F32, tm=tmm, tn=D, tk=_pick(IN, (1280, 1152, 768, 384, 128)))
        dWin = matmul("proj_in_dw", sv["H"], dP, ca=0, cb=0, tm=_pick(D, (1024, 512, 256)), tn=_pick(IN, (1280, 1152, 768, 384, 128)),
                      tk=_pick(T, (1408, 768, 384, 128)))
        g_in = cols_to_devs("w_in_devs", dWin, l, L, g_in)
        dX, dgpre, dsh, dsc = modulate_bwd(sv["X"], dH, dX, row2(g_pre[l]), shift[l], scale[l], Cn, tm)
        gsm["g_pre"][l] = dgpre[0]
        dmod[l] = jnp.concatenate([dsh[:, 0], dsc[:, 0], dgate[:, 0]], axis=1)
    grad_x = dX[Cn:][None]

    idx = jnp.stack([4 * _flip(mx, fx) + 2 * _flip(my, fy) + mc for fx, fy in _REL]).astype(jnp.int32)
    g_pw = jnp.transpose(jnp.stack(gbig["conv_pw"]).reshape(L, N_DEV, -1), (1, 0, 2)).reshape(N_DEV, -1)
    g_cw = jnp.transpose(jnp.stack(gbig["conv_w"]).reshape(L, CONV_PAD, N_DEV, BRANCH_W // N_DEV), (2, 0, 1, 3)).reshape(N_DEV, -1)
    g_conv = jnp.concatenate([g_pw, g_cw], axis=1)
    g_conv = jnp.pad(g_conv, ((0, 0), (0, own2["conv"].size - g_conv.shape[1]))).reshape(N_DEV, -1, LANES)
    gbufs = dict(w_in=g_in, w_branch=g_br, w_out=g_out, conv=g_conv)
    mom2 = _own_2d(Mo["w_in"], Mo["w_branch"], Mo["w_out"], Mo["conv_pw"], Mo["conv_w"])
    var2 = _own_2d(Vo["w_in"], Vo["w_branch"], Vo["w_out"], Vo["conv_pw"], Vo["conv_w"])
    big2 = {}
    for tag in ("w_in", "w_branch", "w_out", "conv"):
        R2, C2 = own2[tag].shape
        tr2 = _row_tile(R2, C2)
        recv1 = rs_exchange_sibling(tag, gbufs[tag])
        own, sb = rs_chip_sum(tag, gbufs[tag], recv1, idx, tr2)
        recv2 = rs_exchange_chips(tag, sb)
        big2[tag] = adamw("adamw_" + tag, [(own, None), (recv2, 1), (recv2, 2), (recv2, 3)], own2[tag], mom2[tag], var2[tag], tr2)
    cshapes = [(L, BRANCH_W // N_DEV, BRANCH_W), (L, CONV_PAD, BRANCH_W // N_DEV)]
    conv_unp = [_unpack(r, cshapes) for r in big2["conv"]]
    big_out = dict(w_in=[r.reshape(w_in.shape) for r in big2["w_in"]], w_branch=[r.reshape(w_branch.shape) for r in big2["w_branch"]],
                   w_out=[r.reshape(w_out.shape) for r in big2["w_out"]], conv_pw=[u[0] for u in conv_unp],
                   conv_w=[u[1][:, :CONV_WIDTH] for u in conv_unp])

    small_own = [jnp.stack(gsm[n]) for n in SMALL]
    small_shapes = [a.shape for a in small_own]
    dmod_all = jnp.stack(dmod)
    spack = _pack(small_own + [dmod_all])
    sg = allgather("gather_small", spack)
    Rs = spack.shape[0]
    nsmall = sum(int(np.prod(s)) for s in small_shapes)
    dmod_g = sg.reshape(N_DEV, -1)[:, nsmall:nsmall + dmod_all.size].reshape(N_DEV, L, 2, N_DEV, wc)
    dm_ctx = dmod_g[0, :, 0, :, :]
    for k in range(1, N_DEV):
        dm_ctx = dm_ctx + dmod_g[k, :, 0, :, :]
    dm_ctx = lax.dynamic_index_in_dim(dm_ctx, me, axis=1, keepdims=False)
    dm_b = jnp.transpose(lax.dynamic_index_in_dim(dmod_g[:, :, 1], me, axis=2, keepdims=False), (1, 0, 2))
    dmp = jnp.zeros((L, 16, wc), F32).at[:, 0].set(dm_ctx).at[:, 1:1 + N_DEV].set(dm_b)
    g_wada, dsilu = ada_bwd(cvec, w_ada, dmp)
    cpart = allgather("gather_cctx", dsilu[0:1])
    g_cctx = silu_grad(c_ctx[None], cpart)[0]

    smallw = _pack([W[n] for n in SMALL])
    smallm = _pack([Mo[n] for n in SMALL])
    smallv = _pack([Vo[n] for n in SMALL])
    Rsm = smallw.shape[0]
    small_res = adamw("adamw_small", [(sg[:, :Rsm], k) for k in range(N_DEV)], smallw, smallm, smallv, _row_tile(Rsm))
    small_unp = [_unpack(r, small_shapes) for r in small_res]

    db_parts = dmod_g.reshape(N_DEV, L, 2, N_DEV * wc)
    bshape = (L * N_DEV * wc // LANES, LANES)
    bparts = [(db_parts[k, :, r].reshape(bshape), None) for k in range(N_DEV) for r in range(2)]
    bada_res = adamw("adamw_bada", bparts, b_ada.reshape(bshape), Mo["b_ada"].reshape(bshape), Vo["b_ada"].reshape(bshape), _row_tile(bshape[0]))
    wada_shape = (w_ada.size // LANES, LANES)
    wada_res = adamw("adamw_wada", [(g_wada.reshape(wada_shape), None)], w_ada.reshape(wada_shape), Mo["w_ada"].reshape(wada_shape),
                     Vo["w_ada"].reshape(wada_shape), _row_tile(wada_shape[0]))
    cshape = (D // LANES, LANES)
    cctx_res = adamw("adamw_cctx", [(g_cctx.reshape(cshape), None)], c_ctx.reshape(cshape), Mo["c_ctx"].reshape(cshape), Vo["c_ctx"].reshape(cshape),
                     _row_tile(cshape[0]) if cshape[0] % 8 == 0 else cshape[0])

    res = {}
    for n in BIG:
        res[n] = big_out[n]
    for k, n in enumerate(SMALL):
        res[n] = [u[k] for u in small_unp]
    res["b_ada"] = [r.reshape(b_ada.shape) for r in bada_res]
    res["w_ada"] = [r.reshape(w_ada.shape) for r in wada_res]
    res["c_ctx"] = [r.reshape(c_ctx.shape) for r in cctx_res]
    outs = [loss, grad_x]
    for k in range(4):
        outs += [res[n][k] for n in WEIGHTS]
    return tuple(outs)
```

```python
import functools

import numpy as np
import jax
import jax.numpy as jnp
from jax import lax
from jax.experimental import pallas as pl
from jax.experimental.pallas import tpu as pltpu

F32 = jnp.float32
BF = jnp.bfloat16
HI = lax.Precision.HIGHEST

GRID_W = 64
BRANCH_W = 512
HEAD_DIM = 64
NA_WIN_ROWS = 8
NA_WIN_COLS = 16
NA_TILE_ROWS = 2
NA_BAND = NA_WIN_ROWS + NA_TILE_ROWS - 1
CONV_WIDTH = 31
CONV_PAD = 32
EPS = 1e-6
ROPE_THETA = 10000.0
NEG = -1e30
N_DEV = 8
LANES = 128
VMEM_LIMIT_BYTES = 56 * 1024 * 1024

ADAM_LR, ADAM_B1, ADAM_B2, ADAM_EPS, ADAM_WD, ADAM_STEP = 0.001, 0.9, 0.999, 1e-08, 0.01, 10

CB = dict(a_k=0, a_v=4, c_k=8, c_v=9, a_q=10, c_q=14, a_gate=18, b_in=22, b_gate=26, c_gate=30, d_glu=34, d_gate=42, merge=46)
KV_COLS = 1280
MESH = pl.DeviceIdType.MESH


def _pick(n, cands):
    for c in cands:
        if n % c == 0:
            return c
    raise ValueError(f"no tile for {n} among {cands}")


def _cp(*sem):
    return pltpu.CompilerParams(dimension_semantics=sem if sem else None, vmem_limit_bytes=VMEM_LIMIT_BYTES)


def _dg(x, y, cx, cy):
    return lax.dot_general(x.astype(BF), y.astype(BF), (((cx,), (cy,)), ((), ())), preferred_element_type=F32)


@functools.partial(jax.custom_vjp, nondiff_argnums=(2, 3))
def bdot(a, b, ca=1, cb=0):
    return _dg(a, b, ca, cb)


def _bdot_fwd(a, b, ca, cb):
    return _dg(a, b, ca, cb), (a, b)


def _bdot_bwd(ca, cb, res, g):
    a, b = res
    jb = 1 if cb == 0 else 0
    ia = 0 if ca == 1 else 1
    da = _dg(g, b, 1, jb) if ca == 1 else _dg(b, g, jb, 1)
    db = _dg(a, g, ia, 0) if cb == 0 else _dg(g, a, 0, ia)
    return da.astype(a.dtype), db.astype(b.dtype)


bdot.defvjp(_bdot_fwd, _bdot_bwd)


def hdot(a, b):
    return jnp.dot(a, b, precision=HI, preferred_element_type=F32)


@jax.custom_vjp
def swap64(x):
    return pltpu.roll(x, 64, 1)


swap64.defvjp(lambda x: (swap64(x), None), lambda _, g: (swap64(g),))


@jax.custom_vjp
def partner(x):
    w = x.shape[-1]
    lane = lax.broadcasted_iota(jnp.int32, x.shape, 1)
    return jnp.where((lane % 32) < 16, pltpu.roll(x, w - 16, 1), pltpu.roll(x, 16, 1))


partner.defvjp(lambda x: (partner(x), None), lambda _, g: (partner(g),))


@functools.partial(jax.custom_vjp, nondiff_argnums=(1,))
def split_rows(x, n):
    k = x.shape[0] // n
    return tuple(x[i * k:(i + 1) * k] for i in range(n))


split_rows.defvjp(lambda x, n: (split_rows(x, n), None), lambda n, _, gs: (jnp.concatenate(gs, axis=0),))


def _lane(shape):
    return lax.broadcasted_iota(jnp.int32, shape, len(shape) - 1)


def _silu(x):
    return x * jax.nn.sigmoid(x)


def _normrope(x, w, cos, sin, bd):
    ms = hdot(x * x, bd)
    y = x * lax.rsqrt(ms + EPS) * w
    return y * cos + partner(y) * sin


def matmul(name, a, b, *, ca=1, cb=0, out_dtype=F32, tm, tn, tk):
    M = a.shape[1 - ca]
    K = a.shape[ca]
    N = b.shape[1 - cb]
    assert b.shape[cb] == K and M % tm == 0 and N % tn == 0 and K % tk == 0, (name, a.shape, b.shape)
    nk = K // tk
    a_spec = pl.BlockSpec((tm, tk), lambda i, j, k: (i, k)) if ca == 1 else pl.BlockSpec((tk, tm), lambda i, j, k: (k, i))
    b_spec = pl.BlockSpec((tk, tn), lambda i, j, k: (k, j)) if cb == 0 else pl.BlockSpec((tn, tk), lambda i, j, k: (j, k))

    def kern(a_ref, b_ref, o_ref, *scr):
        part = _dg(a_ref[...], b_ref[...], ca, cb)
        if nk == 1:
            o_ref[...] = part.astype(out_dtype)
            return
        acc_ref, = scr
        k = pl.program_id(2)

        @pl.when(k == 0)
        def _():
            acc_ref[...] = part

        @pl.when(k > 0)
        def _():
            acc_ref[...] += part

        @pl.when(k == nk - 1)
        def _():
            o_ref[...] = acc_ref[...].astype(out_dtype)

    return pl.pallas_call(
        kern, grid=(M // tm, N // tn, nk), in_specs=[a_spec, b_spec],
        out_specs=pl.BlockSpec((tm, tn), lambda i, j, k: (i, j)),
        out_shape=jax.ShapeDtypeStruct((M, N), out_dtype),
        scratch_shapes=[] if nk == 1 else [pltpu.VMEM((tm, tn), F32)],
        compiler_params=_cp("parallel", "parallel", "arbitrary"), name=name,
    )(a, b)


def rowcall(name, body, T, tm, ncol, nctx, rows, consts=(), segs=(), outs=(), accs=(), saccs=()):
    nctxb = nctx // tm
    assert T % tm == 0 and (nctx % tm == 0 or not (segs or saccs))
    nr, nc, ns = len(rows), len(consts), len(segs)

    def seg_map(i, j):
        return (jnp.where(i < nctxb, 0, 1), 0, 0)

    def col_map(cf):
        return lambda i, j: (i, cf(j))

    in_specs, args = [], []
    for arr, bw, cf in rows:
        in_specs.append(pl.BlockSpec((tm, bw), col_map(cf)))
        args.append(arr)
    for arr in consts:
        in_specs.append(pl.BlockSpec(arr.shape, lambda i, j, nd=arr.ndim: (0,) * nd))
        args.append(arr)
    for arr in segs:
        in_specs.append(pl.BlockSpec((None, 1, arr.shape[-1]), seg_map))
        args.append(arr)
    out_shape, out_specs, aliases, out_dtypes = [], [], {}, []
    for o in outs:
        if o[0] == "into":
            _, arr, bw, cf = o
            aliases[len(args)] = len(out_shape)
            in_specs.append(pl.BlockSpec(memory_space=pl.ANY))
            args.append(arr)
            out_shape.append(jax.ShapeDtypeStruct(arr.shape, arr.dtype))
            out_dtypes.append(arr.dtype)
        else:
            cols, dt, bw, cf = o
            out_shape.append(jax.ShapeDtypeStruct((T, cols), dt))
            out_dtypes.append(dt)
        out_specs.append(pl.BlockSpec((tm, bw), col_map(cf)))
    for shp in accs:
        out_shape.append(jax.ShapeDtypeStruct(shp, F32))
        out_specs.append(pl.BlockSpec(shp, lambda i, j, nd=len(shp): (0,) * nd))
    for w in saccs:
        out_shape.append(jax.ShapeDtypeStruct((2, 1, w), F32))
        out_specs.append(pl.BlockSpec((None, 1, w), seg_map))
    n_in = len(args)
    no, na, nsa = len(outs), len(accs), len(saccs)

    def kern(*refs):
        i, j = pl.program_id(0), pl.program_id(1)
        ins = refs[:nr + nc + ns]
        orefs = refs[n_in:n_in + no]
        arefs = refs[n_in + no:n_in + no + na]
        srefs = refs[n_in + no + na:n_in + no + na + nsa]
        ov, av, sv = body(i, j, [r[...] for r in ins[:nr]], [r[...] for r in ins[nr:nr + nc]], [r[...] for r in ins[nr + nc:]])
        for r, v, dt in zip(orefs, ov, out_dtypes):
            r[...] = v.astype(dt)
        if na:
            @pl.when((i == 0) & (j == 0))
            def _():
                for r in arefs:
                    r[...] = jnp.zeros_like(r)
            for r, v in zip(arefs, av):
                r[...] += v
        if nsa:
            @pl.when(((i == 0) | (i == nctxb)) & (j == 0))
            def _():
                for r in srefs:
                    r[...] = jnp.zeros_like(r)
            for r, v in zip(srefs, sv):
                r[...] += v

    res = pl.pallas_call(
        kern, grid=(T // tm, ncol), in_specs=in_specs, out_specs=out_specs, out_shape=out_shape,
        input_output_aliases=aliases, compiler_params=_cp("arbitrary", "arbitrary"), name=name,
    )(*args)
    return res


def _c(k):
    return lambda j: k


def f_modulate(x, g, sh, sc):
    y = x * lax.rsqrt(jnp.mean(x * x, axis=-1, keepdims=True) + EPS) * g
    return y * (1.0 + sc) + sh


def modulate_fwd(X, g_pre, shift, scale, Cn, tm):
    T, D = X.shape

    def body(i, j, r, c, s):
        return [f_modulate(r[0], c[0], s[0], s[1])], [], []

    return rowcall("modulate_fwd", body, T, tm, 1, Cn, [(X, D, _c(0))], [g_pre], [shift, scale], [(D, BF, D, _c(0))])[0]


def modulate_bwd(X, dH, dXn, g_pre, shift, scale, Cn, tm):
    T, D = X.shape

    def body(i, j, r, c, s):
        _, vjp = jax.vjp(f_modulate, r[0], c[0], s[0], s[1])
        dx, dg, dsh, dsc = vjp(r[1])
        return [dx + r[2]], [dg], [dsh, dsc]

    return rowcall("modulate_bwd", body, T, tm, 1, Cn, [(X, D, _c(0)), (dH, D, _c(0)), (dXn, D, _c(0))], [g_pre], [shift, scale],
                   [(D, F32, D, _c(0))], [(1, D)], [D, D])


def f_post(z, gp, gate):
    return gate * (z * lax.rsqrt(jnp.mean(z * z, axis=-1, keepdims=True) + EPS) * gp)


def post_fwd(X, Z, g_post, gate, Cn, tm):
    T, D = X.shape

    def body(i, j, r, c, s):
        return [r[0] + f_post(r[1], c[0], s[0])], [], []

    return rowcall("post_fwd", body, T, tm, 1, Cn, [(X, D, _c(0)), (Z, D, _c(0))], [g_post], [gate], [(D, F32, D, _c(0))])[0]


def post_bwd(Z, dXn, g_post, gate, Cn, tm):
    T, D = Z.shape

    def body(i, j, r, c, s):
        _, vjp = jax.vjp(f_post, r[0], c[0], s[0])
        dz, dgp, dgate = vjp(r[1])
        return [dz], [dgp], [dgate]

    return rowcall("post_bwd", body, T, tm, 1, Cn, [(Z, D, _c(0)), (dXn, D, _c(0))], [g_post], [gate],
                   [(D, BF, D, _c(0))], [(1, D)], [D])


def loss_call(XL, tgt, Cn, tm):
    T, D = XL.shape
    nctxb = Cn // tm

    def body(i, j, r, c, s):
        diff = jnp.where(i >= nctxb, r[0] - r[1], 0.0)
        per_row = jnp.mean(diff * diff, axis=-1, keepdims=True)
        tot = 0.5 * jnp.sum(per_row, axis=0, keepdims=True)
        return [diff / D], [jnp.broadcast_to(tot, (1, LANES))], []

    return rowcall("loss", body, T, tm, 1, Cn, [(XL, D, _c(0)), (tgt, D, _c(0))], [], [], [(D, F32, D, _c(0))], [(1, LANES)])


def f_prep_q(x, qn, cos, sin, bd, jj):
    y = _normrope(x, qn, cos, sin, bd) * 0.125
    lane = _lane(y.shape)
    a = jnp.where(lane < 64, y, 0.0)
    b = jnp.where(lane >= 64, y, 0.0)
    g0 = jj < 2
    return jnp.concatenate([jnp.where(g0, a, swap64(a)), jnp.where(g0, swap64(b), b)], axis=1)


def prep_q_fwd(P, cos, sin, qn, bd, Cn, tm):
    T = P.shape[0]

    def body(i, j, r, c, s):
        return [f_prep_q(r[0], c[0], r[1], r[2], c[1], j)], [], []

    return rowcall("prep_q_fwd", body, T, tm, 4, Cn, [(P, 128, lambda j: CB["c_q"] + j), (cos, 128, _c(0)), (sin, 128, _c(0))], [qn, bd], [],
                   [(1024, BF, 256, lambda j: j)])[0]


def prep_q_bwd(P, dQ, dP, cos, sin, qn, bd, Cn, tm):
    T = P.shape[0]

    def body(i, j, r, c, s):
        _, vjp = jax.vjp(lambda x, w: f_prep_q(x, w, r[1], r[2], c[1], j), r[0], c[0])
        dx, dw = vjp(r[3])
        return [dx], [dw], []

    return rowcall("prep_q_bwd", body, T, tm, 4, Cn,
                   [(P, 128, lambda j: CB["c_q"] + j), (cos, 128, _c(0)), (sin, 128, _c(0)), (dQ, 256, lambda j: j)], [qn, bd], [],
                   [("into", dP, 128, lambda j: CB["c_q"] + j)], [(1, 128)])


def f_prep_kv(x, kn, cos, sin, bd, jj):
    return jnp.where(jj == 0, _normrope(x, kn, cos, sin, bd), x)


def prep_kv_fwd(P, cos, sin, kn, bd, Cn, tm):
    T = P.shape[0]

    def body(i, j, r, c, s):
        return [f_prep_kv(r[0], c[0], r[1], r[2], c[1], j)], [], []

    return rowcall("prep_kv_fwd", body, T, tm, 2, Cn, [(P, 128, lambda j: CB["c_k"] + j), (cos, 128, _c(0)), (sin, 128, _c(0))], [kn, bd], [],
                   [(256, BF, 128, lambda j: j)])[0]


def prep_kv_bwd(P, dKV, dP, cos, sin, kn, bd, Cn, tm):
    T = P.shape[0]

    def body(i, j, r, c, s):
        _, vjp = jax.vjp(lambda x, w: f_prep_kv(x, w, r[1], r[2], c[1], j), r[0], c[0])
        dx, dw = vjp(r[3])
        return [dx], [dw], []

    return rowcall("prep_kv_bwd", body, T, tm, 2, Cn,
                   [(P, 128, lambda j: CB["c_k"] + j), (cos, 128, _c(0)), (sin, 128, _c(0)), (dKV, 128, lambda j: j)], [kn, bd], [],
                   [("into", dP, 128, lambda j: CB["c_k"] + j)], [(1, 128)])


def f_conv_post(y, lg, lb, w):
    mu = jnp.mean(y, axis=-1, keepdims=True)
    var = jnp.mean(jnp.square(y - mu), axis=-1, keepdims=True)
    h = (y - mu) * lax.rsqrt(var + EPS) * lg + lb
    return bdot(_silu(h), w)


def conv_post_fwd(Yc, lg, lb, w, Cn, tm):
    T = Yc.shape[0]

    def body(i, j, r, c, s):
        return [f_conv_post(r[0], c[0], c[1], c[2])], [], []

    return rowcall("conv_post_fwd", body, T, tm, 1, Cn, [(Yc, 512, _c(0))], [lg, lb, w], [], [(512, F32, 512, _c(0))])[0]


def conv_post_bwd(Yc, dO, lg, lb, w, Cn, tm):
    T = Yc.shape[0]

    def body(i, j, r, c, s):
        _, vjp = jax.vjp(f_conv_post, r[0], c[0], c[1], c[2])
        dy, dlg, dlb, dw = vjp(r[1])
        return [dy], [dlg, dlb, dw], []

    return rowcall("conv_post_bwd", body, T, tm, 1, Cn, [(Yc, 512, _c(0)), (dO, 512, _c(3))], [lg, lb, w], [],
                   [(512, F32, 512, _c(0))], [(1, 512), (1, 512), (512, 512)])


def _gate_blk(j):
    b = j // 2
    base = jnp.where(b == 0, CB["a_gate"] // 2, jnp.where(b == 1, CB["b_gate"] // 2, jnp.where(b == 2, CB["c_gate"] // 2, CB["d_gate"] // 2)))
    return base + j % 2


def _sel4(j, vals):
    b = j // 2
    return jnp.where(b == 0, vals[0], jnp.where(b == 1, vals[1], jnp.where(b == 2, vals[2], vals[3])))


def gate_outs_fwd(P, oa, ob, oc, od, Cn, tm):
    T = P.shape[0]

    def body(i, j, r, c, s):
        return [_sel4(j, r[:4]) * _silu(r[4])], [], []

    half = lambda j: j % 2
    return rowcall("gate_outs_fwd", body, T, tm, 8, Cn,
                   [(oa, 256, half), (ob, 256, half), (oc, 256, half), (od, 256, half), (P, 256, _gate_blk)], [], [],
                   [(2048, BF, 256, lambda j: j)])[0]


def gate_outs_bwd(P, oa, ob, oc, od, dG, dP, Cn, tm):
    T = P.shape[0]

    def body(i, j, r, c, s):
        o = _sel4(j, r[:4])
        _, vjp = jax.vjp(lambda oo, gg: oo * _silu(gg), o, r[4])
        do, dg = vjp(r[5])
        return [do, dg], [], []

    half = lambda j: j % 2
    return rowcall("gate_outs_bwd", body, T, tm, 8, Cn,
                   [(oa, 256, half), (ob, 256, half), (oc, 256, half), (od, 256, half), (P, 256, _gate_blk), (dG, 256, lambda j: j)], [], [],
                   [(2048, F32, 256, lambda j: j), ("into", dP, 256, _gate_blk)])


def merge_fwd(P, U, Cn, tm):
    T = P.shape[0]
    D4 = U.shape[1]
    nb = D4 // 4 // 256
    mb = CB["merge"] // 2

    def body(i, j, r, c, s):
        y = 0.0
        for b in range(4):
            y = y + jax.nn.sigmoid(r[b]) * r[4 + b]
        return [y], [], []

    rows = [(P, 256, (lambda j, b=b: mb + b * nb + j)) for b in range(4)] + [(U, 256, (lambda j, b=b: b * nb + j)) for b in range(4)]
    return rowcall("merge_fwd", body, T, tm, nb, Cn, rows, [], [], [(D4 // 4, BF, 256, lambda j: j)])[0]


def merge_bwd(P, U, dY, Cn, tm):
    T, IN = P.shape
    D4 = U.shape[1]
    nb = D4 // 4 // 256
    mb = CB["merge"] // 2

    def body(i, j, r, c, s):
        _, vjp = jax.vjp(lambda m, u: jax.nn.sigmoid(m) * u, r[0], r[1])
        dm, du = vjp(r[2])
        return [du, dm], [], []

    return rowcall("merge_bwd", body, T, tm, 4 * nb, Cn,
                   [(P, 256, lambda j: mb + j), (U, 256, lambda j: j), (dY, 256, lambda j: j % nb)], [], [],
                   [(D4, BF, 256, lambda j: j), (IN, F32, 256, lambda j: mb + j)])


def assemble_na(dK, dV, dQ, dP, Cn, tm):
    T = dP.shape[0]

    def body(i, j, r, c, s):
        return [jnp.where(j < 4, r[0], jnp.where(j < 8, r[1], r[2]))], [], []

    return rowcall("assemble_na", body, T, tm, 12, Cn,
                   [(dK, 128, lambda j: jnp.minimum(j, 3)), (dV, 128, lambda j: jnp.clip(j - 4, 0, 3)), (dQ, 128, lambda j: jnp.clip(j - 8, 0, 3))],
                   [], [], [("into", dP, 128, lambda j: jnp.where(j < 8, j, j + 2))])[0]


def branch_fwd(G, Wb, tm):
    T = G.shape[0]
    D = Wb.shape[2]

    def kern(g_ref, w_ref, o_ref):
        o_ref[...] = _dg(g_ref[...], w_ref[...], 1, 0)

    return pl.pallas_call(
        kern, grid=(T // tm, 4),
        in_specs=[pl.BlockSpec((tm, 512), lambda i, b: (i, b)), pl.BlockSpec((None, 512, D), lambda i, b: (b, 0, 0))],
        out_specs=pl.BlockSpec((tm, D), lambda i, b: (i, b)), out_shape=jax.ShapeDtypeStruct((T, 4 * D), F32),
        compiler_params=_cp("parallel", "arbitrary"), name="branch_fwd")(G, Wb)


def branch_dg(dU, Wb, tm):
    T = dU.shape[0]
    D = Wb.shape[2]

    def kern(u_ref, w_ref, o_ref):
        o_ref[...] = _dg(u_ref[...], w_ref[...], 1, 1)

    return pl.pallas_call(
        kern, grid=(T // tm, 4),
        in_specs=[pl.BlockSpec((tm, D), lambda i, b: (i, b)), pl.BlockSpec((None, 512, D), lambda i, b: (b, 0, 0))],
        out_specs=pl.BlockSpec((tm, 512), lambda i, b: (i, b)), out_shape=jax.ShapeDtypeStruct((T, 2048), F32),
        compiler_params=_cp("parallel", "arbitrary"), name="branch_dg")(dU, Wb)


def branch_merge_fwd(P, G, Wb, tm):
    T = P.shape[0]
    D = Wb.shape[2]
    nj = D // 256
    mb = CB["merge"] // 2

    def kern(g_ref, w_ref, m0, m1, m2, m3, y_ref):
        y = None
        for b, m_ref in enumerate((m0, m1, m2, m3)):
            term = jax.nn.sigmoid(m_ref[...]) * _dg(g_ref[:, 512 * b:512 * (b + 1)], w_ref[b], 1, 0)
            y = term if y is None else y + term
        y_ref[...] = y.astype(BF)

    m_specs = [pl.BlockSpec((tm, 256), (lambda i, j, b=b: (i, mb + b * nj + j))) for b in range(4)]
    return pl.pallas_call(
        kern, grid=(T // tm, nj),
        in_specs=[pl.BlockSpec((tm, 2048), lambda i, j: (i, 0)), pl.BlockSpec((4, 512, 256), lambda i, j: (0, 0, j))] + m_specs,
        out_specs=pl.BlockSpec((tm, 256), lambda i, j: (i, j)), out_shape=jax.ShapeDtypeStruct((T, D), BF),
        compiler_params=_cp("parallel", "arbitrary"), name="branch_merge_fwd")(G, Wb, P, P, P, P)


def branch_merge_bwd(P, G, Wb, dY, tm):
    T, IN = P.shape
    D = Wb.shape[2]
    nj = D // 256
    mb = CB["merge"] // 2

    def kern(g_ref, w_ref, m_ref, dy_ref, du_ref, dg_ref, dm_ref, acc_s):
        j = pl.program_id(2)
        w = w_ref[...]
        u = _dg(g_ref[...], w, 1, 0)
        sig = jax.nn.sigmoid(m_ref[...])
        dy = dy_ref[...]
        dm_ref[...] = (dy * u * sig * (1.0 - sig)).astype(BF)
        du = dy * sig
        du_ref[...] = du.astype(BF)
        part = _dg(du, w, 1, 1)

        @pl.when(j == 0)
        def _():
            acc_s[...] = part

        @pl.when(j > 0)
        def _():
            acc_s[...] += part

        @pl.when(j == nj - 1)
        def _():
            dg_ref[...] = acc_s[...]

    return pl.pallas_call(
        kern, grid=(T // tm, 4, nj),
        in_specs=[pl.BlockSpec((tm, 512), lambda i, b, j: (i, b)), pl.BlockSpec((None, 512, 256), lambda i, b, j: (b, 0, j)),
                  pl.BlockSpec((tm, 256), lambda i, b, j: (i, mb + b * nj + j)), pl.BlockSpec((tm, 256), lambda i, b, j: (i, j))],
        out_specs=[pl.BlockSpec((tm, 256), lambda i, b, j: (i, b * nj + j)), pl.BlockSpec((tm, 512), lambda i, b, j: (i, b)),
                   pl.BlockSpec((tm, 256), lambda i, b, j: (i, mb + b * nj + j))],
        out_shape=[jax.ShapeDtypeStruct((T, 4 * D), BF), jax.ShapeDtypeStruct((T, 2048), F32), jax.ShapeDtypeStruct((T, IN), BF)],
        scratch_shapes=[pltpu.VMEM((tm, 512), F32)],
        compiler_params=_cp("parallel", "arbitrary", "arbitrary"), name="branch_merge_bwd")(G, Wb, P, dY)


def rows_from_devs(name, xg, l, L):
    _, Lr, C = xg.shape
    r = Lr // L

    def body(x_ref, o_ref):
        o_ref[...] = x_ref[...]

    return pl.pallas_call(
        body, grid=(N_DEV,), in_specs=[pl.BlockSpec((None, r, C), lambda d: (d, l, 0))], out_specs=pl.BlockSpec((r, C), lambda d: (d, 0)),
        out_shape=jax.ShapeDtypeStruct((N_DEV * r, C), xg.dtype), compiler_params=_cp("parallel"), name=name)(xg)


def rows_to_devs(name, g, l, L, prev):
    R8, C = g.shape
    r = R8 // N_DEV

    def body(x_ref, *rest):
        rest[-1][...] = x_ref[...]

    in_specs, args, aliases = [pl.BlockSpec((r, C), lambda d: (d, 0))], [g], {}
    if prev is not None:
        in_specs.append(pl.BlockSpec(memory_space=pl.ANY))
        args.append(prev)
        aliases = {1: 0}
    return pl.pallas_call(
        body, grid=(N_DEV,), in_specs=in_specs, out_specs=pl.BlockSpec((None, r, C), lambda d: (d, l, 0)),
        out_shape=jax.ShapeDtypeStruct((N_DEV, L * r, C), g.dtype), input_output_aliases=aliases,
        compiler_params=_cp("parallel"), name=name)(*args)


def branch_dw(G, dU, tk):
    T = G.shape[0]
    D = dU.shape[1] // 4
    nk = T // tk

    def kern(g_ref, u_ref, o_ref):
        k = pl.program_id(1)
        part = _dg(g_ref[...], u_ref[...], 0, 0)

        @pl.when(k == 0)
        def _():
            o_ref[...] = part

        @pl.when(k > 0)
        def _():
            o_ref[...] += part

    return pl.pallas_call(
        kern, grid=(4, nk),
        in_specs=[pl.BlockSpec((tk, 512), lambda b, k: (k, b)), pl.BlockSpec((tk, D), lambda b, k: (k, b))],
        out_specs=pl.BlockSpec((None, 512, D), lambda b, k: (b, 0, 0)), out_shape=jax.ShapeDtypeStruct((4, 512, D), F32),
        compiler_params=_cp("parallel", "arbitrary"), name="branch_dw")(G, dU)


def na_tables(S, Cn):
    R = S // GRID_W
    ntile = (Cn + S) // 128
    nct = Cn // 128
    rs = np.clip(np.arange(R) - NA_WIN_ROWS // 2, 0, R - NA_WIN_ROWS)
    seen, pats, cls, ws = {}, [], [], []
    for t in range(ntile):
        pat = -np.ones((NA_TILE_ROWS, NA_BAND), np.int64)
        w0 = 0
        if t >= nct:
            r0 = NA_TILE_ROWS * (t - nct)
            w0 = min(rs[r0], R - NA_BAND)
            for a in range(NA_TILE_ROWS):
                for j in range(NA_BAND):
                    kr = w0 + j
                    if rs[r0 + a] <= kr < rs[r0 + a] + NA_WIN_ROWS:
                        pat[a, j] = kr - (r0 + a) + NA_WIN_ROWS - 1
        key = pat.tobytes()
        if key not in seen:
            seen[key] = len(pats)
            pats.append(pat)
        cls.append(seen[key])
        ws.append(Cn + GRID_W * int(w0))
    cls = np.asarray(cls, np.int32)
    first = np.asarray([1 if t == 0 or cls[t] != cls[t - 1] else 0 for t in range(ntile)], np.int32)
    assert len(set(cls[first == 1].tolist())) == int(first.sum())
    pats = np.stack(pats)
    ncls = pats.shape[0]
    nrow = -(-(ncls * NA_TILE_ROWS * NA_BAND) // 128) * 128
    m1 = np.zeros((nrow, 128), np.float32)
    flat = pats.reshape(-1)
    for k, dr in enumerate(flat):
        if dr >= 0:
            m1[k, dr] = 1.0
    qc = np.arange(GRID_W)
    col_start = np.clip(qc - NA_WIN_COLS // 2, 0, GRID_W - NA_WIN_COLS)
    kc = np.arange(GRID_W)
    col_ok = (kc[None, :] >= col_start[:, None]) & (kc[None, :] < col_start[:, None] + NA_WIN_COLS)
    m2 = np.zeros((128, GRID_W * GRID_W), np.float32)
    for q in range(GRID_W):
        for k in range(GRID_W):
            if col_ok[q, k]:
                m2[k - q + NA_WIN_COLS - 1, q * GRID_W + k] = 1.0
    valid = (pats >= 0)[:, :, :, None, None] & col_ok[None, None, None]
    vmask = np.transpose(valid, (0, 2, 4, 1, 3)).reshape(ncls, 1, NA_BAND * GRID_W, 1, NA_TILE_ROWS * GRID_W)
    vmask = np.broadcast_to(vmask, (ncls, 1, NA_BAND * GRID_W, 2, NA_TILE_ROWS * GRID_W)).reshape(ncls, 1, NA_BAND * GRID_W, 256)
    return dict(cls=cls, ws=np.asarray(ws, np.int32), first=first, m1=m1, m2=m2, vmask=vmask, ncls=ncls, nrow=nrow)


def rpb_map(name, x, left, right):
    H = x.shape[0]

    def kern(x_ref, l_ref, r_ref, o_ref):
        o_ref[...] = hdot(hdot(l_ref[...], x_ref[...]), r_ref[...])

    return pl.pallas_call(
        kern, grid=(H,),
        in_specs=[pl.BlockSpec((None,) + x.shape[1:], lambda h: (h, 0, 0)), pl.BlockSpec(left.shape, lambda h: (0, 0)),
                  pl.BlockSpec(right.shape, lambda h: (0, 0))],
        out_specs=pl.BlockSpec((None, left.shape[0], right.shape[1]), lambda h: (h, 0, 0)),
        out_shape=jax.ShapeDtypeStruct((H, left.shape[0], right.shape[1]), F32), compiler_params=_cp("parallel"), name=name)(x, left, right)


def bias_table(rpb, tb):
    H = rpb.shape[0]
    xp = jnp.zeros((H, 128, 128), F32).at[:, :rpb.shape[1], :rpb.shape[2]].set(rpb)
    a = rpb_map("rpb_expand", xp, jnp.asarray(tb["m1"]), jnp.asarray(tb["m2"]))
    ncls = tb["ncls"]
    a = a[:, :ncls * NA_TILE_ROWS * NA_BAND].reshape(H // 2, 2, ncls, NA_TILE_ROWS, NA_BAND, GRID_W, GRID_W)
    a = jnp.transpose(a, (2, 0, 4, 6, 1, 3, 5)).reshape(ncls, H // 2, NA_BAND * GRID_W, 256)
    return jnp.where(jnp.asarray(tb["vmask"]), a, NEG)


def bias_table_grad(dbt, tb):
    ncls = tb["ncls"]
    H2 = dbt.shape[1]
    d = dbt.reshape(ncls, H2, NA_BAND, GRID_W, 2, NA_TILE_ROWS, GRID_W)
    d = jnp.transpose(d, (1, 4, 0, 5, 2, 6, 3)).reshape(H2 * 2, ncls * NA_TILE_ROWS * NA_BAND, GRID_W * GRID_W)
    d = jnp.pad(d, ((0, 0), (0, tb["nrow"] - d.shape[1]), (0, 0)))
    g = rpb_map("rpb_reduce", d, jnp.asarray(tb["m1"].T.copy()), jnp.asarray(tb["m2"].T.copy()))
    return g[:, :2 * NA_WIN_ROWS - 1, :2 * NA_WIN_COLS - 1]


def f_na(q, kb, vb, kc, vc, bias_t):
    d = lax.broadcasted_iota(jnp.int32, (128, 1), 0)
    q_t = q.T
    qbd = jnp.concatenate([jnp.where(d < 64, q_t, 0.0), jnp.where(d >= 64, q_t, 0.0)], axis=1)
    sb = bdot(kb, qbd) * 0.125 + bias_t
    sc = bdot(kc, qbd) * 0.125
    m = jnp.maximum(jnp.max(sb, axis=0, keepdims=True), jnp.max(sc, axis=0, keepdims=True))
    eb = jnp.exp(sb - m)
    ec = jnp.exp(sc - m)
    den = jnp.sum(eb, axis=0, keepdims=True) + jnp.sum(ec, axis=0, keepdims=True)
    of = bdot(eb / den, vb, 0, 0) + bdot(ec / den, vc, 0, 0)
    o0, o1 = split_rows(of, 2)
    return jnp.where(_lane(o0.shape) < 64, o0, o1)


def _na_specs(T, Cn, nband):
    qs = pl.BlockSpec((128, 128), lambda hp, t, *_: (t, CB["a_q"] + hp))
    ks = pl.BlockSpec((T, 128), lambda hp, t, *_: (0, CB["a_k"] + hp))
    vs = pl.BlockSpec((T, 128), lambda hp, t, *_: (0, CB["a_v"] + hp))
    bs = pl.BlockSpec((None, None, nband, 256), lambda hp, t, cls, ws, first: (cls[t], hp, 0, 0))
    return qs, ks, vs, bs


def na_fwd(P, bt, tb, Cn):
    T = P.shape[0]
    nband = NA_BAND * GRID_W
    qs, ks, vs, bs = _na_specs(T, Cn, nband)

    def kern(cls, ws, first, q_ref, k_ref, v_ref, b_ref, o_ref):
        w0 = pl.multiple_of(ws[pl.program_id(1)], 64)
        o_ref[...] = f_na(q_ref[...], k_ref[pl.ds(w0, nband), :], v_ref[pl.ds(w0, nband), :], k_ref[0:Cn, :], v_ref[0:Cn, :], b_ref[...])

    return pl.pallas_call(
        kern, grid_spec=pltpu.PrefetchScalarGridSpec(
            num_scalar_prefetch=3, grid=(4, T // 128), in_specs=[qs, ks, vs, bs],
            out_specs=pl.BlockSpec((128, 128), lambda hp, t, *_: (t, hp))),
        out_shape=jax.ShapeDtypeStruct((T, 512), F32), compiler_params=_cp("arbitrary", "arbitrary"), name="na_fwd",
    )(jnp.asarray(tb["cls"]), jnp.asarray(tb["ws"]), jnp.asarray(tb["first"]), P, P, P, bt)


def na_bwd(P, bt, dO, tb, Cn):
    T = P.shape[0]
    nband = NA_BAND * GRID_W
    qs, ks, vs, bs = _na_specs(T, Cn, nband)
    dos = pl.BlockSpec((128, 128), lambda hp, t, *_: (t, hp))

    def kern(cls, ws, first, q_ref, k_ref, v_ref, b_ref, do_ref, dq_ref, dk_ref, dv_ref, db_ref):
        t = pl.program_id(1)
        w0 = pl.multiple_of(ws[t], 64)
        band = pl.ds(w0, nband)
        _, vjp = jax.vjp(f_na, q_ref[...], k_ref[band, :], v_ref[band, :], k_ref[0:Cn, :], v_ref[0:Cn, :], b_ref[...])
        dq, dkb, dvb, dkc, dvc, db = vjp(do_ref[...])
        dq_ref[...] = dq

        @pl.when(t == 0)
        def _():
            dk_ref[...] = jnp.zeros_like(dk_ref)
            dv_ref[...] = jnp.zeros_like(dv_ref)

        dk_ref[band, :] += dkb
        dv_ref[band, :] += dvb
        dk_ref[0:Cn, :] += dkc
        dv_ref[0:Cn, :] += dvc

        @pl.when(first[t] == 1)
        def _():
            db_ref[...] = db

        @pl.when(first[t] == 0)
        def _():
            db_ref[...] += db

    full = lambda hp, t, *_: (0, hp)
    return pl.pallas_call(
        kern, grid_spec=pltpu.PrefetchScalarGridSpec(
            num_scalar_prefetch=3, grid=(4, T // 128), in_specs=[qs, ks, vs, bs, dos],
            out_specs=[pl.BlockSpec((128, 128), lambda hp, t, *_: (t, hp)), pl.BlockSpec((T, 128), full), pl.BlockSpec((T, 128), full),
                       pl.BlockSpec((None, None, nband, 256), lambda hp, t, cls, ws, first: (cls[t], hp, 0, 0))]),
        out_shape=[jax.ShapeDtypeStruct((T, 512), F32), jax.ShapeDtypeStruct((T, 512), F32), jax.ShapeDtypeStruct((T, 512), F32),
                   jax.ShapeDtypeStruct(bt.shape, F32)],
        compiler_params=_cp("arbitrary", "arbitrary"), name="na_bwd",
    )(jnp.asarray(tb["cls"]), jnp.asarray(tb["ws"]), jnp.asarray(tb["first"]), P, P, P, bt, dO)


def _expand_heads(blk, g):
    out = []
    for p in range(2):
        pair = blk[:, 128 * p:128 * (p + 1)]
        lane = _lane(pair.shape)
        a = jnp.where(lane < 64, pair, 0.0)
        b = jnp.where(lane >= 64, pair, 0.0)
        out.append(jnp.where(g == 0, a, pltpu.roll(a, 64, 1)))
        out.append(jnp.where(g == 0, pltpu.roll(b, 64, 1), b))
    return out


def _compact_heads(hs, g):
    out = []
    for p in range(2):
        e, o = hs[2 * p], hs[2 * p + 1]
        lane = _lane(e.shape)
        e0 = jnp.where(g == 0, e, pltpu.roll(e, 64, 1))
        o1 = jnp.where(g == 0, pltpu.roll(o, 64, 1), o)
        out.append(jnp.where(lane < 64, e0, o1))
    return jnp.concatenate(out, axis=1)


def _kv_map(Cn, tq, tk, col):
    nq_ctx = Cn // tq
    last_ctx = (Cn - 1) // tk

    def f(g, qi, kj):
        return (jnp.where(qi < nq_ctx, jnp.minimum(kj, last_ctx), kj), col)
    return f


def flash_fwd(Qp, KV, Cn, tq, tk):
    T = Qp.shape[0]
    nq, nk = T // tq, T // tk
    nq_ctx = Cn // tq

    def kern(q_ref, k_ref, v_ref, o_ref, lse_ref, m_s, l_s, acc_s):
        g, qi, kj = pl.program_id(0), pl.program_id(1), pl.program_id(2)

        @pl.when(kj == 0)
        def _():
            m_s[...] = jnp.full_like(m_s, NEG)
            l_s[...] = jnp.zeros_like(l_s)
            acc_s[...] = jnp.zeros_like(acc_s)

        def step(masked):
            k = k_ref[...]
            v = v_ref[...]
            if masked:
                valid = kj * tk + lax.broadcasted_iota(jnp.int32, (1, tk), 1) < Cn
            for h in range(4):
                rows = pl.ds(tq * h, tq)
                s = _dg(q_ref[:, 128 * h:128 * (h + 1)], k, 1, 1)
                if masked:
                    s = jnp.where(valid, s, NEG)
                m_old = m_s[rows, :]
                m_new = jnp.maximum(m_old, jnp.max(s, axis=1, keepdims=True))
                alpha = jnp.exp(m_old - m_new)
                p = jnp.exp(s - m_new)
                l_s[rows, :] = alpha * l_s[rows, :] + jnp.sum(p, axis=1, keepdims=True)
                acc_s[rows, :] = alpha * acc_s[rows, :] + _dg(p, v, 1, 0)
                m_s[rows, :] = m_new

        @pl.when(qi >= nq_ctx)
        def _():
            step(False)

        @pl.when((qi < nq_ctx) & (kj * tk < Cn))
        def _():
            step(True)

        @pl.when(kj == nk - 1)
        def _():
            o4 = acc_s[...] / l_s[...]
            o_ref[...] = _compact_heads([o4[tq * h:tq * (h + 1)] for h in range(4)], g)
            lse = m_s[...] + jnp.log(l_s[...])
            lse_ref[...] = jnp.concatenate([jnp.broadcast_to(lse[tq * h:tq * (h + 1)], (tq, 128)) for h in range(4)], axis=1)

    return pl.pallas_call(
        kern, grid=(2, nq, nk),
        in_specs=[pl.BlockSpec((tq, 512), lambda g, qi, kj: (qi, g)), pl.BlockSpec((tk, 128), _kv_map(Cn, tq, tk, 0)),
                  pl.BlockSpec((tk, 128), _kv_map(Cn, tq, tk, 1))],
        out_specs=[pl.BlockSpec((tq, 256), lambda g, qi, kj: (qi, g)), pl.BlockSpec((tq, 512), lambda g, qi, kj: (qi, g))],
        out_shape=[jax.ShapeDtypeStruct((T, 512), F32), jax.ShapeDtypeStruct((T, 1024), F32)],
        scratch_shapes=[pltpu.VMEM((4 * tq, 1), F32), pltpu.VMEM((4 * tq, 1), F32), pltpu.VMEM((4 * tq, 128), F32)],
        compiler_params=_cp("arbitrary", "arbitrary", "arbitrary"), name="flash_fwd")(Qp, KV, KV)


def flash_bwd(Qp, KV, O, LSE, dOall, Cn, tq, tk):
    T = Qp.shape[0]
    nq, nk = T // tq, T // tk
    nq_ctx = Cn // tq

    def kern(q_ref, k_ref, v_ref, o_ref, lse_ref, do_ref, dq_ref, dkv_ref, dq_s, do_s, dl_s, ls_s):
        g, qi, kj = pl.program_id(0), pl.program_id(1), pl.program_id(2)

        @pl.when((g == 0) & (qi == 0) & (kj == 0))
        def _():
            dkv_ref[...] = jnp.zeros_like(dkv_ref)

        @pl.when(kj == 0)
        def _():
            do4 = jnp.concatenate(_expand_heads(do_ref[...], g), axis=0)
            o4 = jnp.concatenate(_expand_heads(o_ref[...], g), axis=0)
            do_s[...] = do4.astype(BF)
            dl_s[...] = jnp.sum(do4 * o4, axis=1, keepdims=True)
            ls_s[...] = jnp.concatenate([jnp.max(lse_ref[:, 128 * h:128 * (h + 1)], axis=1, keepdims=True) for h in range(4)], axis=0)
            dq_s[...] = jnp.zeros_like(dq_s)

        def step(masked):
            k = k_ref[...]
            v = v_ref[...]
            if masked:
                valid = kj * tk + lax.broadcasted_iota(jnp.int32, (1, tk), 1) < Cn
            dk = dv = None
            for h in range(4):
                rows = pl.ds(tq * h, tq)
                qh = q_ref[:, 128 * h:128 * (h + 1)]
                s = _dg(qh, k, 1, 1)
                if masked:
                    s = jnp.where(valid, s, NEG)
                p = jnp.exp(s - ls_s[rows, :])
                doh = do_s[rows, :]
                dvh = _dg(p, doh, 0, 0)
                dp = _dg(doh, v, 1, 1)
                ds = p * (dp - dl_s[rows, :])
                dq_s[rows, :] += _dg(ds, k, 1, 0)
                dkh = _dg(ds, qh, 0, 0)
                dk = dkh if dk is None else dk + dkh
                dv = dvh if dv is None else dv + dvh
            krows = pl.ds(pl.multiple_of(kj * tk, tk), tk)
            dkv_ref[krows, 0:128] += dk
            dkv_ref[krows, 128:256] += dv

        @pl.when(qi >= nq_ctx)
        def _():
            step(False)

        @pl.when((qi < nq_ctx) & (kj * tk < Cn))
        def _():
            step(True)

        @pl.when(kj == nk - 1)
        def _():
            dq = dq_s[...]
            dq_ref[...] = jnp.concatenate([dq[tq * h:tq * (h + 1)] for h in range(4)], axis=1)

    return pl.pallas_call(
        kern, grid=(2, nq, nk),
        in_specs=[pl.BlockSpec((tq, 512), lambda g, qi, kj: (qi, g)), pl.BlockSpec((tk, 128), _kv_map(Cn, tq, tk, 0)),
                  pl.BlockSpec((tk, 128), _kv_map(Cn, tq, tk, 1)), pl.BlockSpec((tq, 256), lambda g, qi, kj: (qi, g)),
                  pl.BlockSpec((tq, 512), lambda g, qi, kj: (qi, g)), pl.BlockSpec((tq, 256), lambda g, qi, kj: (qi, 4 + g))],
        out_specs=[pl.BlockSpec((tq, 512), lambda g, qi, kj: (qi, g)), pl.BlockSpec((T, 256), lambda g, qi, kj: (0, 0))],
        out_shape=[jax.ShapeDtypeStruct((T, 1024), F32), jax.ShapeDtypeStruct((T, 256), F32)],
        scratch_shapes=[pltpu.VMEM((4 * tq, 128), F32), pltpu.VMEM((4 * tq, 128), BF), pltpu.VMEM((4 * tq, 1), F32), pltpu.VMEM((4 * tq, 1), F32)],
        compiler_params=_cp("arbitrary", "arbitrary", "arbitrary"), name="flash_bwd")(Qp, KV, KV, O, LSE, dOall)


def _pool_band(t0, w0, n, win, gi, Cn, T):
    i = lax.broadcasted_iota(jnp.int32, (n, win), 0)
    jx = lax.broadcasted_iota(jnp.int32, (n, win), 1)
    t = t0 + i
    tp = w0 + jx
    half = lax.shift_left(jnp.int32(1), gi)
    lo = jnp.maximum(t - half, jnp.where(t < Cn, 0, Cn))
    hi = jnp.minimum(t + half - 1, jnp.where(t < Cn, Cn, T) - 1)
    cnt = (hi - lo + 1).astype(F32)
    return jnp.where((tp >= lo) & (tp <= hi), 1.0 / cnt, 0.0) - jnp.where(tp == t, 1.0, 0.0)


def _pool_geom(T):
    n = _pick(T, (256, 128))
    return n, n + 128


def pool_fwd(P, pw, ps, Cn):
    T = P.shape[0]
    n, win = _pool_geom(T)

    def kern(u_ref, w_ref, s_ref, o_ref):
        gi = pl.program_id(0)

        def blk(b, carry):
            t0 = pl.multiple_of(b * n, n)
            w0 = pl.multiple_of(jnp.clip(t0 - 64, 0, T - win), 64)
            d = hdot(_pool_band(t0, w0, n, win, gi, Cn, T), u_ref[pl.ds(w0, win), :])
            o_ref[pl.ds(t0, n), :] = bdot(d, w_ref[...]) * s_ref[...]
            return carry

        lax.fori_loop(0, T // n, blk, 0)

    return pl.pallas_call(
        kern, grid=(4,),
        in_specs=[pl.BlockSpec((T, 128), lambda g: (0, CB["b_in"] + g)), pl.BlockSpec((None, 128, 128), lambda g: (g, 0, 0)),
                  pl.BlockSpec((1, 128), lambda g: (0, g))],
        out_specs=pl.BlockSpec((T, 128), lambda g: (0, g)), out_shape=jax.ShapeDtypeStruct((T, 512), F32),
        compiler_params=_cp("arbitrary"), name="pool_fwd")(P, pw, ps)


def pool_bwd(P, pw, ps, dOall, dP, Cn):
    T = P.shape[0]
    n, win = _pool_geom(T)

    def kern(u_ref, w_ref, s_ref, do_ref, dp_in, du_out, dw_ref, ds_ref, du_ref):
        gi = pl.program_id(0)
        du_ref[...] = jnp.zeros_like(du_ref)
        dw_ref[...] = jnp.zeros_like(dw_ref)
        ds_ref[...] = jnp.zeros_like(ds_ref)

        def blk(b, carry):
            t0 = pl.multiple_of(b * n, n)
            w0 = pl.multiple_of(jnp.clip(t0 - 64, 0, T - win), 64)
            band = _pool_band(t0, w0, n, win, gi, Cn, T)
            _, vjp = jax.vjp(lambda uw, w, s: bdot(hdot(band, uw), w) * s, u_ref[pl.ds(w0, win), :], w_ref[...], s_ref[...])
            duw, dw, ds = vjp(do_ref[pl.ds(t0, n), :])
            du_ref[pl.ds(w0, win), :] += duw
            dw_ref[...] += dw
            ds_ref[...] += ds
            return carry

        lax.fori_loop(0, T // n, blk, 0)
        du_out[...] = du_ref[...].astype(du_out.dtype)

    return pl.pallas_call(
        kern, grid=(4,),
        in_specs=[pl.BlockSpec((T, 128), lambda g: (0, CB["b_in"] + g)), pl.BlockSpec((None, 128, 128), lambda g: (g, 0, 0)),
                  pl.BlockSpec((1, 128), lambda g: (0, g)), pl.BlockSpec((T, 128), lambda g: (0, 4 + g)), pl.BlockSpec(memory_space=pl.ANY)],
        out_specs=[pl.BlockSpec((T, 128), lambda g: (0, CB["b_in"] + g)), pl.BlockSpec((None, 128, 128), lambda g: (g, 0, 0)),
                   pl.BlockSpec((1, 128), lambda g: (0, g))],
        out_shape=[jax.ShapeDtypeStruct(dP.shape, dP.dtype), jax.ShapeDtypeStruct((4, 128, 128), F32), jax.ShapeDtypeStruct((1, 512), F32)],
        scratch_shapes=[pltpu.VMEM((T, 128), F32)],
        input_output_aliases={4: 0}, compiler_params=_cp("arbitrary"), name="pool_bwd")(P, pw, ps, dOall, dP)


CONV_HALO = 16


def _conv_block(T):
    return _pick(T, (256, 128))


def _stage(dst_ref, src, t0, n, T, Cn):
    h = CONV_HALO
    dst_ref[h:h + n, :] = src(pl.ds(t0, n))
    left_ok = (t0 != 0) & (t0 != Cn)
    right_ok = (t0 + n != Cn) & (t0 + n != T)
    lo = pl.multiple_of(jnp.maximum(t0 - h, 0), 8)
    hi = pl.multiple_of(jnp.minimum(t0 + n, T - h), 8)
    dst_ref[0:h, :] = jnp.where(left_ok, src(pl.ds(lo, h)), 0.0)
    dst_ref[h + n:2 * h + n, :] = jnp.where(right_ok, src(pl.ds(hi, h)), 0.0)


def conv_fwd(P, cw, cb, Cn):
    T = P.shape[0]
    n = _conv_block(T)
    off = CONV_HALO - CONV_WIDTH // 2

    def kern(a_ref, g_ref, w_ref, b_ref, y_ref, us):
        def blk(b, carry):
            t0 = pl.multiple_of(b * n, n)
            _stage(us, lambda r: a_ref[r, :] * jax.nn.sigmoid(g_ref[r, :]), t0, n, T, Cn)
            acc = jnp.zeros((n, 128), F32)
            for k in range(CONV_WIDTH):
                acc = acc + us[k + off:k + off + n, :] * w_ref[k:k + 1, :]
            y_ref[pl.ds(t0, n), :] = acc + b_ref[...]
            return carry

        lax.fori_loop(0, T // n, blk, 0)

    slab = lambda o: pl.BlockSpec((T, 128), lambda c: (0, o + c))
    return pl.pallas_call(
        kern, grid=(4,),
        in_specs=[slab(CB["d_glu"]), slab(CB["d_glu"] + 4), pl.BlockSpec((CONV_PAD, 128), lambda c: (0, c)), pl.BlockSpec((1, 128), lambda c: (0, c))],
        out_specs=pl.BlockSpec((T, 128), lambda c: (0, c)), out_shape=jax.ShapeDtypeStruct((T, 512), F32),
        scratch_shapes=[pltpu.VMEM((n + 2 * CONV_HALO, 128), F32)],
        compiler_params=_cp("arbitrary"), name="conv_fwd")(P, P, cw, cb)


def conv_bwd(P, cw, dY, dP, Cn):
    T = P.shape[0]
    n = _conv_block(T)
    off = CONV_HALO - CONV_WIDTH // 2
    back = CONV_HALO + CONV_WIDTH // 2

    def kern(a_ref, g_ref, w_ref, dy_ref, dp_in, da_ref, dg_ref, dw_ref, db_ref, us, dys):
        dw_ref[...] = jnp.zeros_like(dw_ref)
        db_ref[...] = jnp.zeros_like(db_ref)

        def blk(b, carry):
            t0 = pl.multiple_of(b * n, n)
            cur = pl.ds(t0, n)
            _stage(us, lambda r: a_ref[r, :] * jax.nn.sigmoid(g_ref[r, :]), t0, n, T, Cn)
            _stage(dys, lambda r: dy_ref[r, :], t0, n, T, Cn)
            dyc = dy_ref[cur, :]
            du = jnp.zeros((n, 128), F32)
            for k in range(CONV_WIDTH):
                du = du + dys[back - k:back - k + n, :] * w_ref[k:k + 1, :]
                dw_ref[k:k + 1, :] += jnp.sum(us[k + off:k + off + n, :] * dyc, axis=0, keepdims=True)
            a = a_ref[cur, :]
            sig = jax.nn.sigmoid(g_ref[cur, :])
            da_ref[cur, :] = (du * sig).astype(da_ref.dtype)
            dg_ref[cur, :] = du * a * sig * (1.0 - sig)
            db_ref[...] += jnp.sum(dyc, axis=0, keepdims=True)
            return carry

        lax.fori_loop(0, T // n, blk, 0)

    slab = lambda o: pl.BlockSpec((T, 128), lambda c: (0, o + c))
    return pl.pallas_call(
        kern, grid=(4,),
        in_specs=[slab(CB["d_glu"]), slab(CB["d_glu"] + 4), pl.BlockSpec((CONV_PAD, 128), lambda c: (0, c)), slab(0),
                  pl.BlockSpec(memory_space=pl.ANY)],
        out_specs=[slab(CB["d_glu"]), slab(0), pl.BlockSpec((CONV_PAD, 128), lambda c: (0, c)), pl.BlockSpec((1, 128), lambda c: (0, c))],
        out_shape=[jax.ShapeDtypeStruct(dP.shape, dP.dtype), jax.ShapeDtypeStruct((T, 512), F32), jax.ShapeDtypeStruct((CONV_PAD, 512), F32),
                   jax.ShapeDtypeStruct((1, 512), F32)],
        scratch_shapes=[pltpu.VMEM((n + 2 * CONV_HALO, 128), F32), pltpu.VMEM((n + 2 * CONV_HALO, 128), F32)],
        input_output_aliases={4: 0}, compiler_params=_cp("arbitrary"), name="conv_bwd")(P, P, cw, dY, dP)


def copy_cols(name, src, dP, col0, Cn, tm):
    T = dP.shape[0]

    def body(i, j, r, c, s):
        return [r[0]], [], []

    return rowcall(name, body, T, tm, src.shape[1] // 128, Cn, [(src, 128, lambda j: j)], [], [], [("into", dP, 128, lambda j: col0 + j)])[0]


def ada_fwd(cvec, w, b):
    L, D, wc = w.shape

    def kern(c_ref, w_ref, b_ref, o_ref):
        o_ref[...] = _dg(_silu(c_ref[...]), w_ref[...], 1, 0) + b_ref[...]

    return pl.pallas_call(
        kern, grid=(L,),
        in_specs=[pl.BlockSpec((16, D), lambda l: (0, 0)), pl.BlockSpec((None, D, wc), lambda l: (l, 0, 0)), pl.BlockSpec((None, 1, wc), lambda l: (l, 0, 0))],
        out_specs=pl.BlockSpec((None, 16, wc), lambda l: (l, 0, 0)), out_shape=jax.ShapeDtypeStruct((L, 16, wc), F32),
        compiler_params=_cp("arbitrary"), name="ada_fwd")(cvec, w, b)


def ada_bwd(cvec, w, dm):
    L, D, wc = w.shape

    def kern(c_ref, w_ref, d_ref, gw_ref, ds_ref):
        l = pl.program_id(0)
        d = d_ref[...]
        gw_ref[...] = _dg(_silu(c_ref[...]), d, 0, 0)
        part = _dg(d, w_ref[...], 1, 1)

        @pl.when(l == 0)
        def _():
            ds_ref[...] = part

        @pl.when(l > 0)
        def _():
            ds_ref[...] += part

    return pl.pallas_call(
        kern, grid=(L,),
        in_specs=[pl.BlockSpec((16, D), lambda l: (0, 0)), pl.BlockSpec((None, D, wc), lambda l: (l, 0, 0)), pl.BlockSpec((None, 16, wc), lambda l: (l, 0, 0))],
        out_specs=[pl.BlockSpec((None, D, wc), lambda l: (l, 0, 0)), pl.BlockSpec((16, D), lambda l: (0, 0))],
        out_shape=[jax.ShapeDtypeStruct((L, D, wc), F32), jax.ShapeDtypeStruct((16, D), F32)],
        compiler_params=_cp("arbitrary"), name="ada_bwd")(cvec, w, dm)


def silu_grad(cc, parts):
    D = cc.shape[1]

    def kern(c_ref, p_ref, o_ref):
        tot = p_ref[0]
        for k in range(1, N_DEV):
            tot = tot + p_ref[k]
        _, vjp = jax.vjp(_silu, c_ref[...])
        o_ref[...] = vjp(tot)[0]

    return pl.pallas_call(kern, out_shape=jax.ShapeDtypeStruct((1, D), F32), name="silu_grad")(cc, parts)


def _flip(v, f):
    return 1 - v if f else v


_REL = ((0, 0), (1, 0), (0, 1), (1, 1))


def allgather(name, x):
    def body(x_ref, out_ref, send_sems, recv_sems, local_sem):
        mx, my, mc = lax.axis_index("x"), lax.axis_index("y"), lax.axis_index("c")
        me, sibling = (mx, my, mc), (mx, my, 1 - mc)
        chips = [(_flip(mx, fx), _flip(my, fy)) for fx, fy in _REL[1:]]

        def slot(px, py, pc):
            return out_ref.at[4 * px + 2 * py + pc]

        def copy(k, block, to, src=None):
            return pltpu.make_async_remote_copy(
                src_ref=slot(*block) if src is None else src, dst_ref=slot(*block),
                send_sem=send_sems.at[k], recv_sem=recv_sems.at[k], device_id=to, device_id_type=MESH)

        mine = pltpu.make_async_copy(x_ref, slot(*me), local_sem)
        mine.start()
        first = [copy(0, me, sibling, src=x_ref)] + [copy(1 + j, me, (*chip, mc), src=x_ref) for j, chip in enumerate(chips)]
        for cp in first:
            cp.start()
        passed = [copy(4 + j, (*chip, mc), sibling) for j, chip in enumerate(chips)]
        for j, chip in enumerate(chips):
            copy(1 + j, (*chip, mc), me).wait_recv()
            passed[j].start()
        copy(0, sibling, me).wait_recv()
        for j, chip in enumerate(chips):
            copy(4 + j, (*chip, 1 - mc), me).wait_recv()
        for cp in first + passed:
            cp.wait_send()
        mine.wait()

    return pl.pallas_call(
        body, out_shape=jax.ShapeDtypeStruct((N_DEV,) + x.shape, x.dtype),
        in_specs=[pl.BlockSpec(memory_space=pl.ANY)], out_specs=pl.BlockSpec(memory_space=pl.ANY),
        scratch_shapes=[pltpu.SemaphoreType.DMA((7,)), pltpu.SemaphoreType.DMA((7,)), pltpu.SemaphoreType.DMA(())], name=name)(x)


def cols_from_devs(name, xg, l, nrow):
    _, _, C = xg.shape
    tr = _row_tile(nrow, N_DEV * C, xg.dtype.itemsize, 16, 8 << 20)
    nblk = nrow // tr

    def body(x_ref, o_ref):
        for d in range(N_DEV):
            o_ref[:, d * C:(d + 1) * C] = x_ref[d]

    return pl.pallas_call(
        body, grid=(nblk,), in_specs=[pl.BlockSpec((N_DEV, tr, C), lambda i: (0, l * nblk + i, 0))],
        out_specs=pl.BlockSpec((tr, N_DEV * C), lambda i: (i, 0)), out_shape=jax.ShapeDtypeStruct((nrow, N_DEV * C), xg.dtype),
        compiler_params=_cp("parallel"), name=name)(xg)


def cols_to_devs(name, g, l, L, prev):
    nrow, W = g.shape
    C = W // N_DEV
    tr = _row_tile(nrow, W, 4, 8, 8 << 20)
    nblk = nrow // tr

    def body(x_ref, *rest):
        o_ref = rest[-1]
        for d in range(N_DEV):
            o_ref[d] = x_ref[:, d * C:(d + 1) * C]

    in_specs, args, aliases = [pl.BlockSpec((tr, W), lambda i: (i, 0))], [g], {}
    if prev is not None:
        in_specs.append(pl.BlockSpec(memory_space=pl.ANY))
        args.append(prev)
        aliases = {1: 0}
    return pl.pallas_call(
        body, grid=(nblk,), in_specs=in_specs, out_specs=pl.BlockSpec((N_DEV, tr, C), lambda i: (0, l * nblk + i, 0)),
        out_shape=jax.ShapeDtypeStruct((N_DEV, L * nrow, C), F32), input_output_aliases=aliases,
        compiler_params=_cp("parallel"), name=name)(*args)


def proj_in_gather(H, W, shards, row0, tm, tn):
    T, D = H.shape
    IN = W.shape[1]
    C = shards.shape[1]
    ni, nj = T // tm, IN // tn

    def kern(a_ref, b_ref, x_all, o_ref, g_ref, send_sems, recv_sems, local_sem):
        i, j = pl.program_id(0), pl.program_id(1)
        mx, my, mc = lax.axis_index("x"), lax.axis_index("y"), lax.axis_index("c")
        me, sibling = (mx, my, mc), (mx, my, 1 - mc)
        chips = [(_flip(mx, fx), _flip(my, fy)) for fx, fy in _REL[1:]]
        x_ref = x_all.at[pl.ds(row0, D)]

        def slot(px, py, pc):
            return g_ref.at[4 * px + 2 * py + pc]

        def copy(k, block, to, src=None):
            return pltpu.make_async_remote_copy(
                src_ref=slot(*block) if src is None else src, dst_ref=slot(*block),
                send_sem=send_sems.at[k], recv_sem=recv_sems.at[k], device_id=to, device_id_type=MESH)

        mine = pltpu.make_async_copy(x_ref, slot(*me), local_sem)
        first = [copy(0, me, sibling, src=x_ref)] + [copy(1 + q, me, (*chip, mc), src=x_ref) for q, chip in enumerate(chips)]
        passed = [copy(4 + q, (*chip, mc), sibling) for q, chip in enumerate(chips)]

        @pl.when((i == 0) & (j == 0))
        def _():
            mine.start()
            for cp in first:
                cp.start()

        o_ref[...] = _dg(a_ref[...], b_ref[...], 1, 0)

        @pl.when((i == ni - 2) & (j == 0))
        def _():
            for q, chip in enumerate(chips):
                copy(1 + q, (*chip, mc), me).wait_recv()
                passed[q].start()

        @pl.when((i == ni - 1) & (j == nj - 1))
        def _():
            copy(0, sibling, me).wait_recv()
            for q, chip in enumerate(chips):
                copy(4 + q, (*chip, 1 - mc), me).wait_recv()
            for cp in first + passed:
                cp.wait_send()
            mine.wait()

    assert ni >= 3
    return pl.pallas_call(
        kern, grid=(ni, nj),
        in_specs=[pl.BlockSpec((tm, D), lambda i, j: (i, 0)), pl.BlockSpec((D, tn), lambda i, j: (0, j)), pl.BlockSpec(memory_space=pl.ANY)],
        out_specs=[pl.BlockSpec((tm, tn), lambda i, j: (i, j)), pl.BlockSpec(memory_space=pl.ANY)],
        out_shape=[jax.ShapeDtypeStruct((T, IN), F32), jax.ShapeDtypeStruct((N_DEV, D, C), shards.dtype)],
        scratch_shapes=[pltpu.SemaphoreType.DMA((7,)), pltpu.SemaphoreType.DMA((7,)), pltpu.SemaphoreType.DMA(())],
        compiler_params=_cp("arbitrary", "arbitrary"), name="proj_in_gather")(H, W, shards)


def matmul_nt_exchange(name, a, b, sb, *, tm, tn, tk):
    M, K = a.shape
    N = b.shape[0]
    assert b.shape[1] == K and M % tm == 0 and N % tn == 0 and K % tk == 0
    ni, nj, nk = M // tm, N // tn, K // tk

    def kern(a_ref, b_ref, sb_ref, o_ref, r_ref, acc_ref, send_sems, recv_sems):
        i, j, k = pl.program_id(0), pl.program_id(1), pl.program_id(2)
        mx, my, mc = lax.axis_index("x"), lax.axis_index("y"), lax.axis_index("c")
        cps = [pltpu.make_async_remote_copy(src_ref=sb_ref.at[q], dst_ref=r_ref.at[q], send_sem=send_sems.at[q - 1],
                                            recv_sem=recv_sems.at[q - 1], device_id=(_flip(mx, fx), _flip(my, fy), mc), device_id_type=MESH)
               for q, (fx, fy) in enumerate(_REL) if q > 0]

        @pl.when((i == 0) & (j == 0) & (k == 0))
        def _():
            for cp in cps:
                cp.start()

        part = _dg(a_ref[...], b_ref[...], 1, 1)

        @pl.when(k == 0)
        def _():
            acc_ref[...] = part

        @pl.when(k > 0)
        def _():
            acc_ref[...] += part

        @pl.when(k == nk - 1)
        def _():
            o_ref[...] = acc_ref[...]

        @pl.when((i == ni - 1) & (j == nj - 1) & (k == nk - 1))
        def _():
            for cp in cps:
                cp.wait_recv()
            for cp in cps:
                cp.wait_send()

    return pl.pallas_call(
        kern, grid=(ni, nj, nk),
        in_specs=[pl.BlockSpec((tm, tk), lambda i, j, k: (i, k)), pl.BlockSpec((tn, tk), lambda i, j, k: (j, k)), pl.BlockSpec(memory_space=pl.ANY)],
        out_specs=[pl.BlockSpec((tm, tn), lambda i, j, k: (i, j)), pl.BlockSpec(memory_space=pl.ANY)],
        out_shape=[jax.ShapeDtypeStruct((M, N), F32), jax.ShapeDtypeStruct(sb.shape, sb.dtype)],
        scratch_shapes=[pltpu.VMEM((tm, tn), F32), pltpu.SemaphoreType.DMA((3,)), pltpu.SemaphoreType.DMA((3,))],
        compiler_params=_cp("arbitrary", "arbitrary", "arbitrary"), name=name)(a, b, sb)


def rs_exchange_sibling(tag, buf):
    _, R, C = buf.shape

    def body(buf_ref, out_ref, send_sems, recv_sems):
        mx, my, mc = lax.axis_index("x"), lax.axis_index("y"), lax.axis_index("c")
        sibling = (mx, my, 1 - mc)
        cps = []
        for j, (fx, fy) in enumerate(_REL):
            d = 4 * _flip(mx, fx) + 2 * _flip(my, fy) + (1 - mc)
            cps.append(pltpu.make_async_remote_copy(src_ref=buf_ref.at[d], dst_ref=out_ref.at[j], send_sem=send_sems.at[j],
                                                    recv_sem=recv_sems.at[j], device_id=sibling, device_id_type=MESH))
        for cp in cps:
            cp.start()
        for cp in cps:
            cp.wait_recv()
        for cp in cps:
            cp.wait_send()

    return pl.pallas_call(
        body, out_shape=jax.ShapeDtypeStruct((4, R, C), buf.dtype),
        in_specs=[pl.BlockSpec(memory_space=pl.ANY)], out_specs=pl.BlockSpec(memory_space=pl.ANY),
        scratch_shapes=[pltpu.SemaphoreType.DMA((4,)), pltpu.SemaphoreType.DMA((4,))], name="rs_sibling_" + tag)(buf)


def rs_chip_sum(tag, buf, recv, idx, tr):
    _, R, C = buf.shape

    def kern(idx_ref, b_ref, r_ref, own_ref, sb_ref):
        j = pl.program_id(1)
        s = b_ref[...] + r_ref[...]
        sb_ref[...] = s.astype(BF)

        @pl.when(j == 0)
        def _():
            own_ref[...] = s

    return pl.pallas_call(
        kern, grid_spec=pltpu.PrefetchScalarGridSpec(
            num_scalar_prefetch=1, grid=(R // tr, 4),
            in_specs=[pl.BlockSpec((None, tr, C), lambda r, j, idx: (idx[j], r, 0)), pl.BlockSpec((None, tr, C), lambda r, j, idx: (j, r, 0))],
            out_specs=[pl.BlockSpec((tr, C), lambda r, j, idx: (r, 0)), pl.BlockSpec((None, tr, C), lambda r, j, idx: (j, r, 0))]),
        out_shape=[jax.ShapeDtypeStruct((R, C), F32), jax.ShapeDtypeStruct((4, R, C), BF)],
        compiler_params=_cp("arbitrary", "arbitrary"), name="rs_chip_sum_" + tag)(idx, buf, recv)


def rs_exchange_chips(tag, sb):
    _, R, C = sb.shape

    def body(sb_ref, out_ref, send_sems, recv_sems):
        mx, my, mc = lax.axis_index("x"), lax.axis_index("y"), lax.axis_index("c")
        cps = []
        for j, (fx, fy) in enumerate(_REL):
            if j == 0:
                continue
            cps.append(pltpu.make_async_remote_copy(src_ref=sb_ref.at[j], dst_ref=out_ref.at[j], send_sem=send_sems.at[j - 1],
                                                    recv_sem=recv_sems.at[j - 1], device_id=(_flip(mx, fx), _flip(my, fy), mc),
                                                    device_id_type=MESH))
        for cp in cps:
            cp.start()
        for cp in cps:
            cp.wait_recv()
        for cp in cps:
            cp.wait_send()

    return pl.pallas_call(
        body, out_shape=jax.ShapeDtypeStruct((4, R, C), sb.dtype),
        in_specs=[pl.BlockSpec(memory_space=pl.ANY)], out_specs=pl.BlockSpec(memory_space=pl.ANY),
        scratch_shapes=[pltpu.SemaphoreType.DMA((3,)), pltpu.SemaphoreType.DMA((3,))], name="rs_chips_" + tag)(sb)


def adamw(name, parts, w, m, v, tr, rows=None, prev=None):
    R, C = w.shape
    row0, nrows = rows if rows is not None else (0, R)
    off = row0 // tr
    assert row0 % tr == 0 and nrows % tr == 0
    in_specs, args = [], []
    for arr, lead in parts:
        if lead is None:
            in_specs.append(pl.BlockSpec((tr, C), lambda r: (r, 0)))
        else:
            in_specs.append(pl.BlockSpec((None, tr, C), lambda r, k=lead: (k, r, 0)))
        args.append(arr)
    npart = len(parts)
    blk = pl.BlockSpec((tr, C), lambda r: (r + off, 0))
    extra, aliases = [], {}
    if prev is not None:
        extra = [pl.BlockSpec(memory_space=pl.ANY)] * 4
        aliases = {npart + 3 + k: k for k in range(4)}

    def kern(*refs):
        g = refs[0][...].astype(F32)
        for r in refs[1:npart]:
            g = g + r[...].astype(F32)
        w_ref, m_ref, v_ref = refs[npart:npart + 3]
        g_out, d_out, m_out, v_out = refs[-4:]
        mn = ADAM_B1 * m_ref[...] + (1.0 - ADAM_B1) * g
        vn = ADAM_B2 * v_ref[...] + (1.0 - ADAM_B2) * jnp.square(g)
        m_hat = mn / (1.0 - ADAM_B1 ** ADAM_STEP)
        v_hat = vn / (1.0 - ADAM_B2 ** ADAM_STEP)
        g_out[...] = g
        d_out[...] = -ADAM_LR * (m_hat / (jnp.sqrt(v_hat) + ADAM_EPS) + ADAM_WD * w_ref[...])
        m_out[...] = mn
        v_out[...] = vn

    return pl.pallas_call(
        kern, grid=(nrows // tr,), in_specs=in_specs + [blk, blk, blk] + extra, out_specs=[blk] * 4,
        out_shape=[jax.ShapeDtypeStruct((R, C), F32)] * 4, input_output_aliases=aliases,
        compiler_params=_cp("parallel"), name=name)(*args, w, m, v, *(prev or ()))


def _pack(arrs):
    flat = [a.reshape(-1) for a in arrs]
    n = sum(f.shape[0] for f in flat)
    pad = (-n) % (8 * LANES)
    if pad:
        flat.append(jnp.zeros((pad,), flat[0].dtype))
    return jnp.concatenate(flat).reshape(-1, LANES)


def _unpack(packed, shapes):
    flat = packed.reshape(-1)
    out, off = [], 0
    for s in shapes:
        n = int(np.prod(s))
        out.append(flat[off:off + n].reshape(s))
        off += n
    return out


def _row_tile(R, C=LANES, itemsize=4, mult=16, target=1 << 20):
    best = None
    for t in range(mult, R + 1, mult):
        if R % t == 0 and t * C * itemsize <= target:
            best = t
    return best or R


BIG = ("w_in", "w_branch", "w_out", "conv_pw", "conv_w")
SMALL = ("g_pre", "g_post", "na_rpb", "pool_w", "pool_scale", "q_norm", "k_norm", "conv_b", "conv_ln_g", "conv_ln_b")
WEIGHTS = ['c_ctx', 'w_ada', 'b_ada', 'g_pre', 'g_post', 'w_in', 'na_rpb', 'pool_w', 'pool_scale', 'q_norm', 'k_norm', 'conv_w', 'conv_b',
           'conv_ln_g', 'conv_ln_b', 'conv_pw', 'w_branch', 'w_out']


def _rope_tables(S, Cn):
    t = np.arange(S)
    pos = np.stack([t // GRID_W, t % GRID_W], 1).astype(np.float64)
    lane = np.arange(128)
    within = lane % 64
    axis = within // 32
    f = (within % 32) % 16
    freqs = ROPE_THETA ** (-np.arange(16, dtype=np.float32) / 16)
    ang = pos[:, axis].astype(np.float32) * freqs[f][None, :]
    cos = np.cos(ang).astype(np.float32)
    sin = np.sin(ang).astype(np.float32) * np.where((within % 32) < 16, -1.0, 1.0).astype(np.float32)[None, :]
    cos = np.concatenate([np.ones((Cn, 128), np.float32), cos])
    sin = np.concatenate([np.zeros((Cn, 128), np.float32), sin])
    return jnp.asarray(cos), jnp.asarray(sin)


def _own_2d(w_in, w_branch, w_out, conv_pw, conv_w):
    L = w_in.shape[0]
    cwp = jnp.zeros((L, CONV_PAD, conv_w.shape[2]), conv_w.dtype).at[:, :CONV_WIDTH].set(conv_w)
    return dict(w_in=w_in.reshape(-1, w_in.shape[-1]), w_branch=w_branch.reshape(-1, w_branch.shape[-1]),
                w_out=w_out.reshape(-1, w_out.shape[-1]), conv=_pack([conv_pw, cwp]))


def kernel(x, c, ctx, c_ctx, w_ada, b_ada, g_pre, g_post, w_in, na_rpb, pool_w, pool_scale, q_norm, k_norm, conv_w, conv_b, conv_ln_g, conv_ln_b, conv_pw, w_branch, w_out, loss_target, m_c_ctx, m_w_ada, m_b_ada, m_g_pre, m_g_post, m_w_in, m_na_rpb, m_pool_w, m_pool_scale, m_q_norm, m_k_norm, m_conv_w, m_conv_b, m_conv_ln_g, m_conv_ln_b, m_conv_pw, m_w_branch, m_w_out, v_c_ctx, v_w_ada, v_b_ada, v_g_pre, v_g_post, v_w_in, v_na_rpb, v_pool_w, v_pool_scale, v_q_norm, v_k_norm, v_conv_w, v_conv_b, v_conv_ln_g, v_conv_ln_b, v_conv_pw, v_w_branch, v_w_out):
    W = dict(c_ctx=c_ctx, w_ada=w_ada, b_ada=b_ada, g_pre=g_pre, g_post=g_post, w_in=w_in, na_rpb=na_rpb, pool_w=pool_w, pool_scale=pool_scale,
             q_norm=q_norm, k_norm=k_norm, conv_w=conv_w, conv_b=conv_b, conv_ln_g=conv_ln_g, conv_ln_b=conv_ln_b, conv_pw=conv_pw,
             w_branch=w_branch, w_out=w_out)
    Mo = dict(c_ctx=m_c_ctx, w_ada=m_w_ada, b_ada=m_b_ada, g_pre=m_g_pre, g_post=m_g_post, w_in=m_w_in, na_rpb=m_na_rpb, pool_w=m_pool_w,
              pool_scale=m_pool_scale, q_norm=m_q_norm, k_norm=m_k_norm, conv_w=m_conv_w, conv_b=m_conv_b, conv_ln_g=m_conv_ln_g,
              conv_ln_b=m_conv_ln_b, conv_pw=m_conv_pw, w_branch=m_w_branch, w_out=m_w_out)
    Vo = dict(c_ctx=v_c_ctx, w_ada=v_w_ada, b_ada=v_b_ada, g_pre=v_g_pre, g_post=v_g_post, w_in=v_w_in, na_rpb=v_na_rpb, pool_w=v_pool_w,
              pool_scale=v_pool_scale, q_norm=v_q_norm, k_norm=v_k_norm, conv_w=v_conv_w, conv_b=v_conv_b, conv_ln_g=v_conv_ln_g,
              conv_ln_b=v_conv_ln_b, conv_pw=v_conv_pw, w_branch=v_w_branch, w_out=v_w_out)

    S, D = x.shape[1], x.shape[2]
    Cn = ctx.shape[1]
    T = Cn + S
    L = w_in.shape[0]
    IN = w_in.shape[2] * N_DEV
    mx, my, mc = lax.axis_index("x"), lax.axis_index("y"), lax.axis_index("c")
    me = 4 * mx + 2 * my + mc
    tm = _pick(Cn, (256, 128))
    tme = _pick(T, (768, 256, 128))
    tmb = _pick(T, (1408, 768, 256, 128))
    tmm = _pick(T, (768, 256, 128))
    tq = _pick(Cn, (256, 128))
    tk = _pick(T, (4224, 768, 384, 128))
    tkf = _pick(T, (8448, 4224, 768, 384, 128))

    cg = allgather("gather_c", c)
    cvec = jnp.zeros((16, D), F32).at[0].set(c_ctx).at[1:1 + N_DEV].set(cg[:, 0])
    wc = w_ada.shape[2]
    b_sh = lax.dynamic_slice_in_dim(b_ada, me * wc, wc, axis=1)[:, None, :]
    modp = ada_fwd(cvec, w_ada, b_sh)
    modg = allgather("gather_mod", modp.reshape(L * 16, wc)).reshape(N_DEV, L, 16, wc)
    mod_full = jnp.transpose(modg, (1, 2, 0, 3)).reshape(L, 16, N_DEV * wc)
    mod2 = jnp.stack([mod_full[:, 0], lax.dynamic_index_in_dim(mod_full, 1 + me, axis=1, keepdims=False)], axis=1)
    shift, scale, gate = [mod2[:, :, None, k * D:(k + 1) * D] for k in range(3)]

    own2 = _own_2d(w_in, w_branch, w_out, conv_pw, conv_w)
    w_in_bf = own2["w_in"].astype(BF)
    win_g = allgather("gather_w_in0", w_in_bf[:D])
    wbr_g = allgather("gather_w_branch", own2["w_branch"].astype(BF))
    wout_g = allgather("gather_w_out", own2["w_out"].astype(BF))
    conv_g = allgather("gather_conv", own2["conv"].astype(BF)).reshape(N_DEV, -1)
    npw = L * (BRANCH_W // N_DEV) * BRANCH_W
    cpw_g = conv_g[:, :npw].reshape(N_DEV, L, BRANCH_W // N_DEV, BRANCH_W)
    cw_g = conv_g[:, npw:npw + L * CONV_PAD * (BRANCH_W // N_DEV)].reshape(N_DEV, L, CONV_PAD, BRANCH_W // N_DEV)
    Win = []
    Wb = [cols_from_devs("w_branch_cols", wbr_g, l, 4 * BRANCH_W).reshape(4, BRANCH_W, D) for l in range(L)]
    Wo = [rows_from_devs("w_out_rows", wout_g, l, L) for l in range(L)]
    Cpw = [cpw_g[:, l].reshape(BRANCH_W, BRANCH_W).astype(F32) for l in range(L)]
    Cw = [jnp.transpose(cw_g[:, l], (1, 0, 2)).reshape(CONV_PAD, BRANCH_W).astype(F32) for l in range(L)]

    cos, sin = _rope_tables(S, Cn)
    tb = na_tables(S, Cn)
    bd = jnp.asarray(np.kron(np.eye(2, dtype=np.float32), np.full((64, 64), 1.0 / 64, np.float32)))
    row2 = lambda a: a.reshape(1, -1)

    X = jnp.concatenate([ctx[0], x[0]], axis=0)
    saved = []
    for l in range(L):
        gp = row2(g_pre[l])
        H = modulate_fwd(X, gp, shift[l], scale[l], Cn, tm)
        Win.append(cols_from_devs("w_in_cols", win_g, 0, D))
        tn_in = _pick(IN, (1280, 1152, 768, 384, 128))
        if l + 1 < L:
            P, win_g = proj_in_gather(H, Win[l], w_in_bf, (l + 1) * D, tmm, tn_in)
        else:
            P = matmul("proj_in", H, Win[l], out_dtype=F32, tm=tmm, tn=tn_in, tk=D)
        bt = bias_table(na_rpb[l], tb)
        oa = na_fwd(P, bt, tb, Cn)
        ob = pool_fwd(P, pool_w[l], row2(pool_scale[l]), Cn)
        qn = row2(jnp.tile(q_norm[l], 2))
        kn = row2(jnp.tile(k_norm[l], 2))
        Qp = prep_q_fwd(P, cos, sin, qn, bd, Cn, tm)
        KV = prep_kv_fwd(P, cos, sin, kn, bd, Cn, tm)
        oc, LSE = flash_fwd(Qp, KV, Cn, tq, tkf)
        Yc = conv_fwd(P, Cw[l], row2(conv_b[l]), Cn)
        od = conv_post_fwd(Yc, row2(conv_ln_g[l]), row2(conv_ln_b[l]), Cpw[l], Cn, tm)
        G = gate_outs_fwd(P, oa, ob, oc, od, Cn, tme)
        Y = branch_merge_fwd(P, G, Wb[l], tmb)
        Z = matmul("proj_out", Y, Wo[l], out_dtype=F32, tm=tmm, tn=D, tk=D)
        Xn = post_fwd(X, Z, row2(g_post[l]), gate[l], Cn, tm)
        saved.append(dict(X=X, H=H, P=P, bt=bt, oa=oa, ob=ob, oc=oc, od=od, Qp=Qp, KV=KV, LSE=LSE, Yc=Yc, G=G, Y=Y, Z=Z, qn=qn, kn=kn))
        X = Xn

    tgt = jnp.concatenate([jnp.zeros((Cn, D), F32), loss_target[0]], axis=0)
    dX, loss_acc = loss_call(X, tgt, Cn, tm)
    loss = lax.psum(loss_acc[0, 0], ("x", "y", "c"))

    gsm = {n: [None] * L for n in SMALL}
    gbig = {n: [None] * L for n in BIG}
    g_br = g_out = None
    idx = jnp.stack([4 * _flip(mx, fx) + 2 * _flip(my, fy) + mc for fx, fy in _REL]).astype(jnp.int32)
    mom2 = _own_2d(Mo["w_in"], Mo["w_branch"], Mo["w_out"], Mo["conv_pw"], Mo["conv_w"])
    var2 = _own_2d(Vo["w_in"], Vo["w_branch"], Vo["w_out"], Vo["conv_pw"], Vo["conv_w"])
    tr_in = _row_tile(D, own2["w_in"].shape[1])
    pending = None
    res_in = None

    def finish_w_in(layer, own, recv2, prev):
        return adamw("adamw_w_in", [(own, None), (recv2, 1), (recv2, 2), (recv2, 3)], own2["w_in"], mom2["w_in"], var2["w_in"], tr_in,
                     rows=(layer * D, D), prev=prev)

    dmod = [None] * L
    for l in reversed(range(L)):
        sv = saved[l]
        P = sv["P"]
        dZ, dgp, dgate = post_bwd(sv["Z"], dX, row2(g_post[l]), gate[l], Cn, tm)
        gsm["g_post"][l] = dgp[0]
        dY = matmul("proj_out_dy", dZ, Wo[l], cb=1, out_dtype=F32, tm=tmm, tn=D, tk=D)
        dWo = matmul("proj_out_dw", sv["Y"], dZ, ca=0, cb=0, tm=_pick(D, (1024, 512, 256)), tn=D, tk=tmm)
        g_out = rows_to_devs("w_out_devs", dWo, l, L, g_out)
        dU, dG, dP = branch_merge_bwd(P, sv["G"], Wb[l], dY, tmb)
        g_br = cols_to_devs("w_branch_devs", branch_dw(sv["G"], dU, tmm).reshape(4 * BRANCH_W, D), l, L, g_br)
        dO, dP = gate_outs_bwd(P, sv["oa"], sv["ob"], sv["oc"], sv["od"], dG, dP, Cn, tme)
        dYc, dlg, dlb, dcpw = conv_post_bwd(sv["Yc"], dO, row2(conv_ln_g[l]), row2(conv_ln_b[l]), Cpw[l], Cn, tm)
        gsm["conv_ln_g"][l], gsm["conv_ln_b"][l], gbig["conv_pw"][l] = dlg[0], dlb[0], dcpw
        dP, dGg, dcw, dcb = conv_bwd(P, Cw[l], dYc, dP, Cn)
        dP = copy_cols("copy_glu_gate", dGg, dP, CB["d_glu"] + 4, Cn, tme)
        gbig["conv_w"][l], gsm["conv_b"][l] = dcw, dcb[0]
        dQp, dKV = flash_bwd(sv["Qp"], sv["KV"], sv["oc"], sv["LSE"], dO, Cn, tq, tk)
        dP, dqn = prep_q_bwd(P, dQp, dP, cos, sin, sv["qn"], bd, Cn, tm)
        dP, dkn = prep_kv_bwd(P, dKV, dP, cos, sin, sv["kn"], bd, Cn, tm)
        gsm["q_norm"][l] = dqn[0, :64] + dqn[0, 64:]
        gsm["k_norm"][l] = dkn[0, :64] + dkn[0, 64:]
        dP, dpw, dps = pool_bwd(P, pool_w[l], row2(pool_scale[l]), dO, dP, Cn)
        gsm["pool_w"][l], gsm["pool_scale"][l] = dpw, dps[0]
        dQa, dKa, dVa, dbt = na_bwd(P, sv["bt"], dO, tb, Cn)
        dP = assemble_na(dKa, dVa, dQa, dP, Cn, tme)
        gsm["na_rpb"][l] = bias_table_grad(dbt, tb)
        tk_in = _pick(IN, (1280, 1152, 768, 384, 128))
        if pending is None:
            dH = matmul("proj_in_dh", dP, Win[l], cb=1, out_dtype=F32, tm=tmm, tn=D, tk=tk_in)
        else:
            dH, recv2 = matmul_nt_exchange("proj_in_dh_exchange", dP, Win[l], pending[2], tm=tmm, tn=D, tk=tk_in)
            res_in = finish_w_in(pending[0], pending[1], recv2, res_in)
        dWin = matmul("proj_in_dw", sv["H"], dP, ca=0, cb=0, tm=_pick(D, (1024, 512, 256)), tn=tk_in, tk=_pick(T, (1408, 768, 384, 128)))
        g_l = cols_to_devs("w_in_devs", dWin, 0, 1, None)
        own_l, sb_l = rs_chip_sum("w_in", g_l, rs_exchange_sibling("w_in", g_l), idx, tr_in)
        pending = (l, own_l, sb_l)
        dX, dgpre, dsh, dsc = modulate_bwd(sv["X"], dH, dX, row2(g_pre[l]), shift[l], scale[l], Cn, tm)
        gsm["g_pre"][l] = dgpre[0]
        dmod[l] = jnp.concatenate([dsh[:, 0], dsc[:, 0], dgate[:, 0]], axis=1)
    grad_x = dX[Cn:][None]

    res_in = finish_w_in(pending[0], pending[1], rs_exchange_chips("w_in", pending[2]), res_in)
    g_pw = jnp.transpose(jnp.stack(gbig["conv_pw"]).reshape(L, N_DEV, -1), (1, 0, 2)).reshape(N_DEV, -1)
    g_cw = jnp.transpose(jnp.stack(gbig["conv_w"]).reshape(L, CONV_PAD, N_DEV, BRANCH_W // N_DEV), (2, 0, 1, 3)).reshape(N_DEV, -1)
    g_conv = jnp.concatenate([g_pw, g_cw], axis=1)
    g_conv = jnp.pad(g_conv, ((0, 0), (0, own2["conv"].size - g_conv.shape[1]))).reshape(N_DEV, -1, LANES)
    gbufs = dict(w_branch=g_br, w_out=g_out, conv=g_conv)
    big2 = dict(w_in=res_in)
    for tag in ("w_branch", "w_out", "conv"):
        R2, C2 = own2[tag].shape
        tr2 = _row_tile(R2, C2)
        recv1 = rs_exchange_sibling(tag, gbufs[tag])
        own, sb = rs_chip_sum(tag, gbufs[tag], recv1, idx, tr2)
        recv2 = rs_exchange_chips(tag, sb)
        big2[tag] = adamw("adamw_" + tag, [(own, None), (recv2, 1), (recv2, 2), (recv2, 3)], own2[tag], mom2[tag], var2[tag], tr2)
    cshapes = [(L, BRANCH_W // N_DEV, BRANCH_W), (L, CONV_PAD, BRANCH_W // N_DEV)]
    conv_unp = [_unpack(r, cshapes) for r in big2["conv"]]
    big_out = dict(w_in=[r.reshape(w_in.shape) for r in big2["w_in"]], w_branch=[r.reshape(w_branch.shape) for r in big2["w_branch"]],
                   w_out=[r.reshape(w_out.shape) for r in big2["w_out"]], conv_pw=[u[0] for u in conv_unp],
                   conv_w=[u[1][:, :CONV_WIDTH] for u in conv_unp])

    small_own = [jnp.stack(gsm[n]) for n in SMALL]
    small_shapes = [a.shape for a in small_own]
    dmod_all = jnp.stack(dmod)
    spack = _pack(small_own + [dmod_all])
    sg = allgather("gather_small", spack)
    Rs = spack.shape[0]
    nsmall = sum(int(np.prod(s)) for s in small_shapes)
    dmod_g = sg.reshape(N_DEV, -1)[:, nsmall:nsmall + dmod_all.size].reshape(N_DEV, L, 2, N_DEV, wc)
    dm_ctx = dmod_g[0, :, 0, :, :]
    for k in range(1, N_DEV):
        dm_ctx = dm_ctx + dmod_g[k, :, 0, :, :]
    dm_ctx = lax.dynamic_index_in_dim(dm_ctx, me, axis=1, keepdims=False)
    dm_b = jnp.transpose(lax.dynamic_index_in_dim(dmod_g[:, :, 1], me, axis=2, keepdims=False), (1, 0, 2))
    dmp = jnp.zeros((L, 16, wc), F32).at[:, 0].set(dm_ctx).at[:, 1:1 + N_DEV].set(dm_b)
    g_wada, dsilu = ada_bwd(cvec, w_ada, dmp)
    cpart = allgather("gather_cctx", dsilu[0:1])
    g_cctx = silu_grad(c_ctx[None], cpart)[0]

    smallw = _pack([W[n] for n in SMALL])
    smallm = _pack([Mo[n] for n in SMALL])
    smallv = _pack([Vo[n] for n in SMALL])
    Rsm = smallw.shape[0]
    small_res = adamw("adamw_small", [(sg[:, :Rsm], k) for k in range(N_DEV)], smallw, smallm, smallv, _row_tile(Rsm))
    small_unp = [_unpack(r, small_shapes) for r in small_res]

    db_parts = dmod_g.reshape(N_DEV, L, 2, N_DEV * wc)
    bshape = (L * N_DEV * wc // LANES, LANES)
    bparts = [(db_parts[k, :, r].reshape(bshape), None) for k in range(N_DEV) for r in range(2)]
    bada_res = adamw("adamw_bada", bparts, b_ada.reshape(bshape), Mo["b_ada"].reshape(bshape), Vo["b_ada"].reshape(bshape), _row_tile(bshape[0]))
    wada_shape = (w_ada.size // LANES, LANES)
    wada_res = adamw("adamw_wada", [(g_wada.reshape(wada_shape), None)], w_ada.reshape(wada_shape), Mo["w_ada"].reshape(wada_shape),
                     Vo["w_ada"].reshape(wada_shape), _row_tile(wada_shape[0]))
    cshape = (D // LANES, LANES)
    cctx_res = adamw("adamw_cctx", [(g_cctx.reshape(cshape), None)], c_ctx.reshape(cshape), Mo["c_ctx"].reshape(cshape), Vo["c_ctx"].reshape(cshape),
                     _row_tile(cshape[0]) if cshape[0] % 8 == 0 else cshape[0])

    res = {}
    for n in BIG:
        res[n] = big_out[n]
    for k, n in enumerate(SMALL):
        res[n] = [u[k] for u in small_unp]
    res["b_ada"] = [r.reshape(b_ada.shape) for r in bada_res]
    res["w_ada"] = [r.reshape(w_ada.shape) for r in wada_res]
    res["c_ctx"] = [r.reshape(c_ctx.shape) for r in cctx_res]
    outs = [loss, grad_x]
    for k in range(4):
        outs += [res[n][k] for n in WEIGHTS]
    return tuple(outs)
```

```python
import functools

import numpy as np
import jax
import jax.numpy as jnp
from jax import lax
from jax.experimental import pallas as pl
from jax.experimental.pallas import tpu as pltpu

F32 = jnp.float32
BF = jnp.bfloat16
HI = lax.Precision.HIGHEST

GRID_W = 64
BRANCH_W = 512
HEAD_DIM = 64
NA_WIN_ROWS = 8
NA_WIN_COLS = 16
NA_TILE_ROWS = 2
NA_BAND = NA_WIN_ROWS + NA_TILE_ROWS - 1
CONV_WIDTH = 31
CONV_PAD = 32
EPS = 1e-6
ROPE_THETA = 10000.0
NEG = -1e30
N_DEV = 8
LANES = 128
VMEM_LIMIT_BYTES = 56 * 1024 * 1024

ADAM_LR, ADAM_B1, ADAM_B2, ADAM_EPS, ADAM_WD, ADAM_STEP = 0.001, 0.9, 0.999, 1e-08, 0.01, 10

CB = dict(a_k=0, a_v=4, c_k=8, c_v=9, a_q=10, c_q=14, a_gate=18, b_in=22, b_gate=26, c_gate=30, d_glu=34, d_gate=42, merge=46)
KV_COLS = 1280
MESH = pl.DeviceIdType.MESH


def _pick(n, cands):
    for c in cands:
        if n % c == 0:
            return c
    raise ValueError(f"no tile for {n} among {cands}")


def _cp(*sem):
    return pltpu.CompilerParams(dimension_semantics=sem if sem else None, vmem_limit_bytes=VMEM_LIMIT_BYTES)


def _dg(x, y, cx, cy):
    return lax.dot_general(x.astype(BF), y.astype(BF), (((cx,), (cy,)), ((), ())), preferred_element_type=F32)


@functools.partial(jax.custom_vjp, nondiff_argnums=(2, 3))
def bdot(a, b, ca=1, cb=0):
    return _dg(a, b, ca, cb)


def _bdot_fwd(a, b, ca, cb):
    return _dg(a, b, ca, cb), (a, b)


def _bdot_bwd(ca, cb, res, g):
    a, b = res
    jb = 1 if cb == 0 else 0
    ia = 0 if ca == 1 else 1
    da = _dg(g, b, 1, jb) if ca == 1 else _dg(b, g, jb, 1)
    db = _dg(a, g, ia, 0) if cb == 0 else _dg(g, a, 0, ia)
    return da.astype(a.dtype), db.astype(b.dtype)


bdot.defvjp(_bdot_fwd, _bdot_bwd)


def hdot(a, b):
    return jnp.dot(a, b, precision=HI, preferred_element_type=F32)


def hdot3(a, b):
    return jnp.dot(a, b, precision=lax.Precision.HIGH, preferred_element_type=F32)


@jax.custom_vjp
def swap64(x):
    return pltpu.roll(x, 64, 1)


swap64.defvjp(lambda x: (swap64(x), None), lambda _, g: (swap64(g),))


@jax.custom_vjp
def partner(x):
    w = x.shape[-1]
    lane = lax.broadcasted_iota(jnp.int32, x.shape, 1)
    return jnp.where((lane % 32) < 16, pltpu.roll(x, w - 16, 1), pltpu.roll(x, 16, 1))


partner.defvjp(lambda x: (partner(x), None), lambda _, g: (partner(g),))


@functools.partial(jax.custom_vjp, nondiff_argnums=(1,))
def split_rows(x, n):
    k = x.shape[0] // n
    return tuple(x[i * k:(i + 1) * k] for i in range(n))


split_rows.defvjp(lambda x, n: (split_rows(x, n), None), lambda n, _, gs: (jnp.concatenate(gs, axis=0),))


def _lane(shape):
    return lax.broadcasted_iota(jnp.int32, shape, len(shape) - 1)


def _silu(x):
    return x * jax.nn.sigmoid(x)


def _normrope(x, w, cos, sin, bd):
    ms = hdot3(x * x, bd)
    y = x * lax.rsqrt(ms + EPS) * w
    return y * cos + partner(y) * sin


def matmul(name, a, b, *, ca=1, cb=0, out_dtype=F32, tm, tn, tk):
    M = a.shape[1 - ca]
    K = a.shape[ca]
    N = b.shape[1 - cb]
    assert b.shape[cb] == K and M % tm == 0 and N % tn == 0 and K % tk == 0, (name, a.shape, b.shape)
    nk = K // tk
    a_spec = pl.BlockSpec((tm, tk), lambda i, j, k: (i, k)) if ca == 1 else pl.BlockSpec((tk, tm), lambda i, j, k: (k, i))
    b_spec = pl.BlockSpec((tk, tn), lambda i, j, k: (k, j)) if cb == 0 else pl.BlockSpec((tn, tk), lambda i, j, k: (j, k))

    def kern(a_ref, b_ref, o_ref, *scr):
        part = _dg(a_ref[...], b_ref[...], ca, cb)
        if nk == 1:
            o_ref[...] = part.astype(out_dtype)
            return
        acc_ref, = scr
        k = pl.program_id(2)

        @pl.when(k == 0)
        def _():
            acc_ref[...] = part

        @pl.when(k > 0)
        def _():
            acc_ref[...] += part

        @pl.when(k == nk - 1)
        def _():
            o_ref[...] = acc_ref[...].astype(out_dtype)

    return pl.pallas_call(
        kern, grid=(M // tm, N // tn, nk), in_specs=[a_spec, b_spec],
        out_specs=pl.BlockSpec((tm, tn), lambda i, j, k: (i, j)),
        out_shape=jax.ShapeDtypeStruct((M, N), out_dtype),
        scratch_shapes=[] if nk == 1 else [pltpu.VMEM((tm, tn), F32)],
        compiler_params=_cp("parallel", "parallel", "arbitrary"), name=name,
    )(a, b)


def rowcall(name, body, T, tm, ncol, nctx, rows, consts=(), segs=(), outs=(), accs=(), saccs=()):
    nctxb = nctx // tm
    assert T % tm == 0 and (nctx % tm == 0 or not (segs or saccs))
    nr, nc, ns = len(rows), len(consts), len(segs)

    def seg_map(i, j):
        return (jnp.where(i < nctxb, 0, 1), 0, 0)

    def col_map(cf):
        return lambda i, j: (i, cf(j))

    in_specs, args = [], []
    for arr, bw, cf in rows:
        in_specs.append(pl.BlockSpec((tm, bw), col_map(cf)))
        args.append(arr)
    for arr in consts:
        in_specs.append(pl.BlockSpec(arr.shape, lambda i, j, nd=arr.ndim: (0,) * nd))
        args.append(arr)
    for arr in segs:
        in_specs.append(pl.BlockSpec((None, 1, arr.shape[-1]), seg_map))
        args.append(arr)
    out_shape, out_specs, aliases, out_dtypes = [], [], {}, []
    for o in outs:
        if o[0] == "into":
            _, arr, bw, cf = o
            aliases[len(args)] = len(out_shape)
            in_specs.append(pl.BlockSpec(memory_space=pl.ANY))
            args.append(arr)
            out_shape.append(jax.ShapeDtypeStruct(arr.shape, arr.dtype))
            out_dtypes.append(arr.dtype)
        else:
            cols, dt, bw, cf = o
            out_shape.append(jax.ShapeDtypeStruct((T, cols), dt))
            out_dtypes.append(dt)
        out_specs.append(pl.BlockSpec((tm, bw), col_map(cf)))
    for shp in accs:
        out_shape.append(jax.ShapeDtypeStruct(shp, F32))
        out_specs.append(pl.BlockSpec(shp, lambda i, j, nd=len(shp): (0,) * nd))
    for w in saccs:
        out_shape.append(jax.ShapeDtypeStruct((2, 1, w), F32))
        out_specs.append(pl.BlockSpec((None, 1, w), seg_map))
    n_in = len(args)
    no, na, nsa = len(outs), len(accs), len(saccs)

    def kern(*refs):
        i, j = pl.program_id(0), pl.program_id(1)
        ins = refs[:nr + nc + ns]
        orefs = refs[n_in:n_in + no]
        arefs = refs[n_in + no:n_in + no + na]
        srefs = refs[n_in + no + na:n_in + no + na + nsa]
        ov, av, sv = body(i, j, [r[...] for r in ins[:nr]], [r[...] for r in ins[nr:nr + nc]], [r[...] for r in ins[nr + nc:]])
        for r, v, dt in zip(orefs, ov, out_dtypes):
            r[...] = v.astype(dt)
        if na:
            @pl.when((i == 0) & (j == 0))
            def _():
                for r in arefs:
                    r[...] = jnp.zeros_like(r)
            for r, v in zip(arefs, av):
                r[...] += v
        if nsa:
            @pl.when(((i == 0) | (i == nctxb)) & (j == 0))
            def _():
                for r in srefs:
                    r[...] = jnp.zeros_like(r)
            for r, v in zip(srefs, sv):
                r[...] += v

    res = pl.pallas_call(
        kern, grid=(T // tm, ncol), in_specs=in_specs, out_specs=out_specs, out_shape=out_shape,
        input_output_aliases=aliases, compiler_params=_cp("arbitrary", "arbitrary"), name=name,
    )(*args)
    return res


def _c(k):
    return lambda j: k


def f_modulate(x, g, sh, sc):
    y = x * lax.rsqrt(jnp.mean(x * x, axis=-1, keepdims=True) + EPS) * g
    return y * (1.0 + sc) + sh


def modulate_fwd(X, g_pre, shift, scale, Cn, tm):
    T, D = X.shape

    def body(i, j, r, c, s):
        return [f_modulate(r[0], c[0], s[0], s[1])], [], []

    return rowcall("modulate_fwd", body, T, tm, 1, Cn, [(X, D, _c(0))], [g_pre], [shift, scale], [(D, BF, D, _c(0))])[0]


def modulate_bwd(X, dH, dXn, g_pre, shift, scale, Cn, tm):
    T, D = X.shape

    def body(i, j, r, c, s):
        _, vjp = jax.vjp(f_modulate, r[0], c[0], s[0], s[1])
        dx, dg, dsh, dsc = vjp(r[1])
        return [dx + r[2]], [dg], [dsh, dsc]

    return rowcall("modulate_bwd", body, T, tm, 1, Cn, [(X, D, _c(0)), (dH, D, _c(0)), (dXn, D, _c(0))], [g_pre], [shift, scale],
                   [(D, F32, D, _c(0))], [(1, D)], [D, D])


def f_post(z, gp, gate):
    return gate * (z * lax.rsqrt(jnp.mean(z * z, axis=-1, keepdims=True) + EPS) * gp)


def post_fwd(X, Z, g_post, gate, Cn, tm):
    T, D = X.shape

    def body(i, j, r, c, s):
        return [r[0] + f_post(r[1], c[0], s[0])], [], []

    return rowcall("post_fwd", body, T, tm, 1, Cn, [(X, D, _c(0)), (Z, D, _c(0))], [g_post], [gate], [(D, F32, D, _c(0))])[0]


def post_bwd(Z, dXn, g_post, gate, Cn, tm):
    T, D = Z.shape

    def body(i, j, r, c, s):
        _, vjp = jax.vjp(f_post, r[0], c[0], s[0])
        dz, dgp, dgate = vjp(r[1])
        return [dz], [dgp], [dgate]

    return rowcall("post_bwd", body, T, tm, 1, Cn, [(Z, D, _c(0)), (dXn, D, _c(0))], [g_post], [gate],
                   [(D, BF, D, _c(0))], [(1, D)], [D])


def loss_call(XL, tgt, Cn, tm):
    T, D = XL.shape
    nctxb = Cn // tm

    def body(i, j, r, c, s):
        diff = jnp.where(i >= nctxb, r[0] - r[1], 0.0)
        per_row = jnp.mean(diff * diff, axis=-1, keepdims=True)
        tot = 0.5 * jnp.sum(per_row, axis=0, keepdims=True)
        return [diff / D], [jnp.broadcast_to(tot, (1, LANES))], []

    return rowcall("loss", body, T, tm, 1, Cn, [(XL, D, _c(0)), (tgt, D, _c(0))], [], [], [(D, F32, D, _c(0))], [(1, LANES)])


def f_prep_q(x, qn, cos, sin, bd, jj):
    y = _normrope(x, qn, cos, sin, bd) * 0.125
    lane = _lane(y.shape)
    a = jnp.where(lane < 64, y, 0.0)
    b = jnp.where(lane >= 64, y, 0.0)
    g0 = jj < 2
    return jnp.concatenate([jnp.where(g0, a, swap64(a)), jnp.where(g0, swap64(b), b)], axis=1)


def prep_q_fwd(P, cos, sin, qn, bd, Cn, tm):
    T = P.shape[0]

    def body(i, j, r, c, s):
        return [f_prep_q(r[0], c[0], r[1], r[2], c[1], j)], [], []

    return rowcall("prep_q_fwd", body, T, tm, 4, Cn, [(P, 128, lambda j: CB["c_q"] + j), (cos, 128, _c(0)), (sin, 128, _c(0))], [qn, bd], [],
                   [(1024, BF, 256, lambda j: j)])[0]


def prep_q_bwd(P, dQ, dP, cos, sin, qn, bd, Cn, tm):
    T = P.shape[0]

    def body(i, j, r, c, s):
        _, vjp = jax.vjp(lambda x, w: f_prep_q(x, w, r[1], r[2], c[1], j), r[0], c[0])
        dx, dw = vjp(r[3])
        return [dx], [dw], []

    return rowcall("prep_q_bwd", body, T, tm, 4, Cn,
                   [(P, 128, lambda j: CB["c_q"] + j), (cos, 128, _c(0)), (sin, 128, _c(0)), (dQ, 256, lambda j: j)], [qn, bd], [],
                   [("into", dP, 128, lambda j: CB["c_q"] + j)], [(1, 128)])


def f_prep_kv(x, kn, cos, sin, bd, jj):
    return jnp.where(jj == 0, _normrope(x, kn, cos, sin, bd), x)


def prep_kv_fwd(P, cos, sin, kn, bd, Cn, tm):
    T = P.shape[0]

    def body(i, j, r, c, s):
        return [f_prep_kv(r[0], c[0], r[1], r[2], c[1], j)], [], []

    return rowcall("prep_kv_fwd", body, T, tm, 2, Cn, [(P, 128, lambda j: CB["c_k"] + j), (cos, 128, _c(0)), (sin, 128, _c(0))], [kn, bd], [],
                   [(256, BF, 128, lambda j: j)])[0]


def prep_kv_bwd(P, dKV, dP, cos, sin, kn, bd, Cn, tm):
    T = P.shape[0]

    def body(i, j, r, c, s):
        _, vjp = jax.vjp(lambda x, w: f_prep_kv(x, w, r[1], r[2], c[1], j), r[0], c[0])
        dx, dw = vjp(r[3])
        return [dx], [dw], []

    return rowcall("prep_kv_bwd", body, T, tm, 2, Cn,
                   [(P, 128, lambda j: CB["c_k"] + j), (cos, 128, _c(0)), (sin, 128, _c(0)), (dKV, 128, lambda j: j)], [kn, bd], [],
                   [("into", dP, 128, lambda j: CB["c_k"] + j)], [(1, 128)])


def f_conv_post(y, lg, lb, w):
    mu = jnp.mean(y, axis=-1, keepdims=True)
    var = jnp.mean(jnp.square(y - mu), axis=-1, keepdims=True)
    h = (y - mu) * lax.rsqrt(var + EPS) * lg + lb
    return bdot(_silu(h), w)


def conv_post_fwd(Yc, lg, lb, w, Cn, tm):
    T = Yc.shape[0]

    def body(i, j, r, c, s):
        return [f_conv_post(r[0], c[0], c[1], c[2])], [], []

    return rowcall("conv_post_fwd", body, T, tm, 1, Cn, [(Yc, 512, _c(0))], [lg, lb, w], [], [(512, F32, 512, _c(0))])[0]


def conv_post_bwd(Yc, dO, lg, lb, w, Cn, tm):
    T = Yc.shape[0]

    def body(i, j, r, c, s):
        _, vjp = jax.vjp(f_conv_post, r[0], c[0], c[1], c[2])
        dy, dlg, dlb, dw = vjp(r[1])
        return [dy], [dlg, dlb, dw], []

    return rowcall("conv_post_bwd", body, T, tm, 1, Cn, [(Yc, 512, _c(0)), (dO, 512, _c(3))], [lg, lb, w], [],
                   [(512, F32, 512, _c(0))], [(1, 512), (1, 512), (512, 512)])


def _gate_blk(j):
    b = j // 2
    base = jnp.where(b == 0, CB["a_gate"] // 2, jnp.where(b == 1, CB["b_gate"] // 2, jnp.where(b == 2, CB["c_gate"] // 2, CB["d_gate"] // 2)))
    return base + j % 2


def _sel4(j, vals):
    b = j // 2
    return jnp.where(b == 0, vals[0], jnp.where(b == 1, vals[1], jnp.where(b == 2, vals[2], vals[3])))


def gate_outs_fwd(P, oa, ob, oc, od, Cn, tm):
    T = P.shape[0]

    def body(i, j, r, c, s):
        return [_sel4(j, r[:4]) * _silu(r[4])], [], []

    half = lambda j: j % 2
    return rowcall("gate_outs_fwd", body, T, tm, 8, Cn,
                   [(oa, 256, half), (ob, 256, half), (oc, 256, half), (od, 256, half), (P, 256, _gate_blk)], [], [],
                   [(2048, BF, 256, lambda j: j)])[0]


def gate_outs_bwd(P, oa, ob, oc, od, dG, dP, Cn, tm):
    T = P.shape[0]

    def body(i, j, r, c, s):
        o = _sel4(j, r[:4])
        _, vjp = jax.vjp(lambda oo, gg: oo * _silu(gg), o, r[4])
        do, dg = vjp(r[5])
        return [do, dg], [], []

    half = lambda j: j % 2
    return rowcall("gate_outs_bwd", body, T, tm, 8, Cn,
                   [(oa, 256, half), (ob, 256, half), (oc, 256, half), (od, 256, half), (P, 256, _gate_blk), (dG, 256, lambda j: j)], [], [],
                   [(2048, F32, 256, lambda j: j), ("into", dP, 256, _gate_blk)])


def merge_fwd(P, U, Cn, tm):
    T = P.shape[0]
    D4 = U.shape[1]
    nb = D4 // 4 // 256
    mb = CB["merge"] // 2

    def body(i, j, r, c, s):
        y = 0.0
        for b in range(4):
            y = y + jax.nn.sigmoid(r[b]) * r[4 + b]
        return [y], [], []

    rows = [(P, 256, (lambda j, b=b: mb + b * nb + j)) for b in range(4)] + [(U, 256, (lambda j, b=b: b * nb + j)) for b in range(4)]
    return rowcall("merge_fwd", body, T, tm, nb, Cn, rows, [], [], [(D4 // 4, BF, 256, lambda j: j)])[0]


def merge_bwd(P, U, dY, Cn, tm):
    T, IN = P.shape
    D4 = U.shape[1]
    nb = D4 // 4 // 256
    mb = CB["merge"] // 2

    def body(i, j, r, c, s):
        _, vjp = jax.vjp(lambda m, u: jax.nn.sigmoid(m) * u, r[0], r[1])
        dm, du = vjp(r[2])
        return [du, dm], [], []

    return rowcall("merge_bwd", body, T, tm, 4 * nb, Cn,
                   [(P, 256, lambda j: mb + j), (U, 256, lambda j: j), (dY, 256, lambda j: j % nb)], [], [],
                   [(D4, BF, 256, lambda j: j), (IN, F32, 256, lambda j: mb + j)])


def assemble_na(dK, dV, dQ, dP, Cn, tm):
    T = dP.shape[0]

    def body(i, j, r, c, s):
        return [jnp.where(j < 4, r[0], jnp.where(j < 8, r[1], r[2]))], [], []

    return rowcall("assemble_na", body, T, tm, 12, Cn,
                   [(dK, 128, lambda j: jnp.minimum(j, 3)), (dV, 128, lambda j: jnp.clip(j - 4, 0, 3)), (dQ, 128, lambda j: jnp.clip(j - 8, 0, 3))],
                   [], [], [("into", dP, 128, lambda j: jnp.where(j < 8, j, j + 2))])[0]


def branch_fwd(G, Wb, tm):
    T = G.shape[0]
    D = Wb.shape[2]

    def kern(g_ref, w_ref, o_ref):
        o_ref[...] = _dg(g_ref[...], w_ref[...], 1, 0)

    return pl.pallas_call(
        kern, grid=(T // tm, 4),
        in_specs=[pl.BlockSpec((tm, 512), lambda i, b: (i, b)), pl.BlockSpec((None, 512, D), lambda i, b: (b, 0, 0))],
        out_specs=pl.BlockSpec((tm, D), lambda i, b: (i, b)), out_shape=jax.ShapeDtypeStruct((T, 4 * D), F32),
        compiler_params=_cp("parallel", "arbitrary"), name="branch_fwd")(G, Wb)


def branch_dg(dU, Wb, tm):
    T = dU.shape[0]
    D = Wb.shape[2]

    def kern(u_ref, w_ref, o_ref):
        o_ref[...] = _dg(u_ref[...], w_ref[...], 1, 1)

    return pl.pallas_call(
        kern, grid=(T // tm, 4),
        in_specs=[pl.BlockSpec((tm, D), lambda i, b: (i, b)), pl.BlockSpec((None, 512, D), lambda i, b: (b, 0, 0))],
        out_specs=pl.BlockSpec((tm, 512), lambda i, b: (i, b)), out_shape=jax.ShapeDtypeStruct((T, 2048), F32),
        compiler_params=_cp("parallel", "arbitrary"), name="branch_dg")(dU, Wb)


def branch_merge_fwd(P, G, Wb, tm):
    T = P.shape[0]
    D = Wb.shape[2]
    nj = D // 256
    mb = CB["merge"] // 2

    def kern(g_ref, w_ref, m0, m1, m2, m3, y_ref):
        y = None
        for b, m_ref in enumerate((m0, m1, m2, m3)):
            term = jax.nn.sigmoid(m_ref[...]) * _dg(g_ref[:, 512 * b:512 * (b + 1)], w_ref[b], 1, 0)
            y = term if y is None else y + term
        y_ref[...] = y.astype(BF)

    m_specs = [pl.BlockSpec((tm, 256), (lambda i, j, b=b: (i, mb + b * nj + j))) for b in range(4)]
    return pl.pallas_call(
        kern, grid=(T // tm, nj),
        in_specs=[pl.BlockSpec((tm, 2048), lambda i, j: (i, 0)), pl.BlockSpec((4, 512, 256), lambda i, j: (0, 0, j))] + m_specs,
        out_specs=pl.BlockSpec((tm, 256), lambda i, j: (i, j)), out_shape=jax.ShapeDtypeStruct((T, D), BF),
        compiler_params=_cp("parallel", "arbitrary"), name="branch_merge_fwd")(G, Wb, P, P, P, P)


def branch_merge_bwd(P, G, Wb, dY, tm):
    T, IN = P.shape
    D = Wb.shape[2]
    nj = D // 256
    mb = CB["merge"] // 2

    def kern(g_ref, w_ref, m_ref, dy_ref, du_ref, dg_ref, dm_ref, acc_s):
        j = pl.program_id(2)
        w = w_ref[...]
        u = _dg(g_ref[...], w, 1, 0)
        sig = jax.nn.sigmoid(m_ref[...])
        dy = dy_ref[...]
        dm_ref[...] = (dy * u * sig * (1.0 - sig)).astype(BF)
        du = dy * sig
        du_ref[...] = du.astype(BF)
        part = _dg(du, w, 1, 1)

        @pl.when(j == 0)
        def _():
            acc_s[...] = part

        @pl.when(j > 0)
        def _():
            acc_s[...] += part

        @pl.when(j == nj - 1)
        def _():
            dg_ref[...] = acc_s[...]

    return pl.pallas_call(
        kern, grid=(T // tm, 4, nj),
        in_specs=[pl.BlockSpec((tm, 512), lambda i, b, j: (i, b)), pl.BlockSpec((None, 512, 256), lambda i, b, j: (b, 0, j)),
                  pl.BlockSpec((tm, 256), lambda i, b, j: (i, mb + b * nj + j)), pl.BlockSpec((tm, 256), lambda i, b, j: (i, j))],
        out_specs=[pl.BlockSpec((tm, 256), lambda i, b, j: (i, b * nj + j)), pl.BlockSpec((tm, 512), lambda i, b, j: (i, b)),
                   pl.BlockSpec((tm, 256), lambda i, b, j: (i, mb + b * nj + j))],
        out_shape=[jax.ShapeDtypeStruct((T, 4 * D), BF), jax.ShapeDtypeStruct((T, 2048), F32), jax.ShapeDtypeStruct((T, IN), BF)],
        scratch_shapes=[pltpu.VMEM((tm, 512), F32)],
        compiler_params=_cp("parallel", "arbitrary", "arbitrary"), name="branch_merge_bwd")(G, Wb, P, dY)


def rows_from_devs(name, xg, l, L):
    _, Lr, C = xg.shape
    r = Lr // L

    def body(x_ref, o_ref):
        o_ref[...] = x_ref[...]

    return pl.pallas_call(
        body, grid=(N_DEV,), in_specs=[pl.BlockSpec((None, r, C), lambda d: (d, l, 0))], out_specs=pl.BlockSpec((r, C), lambda d: (d, 0)),
        out_shape=jax.ShapeDtypeStruct((N_DEV * r, C), xg.dtype), compiler_params=_cp("parallel"), name=name)(xg)


def rows_to_devs(name, g, l, L, prev):
    R8, C = g.shape
    r = R8 // N_DEV

    def body(x_ref, *rest):
        rest[-1][...] = x_ref[...]

    in_specs, args, aliases = [pl.BlockSpec((r, C), lambda d: (d, 0))], [g], {}
    if prev is not None:
        in_specs.append(pl.BlockSpec(memory_space=pl.ANY))
        args.append(prev)
        aliases = {1: 0}
    return pl.pallas_call(
        body, grid=(N_DEV,), in_specs=in_specs, out_specs=pl.BlockSpec((None, r, C), lambda d: (d, l, 0)),
        out_shape=jax.ShapeDtypeStruct((N_DEV, L * r, C), g.dtype), input_output_aliases=aliases,
        compiler_params=_cp("parallel"), name=name)(*args)


def branch_dw(G, dU, tk):
    T = G.shape[0]
    D = dU.shape[1] // 4
    nk = T // tk

    def kern(g_ref, u_ref, o_ref):
        k = pl.program_id(1)
        part = _dg(g_ref[...], u_ref[...], 0, 0)

        @pl.when(k == 0)
        def _():
            o_ref[...] = part

        @pl.when(k > 0)
        def _():
            o_ref[...] += part

    return pl.pallas_call(
        kern, grid=(4, nk),
        in_specs=[pl.BlockSpec((tk, 512), lambda b, k: (k, b)), pl.BlockSpec((tk, D), lambda b, k: (k, b))],
        out_specs=pl.BlockSpec((None, 512, D), lambda b, k: (b, 0, 0)), out_shape=jax.ShapeDtypeStruct((4, 512, D), F32),
        compiler_params=_cp("parallel", "arbitrary"), name="branch_dw")(G, dU)


def na_tables(S, Cn):
    R = S // GRID_W
    ntile = (Cn + S) // 128
    nct = Cn // 128
    rs = np.clip(np.arange(R) - NA_WIN_ROWS // 2, 0, R - NA_WIN_ROWS)
    seen, pats, cls, ws = {}, [], [], []
    for t in range(ntile):
        pat = -np.ones((NA_TILE_ROWS, NA_BAND), np.int64)
        w0 = 0
        if t >= nct:
            r0 = NA_TILE_ROWS * (t - nct)
            w0 = min(rs[r0], R - NA_BAND)
            for a in range(NA_TILE_ROWS):
                for j in range(NA_BAND):
                    kr = w0 + j
                    if rs[r0 + a] <= kr < rs[r0 + a] + NA_WIN_ROWS:
                        pat[a, j] = kr - (r0 + a) + NA_WIN_ROWS - 1
        key = pat.tobytes()
        if key not in seen:
            seen[key] = len(pats)
            pats.append(pat)
        cls.append(seen[key])
        ws.append(Cn + GRID_W * int(w0))
    cls = np.asarray(cls, np.int32)
    first = np.asarray([1 if t == 0 or cls[t] != cls[t - 1] else 0 for t in range(ntile)], np.int32)
    assert len(set(cls[first == 1].tolist())) == int(first.sum())
    pats = np.stack(pats)
    ncls = pats.shape[0]
    nrow = -(-(ncls * NA_TILE_ROWS * NA_BAND) // 128) * 128
    m1 = np.zeros((nrow, 128), np.float32)
    flat = pats.reshape(-1)
    for k, dr in enumerate(flat):
        if dr >= 0:
            m1[k, dr] = 1.0
    qc = np.arange(GRID_W)
    col_start = np.clip(qc - NA_WIN_COLS // 2, 0, GRID_W - NA_WIN_COLS)
    kc = np.arange(GRID_W)
    col_ok = (kc[None, :] >= col_start[:, None]) & (kc[None, :] < col_start[:, None] + NA_WIN_COLS)
    m2 = np.zeros((128, GRID_W * GRID_W), np.float32)
    for q in range(GRID_W):
        for k in range(GRID_W):
            if col_ok[q, k]:
                m2[k - q + NA_WIN_COLS - 1, q * GRID_W + k] = 1.0
    valid = (pats >= 0)[:, :, :, None, None] & col_ok[None, None, None]
    vmask = np.transpose(valid, (0, 2, 4, 1, 3)).reshape(ncls, 1, NA_BAND * GRID_W, 1, NA_TILE_ROWS * GRID_W)
    vmask = np.broadcast_to(vmask, (ncls, 1, NA_BAND * GRID_W, 2, NA_TILE_ROWS * GRID_W)).reshape(ncls, 1, NA_BAND * GRID_W, 256)
    return dict(cls=cls, ws=np.asarray(ws, np.int32), first=first, m1=m1, m2=m2, vmask=vmask, ncls=ncls, nrow=nrow)


def rpb_map(name, x, left, right):
    H = x.shape[0]

    def kern(x_ref, l_ref, r_ref, o_ref):
        o_ref[...] = hdot(hdot(l_ref[...], x_ref[...]), r_ref[...])

    return pl.pallas_call(
        kern, grid=(H,),
        in_specs=[pl.BlockSpec((None,) + x.shape[1:], lambda h: (h, 0, 0)), pl.BlockSpec(left.shape, lambda h: (0, 0)),
                  pl.BlockSpec(right.shape, lambda h: (0, 0))],
        out_specs=pl.BlockSpec((None, left.shape[0], right.shape[1]), lambda h: (h, 0, 0)),
        out_shape=jax.ShapeDtypeStruct((H, left.shape[0], right.shape[1]), F32), compiler_params=_cp("parallel"), name=name)(x, left, right)


def bias_table(rpb, tb):
    H = rpb.shape[0]
    xp = jnp.zeros((H, 128, 128), F32).at[:, :rpb.shape[1], :rpb.shape[2]].set(rpb)
    a = rpb_map("rpb_expand", xp, jnp.asarray(tb["m1"]), jnp.asarray(tb["m2"]))
    ncls = tb["ncls"]
    a = a[:, :ncls * NA_TILE_ROWS * NA_BAND].reshape(H // 2, 2, ncls, NA_TILE_ROWS, NA_BAND, GRID_W, GRID_W)
    a = jnp.transpose(a, (2, 0, 4, 6, 1, 3, 5)).reshape(ncls, H // 2, NA_BAND * GRID_W, 256)
    return jnp.where(jnp.asarray(tb["vmask"]), a, NEG)


def bias_table_grad(dbt, tb):
    ncls = tb["ncls"]
    H2 = dbt.shape[1]
    d = dbt.reshape(ncls, H2, NA_BAND, GRID_W, 2, NA_TILE_ROWS, GRID_W)
    d = jnp.transpose(d, (1, 4, 0, 5, 2, 6, 3)).reshape(H2 * 2, ncls * NA_TILE_ROWS * NA_BAND, GRID_W * GRID_W)
    d = jnp.pad(d, ((0, 0), (0, tb["nrow"] - d.shape[1]), (0, 0)))
    g = rpb_map("rpb_reduce", d, jnp.asarray(tb["m1"].T.copy()), jnp.asarray(tb["m2"].T.copy()))
    return g[:, :2 * NA_WIN_ROWS - 1, :2 * NA_WIN_COLS - 1]


def f_na(q, kb, vb, kc, vc, bias_t):
    d = lax.broadcasted_iota(jnp.int32, (128, 1), 0)
    q_t = q.T
    qbd = jnp.concatenate([jnp.where(d < 64, q_t, 0.0), jnp.where(d >= 64, q_t, 0.0)], axis=1)
    sb = bdot(kb, qbd) * 0.125 + bias_t
    sc = bdot(kc, qbd) * 0.125
    m = jnp.maximum(jnp.max(sb, axis=0, keepdims=True), jnp.max(sc, axis=0, keepdims=True))
    eb = jnp.exp(sb - m)
    ec = jnp.exp(sc - m)
    den = jnp.sum(eb, axis=0, keepdims=True) + jnp.sum(ec, axis=0, keepdims=True)
    of = bdot(eb / den, vb, 0, 0) + bdot(ec / den, vc, 0, 0)
    o0, o1 = split_rows(of, 2)
    return jnp.where(_lane(o0.shape) < 64, o0, o1)


def _na_specs(T, Cn, nband):
    qs = pl.BlockSpec((128, 128), lambda hp, t, *_: (t, CB["a_q"] + hp))
    ks = pl.BlockSpec((T, 128), lambda hp, t, *_: (0, CB["a_k"] + hp))
    vs = pl.BlockSpec((T, 128), lambda hp, t, *_: (0, CB["a_v"] + hp))
    bs = pl.BlockSpec((None, None, nband, 256), lambda hp, t, cls, ws, first: (cls[t], hp, 0, 0))
    return qs, ks, vs, bs


def na_fwd(P, bt, tb, Cn):
    T = P.shape[0]
    nband = NA_BAND * GRID_W
    qs, ks, vs, bs = _na_specs(T, Cn, nband)

    def kern(cls, ws, first, q_ref, k_ref, v_ref, b_ref, o_ref):
        w0 = pl.multiple_of(ws[pl.program_id(1)], 64)
        o_ref[...] = f_na(q_ref[...], k_ref[pl.ds(w0, nband), :], v_ref[pl.ds(w0, nband), :], k_ref[0:Cn, :], v_ref[0:Cn, :], b_ref[...])

    return pl.pallas_call(
        kern, grid_spec=pltpu.PrefetchScalarGridSpec(
            num_scalar_prefetch=3, grid=(4, T // 128), in_specs=[qs, ks, vs, bs],
            out_specs=pl.BlockSpec((128, 128), lambda hp, t, *_: (t, hp))),
        out_shape=jax.ShapeDtypeStruct((T, 512), F32), compiler_params=_cp("arbitrary", "arbitrary"), name="na_fwd",
    )(jnp.asarray(tb["cls"]), jnp.asarray(tb["ws"]), jnp.asarray(tb["first"]), P, P, P, bt)


def na_bwd(P, bt, dO, tb, Cn):
    T = P.shape[0]
    nband = NA_BAND * GRID_W
    qs, ks, vs, bs = _na_specs(T, Cn, nband)
    dos = pl.BlockSpec((128, 128), lambda hp, t, *_: (t, hp))

    def kern(cls, ws, first, q_ref, k_ref, v_ref, b_ref, do_ref, dq_ref, dk_ref, dv_ref, db_ref):
        t = pl.program_id(1)
        w0 = pl.multiple_of(ws[t], 64)
        band = pl.ds(w0, nband)
        _, vjp = jax.vjp(f_na, q_ref[...], k_ref[band, :], v_ref[band, :], k_ref[0:Cn, :], v_ref[0:Cn, :], b_ref[...])
        dq, dkb, dvb, dkc, dvc, db = vjp(do_ref[...])
        dq_ref[...] = dq

        @pl.when(t == 0)
        def _():
            dk_ref[...] = jnp.zeros_like(dk_ref)
            dv_ref[...] = jnp.zeros_like(dv_ref)

        dk_ref[band, :] += dkb
        dv_ref[band, :] += dvb
        dk_ref[0:Cn, :] += dkc
        dv_ref[0:Cn, :] += dvc

        @pl.when(first[t] == 1)
        def _():
            db_ref[...] = db

        @pl.when(first[t] == 0)
        def _():
            db_ref[...] += db

    full = lambda hp, t, *_: (0, hp)
    return pl.pallas_call(
        kern, grid_spec=pltpu.PrefetchScalarGridSpec(
            num_scalar_prefetch=3, grid=(4, T // 128), in_specs=[qs, ks, vs, bs, dos],
            out_specs=[pl.BlockSpec((128, 128), lambda hp, t, *_: (t, hp)), pl.BlockSpec((T, 128), full), pl.BlockSpec((T, 128), full),
                       pl.BlockSpec((None, None, nband, 256), lambda hp, t, cls, ws, first: (cls[t], hp, 0, 0))]),
        out_shape=[jax.ShapeDtypeStruct((T, 512), F32), jax.ShapeDtypeStruct((T, 512), F32), jax.ShapeDtypeStruct((T, 512), F32),
                   jax.ShapeDtypeStruct(bt.shape, F32)],
        compiler_params=_cp("arbitrary", "arbitrary"), name="na_bwd",
    )(jnp.asarray(tb["cls"]), jnp.asarray(tb["ws"]), jnp.asarray(tb["first"]), P, P, P, bt, dO)


def _expand_heads(blk, g):
    out = []
    for p in range(2):
        pair = blk[:, 128 * p:128 * (p + 1)]
        lane = _lane(pair.shape)
        a = jnp.where(lane < 64, pair, 0.0)
        b = jnp.where(lane >= 64, pair, 0.0)
        out.append(jnp.where(g == 0, a, pltpu.roll(a, 64, 1)))
        out.append(jnp.where(g == 0, pltpu.roll(b, 64, 1), b))
    return out


def _compact_heads(hs, g):
    out = []
    for p in range(2):
        e, o = hs[2 * p], hs[2 * p + 1]
        lane = _lane(e.shape)
        e0 = jnp.where(g == 0, e, pltpu.roll(e, 64, 1))
        o1 = jnp.where(g == 0, pltpu.roll(o, 64, 1), o)
        out.append(jnp.where(lane < 64, e0, o1))
    return jnp.concatenate(out, axis=1)


def _kv_map(Cn, tq, tk, col):
    nq_ctx = Cn // tq
    last_ctx = (Cn - 1) // tk

    def f(g, qi, kj):
        return (jnp.where(qi < nq_ctx, jnp.minimum(kj, last_ctx), kj), col)
    return f


def flash_fwd(Qp, KV, Cn, tq, tk):
    T = Qp.shape[0]
    nq, nk = T // tq, T // tk
    nq_ctx = Cn // tq

    def kern(q_ref, k_ref, v_ref, o_ref, lse_ref, m_s, l_s, acc_s):
        g, qi, kj = pl.program_id(0), pl.program_id(1), pl.program_id(2)

        @pl.when(kj == 0)
        def _():
            m_s[...] = jnp.full_like(m_s, NEG)
            l_s[...] = jnp.zeros_like(l_s)
            acc_s[...] = jnp.zeros_like(acc_s)

        def step(masked):
            k = k_ref[...]
            v = v_ref[...]
            if masked:
                valid = kj * tk + lax.broadcasted_iota(jnp.int32, (1, tk), 1) < Cn
            for h in range(4):
                rows = pl.ds(tq * h, tq)
                s = _dg(q_ref[:, 128 * h:128 * (h + 1)], k, 1, 1)
                if masked:
                    s = jnp.where(valid, s, NEG)
                m_old = m_s[rows, :]
                m_new = jnp.maximum(m_old, jnp.max(s, axis=1, keepdims=True))
                alpha = jnp.exp(m_old - m_new)
                p = jnp.exp(s - m_new)
                l_s[rows, :] = alpha * l_s[rows, :] + jnp.sum(p, axis=1, keepdims=True)
                acc_s[rows, :] = alpha * acc_s[rows, :] + _dg(p, v, 1, 0)
                m_s[rows, :] = m_new

        @pl.when(qi >= nq_ctx)
        def _():
            step(False)

        @pl.when((qi < nq_ctx) & (kj * tk < Cn))
        def _():
            step(True)

        @pl.when(kj == nk - 1)
        def _():
            o4 = acc_s[...] / l_s[...]
            o_ref[...] = _compact_heads([o4[tq * h:tq * (h + 1)] for h in range(4)], g)
            lse = m_s[...] + jnp.log(l_s[...])
            lse_ref[...] = jnp.concatenate([jnp.broadcast_to(lse[tq * h:tq * (h + 1)], (tq, 128)) for h in range(4)], axis=1)

    return pl.pallas_call(
        kern, grid=(2, nq, nk),
        in_specs=[pl.BlockSpec((tq, 512), lambda g, qi, kj: (qi, g)), pl.BlockSpec((tk, 128), _kv_map(Cn, tq, tk, 0)),
                  pl.BlockSpec((tk, 128), _kv_map(Cn, tq, tk, 1))],
        out_specs=[pl.BlockSpec((tq, 256), lambda g, qi, kj: (qi, g)), pl.BlockSpec((tq, 512), lambda g, qi, kj: (qi, g))],
        out_shape=[jax.ShapeDtypeStruct((T, 512), F32), jax.ShapeDtypeStruct((T, 1024), F32)],
        scratch_shapes=[pltpu.VMEM((4 * tq, 1), F32), pltpu.VMEM((4 * tq, 1), F32), pltpu.VMEM((4 * tq, 128), F32)],
        compiler_params=_cp("arbitrary", "arbitrary", "arbitrary"), name="flash_fwd")(Qp, KV, KV)


def flash_bwd(Qp, KV, O, LSE, dOall, Cn, tq, tk):
    T = Qp.shape[0]
    nq, nk = T // tq, T // tk
    nq_ctx = Cn // tq

    def kern(q_ref, k_ref, v_ref, o_ref, lse_ref, do_ref, dq_ref, dkv_ref, dq_s, do_s, dl_s, ls_s):
        g, qi, kj = pl.program_id(0), pl.program_id(1), pl.program_id(2)

        @pl.when((g == 0) & (qi == 0) & (kj == 0))
        def _():
            dkv_ref[...] = jnp.zeros_like(dkv_ref)

        @pl.when(kj == 0)
        def _():
            do4 = jnp.concatenate(_expand_heads(do_ref[...], g), axis=0)
            o4 = jnp.concatenate(_expand_heads(o_ref[...], g), axis=0)
            do_s[...] = do4.astype(BF)
            dl_s[...] = jnp.sum(do4 * o4, axis=1, keepdims=True)
            ls_s[...] = jnp.concatenate([jnp.max(lse_ref[:, 128 * h:128 * (h + 1)], axis=1, keepdims=True) for h in range(4)], axis=0)
            dq_s[...] = jnp.zeros_like(dq_s)

        def step(masked):
            k = k_ref[...]
            v = v_ref[...]
            if masked:
                valid = kj * tk + lax.broadcasted_iota(jnp.int32, (1, tk), 1) < Cn
            dk = dv = None
            for h in range(4):
                rows = pl.ds(tq * h, tq)
                qh = q_ref[:, 128 * h:128 * (h + 1)]
                s = _dg(qh, k, 1, 1)
                if masked:
                    s = jnp.where(valid, s, NEG)
                p = jnp.exp(s - ls_s[rows, :])
                doh = do_s[rows, :]
                dvh = _dg(p, doh, 0, 0)
                dp = _dg(doh, v, 1, 1)
                ds = p * (dp - dl_s[rows, :])
                dq_s[rows, :] += _dg(ds, k, 1, 0)
                dkh = _dg(ds, qh, 0, 0)
                dk = dkh if dk is None else dk + dkh
                dv = dvh if dv is None else dv + dvh
            krows = pl.ds(pl.multiple_of(kj * tk, tk), tk)
            dkv_ref[krows, 0:128] += dk
            dkv_ref[krows, 128:256] += dv

        @pl.when(qi >= nq_ctx)
        def _():
            step(False)

        @pl.when((qi < nq_ctx) & (kj * tk < Cn))
        def _():
            step(True)

        @pl.when(kj == nk - 1)
        def _():
            dq = dq_s[...]
            dq_ref[...] = jnp.concatenate([dq[tq * h:tq * (h + 1)] for h in range(4)], axis=1)

    return pl.pallas_call(
        kern, grid=(2, nq, nk),
        in_specs=[pl.BlockSpec((tq, 512), lambda g, qi, kj: (qi, g)), pl.BlockSpec((tk, 128), _kv_map(Cn, tq, tk, 0)),
                  pl.BlockSpec((tk, 128), _kv_map(Cn, tq, tk, 1)), pl.BlockSpec((tq, 256), lambda g, qi, kj: (qi, g)),
                  pl.BlockSpec((tq, 512), lambda g, qi, kj: (qi, g)), pl.BlockSpec((tq, 256), lambda g, qi, kj: (qi, 4 + g))],
        out_specs=[pl.BlockSpec((tq, 512), lambda g, qi, kj: (qi, g)), pl.BlockSpec((T, 256), lambda g, qi, kj: (0, 0))],
        out_shape=[jax.ShapeDtypeStruct((T, 1024), F32), jax.ShapeDtypeStruct((T, 256), F32)],
        scratch_shapes=[pltpu.VMEM((4 * tq, 128), F32), pltpu.VMEM((4 * tq, 128), BF), pltpu.VMEM((4 * tq, 1), F32), pltpu.VMEM((4 * tq, 1), F32)],
        compiler_params=_cp("arbitrary", "arbitrary", "arbitrary"), name="flash_bwd")(Qp, KV, KV, O, LSE, dOall)


def _pool_band(t0, w0, n, win, gi, Cn, T):
    i = lax.broadcasted_iota(jnp.int32, (n, win), 0)
    jx = lax.broadcasted_iota(jnp.int32, (n, win), 1)
    t = t0 + i
    tp = w0 + jx
    half = lax.shift_left(jnp.int32(1), gi)
    lo = jnp.maximum(t - half, jnp.where(t < Cn, 0, Cn))
    hi = jnp.minimum(t + half - 1, jnp.where(t < Cn, Cn, T) - 1)
    cnt = (hi - lo + 1).astype(F32)
    return jnp.where((tp >= lo) & (tp <= hi), 1.0 / cnt, 0.0) - jnp.where(tp == t, 1.0, 0.0)


def _pool_geom(T):
    n = _pick(T, (256, 128))
    return n, n + 128


def pool_fwd(P, pw, ps, Cn):
    T = P.shape[0]
    n, win = _pool_geom(T)

    def kern(u_ref, w_ref, s_ref, o_ref):
        gi = pl.program_id(0)

        def blk(b, carry):
            t0 = pl.multiple_of(b * n, n)
            w0 = pl.multiple_of(jnp.clip(t0 - 64, 0, T - win), 64)
            d = hdot3(_pool_band(t0, w0, n, win, gi, Cn, T), u_ref[pl.ds(w0, win), :])
            o_ref[pl.ds(t0, n), :] = bdot(d, w_ref[...]) * s_ref[...]
            return carry

        lax.fori_loop(0, T // n, blk, 0)

    return pl.pallas_call(
        kern, grid=(4,),
        in_specs=[pl.BlockSpec((T, 128), lambda g: (0, CB["b_in"] + g)), pl.BlockSpec((None, 128, 128), lambda g: (g, 0, 0)),
                  pl.BlockSpec((1, 128), lambda g: (0, g))],
        out_specs=pl.BlockSpec((T, 128), lambda g: (0, g)), out_shape=jax.ShapeDtypeStruct((T, 512), F32),
        compiler_params=_cp("arbitrary"), name="pool_fwd")(P, pw, ps)


def pool_bwd(P, pw, ps, dOall, dP, Cn):
    T = P.shape[0]
    n, win = _pool_geom(T)

    def kern(u_ref, w_ref, s_ref, do_ref, dp_in, du_out, dw_ref, ds_ref, du_ref):
        gi = pl.program_id(0)
        du_ref[...] = jnp.zeros_like(du_ref)
        dw_ref[...] = jnp.zeros_like(dw_ref)
        ds_ref[...] = jnp.zeros_like(ds_ref)

        def blk(b, carry):
            t0 = pl.multiple_of(b * n, n)
            w0 = pl.multiple_of(jnp.clip(t0 - 64, 0, T - win), 64)
            band = _pool_band(t0, w0, n, win, gi, Cn, T)
            _, vjp = jax.vjp(lambda uw, w, s: bdot(hdot3(band, uw), w) * s, u_ref[pl.ds(w0, win), :], w_ref[...], s_ref[...])
            duw, dw, ds = vjp(do_ref[pl.ds(t0, n), :])
            du_ref[pl.ds(w0, win), :] += duw
            dw_ref[...] += dw
            ds_ref[...] += ds
            return carry

        lax.fori_loop(0, T // n, blk, 0)
        du_out[...] = du_ref[...].astype(du_out.dtype)

    return pl.pallas_call(
        kern, grid=(4,),
        in_specs=[pl.BlockSpec((T, 128), lambda g: (0, CB["b_in"] + g)), pl.BlockSpec((None, 128, 128), lambda g: (g, 0, 0)),
                  pl.BlockSpec((1, 128), lambda g: (0, g)), pl.BlockSpec((T, 128), lambda g: (0, 4 + g)), pl.BlockSpec(memory_space=pl.ANY)],
        out_specs=[pl.BlockSpec((T, 128), lambda g: (0, CB["b_in"] + g)), pl.BlockSpec((None, 128, 128), lambda g: (g, 0, 0)),
                   pl.BlockSpec((1, 128), lambda g: (0, g))],
        out_shape=[jax.ShapeDtypeStruct(dP.shape, dP.dtype), jax.ShapeDtypeStruct((4, 128, 128), F32), jax.ShapeDtypeStruct((1, 512), F32)],
        scratch_shapes=[pltpu.VMEM((T, 128), F32)],
        input_output_aliases={4: 0}, compiler_params=_cp("arbitrary"), name="pool_bwd")(P, pw, ps, dOall, dP)


CONV_HALO = 16


def _conv_block(T):
    return _pick(T, (256, 128))


def _stage(dst_ref, src, t0, n, T, Cn):
    h = CONV_HALO
    dst_ref[h:h + n, :] = src(pl.ds(t0, n))
    left_ok = (t0 != 0) & (t0 != Cn)
    right_ok = (t0 + n != Cn) & (t0 + n != T)
    lo = pl.multiple_of(jnp.maximum(t0 - h, 0), 8)
    hi = pl.multiple_of(jnp.minimum(t0 + n, T - h), 8)
    dst_ref[0:h, :] = jnp.where(left_ok, src(pl.ds(lo, h)), 0.0)
    dst_ref[h + n:2 * h + n, :] = jnp.where(right_ok, src(pl.ds(hi, h)), 0.0)


def conv_fwd(P, cw, cb, Cn):
    T = P.shape[0]
    n = _conv_block(T)
    off = CONV_HALO - CONV_WIDTH // 2

    def kern(a_ref, g_ref, w_ref, b_ref, y_ref, us):
        def blk(b, carry):
            t0 = pl.multiple_of(b * n, n)
            _stage(us, lambda r: a_ref[r, :] * jax.nn.sigmoid(g_ref[r, :]), t0, n, T, Cn)
            acc = jnp.zeros((n, 128), F32)
            for k in range(CONV_WIDTH):
                acc = acc + us[k + off:k + off + n, :] * w_ref[k:k + 1, :]
            y_ref[pl.ds(t0, n), :] = acc + b_ref[...]
            return carry

        lax.fori_loop(0, T // n, blk, 0)

    slab = lambda o: pl.BlockSpec((T, 128), lambda c: (0, o + c))
    return pl.pallas_call(
        kern, grid=(4,),
        in_specs=[slab(CB["d_glu"]), slab(CB["d_glu"] + 4), pl.BlockSpec((CONV_PAD, 128), lambda c: (0, c)), pl.BlockSpec((1, 128), lambda c: (0, c))],
        out_specs=pl.BlockSpec((T, 128), lambda c: (0, c)), out_shape=jax.ShapeDtypeStruct((T, 512), F32),
        scratch_shapes=[pltpu.VMEM((n + 2 * CONV_HALO, 128), F32)],
        compiler_params=_cp("arbitrary"), name="conv_fwd")(P, P, cw, cb)


def conv_bwd(P, cw, dY, dP, Cn):
    T = P.shape[0]
    n = _conv_block(T)
    off = CONV_HALO - CONV_WIDTH // 2
    back = CONV_HALO + CONV_WIDTH // 2

    def kern(a_ref, g_ref, w_ref, dy_ref, dp_in, da_ref, dg_ref, dw_ref, db_ref, us, dys):
        dw_ref[...] = jnp.zeros_like(dw_ref)
        db_ref[...] = jnp.zeros_like(db_ref)

        def blk(b, carry):
            t0 = pl.multiple_of(b * n, n)
            cur = pl.ds(t0, n)
            _stage(us, lambda r: a_ref[r, :] * jax.nn.sigmoid(g_ref[r, :]), t0, n, T, Cn)
            _stage(dys, lambda r: dy_ref[r, :], t0, n, T, Cn)
            dyc = dy_ref[cur, :]
            du = jnp.zeros((n, 128), F32)
            for k in range(CONV_WIDTH):
                du = du + dys[back - k:back - k + n, :] * w_ref[k:k + 1, :]
                dw_ref[k:k + 1, :] += jnp.sum(us[k + off:k + off + n, :] * dyc, axis=0, keepdims=True)
            a = a_ref[cur, :]
            sig = jax.nn.sigmoid(g_ref[cur, :])
            da_ref[cur, :] = (du * sig).astype(da_ref.dtype)
            dg_ref[cur, :] = du * a * sig * (1.0 - sig)
            db_ref[...] += jnp.sum(dyc, axis=0, keepdims=True)
            return carry

        lax.fori_loop(0, T // n, blk, 0)

    slab = lambda o: pl.BlockSpec((T, 128), lambda c: (0, o + c))
    return pl.pallas_call(
        kern, grid=(4,),
        in_specs=[slab(CB["d_glu"]), slab(CB["d_glu"] + 4), pl.BlockSpec((CONV_PAD, 128), lambda c: (0, c)), slab(0),
                  pl.BlockSpec(memory_space=pl.ANY)],
        out_specs=[slab(CB["d_glu"]), slab(0), pl.BlockSpec((CONV_PAD, 128), lambda c: (0, c)), pl.BlockSpec((1, 128), lambda c: (0, c))],
        out_shape=[jax.ShapeDtypeStruct(dP.shape, dP.dtype), jax.ShapeDtypeStruct((T, 512), F32), jax.ShapeDtypeStruct((CONV_PAD, 512), F32),
                   jax.ShapeDtypeStruct((1, 512), F32)],
        scratch_shapes=[pltpu.VMEM((n + 2 * CONV_HALO, 128), F32), pltpu.VMEM((n + 2 * CONV_HALO, 128), F32)],
        input_output_aliases={4: 0}, compiler_params=_cp("arbitrary"), name="conv_bwd")(P, P, cw, dY, dP)


def copy_cols(name, src, dP, col0, Cn, tm):
    T = dP.shape[0]

    def body(i, j, r, c, s):
        return [r[0]], [], []

    return rowcall(name, body, T, tm, src.shape[1] // 128, Cn, [(src, 128, lambda j: j)], [], [], [("into", dP, 128, lambda j: col0 + j)])[0]


def ada_fwd(cvec, w, b):
    L, D, wc = w.shape

    def kern(c_ref, w_ref, b_ref, o_ref):
        o_ref[...] = _dg(_silu(c_ref[...]), w_ref[...], 1, 0) + b_ref[...]

    return pl.pallas_call(
        kern, grid=(L,),
        in_specs=[pl.BlockSpec((16, D), lambda l: (0, 0)), pl.BlockSpec((None, D, wc), lambda l: (l, 0, 0)), pl.BlockSpec((None, 1, wc), lambda l: (l, 0, 0))],
        out_specs=pl.BlockSpec((None, 16, wc), lambda l: (l, 0, 0)), out_shape=jax.ShapeDtypeStruct((L, 16, wc), F32),
        compiler_params=_cp("arbitrary"), name="ada_fwd")(cvec, w, b)


def ada_bwd(cvec, w, dm):
    L, D, wc = w.shape

    def kern(c_ref, w_ref, d_ref, gw_ref, ds_ref):
        l = pl.program_id(0)
        d = d_ref[...]
        gw_ref[...] = _dg(_silu(c_ref[...]), d, 0, 0)
        part = _dg(d, w_ref[...], 1, 1)

        @pl.when(l == 0)
        def _():
            ds_ref[...] = part

        @pl.when(l > 0)
        def _():
            ds_ref[...] += part

    return pl.pallas_call(
        kern, grid=(L,),
        in_specs=[pl.BlockSpec((16, D), lambda l: (0, 0)), pl.BlockSpec((None, D, wc), lambda l: (l, 0, 0)), pl.BlockSpec((None, 16, wc), lambda l: (l, 0, 0))],
        out_specs=[pl.BlockSpec((None, D, wc), lambda l: (l, 0, 0)), pl.BlockSpec((16, D), lambda l: (0, 0))],
        out_shape=[jax.ShapeDtypeStruct((L, D, wc), F32), jax.ShapeDtypeStruct((16, D), F32)],
        compiler_params=_cp("arbitrary"), name="ada_bwd")(cvec, w, dm)


def silu_grad(cc, parts):
    D = cc.shape[1]

    def kern(c_ref, p_ref, o_ref):
        tot = p_ref[0]
        for k in range(1, N_DEV):
            tot = tot + p_ref[k]
        _, vjp = jax.vjp(_silu, c_ref[...])
        o_ref[...] = vjp(tot)[0]

    return pl.pallas_call(kern, out_shape=jax.ShapeDtypeStruct((1, D), F32), name="silu_grad")(cc, parts)


def _flip(v, f):
    return 1 - v if f else v


_REL = ((0, 0), (1, 0), (0, 1), (1, 1))


def allgather(name, x):
    def body(x_ref, out_ref, send_sems, recv_sems, local_sem):
        mx, my, mc = lax.axis_index("x"), lax.axis_index("y"), lax.axis_index("c")
        me, sibling = (mx, my, mc), (mx, my, 1 - mc)
        chips = [(_flip(mx, fx), _flip(my, fy)) for fx, fy in _REL[1:]]

        def slot(px, py, pc):
            return out_ref.at[4 * px + 2 * py + pc]

        def copy(k, block, to, src=None):
            return pltpu.make_async_remote_copy(
                src_ref=slot(*block) if src is None else src, dst_ref=slot(*block),
                send_sem=send_sems.at[k], recv_sem=recv_sems.at[k], device_id=to, device_id_type=MESH)

        mine = pltpu.make_async_copy(x_ref, slot(*me), local_sem)
        mine.start()
        first = [copy(0, me, sibling, src=x_ref)] + [copy(1 + j, me, (*chip, mc), src=x_ref) for j, chip in enumerate(chips)]
        for cp in first:
            cp.start()
        passed = [copy(4 + j, (*chip, mc), sibling) for j, chip in enumerate(chips)]
        for j, chip in enumerate(chips):
            copy(1 + j, (*chip, mc), me).wait_recv()
            passed[j].start()
        copy(0, sibling, me).wait_recv()
        for j, chip in enumerate(chips):
            copy(4 + j, (*chip, 1 - mc), me).wait_recv()
        for cp in first + passed:
            cp.wait_send()
        mine.wait()

    return pl.pallas_call(
        body, out_shape=jax.ShapeDtypeStruct((N_DEV,) + x.shape, x.dtype),
        in_specs=[pl.BlockSpec(memory_space=pl.ANY)], out_specs=pl.BlockSpec(memory_space=pl.ANY),
        scratch_shapes=[pltpu.SemaphoreType.DMA((7,)), pltpu.SemaphoreType.DMA((7,)), pltpu.SemaphoreType.DMA(())], name=name)(x)


def cols_from_devs(name, xg, l, nrow):
    _, _, C = xg.shape
    tr = _row_tile(nrow, N_DEV * C, xg.dtype.itemsize, 16, 8 << 20)
    nblk = nrow // tr

    def body(x_ref, o_ref):
        for d in range(N_DEV):
            o_ref[:, d * C:(d + 1) * C] = x_ref[d]

    return pl.pallas_call(
        body, grid=(nblk,), in_specs=[pl.BlockSpec((N_DEV, tr, C), lambda i: (0, l * nblk + i, 0))],
        out_specs=pl.BlockSpec((tr, N_DEV * C), lambda i: (i, 0)), out_shape=jax.ShapeDtypeStruct((nrow, N_DEV * C), xg.dtype),
        compiler_params=_cp("parallel"), name=name)(xg)


def cols_to_devs(name, g, l, L, prev):
    nrow, W = g.shape
    C = W // N_DEV
    tr = _row_tile(nrow, W, 4, 8, 8 << 20)
    nblk = nrow // tr

    def body(x_ref, *rest):
        o_ref = rest[-1]
        for d in range(N_DEV):
            o_ref[d] = x_ref[:, d * C:(d + 1) * C]

    in_specs, args, aliases = [pl.BlockSpec((tr, W), lambda i: (i, 0))], [g], {}
    if prev is not None:
        in_specs.append(pl.BlockSpec(memory_space=pl.ANY))
        args.append(prev)
        aliases = {1: 0}
    return pl.pallas_call(
        body, grid=(nblk,), in_specs=in_specs, out_specs=pl.BlockSpec((N_DEV, tr, C), lambda i: (0, l * nblk + i, 0)),
        out_shape=jax.ShapeDtypeStruct((N_DEV, L * nrow, C), F32), input_output_aliases=aliases,
        compiler_params=_cp("parallel"), name=name)(*args)


def proj_in_gather(H, W, jobs, tm, tn):
    T, D = H.shape
    IN = W.shape[1]
    nJ = len(jobs)
    ni, nj = T // tm, IN // tn

    def kern(a_ref, b_ref, *rest):
        x_alls, o_ref, g_refs = rest[:nJ], rest[nJ], rest[nJ + 1:2 * nJ + 1]
        send_sems, recv_sems, local_sems = rest[2 * nJ + 1:]
        i, j = pl.program_id(0), pl.program_id(1)
        mx, my, mc = lax.axis_index("x"), lax.axis_index("y"), lax.axis_index("c")
        me, sibling = (mx, my, mc), (mx, my, 1 - mc)
        chips = [(_flip(mx, fx), _flip(my, fy)) for fx, fy in _REL[1:]]
        plans = []
        for n, (_, row0, nrows) in enumerate(jobs):
            x_ref = x_alls[n].at[pl.ds(row0, nrows)]

            def slot(px, py, pc, g_ref=g_refs[n]):
                return g_ref.at[4 * px + 2 * py + pc]

            def copy(k, block, to, src=None, slot=slot, base=7 * n):
                return pltpu.make_async_remote_copy(
                    src_ref=slot(*block) if src is None else src, dst_ref=slot(*block),
                    send_sem=send_sems.at[base + k], recv_sem=recv_sems.at[base + k], device_id=to, device_id_type=MESH)

            mine = pltpu.make_async_copy(x_ref, slot(*me), local_sems.at[n])
            first = [copy(0, me, sibling, src=x_ref)] + [copy(1 + q, me, (*chip, mc), src=x_ref) for q, chip in enumerate(chips)]
            passed = [copy(4 + q, (*chip, mc), sibling) for q, chip in enumerate(chips)]
            plans.append((copy, mine, first, passed))

        @pl.when((i == 0) & (j == 0))
        def _():
            for copy, mine, first, passed in plans:
                mine.start()
                for cp in first:
                    cp.start()

        o_ref[...] = _dg(a_ref[...], b_ref[...], 1, 0)

        @pl.when((i == ni - 2) & (j == 0))
        def _():
            for copy, mine, first, passed in plans:
                for q, chip in enumerate(chips):
                    copy(1 + q, (*chip, mc), me).wait_recv()
                    passed[q].start()

        @pl.when((i == ni - 1) & (j == nj - 1))
        def _():
            for copy, mine, first, passed in plans:
                copy(0, sibling, me).wait_recv()
                for q, chip in enumerate(chips):
                    copy(4 + q, (*chip, 1 - mc), me).wait_recv()
                for cp in first + passed:
                    cp.wait_send()
                mine.wait()

    assert ni >= 3
    hbm = pl.BlockSpec(memory_space=pl.ANY)
    res = pl.pallas_call(
        kern, grid=(ni, nj),
        in_specs=[pl.BlockSpec((tm, D), lambda i, j: (i, 0)), pl.BlockSpec((D, tn), lambda i, j: (0, j))] + [hbm] * nJ,
        out_specs=[pl.BlockSpec((tm, tn), lambda i, j: (i, j))] + [hbm] * nJ,
        out_shape=[jax.ShapeDtypeStruct((T, IN), F32)] + [jax.ShapeDtypeStruct((N_DEV, nrows, arr.shape[1]), arr.dtype) for arr, _, nrows in jobs],
        scratch_shapes=[pltpu.SemaphoreType.DMA((7 * nJ,)), pltpu.SemaphoreType.DMA((7 * nJ,)), pltpu.SemaphoreType.DMA((nJ,))],
        compiler_params=_cp("arbitrary", "arbitrary"), name="proj_in_gather")(H, W, *[arr for arr, _, _ in jobs])
    return res[0], list(res[1:])


def matmul_nt_exchange(name, a, b, sb, *, tm, tn, tk):
    M, K = a.shape
    N = b.shape[0]
    assert b.shape[1] == K and M % tm == 0 and N % tn == 0 and K % tk == 0
    ni, nj, nk = M // tm, N // tn, K // tk

    def kern(a_ref, b_ref, sb_ref, o_ref, r_ref, acc_ref, send_sems, recv_sems):
        i, j, k = pl.program_id(0), pl.program_id(1), pl.program_id(2)
        mx, my, mc = lax.axis_index("x"), lax.axis_index("y"), lax.axis_index("c")
        cps = [pltpu.make_async_remote_copy(src_ref=sb_ref.at[q], dst_ref=r_ref.at[q], send_sem=send_sems.at[q - 1],
                                            recv_sem=recv_sems.at[q - 1], device_id=(_flip(mx, fx), _flip(my, fy), mc), device_id_type=MESH)
               for q, (fx, fy) in enumerate(_REL) if q > 0]

        @pl.when((i == 0) & (j == 0) & (k == 0))
        def _():
            for cp in cps:
                cp.start()

        part = _dg(a_ref[...], b_ref[...], 1, 1)

        @pl.when(k == 0)
        def _():
            acc_ref[...] = part

        @pl.when(k > 0)
        def _():
            acc_ref[...] += part

        @pl.when(k == nk - 1)
        def _():
            o_ref[...] = acc_ref[...]

        @pl.when((i == ni - 1) & (j == nj - 1) & (k == nk - 1))
        def _():
            for cp in cps:
                cp.wait_recv()
            for cp in cps:
                cp.wait_send()

    return pl.pallas_call(
        kern, grid=(ni, nj, nk),
        in_specs=[pl.BlockSpec((tm, tk), lambda i, j, k: (i, k)), pl.BlockSpec((tn, tk), lambda i, j, k: (j, k)), pl.BlockSpec(memory_space=pl.ANY)],
        out_specs=[pl.BlockSpec((tm, tn), lambda i, j, k: (i, j)), pl.BlockSpec(memory_space=pl.ANY)],
        out_shape=[jax.ShapeDtypeStruct((M, N), F32), jax.ShapeDtypeStruct(sb.shape, sb.dtype)],
        scratch_shapes=[pltpu.VMEM((tm, tn), F32), pltpu.SemaphoreType.DMA((3,)), pltpu.SemaphoreType.DMA((3,))],
        compiler_params=_cp("arbitrary", "arbitrary", "arbitrary"), name=name)(a, b, sb)


def rs_exchange_sibling(tag, buf):
    _, R, C = buf.shape

    def body(buf_ref, out_ref, send_sems, recv_sems):
        mx, my, mc = lax.axis_index("x"), lax.axis_index("y"), lax.axis_index("c")
        sibling = (mx, my, 1 - mc)
        cps = []
        for j, (fx, fy) in enumerate(_REL):
            d = 4 * _flip(mx, fx) + 2 * _flip(my, fy) + (1 - mc)
            cps.append(pltpu.make_async_remote_copy(src_ref=buf_ref.at[d], dst_ref=out_ref.at[j], send_sem=send_sems.at[j],
                                                    recv_sem=recv_sems.at[j], device_id=sibling, device_id_type=MESH))
        for cp in cps:
            cp.start()
        for cp in cps:
            cp.wait_recv()
        for cp in cps:
            cp.wait_send()

    return pl.pallas_call(
        body, out_shape=jax.ShapeDtypeStruct((4, R, C), buf.dtype),
        in_specs=[pl.BlockSpec(memory_space=pl.ANY)], out_specs=pl.BlockSpec(memory_space=pl.ANY),
        scratch_shapes=[pltpu.SemaphoreType.DMA((4,)), pltpu.SemaphoreType.DMA((4,))], name="rs_sibling_" + tag)(buf)


def rs_chip_sum(tag, buf, recv, idx, tr):
    _, R, C = buf.shape

    def kern(idx_ref, b_ref, r_ref, own_ref, sb_ref):
        j = pl.program_id(1)
        s = b_ref[...] + r_ref[...]
        sb_ref[...] = s.astype(BF)

        @pl.when(j == 0)
        def _():
            own_ref[...] = s

    return pl.pallas_call(
        kern, grid_spec=pltpu.PrefetchScalarGridSpec(
            num_scalar_prefetch=1, grid=(R // tr, 4),
            in_specs=[pl.BlockSpec((None, tr, C), lambda r, j, idx: (idx[j], r, 0)), pl.BlockSpec((None, tr, C), lambda r, j, idx: (j, r, 0))],
            out_specs=[pl.BlockSpec((tr, C), lambda r, j, idx: (r, 0)), pl.BlockSpec((None, tr, C), lambda r, j, idx: (j, r, 0))]),
        out_shape=[jax.ShapeDtypeStruct((R, C), F32), jax.ShapeDtypeStruct((4, R, C), BF)],
        compiler_params=_cp("arbitrary", "arbitrary"), name="rs_chip_sum_" + tag)(idx, buf, recv)


def rs_exchange_chips(tag, sb):
    _, R, C = sb.shape

    def body(sb_ref, out_ref, send_sems, recv_sems):
        mx, my, mc = lax.axis_index("x"), lax.axis_index("y"), lax.axis_index("c")
        cps = []
        for j, (fx, fy) in enumerate(_REL):
            if j == 0:
                continue
            cps.append(pltpu.make_async_remote_copy(src_ref=sb_ref.at[j], dst_ref=out_ref.at[j], send_sem=send_sems.at[j - 1],
                                                    recv_sem=recv_sems.at[j - 1], device_id=(_flip(mx, fx), _flip(my, fy), mc),
                                                    device_id_type=MESH))
        for cp in cps:
            cp.start()
        for cp in cps:
            cp.wait_recv()
        for cp in cps:
            cp.wait_send()

    return pl.pallas_call(
        body, out_shape=jax.ShapeDtypeStruct((4, R, C), sb.dtype),
        in_specs=[pl.BlockSpec(memory_space=pl.ANY)], out_specs=pl.BlockSpec(memory_space=pl.ANY),
        scratch_shapes=[pltpu.SemaphoreType.DMA((3,)), pltpu.SemaphoreType.DMA((3,))], name="rs_chips_" + tag)(sb)


def adamw(name, parts, w, m, v, tr, rows=None, prev=None):
    R, C = w.shape
    row0, nrows = rows if rows is not None else (0, R)
    off = row0 // tr
    assert row0 % tr == 0 and nrows % tr == 0
    in_specs, args = [], []
    for arr, lead in parts:
        if lead is None:
            in_specs.append(pl.BlockSpec((tr, C), lambda r: (r, 0)))
        else:
            in_specs.append(pl.BlockSpec((None, tr, C), lambda r, k=lead: (k, r, 0)))
        args.append(arr)
    npart = len(parts)
    blk = pl.BlockSpec((tr, C), lambda r: (r + off, 0))
    extra, aliases = [], {}
    if prev is not None:
        extra = [pl.BlockSpec(memory_space=pl.ANY)] * 4
        aliases = {npart + 3 + k: k for k in range(4)}

    def kern(*refs):
        g = refs[0][...].astype(F32)
        for r in refs[1:npart]:
            g = g + r[...].astype(F32)
        w_ref, m_ref, v_ref = refs[npart:npart + 3]
        g_out, d_out, m_out, v_out = refs[-4:]
        mn = ADAM_B1 * m_ref[...] + (1.0 - ADAM_B1) * g
        vn = ADAM_B2 * v_ref[...] + (1.0 - ADAM_B2) * jnp.square(g)
        m_hat = mn / (1.0 - ADAM_B1 ** ADAM_STEP)
        v_hat = vn / (1.0 - ADAM_B2 ** ADAM_STEP)
        g_out[...] = g
        d_out[...] = -ADAM_LR * (m_hat / (jnp.sqrt(v_hat) + ADAM_EPS) + ADAM_WD * w_ref[...])
        m_out[...] = mn
        v_out[...] = vn

    return pl.pallas_call(
        kern, grid=(nrows // tr,), in_specs=in_specs + [blk, blk, blk] + extra, out_specs=[blk] * 4,
        out_shape=[jax.ShapeDtypeStruct((R, C), F32)] * 4, input_output_aliases=aliases,
        compiler_params=_cp("parallel"), name=name)(*args, w, m, v, *(prev or ()))


def _pack(arrs):
    flat = [a.reshape(-1) for a in arrs]
    n = sum(f.shape[0] for f in flat)
    pad = (-n) % (8 * LANES)
    if pad:
        flat.append(jnp.zeros((pad,), flat[0].dtype))
    return jnp.concatenate(flat).reshape(-1, LANES)


def _unpack(packed, shapes):
    flat = packed.reshape(-1)
    out, off = [], 0
    for s in shapes:
        n = int(np.prod(s))
        out.append(flat[off:off + n].reshape(s))
        off += n
    return out


def _row_tile(R, C=LANES, itemsize=4, mult=16, target=1 << 20):
    best = None
    for t in range(mult, R + 1, mult):
        if R % t == 0 and t * C * itemsize <= target:
            best = t
    return best or R


BIG = ("w_in", "w_branch", "w_out", "conv_pw", "conv_w")
SMALL = ("g_pre", "g_post", "na_rpb", "pool_w", "pool_scale", "q_norm", "k_norm", "conv_b", "conv_ln_g", "conv_ln_b")
WEIGHTS = ['c_ctx', 'w_ada', 'b_ada', 'g_pre', 'g_post', 'w_in', 'na_rpb', 'pool_w', 'pool_scale', 'q_norm', 'k_norm', 'conv_w', 'conv_b',
           'conv_ln_g', 'conv_ln_b', 'conv_pw', 'w_branch', 'w_out']


def _rope_tables(S, Cn):
    t = np.arange(S)
    pos = np.stack([t // GRID_W, t % GRID_W], 1).astype(np.float64)
    lane = np.arange(128)
    within = lane % 64
    axis = within // 32
    f = (within % 32) % 16
    freqs = ROPE_THETA ** (-np.arange(16, dtype=np.float32) / 16)
    ang = pos[:, axis].astype(np.float32) * freqs[f][None, :]
    cos = np.cos(ang).astype(np.float32)
    sin = np.sin(ang).astype(np.float32) * np.where((within % 32) < 16, -1.0, 1.0).astype(np.float32)[None, :]
    cos = np.concatenate([np.ones((Cn, 128), np.float32), cos])
    sin = np.concatenate([np.zeros((Cn, 128), np.float32), sin])
    return jnp.asarray(cos), jnp.asarray(sin)


def _own_2d(w_in, w_branch, w_out, conv_pw, conv_w):
    L = w_in.shape[0]
    cwp = jnp.zeros((L, CONV_PAD, conv_w.shape[2]), conv_w.dtype).at[:, :CONV_WIDTH].set(conv_w)
    return dict(w_in=w_in.reshape(-1, w_in.shape[-1]), w_branch=w_branch.reshape(-1, w_branch.shape[-1]),
                w_out=w_out.reshape(-1, w_out.shape[-1]), conv=_pack([conv_pw, cwp]))


def kernel(x, c, ctx, c_ctx, w_ada, b_ada, g_pre, g_post, w_in, na_rpb, pool_w, pool_scale, q_norm, k_norm, conv_w, conv_b, conv_ln_g, conv_ln_b, conv_pw, w_branch, w_out, loss_target, m_c_ctx, m_w_ada, m_b_ada, m_g_pre, m_g_post, m_w_in, m_na_rpb, m_pool_w, m_pool_scale, m_q_norm, m_k_norm, m_conv_w, m_conv_b, m_conv_ln_g, m_conv_ln_b, m_conv_pw, m_w_branch, m_w_out, v_c_ctx, v_w_ada, v_b_ada, v_g_pre, v_g_post, v_w_in, v_na_rpb, v_pool_w, v_pool_scale, v_q_norm, v_k_norm, v_conv_w, v_conv_b, v_conv_ln_g, v_conv_ln_b, v_conv_pw, v_w_branch, v_w_out):
    W = dict(c_ctx=c_ctx, w_ada=w_ada, b_ada=b_ada, g_pre=g_pre, g_post=g_post, w_in=w_in, na_rpb=na_rpb, pool_w=pool_w, pool_scale=pool_scale,
             q_norm=q_norm, k_norm=k_norm, conv_w=conv_w, conv_b=conv_b, conv_ln_g=conv_ln_g, conv_ln_b=conv_ln_b, conv_pw=conv_pw,
             w_branch=w_branch, w_out=w_out)
    Mo = dict(c_ctx=m_c_ctx, w_ada=m_w_ada, b_ada=m_b_ada, g_pre=m_g_pre, g_post=m_g_post, w_in=m_w_in, na_rpb=m_na_rpb, pool_w=m_pool_w,
              pool_scale=m_pool_scale, q_norm=m_q_norm, k_norm=m_k_norm, conv_w=m_conv_w, conv_b=m_conv_b, conv_ln_g=m_conv_ln_g,
              conv_ln_b=m_conv_ln_b, conv_pw=m_conv_pw, w_branch=m_w_branch, w_out=m_w_out)
    Vo = dict(c_ctx=v_c_ctx, w_ada=v_w_ada, b_ada=v_b_ada, g_pre=v_g_pre, g_post=v_g_post, w_in=v_w_in, na_rpb=v_na_rpb, pool_w=v_pool_w,
              pool_scale=v_pool_scale, q_norm=v_q_norm, k_norm=v_k_norm, conv_w=v_conv_w, conv_b=v_conv_b, conv_ln_g=v_conv_ln_g,
              conv_ln_b=v_conv_ln_b, conv_pw=v_conv_pw, w_branch=v_w_branch, w_out=v_w_out)

    S, D = x.shape[1], x.shape[2]
    Cn = ctx.shape[1]
    T = Cn + S
    L = w_in.shape[0]
    IN = w_in.shape[2] * N_DEV
    mx, my, mc = lax.axis_index("x"), lax.axis_index("y"), lax.axis_index("c")
    me = 4 * mx + 2 * my + mc
    tm = _pick(Cn, (256, 128))
    tme = _pick(T, (768, 256, 128))
    tmb = _pick(T, (1408, 768, 256, 128))
    tmm = _pick(T, (768, 256, 128))
    tq = _pick(Cn, (256, 128))
    tk = _pick(T, (4224, 768, 384, 128))
    tkf = _pick(T, (8448, 4224, 768, 384, 128))

    cg = allgather("gather_c", c)
    cvec = jnp.zeros((16, D), F32).at[0].set(c_ctx).at[1:1 + N_DEV].set(cg[:, 0])
    wc = w_ada.shape[2]
    b_sh = lax.dynamic_slice_in_dim(b_ada, me * wc, wc, axis=1)[:, None, :]
    modp = ada_fwd(cvec, w_ada, b_sh)
    modg = allgather("gather_mod", modp.reshape(L * 16, wc)).reshape(N_DEV, L, 16, wc)
    mod_full = jnp.transpose(modg, (1, 2, 0, 3)).reshape(L, 16, N_DEV * wc)
    mod2 = jnp.stack([mod_full[:, 0], lax.dynamic_index_in_dim(mod_full, 1 + me, axis=1, keepdims=False)], axis=1)
    shift, scale, gate = [mod2[:, :, None, k * D:(k + 1) * D] for k in range(3)]

    own2 = _own_2d(w_in, w_branch, w_out, conv_pw, conv_w)
    w_in_bf = own2["w_in"].astype(BF)
    win_g = allgather("gather_w_in0", w_in_bf[:D])
    wbr_bf = own2["w_branch"].astype(BF)
    wout_bf = own2["w_out"].astype(BF)
    rb, ro = 4 * BRANCH_W, D // N_DEV
    split = 1 if L >= 3 else L
    parts = []
    if L == 1:
        parts.append((0, 1, allgather("gather_w_branch", wbr_bf), allgather("gather_w_out", wout_bf)))
    conv_g = allgather("gather_conv", own2["conv"].astype(BF)).reshape(N_DEV, -1)
    npw = L * (BRANCH_W // N_DEV) * BRANCH_W
    cpw_g = conv_g[:, :npw].reshape(N_DEV, L, BRANCH_W // N_DEV, BRANCH_W)
    cw_g = conv_g[:, npw:npw + L * CONV_PAD * (BRANCH_W // N_DEV)].reshape(N_DEV, L, CONV_PAD, BRANCH_W // N_DEV)
    Win = []
    Wb, Wo = [None] * L, [None] * L
    Cpw = [cpw_g[:, l].reshape(BRANCH_W, BRANCH_W).astype(F32) for l in range(L)]
    Cw = [jnp.transpose(cw_g[:, l], (1, 0, 2)).reshape(CONV_PAD, BRANCH_W).astype(F32) for l in range(L)]

    cos, sin = _rope_tables(S, Cn)
    tb = na_tables(S, Cn)
    bd = jnp.asarray(np.kron(np.eye(2, dtype=np.float32), np.full((64, 64), 1.0 / 64, np.float32)))
    row2 = lambda a: a.reshape(1, -1)

    X = jnp.concatenate([ctx[0], x[0]], axis=0)
    saved = []
    for l in range(L):
        gp = row2(g_pre[l])
        H = modulate_fwd(X, gp, shift[l], scale[l], Cn, tm)
        Win.append(cols_from_devs("w_in_cols", win_g, 0, D))
        tn_in = _pick(IN, (1280, 1152, 768, 384, 128))
        if l + 1 < L:
            jobs = [(w_in_bf, (l + 1) * D, D)]
            span = (0, split) if l == 0 else ((split, L - split) if (l == 1 and split < L) else None)
            if span is not None:
                jobs += [(wbr_bf, span[0] * rb, span[1] * rb), (wout_bf, span[0] * ro, span[1] * ro)]
            P, got = proj_in_gather(H, Win[l], jobs, tmm, tn_in)
            win_g = got[0]
            if span is not None:
                parts.append((span[0], span[1], got[1], got[2]))
        else:
            P = matmul("proj_in", H, Win[l], out_dtype=F32, tm=tmm, tn=tn_in, tk=D)
        bt = bias_table(na_rpb[l], tb)
        oa = na_fwd(P, bt, tb, Cn)
        ob = pool_fwd(P, pool_w[l], row2(pool_scale[l]), Cn)
        qn = row2(jnp.tile(q_norm[l], 2))
        kn = row2(jnp.tile(k_norm[l], 2))
        Qp = prep_q_fwd(P, cos, sin, qn, bd, Cn, tm)
        KV = prep_kv_fwd(P, cos, sin, kn, bd, Cn, tm)
        oc, LSE = flash_fwd(Qp, KV, Cn, tq, tkf)
        Yc = conv_fwd(P, Cw[l], row2(conv_b[l]), Cn)
        od = conv_post_fwd(Yc, row2(conv_ln_g[l]), row2(conv_ln_b[l]), Cpw[l], Cn, tm)
        G = gate_outs_fwd(P, oa, ob, oc, od, Cn, tme)
        f0, nl, gb, go = next(p for p in parts if p[0] <= l < p[0] + p[1])
        Wb[l] = cols_from_devs("w_branch_cols", gb, l - f0, rb).reshape(4, BRANCH_W, D)
        Wo[l] = rows_from_devs("w_out_rows", go, l - f0, nl)
        Y = branch_merge_fwd(P, G, Wb[l], tmb)
        Z = matmul("proj_out", Y, Wo[l], out_dtype=F32, tm=tmm, tn=D, tk=D)
        Xn = post_fwd(X, Z, row2(g_post[l]), gate[l], Cn, tm)
        saved.append(dict(X=X, H=H, P=P, bt=bt, oa=oa, ob=ob, oc=oc, od=od, Qp=Qp, KV=KV, LSE=LSE, Yc=Yc, G=G, Y=Y, Z=Z, qn=qn, kn=kn))
        X = Xn

    tgt = jnp.concatenate([jnp.zeros((Cn, D), F32), loss_target[0]], axis=0)
    dX, loss_acc = loss_call(X, tgt, Cn, tm)
    loss = lax.psum(loss_acc[0, 0], ("x", "y", "c"))

    gsm = {n: [None] * L for n in SMALL}
    gbig = {n: [None] * L for n in BIG}
    g_br = g_out = None
    idx = jnp.stack([4 * _flip(mx, fx) + 2 * _flip(my, fy) + mc for fx, fy in _REL]).astype(jnp.int32)
    mom2 = _own_2d(Mo["w_in"], Mo["w_branch"], Mo["w_out"], Mo["conv_pw"], Mo["conv_w"])
    var2 = _own_2d(Vo["w_in"], Vo["w_branch"], Vo["w_out"], Vo["conv_pw"], Vo["conv_w"])
    tr_in = _row_tile(D, own2["w_in"].shape[1])
    pending = None
    res_in = None

    def finish_w_in(layer, own, recv2, prev):
        return adamw("adamw_w_in", [(own, None), (recv2, 1), (recv2, 2), (recv2, 3)], own2["w_in"], mom2["w_in"], var2["w_in"], tr_in,
                     rows=(layer * D, D), prev=prev)

    dmod = [None] * L
    for l in reversed(range(L)):
        sv = saved[l]
        P = sv["P"]
        dZ, dgp, dgate = post_bwd(sv["Z"], dX, row2(g_post[l]), gate[l], Cn, tm)
        gsm["g_post"][l] = dgp[0]
        dY = matmul("proj_out_dy", dZ, Wo[l], cb=1, out_dtype=F32, tm=tmm, tn=D, tk=D)
        dWo = matmul("proj_out_dw", sv["Y"], dZ, ca=0, cb=0, tm=_pick(D, (1024, 512, 256)), tn=D, tk=tmm)
        g_out = rows_to_devs("w_out_devs", dWo, l, L, g_out)
        dU, dG, dP = branch_merge_bwd(P, sv["G"], Wb[l], dY, tmb)
        g_br = cols_to_devs("w_branch_devs", branch_dw(sv["G"], dU, tmm).reshape(4 * BRANCH_W, D), l, L, g_br)
        dO, dP = gate_outs_bwd(P, sv["oa"], sv["ob"], sv["oc"], sv["od"], dG, dP, Cn, tme)
        dYc, dlg, dlb, dcpw = conv_post_bwd(sv["Yc"], dO, row2(conv_ln_g[l]), row2(conv_ln_b[l]), Cpw[l], Cn, tm)
        gsm["conv_ln_g"][l], gsm["conv_ln_b"][l], gbig["conv_pw"][l] = dlg[0], dlb[0], dcpw
        dP, dGg, dcw, dcb = conv_bwd(P, Cw[l], dYc, dP, Cn)
        dP = copy_cols("copy_glu_gate", dGg, dP, CB["d_glu"] + 4, Cn, tme)
        gbig["conv_w"][l], gsm["conv_b"][l] = dcw, dcb[0]
        dQp, dKV = flash_bwd(sv["Qp"], sv["KV"], sv["oc"], sv["LSE"], dO, Cn, tq, tk)
        dP, dqn = prep_q_bwd(P, dQp, dP, cos, sin, sv["qn"], bd, Cn, tm)
        dP, dkn = prep_kv_bwd(P, dKV, dP, cos, sin, sv["kn"], bd, Cn, tm)
        gsm["q_norm"][l] = dqn[0, :64] + dqn[0, 64:]
        gsm["k_norm"][l] = dkn[0, :64] + dkn[0, 64:]
        dP, dpw, dps = pool_bwd(P, pool_w[l], row2(pool_scale[l]), dO, dP, Cn)
        gsm["pool_w"][l], gsm["pool_scale"][l] = dpw, dps[0]
        dQa, dKa, dVa, dbt = na_bwd(P, sv["bt"], dO, tb, Cn)
        dP = assemble_na(dKa, dVa, dQa, dP, Cn, tme)
        gsm["na_rpb"][l] = bias_table_grad(dbt, tb)
        tk_in = _pick(IN, (1280, 1152, 768, 384, 128))
        if pending is None:
            dH = matmul("proj_in_dh", dP, Win[l], cb=1, out_dtype=F32, tm=tmm, tn=D, tk=tk_in)
        else:
            dH, recv2 = matmul_nt_exchange("proj_in_dh_exchange", dP, Win[l], pending[2], tm=tmm, tn=D, tk=tk_in)
            res_in = finish_w_in(pending[0], pending[1], recv2, res_in)
        dWin = matmul("proj_in_dw", sv["H"], dP, ca=0, cb=0, tm=_pick(D, (1024, 512, 256)), tn=tk_in, tk=_pick(T, (1408, 768, 384, 128)))
        g_l = cols_to_devs("w_in_devs", dWin, 0, 1, None)
        own_l, sb_l = rs_chip_sum("w_in", g_l, rs_exchange_sibling("w_in", g_l), idx, tr_in)
        pending = (l, own_l, sb_l)
        dX, dgpre, dsh, dsc = modulate_bwd(sv["X"], dH, dX, row2(g_pre[l]), shift[l], scale[l], Cn, tm)
        gsm["g_pre"][l] = dgpre[0]
        dmod[l] = jnp.concatenate([dsh[:, 0], dsc[:, 0], dgate[:, 0]], axis=1)
    grad_x = dX[Cn:][None]

    res_in = finish_w_in(pending[0], pending[1], rs_exchange_chips("w_in", pending[2]), res_in)
    g_pw = jnp.transpose(jnp.stack(gbig["conv_pw"]).reshape(L, N_DEV, -1), (1, 0, 2)).reshape(N_DEV, -1)
    g_cw = jnp.transpose(jnp.stack(gbig["conv_w"]).reshape(L, CONV_PAD, N_DEV, BRANCH_W // N_DEV), (2, 0, 1, 3)).reshape(N_DEV, -1)
    g_conv = jnp.concatenate([g_pw, g_cw], axis=1)
    g_conv = jnp.pad(g_conv, ((0, 0), (0, own2["conv"].size - g_conv.shape[1]))).reshape(N_DEV, -1, LANES)
    gbufs = dict(w_branch=g_br, w_out=g_out, conv=g_conv)
    big2 = dict(w_in=res_in)
    for tag in ("w_branch", "w_out", "conv"):
        R2, C2 = own2[tag].shape
        tr2 = _row_tile(R2, C2)
        recv1 = rs_exchange_sibling(tag, gbufs[tag])
        own, sb = rs_chip_sum(tag, gbufs[tag], recv1, idx, tr2)
        recv2 = rs_exchange_chips(tag, sb)
        big2[tag] = adamw("adamw_" + tag, [(own, None), (recv2, 1), (recv2, 2), (recv2, 3)], own2[tag], mom2[tag], var2[tag], tr2)
    cshapes = [(L, BRANCH_W // N_DEV, BRANCH_W), (L, CONV_PAD, BRANCH_W // N_DEV)]
    conv_unp = [_unpack(r, cshapes) for r in big2["conv"]]
    big_out = dict(w_in=[r.reshape(w_in.shape) for r in big2["w_in"]], w_branch=[r.reshape(w_branch.shape) for r in big2["w_branch"]],
                   w_out=[r.reshape(w_out.shape) for r in big2["w_out"]], conv_pw=[u[0] for u in conv_unp],
                   conv_w=[u[1][:, :CONV_WIDTH] for u in conv_unp])

    small_own = [jnp.stack(gsm[n]) for n in SMALL]
    small_shapes = [a.shape for a in small_own]
    dmod_all = jnp.stack(dmod)
    spack = _pack(small_own + [dmod_all])
    sg = allgather("gather_small", spack)
    Rs = spack.shape[0]
    nsmall = sum(int(np.prod(s)) for s in small_shapes)
    dmod_g = sg.reshape(N_DEV, -1)[:, nsmall:nsmall + dmod_all.size].reshape(N_DEV, L, 2, N_DEV, wc)
    dm_ctx = dmod_g[0, :, 0, :, :]
    for k in range(1, N_DEV):
        dm_ctx = dm_ctx + dmod_g[k, :, 0, :, :]
    dm_ctx = lax.dynamic_index_in_dim(dm_ctx, me, axis=1, keepdims=False)
    dm_b = jnp.transpose(lax.dynamic_index_in_dim(dmod_g[:, :, 1], me, axis=2, keepdims=False), (1, 0, 2))
    dmp = jnp.zeros((L, 16, wc), F32).at[:, 0].set(dm_ctx).at[:, 1:1 + N_DEV].set(dm_b)
    g_wada, dsilu = ada_bwd(cvec, w_ada, dmp)
    cpart = allgather("gather_cctx", dsilu[0:1])
    g_cctx = silu_grad(c_ctx[None], cpart)[0]

    smallw = _pack([W[n] for n in SMALL])
    smallm = _pack([Mo[n] for n in SMALL])
    smallv = _pack([Vo[n] for n in SMALL])
    Rsm = smallw.shape[0]
    small_res = adamw("adamw_small", [(sg[:, :Rsm], k) for k in range(N_DEV)], smallw, smallm, smallv, _row_tile(Rsm))
    small_unp = [_unpack(r, small_shapes) for r in small_res]

    db_parts = dmod_g.reshape(N_DEV, L, 2, N_DEV * wc)
    bshape = (L * N_DEV * wc // LANES, LANES)
    bparts = [(db_parts[k, :, r].reshape(bshape), None) for k in range(N_DEV) for r in range(2)]
    bada_res = adamw("adamw_bada", bparts, b_ada.reshape(bshape), Mo["b_ada"].reshape(bshape), Vo["b_ada"].reshape(bshape), _row_tile(bshape[0]))
    wada_shape = (w_ada.size // LANES, LANES)
    wada_res = adamw("adamw_wada", [(g_wada.reshape(wada_shape), None)], w_ada.reshape(wada_shape), Mo["w_ada"].reshape(wada_shape),
                     Vo["w_ada"].reshape(wada_shape), _row_tile(wada_shape[0]))
    cshape = (D // LANES, LANES)
    cctx_res = adamw("adamw_cctx", [(g_cctx.reshape(cshape), None)], c_ctx.reshape(cshape), Mo["c_ctx"].reshape(cshape), Vo["c_ctx"].reshape(cshape),
                     _row_tile(cshape[0]) if cshape[0] % 8 == 0 else cshape[0])

    res = {}
    for n in BIG:
        res[n] = big_out[n]
    for k, n in enumerate(SMALL):
        res[n] = [u[k] for u in small_unp]
    res["b_ada"] = [r.reshape(b_ada.shape) for r in bada_res]
    res["w_ada"] = [r.reshape(w_ada.shape) for r in wada_res]
    res["c_ctx"] = [r.reshape(c_ctx.shape) for r in cctx_res]
    outs = [loss, grad_x]
    for k in range(4):
        outs += [res[n][k] for n in WEIGHTS]
    return tuple(outs)
```

```python
import functools

import numpy as np
import jax
import jax.numpy as jnp
from jax import lax
from jax.experimental import pallas as pl
from jax.experimental.pallas import tpu as pltpu

F32 = jnp.float32
BF = jnp.bfloat16
HI = lax.Precision.HIGHEST

GRID_W = 64
BRANCH_W = 512
HEAD_DIM = 64
NA_WIN_ROWS = 8
NA_WIN_COLS = 16
NA_TILE_ROWS = 2
NA_BAND = NA_WIN_ROWS + NA_TILE_ROWS - 1
CONV_WIDTH = 31
CONV_PAD = 32
EPS = 1e-6
ROPE_THETA = 10000.0
NEG = -1e30
N_DEV = 8
LANES = 128
VMEM_LIMIT_BYTES = 56 * 1024 * 1024

ADAM_LR, ADAM_B1, ADAM_B2, ADAM_EPS, ADAM_WD, ADAM_STEP = 0.001, 0.9, 0.999, 1e-08, 0.01, 10

CB = dict(a_k=0, a_v=4, c_k=8, c_v=9, a_q=10, c_q=14, a_gate=18, b_in=22, b_gate=26, c_gate=30, d_glu=34, d_gate=42, merge=46)
KV_COLS = 1280
MESH = pl.DeviceIdType.MESH


def _pick(n, cands):
    for c in cands:
        if n % c == 0:
            return c
    raise ValueError(f"no tile for {n} among {cands}")


def _cp(*sem):
    return pltpu.CompilerParams(dimension_semantics=sem if sem else None, vmem_limit_bytes=VMEM_LIMIT_BYTES)


def _dg(x, y, cx, cy):
    return lax.dot_general(x.astype(BF), y.astype(BF), (((cx,), (cy,)), ((), ())), preferred_element_type=F32)


@functools.partial(jax.custom_vjp, nondiff_argnums=(2, 3))
def bdot(a, b, ca=1, cb=0):
    return _dg(a, b, ca, cb)


def _bdot_fwd(a, b, ca, cb):
    return _dg(a, b, ca, cb), (a, b)


def _bdot_bwd(ca, cb, res, g):
    a, b = res
    jb = 1 if cb == 0 else 0
    ia = 0 if ca == 1 else 1
    da = _dg(g, b, 1, jb) if ca == 1 else _dg(b, g, jb, 1)
    db = _dg(a, g, ia, 0) if cb == 0 else _dg(g, a, 0, ia)
    return da.astype(a.dtype), db.astype(b.dtype)


bdot.defvjp(_bdot_fwd, _bdot_bwd)


def hdot(a, b):
    return jnp.dot(a, b, precision=HI, preferred_element_type=F32)


def hdot3(a, b):
    return jnp.dot(a, b, precision=lax.Precision.HIGH, preferred_element_type=F32)


@jax.custom_vjp
def swap64(x):
    return pltpu.roll(x, 64, 1)


swap64.defvjp(lambda x: (swap64(x), None), lambda _, g: (swap64(g),))


@jax.custom_vjp
def partner(x):
    w = x.shape[-1]
    lane = lax.broadcasted_iota(jnp.int32, x.shape, 1)
    return jnp.where((lane % 32) < 16, pltpu.roll(x, w - 16, 1), pltpu.roll(x, 16, 1))


partner.defvjp(lambda x: (partner(x), None), lambda _, g: (partner(g),))


@functools.partial(jax.custom_vjp, nondiff_argnums=(1,))
def split_rows(x, n):
    k = x.shape[0] // n
    return tuple(x[i * k:(i + 1) * k] for i in range(n))


split_rows.defvjp(lambda x, n: (split_rows(x, n), None), lambda n, _, gs: (jnp.concatenate(gs, axis=0),))


def _lane(shape):
    return lax.broadcasted_iota(jnp.int32, shape, len(shape) - 1)


def _silu(x):
    return x * jax.nn.sigmoid(x)


def _normrope(x, w, cos, sin, bd):
    ms = hdot3(x * x, bd)
    y = x * lax.rsqrt(ms + EPS) * w
    return y * cos + partner(y) * sin


def matmul(name, a, b, *, ca=1, cb=0, out_dtype=F32, tm, tn, tk):
    M = a.shape[1 - ca]
    K = a.shape[ca]
    N = b.shape[1 - cb]
    assert b.shape[cb] == K and M % tm == 0 and N % tn == 0 and K % tk == 0, (name, a.shape, b.shape)
    nk = K // tk
    a_spec = pl.BlockSpec((tm, tk), lambda i, j, k: (i, k)) if ca == 1 else pl.BlockSpec((tk, tm), lambda i, j, k: (k, i))
    b_spec = pl.BlockSpec((tk, tn), lambda i, j, k: (k, j)) if cb == 0 else pl.BlockSpec((tn, tk), lambda i, j, k: (j, k))

    def kern(a_ref, b_ref, o_ref, *scr):
        part = _dg(a_ref[...], b_ref[...], ca, cb)
        if nk == 1:
            o_ref[...] = part.astype(out_dtype)
            return
        acc_ref, = scr
        k = pl.program_id(2)

        @pl.when(k == 0)
        def _():
            acc_ref[...] = part

        @pl.when(k > 0)
        def _():
            acc_ref[...] += part

        @pl.when(k == nk - 1)
        def _():
            o_ref[...] = acc_ref[...].astype(out_dtype)

    return pl.pallas_call(
        kern, grid=(M // tm, N // tn, nk), in_specs=[a_spec, b_spec],
        out_specs=pl.BlockSpec((tm, tn), lambda i, j, k: (i, j)),
        out_shape=jax.ShapeDtypeStruct((M, N), out_dtype),
        scratch_shapes=[] if nk == 1 else [pltpu.VMEM((tm, tn), F32)],
        compiler_params=_cp("parallel", "parallel", "arbitrary"), name=name,
    )(a, b)


def rowcall(name, body, T, tm, ncol, nctx, rows, consts=(), segs=(), outs=(), accs=(), saccs=()):
    nctxb = nctx // tm
    assert T % tm == 0 and (nctx % tm == 0 or not (segs or saccs))
    nr, nc, ns = len(rows), len(consts), len(segs)

    def seg_map(i, j):
        return (jnp.where(i < nctxb, 0, 1), 0, 0)

    def col_map(cf):
        return lambda i, j: (i, cf(j))

    in_specs, args = [], []
    for arr, bw, cf in rows:
        in_specs.append(pl.BlockSpec((tm, bw), col_map(cf)))
        args.append(arr)
    for arr in consts:
        in_specs.append(pl.BlockSpec(arr.shape, lambda i, j, nd=arr.ndim: (0,) * nd))
        args.append(arr)
    for arr in segs:
        in_specs.append(pl.BlockSpec((None, 1, arr.shape[-1]), seg_map))
        args.append(arr)
    out_shape, out_specs, aliases, out_dtypes = [], [], {}, []
    for o in outs:
        if o[0] == "into":
            _, arr, bw, cf = o
            aliases[len(args)] = len(out_shape)
            in_specs.append(pl.BlockSpec(memory_space=pl.ANY))
            args.append(arr)
            out_shape.append(jax.ShapeDtypeStruct(arr.shape, arr.dtype))
            out_dtypes.append(arr.dtype)
        else:
            cols, dt, bw, cf = o
            out_shape.append(jax.ShapeDtypeStruct((T, cols), dt))
            out_dtypes.append(dt)
        out_specs.append(pl.BlockSpec((tm, bw), col_map(cf)))
    for shp in accs:
        out_shape.append(jax.ShapeDtypeStruct(shp, F32))
        out_specs.append(pl.BlockSpec(shp, lambda i, j, nd=len(shp): (0,) * nd))
    for w in saccs:
        out_shape.append(jax.ShapeDtypeStruct((2, 1, w), F32))
        out_specs.append(pl.BlockSpec((None, 1, w), seg_map))
    n_in = len(args)
    no, na, nsa = len(outs), len(accs), len(saccs)

    def kern(*refs):
        i, j = pl.program_id(0), pl.program_id(1)
        ins = refs[:nr + nc + ns]
        orefs = refs[n_in:n_in + no]
        arefs = refs[n_in + no:n_in + no + na]
        srefs = refs[n_in + no + na:n_in + no + na + nsa]
        ov, av, sv = body(i, j, [r[...] for r in ins[:nr]], [r[...] for r in ins[nr:nr + nc]], [r[...] for r in ins[nr + nc:]])
        for r, v, dt in zip(orefs, ov, out_dtypes):
            r[...] = v.astype(dt)
        if na:
            @pl.when((i == 0) & (j == 0))
            def _():
                for r in arefs:
                    r[...] = jnp.zeros_like(r)
            for r, v in zip(arefs, av):
                r[...] += v
        if nsa:
            @pl.when(((i == 0) | (i == nctxb)) & (j == 0))
            def _():
                for r in srefs:
                    r[...] = jnp.zeros_like(r)
            for r, v in zip(srefs, sv):
                r[...] += v

    res = pl.pallas_call(
        kern, grid=(T // tm, ncol), in_specs=in_specs, out_specs=out_specs, out_shape=out_shape,
        input_output_aliases=aliases, compiler_params=_cp("arbitrary", "arbitrary"), name=name,
    )(*args)
    return res


def _c(k):
    return lambda j: k


def f_modulate(x, g, sh, sc):
    y = x * lax.rsqrt(jnp.mean(x * x, axis=-1, keepdims=True) + EPS) * g
    return y * (1.0 + sc) + sh


def modulate_fwd(X, g_pre, shift, scale, Cn, tm):
    T, D = X.shape

    def body(i, j, r, c, s):
        return [f_modulate(r[0], c[0], s[0], s[1])], [], []

    return rowcall("modulate_fwd", body, T, tm, 1, Cn, [(X, D, _c(0))], [g_pre], [shift, scale], [(D, BF, D, _c(0))])[0]


def modulate_bwd(X, dH, dXn, g_pre, shift, scale, Cn, tm):
    T, D = X.shape

    def body(i, j, r, c, s):
        _, vjp = jax.vjp(f_modulate, r[0], c[0], s[0], s[1])
        dx, dg, dsh, dsc = vjp(r[1])
        return [dx + r[2]], [dg], [dsh, dsc]

    return rowcall("modulate_bwd", body, T, tm, 1, Cn, [(X, D, _c(0)), (dH, D, _c(0)), (dXn, D, _c(0))], [g_pre], [shift, scale],
                   [(D, F32, D, _c(0))], [(1, D)], [D, D])


def f_post(z, gp, gate):
    return gate * (z * lax.rsqrt(jnp.mean(z * z, axis=-1, keepdims=True) + EPS) * gp)


def post_fwd(X, Z, g_post, gate, Cn, tm):
    T, D = X.shape

    def body(i, j, r, c, s):
        return [r[0] + f_post(r[1], c[0], s[0])], [], []

    return rowcall("post_fwd", body, T, tm, 1, Cn, [(X, D, _c(0)), (Z, D, _c(0))], [g_post], [gate], [(D, F32, D, _c(0))])[0]


def post_bwd(Z, dXn, g_post, gate, Cn, tm):
    T, D = Z.shape

    def body(i, j, r, c, s):
        _, vjp = jax.vjp(f_post, r[0], c[0], s[0])
        dz, dgp, dgate = vjp(r[1])
        return [dz], [dgp], [dgate]

    return rowcall("post_bwd", body, T, tm, 1, Cn, [(Z, D, _c(0)), (dXn, D, _c(0))], [g_post], [gate],
                   [(D, BF, D, _c(0))], [(1, D)], [D])


def loss_call(XL, tgt, Cn, tm):
    T, D = XL.shape
    nctxb = Cn // tm

    def body(i, j, r, c, s):
        diff = jnp.where(i >= nctxb, r[0] - r[1], 0.0)
        per_row = jnp.mean(diff * diff, axis=-1, keepdims=True)
        tot = 0.5 * jnp.sum(per_row, axis=0, keepdims=True)
        return [diff / D], [jnp.broadcast_to(tot, (1, LANES))], []

    return rowcall("loss", body, T, tm, 1, Cn, [(XL, D, _c(0)), (tgt, D, _c(0))], [], [], [(D, F32, D, _c(0))], [(1, LANES)])


def f_prep_q(x, qn, cos, sin, bd, jj):
    y = _normrope(x, qn, cos, sin, bd) * 0.125
    lane = _lane(y.shape)
    a = jnp.where(lane < 64, y, 0.0)
    b = jnp.where(lane >= 64, y, 0.0)
    g0 = jj < 2
    return jnp.concatenate([jnp.where(g0, a, swap64(a)), jnp.where(g0, swap64(b), b)], axis=1)


def prep_q_fwd(P, cos, sin, qn, bd, Cn, tm):
    T = P.shape[0]

    def body(i, j, r, c, s):
        return [f_prep_q(r[0], c[0], r[1], r[2], c[1], j)], [], []

    return rowcall("prep_q_fwd", body, T, tm, 4, Cn, [(P, 128, lambda j: CB["c_q"] + j), (cos, 128, _c(0)), (sin, 128, _c(0))], [qn, bd], [],
                   [(1024, BF, 256, lambda j: j)])[0]


def prep_q_bwd(P, dQ, dP, cos, sin, qn, bd, Cn, tm):
    T = P.shape[0]

    def body(i, j, r, c, s):
        _, vjp = jax.vjp(lambda x, w: f_prep_q(x, w, r[1], r[2], c[1], j), r[0], c[0])
        dx, dw = vjp(r[3])
        return [dx], [dw], []

    return rowcall("prep_q_bwd", body, T, tm, 4, Cn,
                   [(P, 128, lambda j: CB["c_q"] + j), (cos, 128, _c(0)), (sin, 128, _c(0)), (dQ, 256, lambda j: j)], [qn, bd], [],
                   [("into", dP, 128, lambda j: CB["c_q"] + j)], [(1, 128)])


def f_prep_kv(x, kn, cos, sin, bd, jj):
    return jnp.where(jj == 0, _normrope(x, kn, cos, sin, bd), x)


def prep_kv_fwd(P, cos, sin, kn, bd, Cn, tm):
    T = P.shape[0]

    def body(i, j, r, c, s):
        return [f_prep_kv(r[0], c[0], r[1], r[2], c[1], j)], [], []

    return rowcall("prep_kv_fwd", body, T, tm, 2, Cn, [(P, 128, lambda j: CB["c_k"] + j), (cos, 128, _c(0)), (sin, 128, _c(0))], [kn, bd], [],
                   [(256, BF, 128, lambda j: j)])[0]


def prep_kv_bwd(P, dKV, dP, cos, sin, kn, bd, Cn, tm):
    T = P.shape[0]

    def body(i, j, r, c, s):
        _, vjp = jax.vjp(lambda x, w: f_prep_kv(x, w, r[1], r[2], c[1], j), r[0], c[0])
        dx, dw = vjp(r[3])
        return [dx], [dw], []

    return rowcall("prep_kv_bwd", body, T, tm, 2, Cn,
                   [(P, 128, lambda j: CB["c_k"] + j), (cos, 128, _c(0)), (sin, 128, _c(0)), (dKV, 128, lambda j: j)], [kn, bd], [],
                   [("into", dP, 128, lambda j: CB["c_k"] + j)], [(1, 128)])


def f_conv_post(y, lg, lb, w):
    mu = jnp.mean(y, axis=-1, keepdims=True)
    var = jnp.mean(jnp.square(y - mu), axis=-1, keepdims=True)
    h = (y - mu) * lax.rsqrt(var + EPS) * lg + lb
    return bdot(_silu(h), w)


def conv_post_fwd(Yc, lg, lb, w, Cn, tm):
    T = Yc.shape[0]

    def body(i, j, r, c, s):
        return [f_conv_post(r[0], c[0], c[1], c[2])], [], []

    return rowcall("conv_post_fwd", body, T, tm, 1, Cn, [(Yc, 512, _c(0))], [lg, lb, w], [], [(512, F32, 512, _c(0))])[0]


def conv_post_bwd(Yc, dO, lg, lb, w, Cn, tm):
    T = Yc.shape[0]

    def body(i, j, r, c, s):
        _, vjp = jax.vjp(f_conv_post, r[0], c[0], c[1], c[2])
        dy, dlg, dlb, dw = vjp(r[1])
        return [dy], [dlg, dlb, dw], []

    return rowcall("conv_post_bwd", body, T, tm, 1, Cn, [(Yc, 512, _c(0)), (dO, 512, _c(3))], [lg, lb, w], [],
                   [(512, F32, 512, _c(0))], [(1, 512), (1, 512), (512, 512)])


def _gate_blk(j):
    b = j // 2
    base = jnp.where(b == 0, CB["a_gate"] // 2, jnp.where(b == 1, CB["b_gate"] // 2, jnp.where(b == 2, CB["c_gate"] // 2, CB["d_gate"] // 2)))
    return base + j % 2


def _sel4(j, vals):
    b = j // 2
    return jnp.where(b == 0, vals[0], jnp.where(b == 1, vals[1], jnp.where(b == 2, vals[2], vals[3])))


def gate_outs_fwd(P, oa, ob, oc, od, Cn, tm):
    T = P.shape[0]

    def body(i, j, r, c, s):
        return [_sel4(j, r[:4]) * _silu(r[4])], [], []

    half = lambda j: j % 2
    return rowcall("gate_outs_fwd", body, T, tm, 8, Cn,
                   [(oa, 256, half), (ob, 256, half), (oc, 256, half), (od, 256, half), (P, 256, _gate_blk)], [], [],
                   [(2048, BF, 256, lambda j: j)])[0]


def gate_outs_bwd(P, oa, ob, oc, od, dG, dP, Cn, tm):
    T = P.shape[0]

    def body(i, j, r, c, s):
        o = _sel4(j, r[:4])
        _, vjp = jax.vjp(lambda oo, gg: oo * _silu(gg), o, r[4])
        do, dg = vjp(r[5])
        return [do, dg], [], []

    half = lambda j: j % 2
    return rowcall("gate_outs_bwd", body, T, tm, 8, Cn,
                   [(oa, 256, half), (ob, 256, half), (oc, 256, half), (od, 256, half), (P, 256, _gate_blk), (dG, 256, lambda j: j)], [], [],
                   [(2048, F32, 256, lambda j: j), ("into", dP, 256, _gate_blk)])


def merge_fwd(P, U, Cn, tm):
    T = P.shape[0]
    D4 = U.shape[1]
    nb = D4 // 4 // 256
    mb = CB["merge"] // 2

    def body(i, j, r, c, s):
        y = 0.0
        for b in range(4):
            y = y + jax.nn.sigmoid(r[b]) * r[4 + b]
        return [y], [], []

    rows = [(P, 256, (lambda j, b=b: mb + b * nb + j)) for b in range(4)] + [(U, 256, (lambda j, b=b: b * nb + j)) for b in range(4)]
    return rowcall("merge_fwd", body, T, tm, nb, Cn, rows, [], [], [(D4 // 4, BF, 256, lambda j: j)])[0]


def merge_bwd(P, U, dY, Cn, tm):
    T, IN = P.shape
    D4 = U.shape[1]
    nb = D4 // 4 // 256
    mb = CB["merge"] // 2

    def body(i, j, r, c, s):
        _, vjp = jax.vjp(lambda m, u: jax.nn.sigmoid(m) * u, r[0], r[1])
        dm, du = vjp(r[2])
        return [du, dm], [], []

    return rowcall("merge_bwd", body, T, tm, 4 * nb, Cn,
                   [(P, 256, lambda j: mb + j), (U, 256, lambda j: j), (dY, 256, lambda j: j % nb)], [], [],
                   [(D4, BF, 256, lambda j: j), (IN, F32, 256, lambda j: mb + j)])


def assemble_na(dK, dV, dP, Cn, tm):
    T = dP.shape[0]

    def body(i, j, r, c, s):
        return [jnp.where(j < 4, r[0], r[1])], [], []

    return rowcall("assemble_na", body, T, tm, 8, Cn,
                   [(dK, 128, lambda j: jnp.minimum(j, 3)), (dV, 128, lambda j: jnp.clip(j - 4, 0, 3))],
                   [], [], [("into", dP, 128, lambda j: j)])[0]


def branch_fwd(G, Wb, tm):
    T = G.shape[0]
    D = Wb.shape[2]

    def kern(g_ref, w_ref, o_ref):
        o_ref[...] = _dg(g_ref[...], w_ref[...], 1, 0)

    return pl.pallas_call(
        kern, grid=(T // tm, 4),
        in_specs=[pl.BlockSpec((tm, 512), lambda i, b: (i, b)), pl.BlockSpec((None, 512, D), lambda i, b: (b, 0, 0))],
        out_specs=pl.BlockSpec((tm, D), lambda i, b: (i, b)), out_shape=jax.ShapeDtypeStruct((T, 4 * D), F32),
        compiler_params=_cp("parallel", "arbitrary"), name="branch_fwd")(G, Wb)


def branch_dg(dU, Wb, tm):
    T = dU.shape[0]
    D = Wb.shape[2]

    def kern(u_ref, w_ref, o_ref):
        o_ref[...] = _dg(u_ref[...], w_ref[...], 1, 1)

    return pl.pallas_call(
        kern, grid=(T // tm, 4),
        in_specs=[pl.BlockSpec((tm, D), lambda i, b: (i, b)), pl.BlockSpec((None, 512, D), lambda i, b: (b, 0, 0))],
        out_specs=pl.BlockSpec((tm, 512), lambda i, b: (i, b)), out_shape=jax.ShapeDtypeStruct((T, 2048), F32),
        compiler_params=_cp("parallel", "arbitrary"), name="branch_dg")(dU, Wb)


def branch_merge_fwd(P, G, Wb, tm):
    T = P.shape[0]
    D = Wb.shape[2]
    nj = D // 256
    mb = CB["merge"] // 2

    def kern(g_ref, w_ref, m0, m1, m2, m3, y_ref):
        y = None
        for b, m_ref in enumerate((m0, m1, m2, m3)):
            term = jax.nn.sigmoid(m_ref[...]) * _dg(g_ref[:, 512 * b:512 * (b + 1)], w_ref[b], 1, 0)
            y = term if y is None else y + term
        y_ref[...] = y.astype(BF)

    m_specs = [pl.BlockSpec((tm, 256), (lambda i, j, b=b: (i, mb + b * nj + j))) for b in range(4)]
    return pl.pallas_call(
        kern, grid=(T // tm, nj),
        in_specs=[pl.BlockSpec((tm, 2048), lambda i, j: (i, 0)), pl.BlockSpec((4, 512, 256), lambda i, j: (0, 0, j))] + m_specs,
        out_specs=pl.BlockSpec((tm, 256), lambda i, j: (i, j)), out_shape=jax.ShapeDtypeStruct((T, D), BF),
        compiler_params=_cp("parallel", "arbitrary"), name="branch_merge_fwd")(G, Wb, P, P, P, P)


def branch_merge_bwd(P, G, Wb, dY, tm):
    T, IN = P.shape
    D = Wb.shape[2]
    nj = D // 256
    mb = CB["merge"] // 2

    def kern(g_ref, w_ref, m_ref, dy_ref, du_ref, dg_ref, dm_ref, acc_s):
        j = pl.program_id(2)
        w = w_ref[...]
        u = _dg(g_ref[...], w, 1, 0)
        sig = jax.nn.sigmoid(m_ref[...])
        dy = dy_ref[...]
        dm_ref[...] = (dy * u * sig * (1.0 - sig)).astype(BF)
        du = dy * sig
        du_ref[...] = du.astype(BF)
        part = _dg(du, w, 1, 1)

        @pl.when(j == 0)
        def _():
            acc_s[...] = part

        @pl.when(j > 0)
        def _():
            acc_s[...] += part

        @pl.when(j == nj - 1)
        def _():
            dg_ref[...] = acc_s[...]

    return pl.pallas_call(
        kern, grid=(T // tm, 4, nj),
        in_specs=[pl.BlockSpec((tm, 512), lambda i, b, j: (i, b)), pl.BlockSpec((None, 512, 256), lambda i, b, j: (b, 0, j)),
                  pl.BlockSpec((tm, 256), lambda i, b, j: (i, mb + b * nj + j)), pl.BlockSpec((tm, 256), lambda i, b, j: (i, j))],
        out_specs=[pl.BlockSpec((tm, 256), lambda i, b, j: (i, b * nj + j)), pl.BlockSpec((tm, 512), lambda i, b, j: (i, b)),
                   pl.BlockSpec((tm, 256), lambda i, b, j: (i, mb + b * nj + j))],
        out_shape=[jax.ShapeDtypeStruct((T, 4 * D), BF), jax.ShapeDtypeStruct((T, 2048), F32), jax.ShapeDtypeStruct((T, IN), BF)],
        scratch_shapes=[pltpu.VMEM((tm, 512), F32)],
        compiler_params=_cp("parallel", "arbitrary", "arbitrary"), name="branch_merge_bwd")(G, Wb, P, dY)


def rows_from_devs(name, xg, l, L):
    _, Lr, C = xg.shape
    r = Lr // L

    def body(x_ref, o_ref):
        o_ref[...] = x_ref[...]

    return pl.pallas_call(
        body, grid=(N_DEV,), in_specs=[pl.BlockSpec((None, r, C), lambda d: (d, l, 0))], out_specs=pl.BlockSpec((r, C), lambda d: (d, 0)),
        out_shape=jax.ShapeDtypeStruct((N_DEV * r, C), xg.dtype), compiler_params=_cp("parallel"), name=name)(xg)


def rows_to_devs(name, g, l, L, prev):
    R8, C = g.shape
    r = R8 // N_DEV

    def body(x_ref, *rest):
        rest[-1][...] = x_ref[...]

    in_specs, args, aliases = [pl.BlockSpec((r, C), lambda d: (d, 0))], [g], {}
    if prev is not None:
        in_specs.append(pl.BlockSpec(memory_space=pl.ANY))
        args.append(prev)
        aliases = {1: 0}
    return pl.pallas_call(
        body, grid=(N_DEV,), in_specs=in_specs, out_specs=pl.BlockSpec((None, r, C), lambda d: (d, l, 0)),
        out_shape=jax.ShapeDtypeStruct((N_DEV, L * r, C), g.dtype), input_output_aliases=aliases,
        compiler_params=_cp("parallel"), name=name)(*args)


def branch_dw(G, dU, tk):
    T = G.shape[0]
    D = dU.shape[1] // 4
    nk = T // tk

    def kern(g_ref, u_ref, o_ref):
        k = pl.program_id(1)
        part = _dg(g_ref[...], u_ref[...], 0, 0)

        @pl.when(k == 0)
        def _():
            o_ref[...] = part

        @pl.when(k > 0)
        def _():
            o_ref[...] += part

    return pl.pallas_call(
        kern, grid=(4, nk),
        in_specs=[pl.BlockSpec((tk, 512), lambda b, k: (k, b)), pl.BlockSpec((tk, D), lambda b, k: (k, b))],
        out_specs=pl.BlockSpec((None, 512, D), lambda b, k: (b, 0, 0)), out_shape=jax.ShapeDtypeStruct((4, 512, D), F32),
        compiler_params=_cp("parallel", "arbitrary"), name="branch_dw")(G, dU)


def na_tables(S, Cn):
    R = S // GRID_W
    ntile = (Cn + S) // 128
    nct = Cn // 128
    rs = np.clip(np.arange(R) - NA_WIN_ROWS // 2, 0, R - NA_WIN_ROWS)
    seen, pats, cls, ws = {}, [], [], []
    for t in range(ntile):
        pat = -np.ones((NA_TILE_ROWS, NA_BAND), np.int64)
        w0 = 0
        if t >= nct:
            r0 = NA_TILE_ROWS * (t - nct)
            w0 = min(rs[r0], R - NA_BAND)
            for a in range(NA_TILE_ROWS):
                for j in range(NA_BAND):
                    kr = w0 + j
                    if rs[r0 + a] <= kr < rs[r0 + a] + NA_WIN_ROWS:
                        pat[a, j] = kr - (r0 + a) + NA_WIN_ROWS - 1
        key = pat.tobytes()
        if key not in seen:
            seen[key] = len(pats)
            pats.append(pat)
        cls.append(seen[key])
        ws.append(Cn + GRID_W * int(w0))
    cls = np.asarray(cls, np.int32)
    first = np.asarray([1 if t == 0 or cls[t] != cls[t - 1] else 0 for t in range(ntile)], np.int32)
    assert len(set(cls[first == 1].tolist())) == int(first.sum())
    pats = np.stack(pats)
    ncls = pats.shape[0]
    nrow = -(-(ncls * NA_TILE_ROWS * NA_BAND) // 128) * 128
    m1 = np.zeros((nrow, 128), np.float32)
    flat = pats.reshape(-1)
    for k, dr in enumerate(flat):
        if dr >= 0:
            m1[k, dr] = 1.0
    qc = np.arange(GRID_W)
    col_start = np.clip(qc - NA_WIN_COLS // 2, 0, GRID_W - NA_WIN_COLS)
    kc = np.arange(GRID_W)
    col_ok = (kc[None, :] >= col_start[:, None]) & (kc[None, :] < col_start[:, None] + NA_WIN_COLS)
    m2 = np.zeros((128, GRID_W * GRID_W), np.float32)
    for q in range(GRID_W):
        for k in range(GRID_W):
            if col_ok[q, k]:
                m2[k - q + NA_WIN_COLS - 1, q * GRID_W + k] = 1.0
    valid = (pats >= 0)[:, :, :, None, None] & col_ok[None, None, None]
    vmask = np.transpose(valid, (0, 2, 4, 1, 3)).reshape(ncls, 1, NA_BAND * GRID_W, 1, NA_TILE_ROWS * GRID_W)
    vmask = np.broadcast_to(vmask, (ncls, 1, NA_BAND * GRID_W, 2, NA_TILE_ROWS * GRID_W)).reshape(ncls, 1, NA_BAND * GRID_W, 256)
    return dict(cls=cls, ws=np.asarray(ws, np.int32), first=first, m1=m1, m2=m2, vmask=vmask, ncls=ncls, nrow=nrow)


def rpb_map(name, x, left, right):
    H = x.shape[0]

    def kern(x_ref, l_ref, r_ref, o_ref):
        o_ref[...] = hdot(hdot(l_ref[...], x_ref[...]), r_ref[...])

    return pl.pallas_call(
        kern, grid=(H,),
        in_specs=[pl.BlockSpec((None,) + x.shape[1:], lambda h: (h, 0, 0)), pl.BlockSpec(left.shape, lambda h: (0, 0)),
                  pl.BlockSpec(right.shape, lambda h: (0, 0))],
        out_specs=pl.BlockSpec((None, left.shape[0], right.shape[1]), lambda h: (h, 0, 0)),
        out_shape=jax.ShapeDtypeStruct((H, left.shape[0], right.shape[1]), F32), compiler_params=_cp("parallel"), name=name)(x, left, right)


def bias_table(rpb, tb):
    H = rpb.shape[0]
    xp = jnp.zeros((H, 128, 128), F32).at[:, :rpb.shape[1], :rpb.shape[2]].set(rpb)
    a = rpb_map("rpb_expand", xp, jnp.asarray(tb["m1"]), jnp.asarray(tb["m2"]))
    ncls = tb["ncls"]
    a = a[:, :ncls * NA_TILE_ROWS * NA_BAND].reshape(H // 2, 2, ncls, NA_TILE_ROWS, NA_BAND, GRID_W, GRID_W)
    a = jnp.transpose(a, (2, 0, 4, 6, 1, 3, 5)).reshape(ncls, H // 2, NA_BAND * GRID_W, 256)
    return jnp.where(jnp.asarray(tb["vmask"]), a, NEG)


def bias_table_grad(dbt, tb):
    ncls = tb["ncls"]
    H2 = dbt.shape[1]
    d = dbt.reshape(ncls, H2, NA_BAND, GRID_W, 2, NA_TILE_ROWS, GRID_W)
    d = jnp.transpose(d, (1, 4, 0, 5, 2, 6, 3)).reshape(H2 * 2, ncls * NA_TILE_ROWS * NA_BAND, GRID_W * GRID_W)
    d = jnp.pad(d, ((0, 0), (0, tb["nrow"] - d.shape[1]), (0, 0)))
    g = rpb_map("rpb_reduce", d, jnp.asarray(tb["m1"].T.copy()), jnp.asarray(tb["m2"].T.copy()))
    return g[:, :2 * NA_WIN_ROWS - 1, :2 * NA_WIN_COLS - 1]


def f_na(q, kb, vb, kc, vc, bias_t):
    d = lax.broadcasted_iota(jnp.int32, (128, 1), 0)
    q_t = q.T
    qbd = jnp.concatenate([jnp.where(d < 64, q_t, 0.0), jnp.where(d >= 64, q_t, 0.0)], axis=1)
    sb = bdot(kb, qbd) * 0.125 + bias_t
    sc = bdot(kc, qbd) * 0.125
    m = jnp.maximum(jnp.max(sb, axis=0, keepdims=True), jnp.max(sc, axis=0, keepdims=True))
    eb = jnp.exp(sb - m)
    ec = jnp.exp(sc - m)
    den = jnp.sum(eb, axis=0, keepdims=True) + jnp.sum(ec, axis=0, keepdims=True)
    of = bdot(eb / den, vb, 0, 0) + bdot(ec / den, vc, 0, 0)
    o0, o1 = split_rows(of, 2)
    return jnp.where(_lane(o0.shape) < 64, o0, o1)


def _na_specs(T, Cn, nband):
    qs = pl.BlockSpec((128, 128), lambda hp, t, *_: (t, CB["a_q"] + hp))
    ks = pl.BlockSpec((T, 128), lambda hp, t, *_: (0, CB["a_k"] + hp))
    vs = pl.BlockSpec((T, 128), lambda hp, t, *_: (0, CB["a_v"] + hp))
    bs = pl.BlockSpec((None, None, nband, 256), lambda hp, t, cls, ws, first: (cls[t], hp, 0, 0))
    return qs, ks, vs, bs


def na_fwd(P, bt, tb, Cn):
    T = P.shape[0]
    nband = NA_BAND * GRID_W
    qs, ks, vs, bs = _na_specs(T, Cn, nband)

    def kern(cls, ws, first, q_ref, k_ref, v_ref, b_ref, o_ref):
        w0 = pl.multiple_of(ws[pl.program_id(1)], 64)
        o_ref[...] = f_na(q_ref[...], k_ref[pl.ds(w0, nband), :], v_ref[pl.ds(w0, nband), :], k_ref[0:Cn, :], v_ref[0:Cn, :], b_ref[...])

    return pl.pallas_call(
        kern, grid_spec=pltpu.PrefetchScalarGridSpec(
            num_scalar_prefetch=3, grid=(4, T // 128), in_specs=[qs, ks, vs, bs],
            out_specs=pl.BlockSpec((128, 128), lambda hp, t, *_: (t, hp))),
        out_shape=jax.ShapeDtypeStruct((T, 512), F32), compiler_params=_cp("arbitrary", "arbitrary"), name="na_fwd",
    )(jnp.asarray(tb["cls"]), jnp.asarray(tb["ws"]), jnp.asarray(tb["first"]), P, P, P, bt)


def na_bwd(P, bt, dO, dP, tb, Cn):
    T = P.shape[0]
    nband = NA_BAND * GRID_W
    qs, ks, vs, bs = _na_specs(T, Cn, nband)
    dos = pl.BlockSpec((128, 128), lambda hp, t, *_: (t, hp))

    def kern(cls, ws, first, q_ref, k_ref, v_ref, b_ref, do_ref, dp_in, dq_ref, dk_ref, dv_ref, db_ref):
        t = pl.program_id(1)
        w0 = pl.multiple_of(ws[t], 64)
        band = pl.ds(w0, nband)
        _, vjp = jax.vjp(f_na, q_ref[...], k_ref[band, :], v_ref[band, :], k_ref[0:Cn, :], v_ref[0:Cn, :], b_ref[...])
        dq, dkb, dvb, dkc, dvc, db = vjp(do_ref[...])
        dq_ref[...] = dq.astype(dq_ref.dtype)

        @pl.when(t == 0)
        def _():
            dk_ref[...] = jnp.zeros_like(dk_ref)
            dv_ref[...] = jnp.zeros_like(dv_ref)

        dk_ref[band, :] += dkb
        dv_ref[band, :] += dvb
        dk_ref[0:Cn, :] += dkc
        dv_ref[0:Cn, :] += dvc

        @pl.when(first[t] == 1)
        def _():
            db_ref[...] = db

        @pl.when(first[t] == 0)
        def _():
            db_ref[...] += db

    full = lambda hp, t, *_: (0, hp)
    return pl.pallas_call(
        kern, grid_spec=pltpu.PrefetchScalarGridSpec(
            num_scalar_prefetch=3, grid=(4, T // 128), in_specs=[qs, ks, vs, bs, dos, pl.BlockSpec(memory_space=pl.ANY)],
            out_specs=[pl.BlockSpec((128, 128), lambda hp, t, *_: (t, CB["a_q"] + hp)), pl.BlockSpec((T, 128), full), pl.BlockSpec((T, 128), full),
                       pl.BlockSpec((None, None, nband, 256), lambda hp, t, cls, ws, first: (cls[t], hp, 0, 0))]),
        out_shape=[jax.ShapeDtypeStruct(dP.shape, dP.dtype), jax.ShapeDtypeStruct((T, 512), F32), jax.ShapeDtypeStruct((T, 512), F32),
                   jax.ShapeDtypeStruct(bt.shape, F32)],
        input_output_aliases={8: 0}, compiler_params=_cp("arbitrary", "arbitrary"), name="na_bwd",
    )(jnp.asarray(tb["cls"]), jnp.asarray(tb["ws"]), jnp.asarray(tb["first"]), P, P, P, bt, dO, dP)


def _expand_heads(blk, g):
    out = []
    for p in range(2):
        pair = blk[:, 128 * p:128 * (p + 1)]
        lane = _lane(pair.shape)
        a = jnp.where(lane < 64, pair, 0.0)
        b = jnp.where(lane >= 64, pair, 0.0)
        out.append(jnp.where(g == 0, a, pltpu.roll(a, 64, 1)))
        out.append(jnp.where(g == 0, pltpu.roll(b, 64, 1), b))
    return out


def _compact_heads(hs, g):
    out = []
    for p in range(2):
        e, o = hs[2 * p], hs[2 * p + 1]
        lane = _lane(e.shape)
        e0 = jnp.where(g == 0, e, pltpu.roll(e, 64, 1))
        o1 = jnp.where(g == 0, pltpu.roll(o, 64, 1), o)
        out.append(jnp.where(lane < 64, e0, o1))
    return jnp.concatenate(out, axis=1)


def _kv_map(Cn, tq, tk, col):
    nq_ctx = Cn // tq
    last_ctx = (Cn - 1) // tk

    def f(g, qi, kj):
        return (jnp.where(qi < nq_ctx, jnp.minimum(kj, last_ctx), kj), col)
    return f


def flash_fwd(Qp, KV, Cn, tq, tk):
    T = Qp.shape[0]
    nq, nk = T // tq, T // tk
    nq_ctx = Cn // tq

    def kern(q_ref, k_ref, v_ref, o_ref, lse_ref, m_s, l_s, acc_s):
        g, qi, kj = pl.program_id(0), pl.program_id(1), pl.program_id(2)

        @pl.when(kj == 0)
        def _():
            m_s[...] = jnp.full_like(m_s, NEG)
            l_s[...] = jnp.zeros_like(l_s)
            acc_s[...] = jnp.zeros_like(acc_s)

        def step(masked):
            k = k_ref[...]
            v = v_ref[...]
            if masked:
                valid = kj * tk + lax.broadcasted_iota(jnp.int32, (1, tk), 1) < Cn
            for h in range(4):
                rows = pl.ds(tq * h, tq)
                s = _dg(q_ref[:, 128 * h:128 * (h + 1)], k, 1, 1)
                if masked:
                    s = jnp.where(valid, s, NEG)
                m_old = m_s[rows, :]
                m_new = jnp.maximum(m_old, jnp.max(s, axis=1, keepdims=True))
                alpha = jnp.exp(m_old - m_new)
                p = jnp.exp(s - m_new)
                l_s[rows, :] = alpha * l_s[rows, :] + jnp.sum(p, axis=1, keepdims=True)
                acc_s[rows, :] = alpha * acc_s[rows, :] + _dg(p, v, 1, 0)
                m_s[rows, :] = m_new

        @pl.when(qi >= nq_ctx)
        def _():
            step(False)

        @pl.when((qi < nq_ctx) & (kj * tk < Cn))
        def _():
            step(True)

        @pl.when(kj == nk - 1)
        def _():
            o4 = acc_s[...] / l_s[...]
            o_ref[...] = _compact_heads([o4[tq * h:tq * (h + 1)] for h in range(4)], g)
            lse = m_s[...] + jnp.log(l_s[...])
            lse_ref[...] = jnp.concatenate([jnp.broadcast_to(lse[tq * h:tq * (h + 1)], (tq, 128)) for h in range(4)], axis=1)

    return pl.pallas_call(
        kern, grid=(2, nq, nk),
        in_specs=[pl.BlockSpec((tq, 512), lambda g, qi, kj: (qi, g)), pl.BlockSpec((tk, 128), _kv_map(Cn, tq, tk, 0)),
                  pl.BlockSpec((tk, 128), _kv_map(Cn, tq, tk, 1))],
        out_specs=[pl.BlockSpec((tq, 256), lambda g, qi, kj: (qi, g)), pl.BlockSpec((tq, 512), lambda g, qi, kj: (qi, g))],
        out_shape=[jax.ShapeDtypeStruct((T, 512), F32), jax.ShapeDtypeStruct((T, 1024), F32)],
        scratch_shapes=[pltpu.VMEM((4 * tq, 1), F32), pltpu.VMEM((4 * tq, 1), F32), pltpu.VMEM((4 * tq, 128), F32)],
        compiler_params=_cp("arbitrary", "arbitrary", "arbitrary"), name="flash_fwd")(Qp, KV, KV)


def flash_bwd(Qp, KV, O, LSE, dOall, Cn, tq, tk):
    T = Qp.shape[0]
    nq, nk = T // tq, T // tk
    nq_ctx = Cn // tq

    def kern(q_ref, k_ref, v_ref, o_ref, lse_ref, do_ref, dq_ref, dkv_ref, dq_s, do_s, dl_s, ls_s):
        g, qi, kj = pl.program_id(0), pl.program_id(1), pl.program_id(2)

        @pl.when((g == 0) & (qi == 0) & (kj == 0))
        def _():
            dkv_ref[...] = jnp.zeros_like(dkv_ref)

        @pl.when(kj == 0)
        def _():
            do4 = jnp.concatenate(_expand_heads(do_ref[...], g), axis=0)
            o4 = jnp.concatenate(_expand_heads(o_ref[...], g), axis=0)
            do_s[...] = do4.astype(BF)
            dl_s[...] = jnp.sum(do4 * o4, axis=1, keepdims=True)
            ls_s[...] = jnp.concatenate([jnp.max(lse_ref[:, 128 * h:128 * (h + 1)], axis=1, keepdims=True) for h in range(4)], axis=0)
            dq_s[...] = jnp.zeros_like(dq_s)

        def step(masked):
            k = k_ref[...]
            v = v_ref[...]
            if masked:
                valid = kj * tk + lax.broadcasted_iota(jnp.int32, (1, tk), 1) < Cn
            dk = dv = None
            for h in range(4):
                rows = pl.ds(tq * h, tq)
                qh = q_ref[:, 128 * h:128 * (h + 1)]
                s = _dg(qh, k, 1, 1)
                if masked:
                    s = jnp.where(valid, s, NEG)
                p = jnp.exp(s - ls_s[rows, :])
                doh = do_s[rows, :]
                dvh = _dg(p, doh, 0, 0)
                dp = _dg(doh, v, 1, 1)
                ds = p * (dp - dl_s[rows, :])
                dq_s[rows, :] += _dg(ds, k, 1, 0)
                dkh = _dg(ds, qh, 0, 0)
                dk = dkh if dk is None else dk + dkh
                dv = dvh if dv is None else dv + dvh
            krows = pl.ds(pl.multiple_of(kj * tk, tk), tk)
            dkv_ref[krows, 0:128] += dk
            dkv_ref[krows, 128:256] += dv

        @pl.when(qi >= nq_ctx)
        def _():
            step(False)

        @pl.when((qi < nq_ctx) & (kj * tk < Cn))
        def _():
            step(True)

        @pl.when(kj == nk - 1)
        def _():
            dq = dq_s[...]
            dq_ref[...] = jnp.concatenate([dq[tq * h:tq * (h + 1)] for h in range(4)], axis=1)

    return pl.pallas_call(
        kern, grid=(2, nq, nk),
        in_specs=[pl.BlockSpec((tq, 512), lambda g, qi, kj: (qi, g)), pl.BlockSpec((tk, 128), _kv_map(Cn, tq, tk, 0)),
                  pl.BlockSpec((tk, 128), _kv_map(Cn, tq, tk, 1)), pl.BlockSpec((tq, 256), lambda g, qi, kj: (qi, g)),
                  pl.BlockSpec((tq, 512), lambda g, qi, kj: (qi, g)), pl.BlockSpec((tq, 256), lambda g, qi, kj: (qi, 4 + g))],
        out_specs=[pl.BlockSpec((tq, 512), lambda g, qi, kj: (qi, g)), pl.BlockSpec((T, 256), lambda g, qi, kj: (0, 0))],
        out_shape=[jax.ShapeDtypeStruct((T, 1024), F32), jax.ShapeDtypeStruct((T, 256), F32)],
        scratch_shapes=[pltpu.VMEM((4 * tq, 128), F32), pltpu.VMEM((4 * tq, 128), BF), pltpu.VMEM((4 * tq, 1), F32), pltpu.VMEM((4 * tq, 1), F32)],
        compiler_params=_cp("arbitrary", "arbitrary", "arbitrary"), name="flash_bwd")(Qp, KV, KV, O, LSE, dOall)


def _pool_band(t0, w0, n, win, gi, Cn, T):
    i = lax.broadcasted_iota(jnp.int32, (n, win), 0)
    jx = lax.broadcasted_iota(jnp.int32, (n, win), 1)
    t = t0 + i
    tp = w0 + jx
    half = lax.shift_left(jnp.int32(1), gi)
    lo = jnp.maximum(t - half, jnp.where(t < Cn, 0, Cn))
    hi = jnp.minimum(t + half - 1, jnp.where(t < Cn, Cn, T) - 1)
    cnt = (hi - lo + 1).astype(F32)
    return jnp.where((tp >= lo) & (tp <= hi), 1.0 / cnt, 0.0) - jnp.where(tp == t, 1.0, 0.0)


def _pool_geom(T):
    n = _pick(T, (256, 128))
    return n, n + 128


def pool_fwd(P, pw, ps, Cn):
    T = P.shape[0]
    n, win = _pool_geom(T)

    def kern(u_ref, w_ref, s_ref, o_ref):
        gi = pl.program_id(0)

        def blk(b, carry):
            t0 = pl.multiple_of(b * n, n)
            w0 = pl.multiple_of(jnp.clip(t0 - 64, 0, T - win), 64)
            d = hdot3(_pool_band(t0, w0, n, win, gi, Cn, T), u_ref[pl.ds(w0, win), :])
            o_ref[pl.ds(t0, n), :] = bdot(d, w_ref[...]) * s_ref[...]
            return carry

        lax.fori_loop(0, T // n, blk, 0)

    return pl.pallas_call(
        kern, grid=(4,),
        in_specs=[pl.BlockSpec((T, 128), lambda g: (0, CB["b_in"] + g)), pl.BlockSpec((None, 128, 128), lambda g: (g, 0, 0)),
                  pl.BlockSpec((1, 128), lambda g: (0, g))],
        out_specs=pl.BlockSpec((T, 128), lambda g: (0, g)), out_shape=jax.ShapeDtypeStruct((T, 512), F32),
        compiler_params=_cp("arbitrary"), name="pool_fwd")(P, pw, ps)


def pool_bwd(P, pw, ps, dOall, dP, Cn):
    T = P.shape[0]
    n, win = _pool_geom(T)

    def kern(u_ref, w_ref, s_ref, do_ref, dp_in, du_out, dw_ref, ds_ref, du_ref):
        gi = pl.program_id(0)
        du_ref[...] = jnp.zeros_like(du_ref)
        dw_ref[...] = jnp.zeros_like(dw_ref)
        ds_ref[...] = jnp.zeros_like(ds_ref)

        def blk(b, carry):
            t0 = pl.multiple_of(b * n, n)
            w0 = pl.multiple_of(jnp.clip(t0 - 64, 0, T - win), 64)
            band = _pool_band(t0, w0, n, win, gi, Cn, T)
            _, vjp = jax.vjp(lambda uw, w, s: bdot(hdot3(band, uw), w) * s, u_ref[pl.ds(w0, win), :], w_ref[...], s_ref[...])
            duw, dw, ds = vjp(do_ref[pl.ds(t0, n), :])
            du_ref[pl.ds(w0, win), :] += duw
            dw_ref[...] += dw
            ds_ref[...] += ds
            return carry

        lax.fori_loop(0, T // n, blk, 0)
        du_out[...] = du_ref[...].astype(du_out.dtype)

    return pl.pallas_call(
        kern, grid=(4,),
        in_specs=[pl.BlockSpec((T, 128), lambda g: (0, CB["b_in"] + g)), pl.BlockSpec((None, 128, 128), lambda g: (g, 0, 0)),
                  pl.BlockSpec((1, 128), lambda g: (0, g)), pl.BlockSpec((T, 128), lambda g: (0, 4 + g)), pl.BlockSpec(memory_space=pl.ANY)],
        out_specs=[pl.BlockSpec((T, 128), lambda g: (0, CB["b_in"] + g)), pl.BlockSpec((None, 128, 128), lambda g: (g, 0, 0)),
                   pl.BlockSpec((1, 128), lambda g: (0, g))],
        out_shape=[jax.ShapeDtypeStruct(dP.shape, dP.dtype), jax.ShapeDtypeStruct((4, 128, 128), F32), jax.ShapeDtypeStruct((1, 512), F32)],
        scratch_shapes=[pltpu.VMEM((T, 128), F32)],
        input_output_aliases={4: 0}, compiler_params=_cp("arbitrary"), name="pool_bwd")(P, pw, ps, dOall, dP)


CONV_HALO = 16


def _conv_block(T):
    return _pick(T, (256, 128))


def _stage(dst_ref, src, t0, n, T, Cn):
    h = CONV_HALO
    dst_ref[h:h + n, :] = src(pl.ds(t0, n))
    left_ok = (t0 != 0) & (t0 != Cn)
    right_ok = (t0 + n != Cn) & (t0 + n != T)
    lo = pl.multiple_of(jnp.maximum(t0 - h, 0), 8)
    hi = pl.multiple_of(jnp.minimum(t0 + n, T - h), 8)
    dst_ref[0:h, :] = jnp.where(left_ok, src(pl.ds(lo, h)), 0.0)
    dst_ref[h + n:2 * h + n, :] = jnp.where(right_ok, src(pl.ds(hi, h)), 0.0)


def conv_fwd(P, cw, cb, Cn):
    T = P.shape[0]
    n = _conv_block(T)
    off = CONV_HALO - CONV_WIDTH // 2

    def kern(a_ref, g_ref, w_ref, b_ref, y_ref, us):
        def blk(b, carry):
            t0 = pl.multiple_of(b * n, n)
            _stage(us, lambda r: a_ref[r, :] * jax.nn.sigmoid(g_ref[r, :]), t0, n, T, Cn)
            acc = jnp.zeros((n, 128), F32)
            for k in range(CONV_WIDTH):
                acc = acc + us[k + off:k + off + n, :] * w_ref[k:k + 1, :]
            y_ref[pl.ds(t0, n), :] = acc + b_ref[...]
            return carry

        lax.fori_loop(0, T // n, blk, 0)

    slab = lambda o: pl.BlockSpec((T, 128), lambda c: (0, o + c))
    return pl.pallas_call(
        kern, grid=(4,),
        in_specs=[slab(CB["d_glu"]), slab(CB["d_glu"] + 4), pl.BlockSpec((CONV_PAD, 128), lambda c: (0, c)), pl.BlockSpec((1, 128), lambda c: (0, c))],
        out_specs=pl.BlockSpec((T, 128), lambda c: (0, c)), out_shape=jax.ShapeDtypeStruct((T, 512), F32),
        scratch_shapes=[pltpu.VMEM((n + 2 * CONV_HALO, 128), F32)],
        compiler_params=_cp("arbitrary"), name="conv_fwd")(P, P, cw, cb)


def conv_bwd(P, cw, dY, dP, Cn):
    T = P.shape[0]
    n = _conv_block(T)
    off = CONV_HALO - CONV_WIDTH // 2
    back = CONV_HALO + CONV_WIDTH // 2

    def kern(a_ref, g_ref, w_ref, dy_ref, dp_in, da_ref, dg_ref, dw_ref, db_ref, us, dys):
        dw_ref[...] = jnp.zeros_like(dw_ref)
        db_ref[...] = jnp.zeros_like(db_ref)

        def blk(b, carry):
            t0 = pl.multiple_of(b * n, n)
            cur = pl.ds(t0, n)
            _stage(us, lambda r: a_ref[r, :] * jax.nn.sigmoid(g_ref[r, :]), t0, n, T, Cn)
            _stage(dys, lambda r: dy_ref[r, :], t0, n, T, Cn)
            dyc = dy_ref[cur, :]
            du = jnp.zeros((n, 128), F32)
            for k in range(CONV_WIDTH):
                du = du + dys[back - k:back - k + n, :] * w_ref[k:k + 1, :]
                dw_ref[k:k + 1, :] += jnp.sum(us[k + off:k + off + n, :] * dyc, axis=0, keepdims=True)
            a = a_ref[cur, :]
            sig = jax.nn.sigmoid(g_ref[cur, :])
            da_ref[cur, :] = (du * sig).astype(da_ref.dtype)
            dg_ref[cur, :] = du * a * sig * (1.0 - sig)
            db_ref[...] += jnp.sum(dyc, axis=0, keepdims=True)
            return carry

        lax.fori_loop(0, T // n, blk, 0)

    slab = lambda o: pl.BlockSpec((T, 128), lambda c: (0, o + c))
    return pl.pallas_call(
        kern, grid=(4,),
        in_specs=[slab(CB["d_glu"]), slab(CB["d_glu"] + 4), pl.BlockSpec((CONV_PAD, 128), lambda c: (0, c)), slab(0),
                  pl.BlockSpec(memory_space=pl.ANY)],
        out_specs=[slab(CB["d_glu"]), slab(0), pl.BlockSpec((CONV_PAD, 128), lambda c: (0, c)), pl.BlockSpec((1, 128), lambda c: (0, c))],
        out_shape=[jax.ShapeDtypeStruct(dP.shape, dP.dtype), jax.ShapeDtypeStruct((T, 512), F32), jax.ShapeDtypeStruct((CONV_PAD, 512), F32),
                   jax.ShapeDtypeStruct((1, 512), F32)],
        scratch_shapes=[pltpu.VMEM((n + 2 * CONV_HALO, 128), F32), pltpu.VMEM((n + 2 * CONV_HALO, 128), F32)],
        input_output_aliases={4: 0}, compiler_params=_cp("arbitrary"), name="conv_bwd")(P, P, cw, dY, dP)


def copy_cols(name, src, dP, col0, Cn, tm):
    T = dP.shape[0]

    def body(i, j, r, c, s):
        return [r[0]], [], []

    return rowcall(name, body, T, tm, src.shape[1] // 128, Cn, [(src, 128, lambda j: j)], [], [], [("into", dP, 128, lambda j: col0 + j)])[0]


def ada_fwd(cvec, w, b):
    L, D, wc = w.shape

    def kern(c_ref, w_ref, b_ref, o_ref):
        o_ref[...] = _dg(_silu(c_ref[...]), w_ref[...], 1, 0) + b_ref[...]

    return pl.pallas_call(
        kern, grid=(L,),
        in_specs=[pl.BlockSpec((16, D), lambda l: (0, 0)), pl.BlockSpec((None, D, wc), lambda l: (l, 0, 0)), pl.BlockSpec((None, 1, wc), lambda l: (l, 0, 0))],
        out_specs=pl.BlockSpec((None, 16, wc), lambda l: (l, 0, 0)), out_shape=jax.ShapeDtypeStruct((L, 16, wc), F32),
        compiler_params=_cp("arbitrary"), name="ada_fwd")(cvec, w, b)


def ada_bwd(cvec, w, dm):
    L, D, wc = w.shape

    def kern(c_ref, w_ref, d_ref, gw_ref, ds_ref):
        l = pl.program_id(0)
        d = d_ref[...]
        gw_ref[...] = _dg(_silu(c_ref[...]), d, 0, 0)
        part = _dg(d, w_ref[...], 1, 1)

        @pl.when(l == 0)
        def _():
            ds_ref[...] = part

        @pl.when(l > 0)
        def _():
            ds_ref[...] += part

    return pl.pallas_call(
        kern, grid=(L,),
        in_specs=[pl.BlockSpec((16, D), lambda l: (0, 0)), pl.BlockSpec((None, D, wc), lambda l: (l, 0, 0)), pl.BlockSpec((None, 16, wc), lambda l: (l, 0, 0))],
        out_specs=[pl.BlockSpec((None, D, wc), lambda l: (l, 0, 0)), pl.BlockSpec((16, D), lambda l: (0, 0))],
        out_shape=[jax.ShapeDtypeStruct((L, D, wc), F32), jax.ShapeDtypeStruct((16, D), F32)],
        compiler_params=_cp("arbitrary"), name="ada_bwd")(cvec, w, dm)


def silu_grad(cc, parts):
    D = cc.shape[1]

    def kern(c_ref, p_ref, o_ref):
        tot = p_ref[0]
        for k in range(1, N_DEV):
            tot = tot + p_ref[k]
        _, vjp = jax.vjp(_silu, c_ref[...])
        o_ref[...] = vjp(tot)[0]

    return pl.pallas_call(kern, out_shape=jax.ShapeDtypeStruct((1, D), F32), name="silu_grad")(cc, parts)


def _flip(v, f):
    return 1 - v if f else v


_REL = ((0, 0), (1, 0), (0, 1), (1, 1))


def allgather(name, x):
    def body(x_ref, out_ref, send_sems, recv_sems, local_sem):
        mx, my, mc = lax.axis_index("x"), lax.axis_index("y"), lax.axis_index("c")
        me, sibling = (mx, my, mc), (mx, my, 1 - mc)
        chips = [(_flip(mx, fx), _flip(my, fy)) for fx, fy in _REL[1:]]

        def slot(px, py, pc):
            return out_ref.at[4 * px + 2 * py + pc]

        def copy(k, block, to, src=None):
            return pltpu.make_async_remote_copy(
                src_ref=slot(*block) if src is None else src, dst_ref=slot(*block),
                send_sem=send_sems.at[k], recv_sem=recv_sems.at[k], device_id=to, device_id_type=MESH)

        mine = pltpu.make_async_copy(x_ref, slot(*me), local_sem)
        mine.start()
        first = [copy(0, me, sibling, src=x_ref)] + [copy(1 + j, me, (*chip, mc), src=x_ref) for j, chip in enumerate(chips)]
        for cp in first:
            cp.start()
        passed = [copy(4 + j, (*chip, mc), sibling) for j, chip in enumerate(chips)]
        for j, chip in enumerate(chips):
            copy(1 + j, (*chip, mc), me).wait_recv()
            passed[j].start()
        copy(0, sibling, me).wait_recv()
        for j, chip in enumerate(chips):
            copy(4 + j, (*chip, 1 - mc), me).wait_recv()
        for cp in first + passed:
            cp.wait_send()
        mine.wait()

    return pl.pallas_call(
        body, out_shape=jax.ShapeDtypeStruct((N_DEV,) + x.shape, x.dtype),
        in_specs=[pl.BlockSpec(memory_space=pl.ANY)], out_specs=pl.BlockSpec(memory_space=pl.ANY),
        scratch_shapes=[pltpu.SemaphoreType.DMA((7,)), pltpu.SemaphoreType.DMA((7,)), pltpu.SemaphoreType.DMA(())], name=name)(x)


def cols_from_devs(name, xg, l, nrow):
    _, _, C = xg.shape
    tr = _row_tile(nrow, N_DEV * C, xg.dtype.itemsize, 16, 8 << 20)
    nblk = nrow // tr

    def body(x_ref, o_ref):
        for d in range(N_DEV):
            o_ref[:, d * C:(d + 1) * C] = x_ref[d]

    return pl.pallas_call(
        body, grid=(nblk,), in_specs=[pl.BlockSpec((N_DEV, tr, C), lambda i: (0, l * nblk + i, 0))],
        out_specs=pl.BlockSpec((tr, N_DEV * C), lambda i: (i, 0)), out_shape=jax.ShapeDtypeStruct((nrow, N_DEV * C), xg.dtype),
        compiler_params=_cp("parallel"), name=name)(xg)


def cols_to_devs(name, g, l, L, prev):
    nrow, W = g.shape
    C = W // N_DEV
    tr = _row_tile(nrow, W, 4, 8, 8 << 20)
    nblk = nrow // tr

    def body(x_ref, *rest):
        o_ref = rest[-1]
        for d in range(N_DEV):
            o_ref[d] = x_ref[:, d * C:(d + 1) * C]

    in_specs, args, aliases = [pl.BlockSpec((tr, W), lambda i: (i, 0))], [g], {}
    if prev is not None:
        in_specs.append(pl.BlockSpec(memory_space=pl.ANY))
        args.append(prev)
        aliases = {1: 0}
    return pl.pallas_call(
        body, grid=(nblk,), in_specs=in_specs, out_specs=pl.BlockSpec((N_DEV, tr, C), lambda i: (0, l * nblk + i, 0)),
        out_shape=jax.ShapeDtypeStruct((N_DEV, L * nrow, C), F32), input_output_aliases=aliases,
        compiler_params=_cp("parallel"), name=name)(*args)


def proj_in_gather(H, W, jobs, tm, tn):
    T, D = H.shape
    IN = W.shape[1]
    nJ = len(jobs)
    ni, nj = T // tm, IN // tn

    def kern(a_ref, b_ref, *rest):
        x_alls, o_ref, g_refs = rest[:nJ], rest[nJ], rest[nJ + 1:2 * nJ + 1]
        send_sems, recv_sems, local_sems = rest[2 * nJ + 1:]
        i, j = pl.program_id(0), pl.program_id(1)
        mx, my, mc = lax.axis_index("x"), lax.axis_index("y"), lax.axis_index("c")
        me, sibling = (mx, my, mc), (mx, my, 1 - mc)
        chips = [(_flip(mx, fx), _flip(my, fy)) for fx, fy in _REL[1:]]
        plans = []
        for n, (_, row0, nrows) in enumerate(jobs):
            x_ref = x_alls[n].at[pl.ds(row0, nrows)]

            def slot(px, py, pc, g_ref=g_refs[n]):
                return g_ref.at[4 * px + 2 * py + pc]

            def copy(k, block, to, src=None, slot=slot, base=7 * n):
                return pltpu.make_async_remote_copy(
                    src_ref=slot(*block) if src is None else src, dst_ref=slot(*block),
                    send_sem=send_sems.at[base + k], recv_sem=recv_sems.at[base + k], device_id=to, device_id_type=MESH)

            mine = pltpu.make_async_copy(x_ref, slot(*me), local_sems.at[n])
            first = [copy(0, me, sibling, src=x_ref)] + [copy(1 + q, me, (*chip, mc), src=x_ref) for q, chip in enumerate(chips)]
            passed = [copy(4 + q, (*chip, mc), sibling) for q, chip in enumerate(chips)]
            plans.append((copy, mine, first, passed))

        @pl.when((i == 0) & (j == 0))
        def _():
            for copy, mine, first, passed in plans:
                mine.start()
                for cp in first:
                    cp.start()

        o_ref[...] = _dg(a_ref[...], b_ref[...], 1, 0)

        @pl.when((i == ni - 2) & (j == 0))
        def _():
            for copy, mine, first, passed in plans:
                for q, chip in enumerate(chips):
                    copy(1 + q, (*chip, mc), me).wait_recv()
                    passed[q].start()

        @pl.when((i == ni - 1) & (j == nj - 1))
        def _():
            for copy, mine, first, passed in plans:
                copy(0, sibling, me).wait_recv()
                for q, chip in enumerate(chips):
                    copy(4 + q, (*chip, 1 - mc), me).wait_recv()
                for cp in first + passed:
                    cp.wait_send()
                mine.wait()

    assert ni >= 3
    hbm = pl.BlockSpec(memory_space=pl.ANY)
    res = pl.pallas_call(
        kern, grid=(ni, nj),
        in_specs=[pl.BlockSpec((tm, D), lambda i, j: (i, 0)), pl.BlockSpec((D, tn), lambda i, j: (0, j))] + [hbm] * nJ,
        out_specs=[pl.BlockSpec((tm, tn), lambda i, j: (i, j))] + [hbm] * nJ,
        out_shape=[jax.ShapeDtypeStruct((T, IN), F32)] + [jax.ShapeDtypeStruct((N_DEV, nrows, arr.shape[1]), arr.dtype) for arr, _, nrows in jobs],
        scratch_shapes=[pltpu.SemaphoreType.DMA((7 * nJ,)), pltpu.SemaphoreType.DMA((7 * nJ,)), pltpu.SemaphoreType.DMA((nJ,))],
        compiler_params=_cp("arbitrary", "arbitrary"), name="proj_in_gather")(H, W, *[arr for arr, _, _ in jobs])
    return res[0], list(res[1:])


def matmul_nt_exchange(name, a, b, sb, *, tm, tn, tk):
    M, K = a.shape
    N = b.shape[0]
    assert b.shape[1] == K and M % tm == 0 and N % tn == 0 and K % tk == 0
    ni, nj, nk = M // tm, N // tn, K // tk

    def kern(a_ref, b_ref, sb_ref, o_ref, r_ref, acc_ref, send_sems, recv_sems):
        i, j, k = pl.program_id(0), pl.program_id(1), pl.program_id(2)
        mx, my, mc = lax.axis_index("x"), lax.axis_index("y"), lax.axis_index("c")
        cps = [pltpu.make_async_remote_copy(src_ref=sb_ref.at[q], dst_ref=r_ref.at[q], send_sem=send_sems.at[q - 1],
                                            recv_sem=recv_sems.at[q - 1], device_id=(_flip(mx, fx), _flip(my, fy), mc), device_id_type=MESH)
               for q, (fx, fy) in enumerate(_REL) if q > 0]

        @pl.when((i == 0) & (j == 0) & (k == 0))
        def _():
            for cp in cps:
                cp.start()

        part = _dg(a_ref[...], b_ref[...], 1, 1)

        @pl.when(k == 0)
        def _():
            acc_ref[...] = part

        @pl.when(k > 0)
        def _():
            acc_ref[...] += part

        @pl.when(k == nk - 1)
        def _():
            o_ref[...] = acc_ref[...]

        @pl.when((i == ni - 1) & (j == nj - 1) & (k == nk - 1))
        def _():
            for cp in cps:
                cp.wait_recv()
            for cp in cps:
                cp.wait_send()

    return pl.pallas_call(
        kern, grid=(ni, nj, nk),
        in_specs=[pl.BlockSpec((tm, tk), lambda i, j, k: (i, k)), pl.BlockSpec((tn, tk), lambda i, j, k: (j, k)), pl.BlockSpec(memory_space=pl.ANY)],
        out_specs=[pl.BlockSpec((tm, tn), lambda i, j, k: (i, j)), pl.BlockSpec(memory_space=pl.ANY)],
        out_shape=[jax.ShapeDtypeStruct((M, N), F32), jax.ShapeDtypeStruct(sb.shape, sb.dtype)],
        scratch_shapes=[pltpu.VMEM((tm, tn), F32), pltpu.SemaphoreType.DMA((3,)), pltpu.SemaphoreType.DMA((3,))],
        compiler_params=_cp("arbitrary", "arbitrary", "arbitrary"), name=name)(a, b, sb)


def rs_exchange_sibling(tag, buf):
    _, R, C = buf.shape

    def body(buf_ref, out_ref, send_sems, recv_sems):
        mx, my, mc = lax.axis_index("x"), lax.axis_index("y"), lax.axis_index("c")
        sibling = (mx, my, 1 - mc)
        cps = []
        for j, (fx, fy) in enumerate(_REL):
            d = 4 * _flip(mx, fx) + 2 * _flip(my, fy) + (1 - mc)
            cps.append(pltpu.make_async_remote_copy(src_ref=buf_ref.at[d], dst_ref=out_ref.at[j], send_sem=send_sems.at[j],
                                                    recv_sem=recv_sems.at[j], device_id=sibling, device_id_type=MESH))
        for cp in cps:
            cp.start()
        for cp in cps:
            cp.wait_recv()
        for cp in cps:
            cp.wait_send()

    return pl.pallas_call(
        body, out_shape=jax.ShapeDtypeStruct((4, R, C), buf.dtype),
        in_specs=[pl.BlockSpec(memory_space=pl.ANY)], out_specs=pl.BlockSpec(memory_space=pl.ANY),
        scratch_shapes=[pltpu.SemaphoreType.DMA((4,)), pltpu.SemaphoreType.DMA((4,))], name="rs_sibling_" + tag)(buf)


def rs_chip_sum(tag, buf, recv, idx, tr):
    _, R, C = buf.shape

    def kern(idx_ref, b_ref, r_ref, own_ref, sb_ref):
        j = pl.program_id(1)
        s = b_ref[...] + r_ref[...]
        sb_ref[...] = s.astype(BF)

        @pl.when(j == 0)
        def _():
            own_ref[...] = s

    return pl.pallas_call(
        kern, grid_spec=pltpu.PrefetchScalarGridSpec(
            num_scalar_prefetch=1, grid=(R // tr, 4),
            in_specs=[pl.BlockSpec((None, tr, C), lambda r, j, idx: (idx[j], r, 0)), pl.BlockSpec((None, tr, C), lambda r, j, idx: (j, r, 0))],
            out_specs=[pl.BlockSpec((tr, C), lambda r, j, idx: (r, 0)), pl.BlockSpec((None, tr, C), lambda r, j, idx: (j, r, 0))]),
        out_shape=[jax.ShapeDtypeStruct((R, C), F32), jax.ShapeDtypeStruct((4, R, C), BF)],
        compiler_params=_cp("arbitrary", "arbitrary"), name="rs_chip_sum_" + tag)(idx, buf, recv)


def rs_exchange_chips(tag, sb):
    _, R, C = sb.shape

    def body(sb_ref, out_ref, send_sems, recv_sems):
        mx, my, mc = lax.axis_index("x"), lax.axis_index("y"), lax.axis_index("c")
        cps = []
        for j, (fx, fy) in enumerate(_REL):
            if j == 0:
                continue
            cps.append(pltpu.make_async_remote_copy(src_ref=sb_ref.at[j], dst_ref=out_ref.at[j], send_sem=send_sems.at[j - 1],
                                                    recv_sem=recv_sems.at[j - 1], device_id=(_flip(mx, fx), _flip(my, fy), mc),
                                                    device_id_type=MESH))
        for cp in cps:
            cp.start()
        for cp in cps:
            cp.wait_recv()
        for cp in cps:
            cp.wait_send()

    return pl.pallas_call(
        body, out_shape=jax.ShapeDtypeStruct((4, R, C), sb.dtype),
        in_specs=[pl.BlockSpec(memory_space=pl.ANY)], out_specs=pl.BlockSpec(memory_space=pl.ANY),
        scratch_shapes=[pltpu.SemaphoreType.DMA((3,)), pltpu.SemaphoreType.DMA((3,))], name="rs_chips_" + tag)(sb)


def adamw(name, parts, w, m, v, tr, rows=None, prev=None):
    R, C = w.shape
    row0, nrows = rows if rows is not None else (0, R)
    off = row0 // tr
    assert row0 % tr == 0 and nrows % tr == 0
    in_specs, args = [], []
    for arr, lead in parts:
        if lead is None:
            in_specs.append(pl.BlockSpec((tr, C), lambda r: (r, 0)))
        else:
            in_specs.append(pl.BlockSpec((None, tr, C), lambda r, k=lead: (k, r, 0)))
        args.append(arr)
    npart = len(parts)
    blk = pl.BlockSpec((tr, C), lambda r: (r + off, 0))
    extra, aliases = [], {}
    if prev is not None:
        extra = [pl.BlockSpec(memory_space=pl.ANY)] * 4
        aliases = {npart + 3 + k: k for k in range(4)}

    def kern(*refs):
        g = refs[0][...].astype(F32)
        for r in refs[1:npart]:
            g = g + r[...].astype(F32)
        w_ref, m_ref, v_ref = refs[npart:npart + 3]
        g_out, d_out, m_out, v_out = refs[-4:]
        mn = ADAM_B1 * m_ref[...] + (1.0 - ADAM_B1) * g
        vn = ADAM_B2 * v_ref[...] + (1.0 - ADAM_B2) * jnp.square(g)
        m_hat = mn / (1.0 - ADAM_B1 ** ADAM_STEP)
        v_hat = vn / (1.0 - ADAM_B2 ** ADAM_STEP)
        g_out[...] = g
        d_out[...] = -ADAM_LR * (m_hat / (jnp.sqrt(v_hat) + ADAM_EPS) + ADAM_WD * w_ref[...])
        m_out[...] = mn
        v_out[...] = vn

    return pl.pallas_call(
        kern, grid=(nrows // tr,), in_specs=in_specs + [blk, blk, blk] + extra, out_specs=[blk] * 4,
        out_shape=[jax.ShapeDtypeStruct((R, C), F32)] * 4, input_output_aliases=aliases,
        compiler_params=_cp("parallel"), name=name)(*args, w, m, v, *(prev or ()))


def _pack(arrs):
    flat = [a.reshape(-1) for a in arrs]
    n = sum(f.shape[0] for f in flat)
    pad = (-n) % (8 * LANES)
    if pad:
        flat.append(jnp.zeros((pad,), flat[0].dtype))
    return jnp.concatenate(flat).reshape(-1, LANES)


def _unpack(packed, shapes):
    flat = packed.reshape(-1)
    out, off = [], 0
    for s in shapes:
        n = int(np.prod(s))
        out.append(flat[off:off + n].reshape(s))
        off += n
    return out


def _row_tile(R, C=LANES, itemsize=4, mult=16, target=1 << 20):
    best = None
    for t in range(mult, R + 1, mult):
        if R % t == 0 and t * C * itemsize <= target:
            best = t
    return best or R


BIG = ("w_in", "w_branch", "w_out", "conv_pw", "conv_w")
SMALL = ("g_pre", "g_post", "na_rpb", "pool_w", "pool_scale", "q_norm", "k_norm", "conv_b", "conv_ln_g", "conv_ln_b")
WEIGHTS = ['c_ctx', 'w_ada', 'b_ada', 'g_pre', 'g_post', 'w_in', 'na_rpb', 'pool_w', 'pool_scale', 'q_norm', 'k_norm', 'conv_w', 'conv_b',
           'conv_ln_g', 'conv_ln_b', 'conv_pw', 'w_branch', 'w_out']


def _rope_tables(S, Cn):
    t = np.arange(S)
    pos = np.stack([t // GRID_W, t % GRID_W], 1).astype(np.float64)
    lane = np.arange(128)
    within = lane % 64
    axis = within // 32
    f = (within % 32) % 16
    freqs = ROPE_THETA ** (-np.arange(16, dtype=np.float32) / 16)
    ang = pos[:, axis].astype(np.float32) * freqs[f][None, :]
    cos = np.cos(ang).astype(np.float32)
    sin = np.sin(ang).astype(np.float32) * np.where((within % 32) < 16, -1.0, 1.0).astype(np.float32)[None, :]
    cos = np.concatenate([np.ones((Cn, 128), np.float32), cos])
    sin = np.concatenate([np.zeros((Cn, 128), np.float32), sin])
    return jnp.asarray(cos), jnp.asarray(sin)


def _own_2d(w_in, w_branch, w_out, conv_pw, conv_w):
    L = w_in.shape[0]
    cwp = jnp.zeros((L, CONV_PAD, conv_w.shape[2]), conv_w.dtype).at[:, :CONV_WIDTH].set(conv_w)
    return dict(w_in=w_in.reshape(-1, w_in.shape[-1]), w_branch=w_branch.reshape(-1, w_branch.shape[-1]),
                w_out=w_out.reshape(-1, w_out.shape[-1]), conv=_pack([conv_pw, cwp]))


def kernel(x, c, ctx, c_ctx, w_ada, b_ada, g_pre, g_post, w_in, na_rpb, pool_w, pool_scale, q_norm, k_norm, conv_w, conv_b, conv_ln_g, conv_ln_b, conv_pw, w_branch, w_out, loss_target, m_c_ctx, m_w_ada, m_b_ada, m_g_pre, m_g_post, m_w_in, m_na_rpb, m_pool_w, m_pool_scale, m_q_norm, m_k_norm, m_conv_w, m_conv_b, m_conv_ln_g, m_conv_ln_b, m_conv_pw, m_w_branch, m_w_out, v_c_ctx, v_w_ada, v_b_ada, v_g_pre, v_g_post, v_w_in, v_na_rpb, v_pool_w, v_pool_scale, v_q_norm, v_k_norm, v_conv_w, v_conv_b, v_conv_ln_g, v_conv_ln_b, v_conv_pw, v_w_branch, v_w_out):
    W = dict(c_ctx=c_ctx, w_ada=w_ada, b_ada=b_ada, g_pre=g_pre, g_post=g_post, w_in=w_in, na_rpb=na_rpb, pool_w=pool_w, pool_scale=pool_scale,
             q_norm=q_norm, k_norm=k_norm, conv_w=conv_w, conv_b=conv_b, conv_ln_g=conv_ln_g, conv_ln_b=conv_ln_b, conv_pw=conv_pw,
             w_branch=w_branch, w_out=w_out)
    Mo = dict(c_ctx=m_c_ctx, w_ada=m_w_ada, b_ada=m_b_ada, g_pre=m_g_pre, g_post=m_g_post, w_in=m_w_in, na_rpb=m_na_rpb, pool_w=m_pool_w,
              pool_scale=m_pool_scale, q_norm=m_q_norm, k_norm=m_k_norm, conv_w=m_conv_w, conv_b=m_conv_b, conv_ln_g=m_conv_ln_g,
              conv_ln_b=m_conv_ln_b, conv_pw=m_conv_pw, w_branch=m_w_branch, w_out=m_w_out)
    Vo = dict(c_ctx=v_c_ctx, w_ada=v_w_ada, b_ada=v_b_ada, g_pre=v_g_pre, g_post=v_g_post, w_in=v_w_in, na_rpb=v_na_rpb, pool_w=v_pool_w,
              pool_scale=v_pool_scale, q_norm=v_q_norm, k_norm=v_k_norm, conv_w=v_conv_w, conv_b=v_conv_b, conv_ln_g=v_conv_ln_g,
              conv_ln_b=v_conv_ln_b, conv_pw=v_conv_pw, w_branch=v_w_branch, w_out=v_w_out)

    S, D = x.shape[1], x.shape[2]
    Cn = ctx.shape[1]
    T = Cn + S
    L = w_in.shape[0]
    IN = w_in.shape[2] * N_DEV
    mx, my, mc = lax.axis_index("x"), lax.axis_index("y"), lax.axis_index("c")
    me = 4 * mx + 2 * my + mc
    tm = _pick(Cn, (256, 128))
    tme = _pick(T, (768, 256, 128))
    tmb = _pick(T, (1408, 768, 256, 128))
    tmm = _pick(T, (768, 256, 128))
    tq = _pick(Cn, (256, 128))
    tk = _pick(T, (4224, 768, 384, 128))
    tkf = _pick(T, (8448, 4224, 768, 384, 128))

    cg = allgather("gather_c", c)
    cvec = jnp.zeros((16, D), F32).at[0].set(c_ctx).at[1:1 + N_DEV].set(cg[:, 0])
    wc = w_ada.shape[2]
    b_sh = lax.dynamic_slice_in_dim(b_ada, me * wc, wc, axis=1)[:, None, :]
    modp = ada_fwd(cvec, w_ada, b_sh)
    modg = allgather("gather_mod", modp.reshape(L * 16, wc)).reshape(N_DEV, L, 16, wc)
    mod_full = jnp.transpose(modg, (1, 2, 0, 3)).reshape(L, 16, N_DEV * wc)
    mod2 = jnp.stack([mod_full[:, 0], lax.dynamic_index_in_dim(mod_full, 1 + me, axis=1, keepdims=False)], axis=1)
    shift, scale, gate = [mod2[:, :, None, k * D:(k + 1) * D] for k in range(3)]

    own2 = _own_2d(w_in, w_branch, w_out, conv_pw, conv_w)
    w_in_bf = own2["w_in"].astype(BF)
    win_g = allgather("gather_w_in0", w_in_bf[:D])
    wbr_bf = own2["w_branch"].astype(BF)
    wout_bf = own2["w_out"].astype(BF)
    rb, ro = 4 * BRANCH_W, D // N_DEV
    split = 1 if L >= 3 else L
    parts = []
    if L == 1:
        parts.append((0, 1, allgather("gather_w_branch", wbr_bf), allgather("gather_w_out", wout_bf)))
    conv_g = allgather("gather_conv", own2["conv"].astype(BF)).reshape(N_DEV, -1)
    npw = L * (BRANCH_W // N_DEV) * BRANCH_W
    cpw_g = conv_g[:, :npw].reshape(N_DEV, L, BRANCH_W // N_DEV, BRANCH_W)
    cw_g = conv_g[:, npw:npw + L * CONV_PAD * (BRANCH_W // N_DEV)].reshape(N_DEV, L, CONV_PAD, BRANCH_W // N_DEV)
    Win = []
    Wb, Wo = [None] * L, [None] * L
    Cpw = [cpw_g[:, l].reshape(BRANCH_W, BRANCH_W).astype(F32) for l in range(L)]
    Cw = [jnp.transpose(cw_g[:, l], (1, 0, 2)).reshape(CONV_PAD, BRANCH_W).astype(F32) for l in range(L)]

    cos, sin = _rope_tables(S, Cn)
    tb = na_tables(S, Cn)
    bd = jnp.asarray(np.kron(np.eye(2, dtype=np.float32), np.full((64, 64), 1.0 / 64, np.float32)))
    row2 = lambda a: a.reshape(1, -1)

    X = jnp.concatenate([ctx[0], x[0]], axis=0)
    saved = []
    for l in range(L):
        gp = row2(g_pre[l])
        H = modulate_fwd(X, gp, shift[l], scale[l], Cn, tm)
        Win.append(cols_from_devs("w_in_cols", win_g, 0, D))
        tn_in = _pick(IN, (1280, 1152, 768, 384, 128))
        if l + 1 < L:
            jobs = [(w_in_bf, (l + 1) * D, D)]
            span = (0, split) if l == 0 else ((split, L - split) if (l == 1 and split < L) else None)
            if span is not None:
                jobs += [(wbr_bf, span[0] * rb, span[1] * rb), (wout_bf, span[0] * ro, span[1] * ro)]
            P, got = proj_in_gather(H, Win[l], jobs, tmm, tn_in)
            win_g = got[0]
            if span is not None:
                parts.append((span[0], span[1], got[1], got[2]))
        else:
            P = matmul("proj_in", H, Win[l], out_dtype=F32, tm=tmm, tn=tn_in, tk=D)
        bt = bias_table(na_rpb[l], tb)
        oa = na_fwd(P, bt, tb, Cn)
        ob = pool_fwd(P, pool_w[l], row2(pool_scale[l]), Cn)
        qn = row2(jnp.tile(q_norm[l], 2))
        kn = row2(jnp.tile(k_norm[l], 2))
        Qp = prep_q_fwd(P, cos, sin, qn, bd, Cn, tm)
        KV = prep_kv_fwd(P, cos, sin, kn, bd, Cn, tm)
        oc, LSE = flash_fwd(Qp, KV, Cn, tq, tkf)
        Yc = conv_fwd(P, Cw[l], row2(conv_b[l]), Cn)
        od = conv_post_fwd(Yc, row2(conv_ln_g[l]), row2(conv_ln_b[l]), Cpw[l], Cn, tm)
        G = gate_outs_fwd(P, oa, ob, oc, od, Cn, tme)
        f0, nl, gb, go = next(p for p in parts if p[0] <= l < p[0] + p[1])
        Wb[l] = cols_from_devs("w_branch_cols", gb, l - f0, rb).reshape(4, BRANCH_W, D)
        Wo[l] = rows_from_devs("w_out_rows", go, l - f0, nl)
        Y = branch_merge_fwd(P, G, Wb[l], tmb)
        Z = matmul("proj_out", Y, Wo[l], out_dtype=F32, tm=tmm, tn=D, tk=D)
        Xn = post_fwd(X, Z, row2(g_post[l]), gate[l], Cn, tm)
        saved.append(dict(X=X, H=H, P=P, bt=bt, oa=oa, ob=ob, oc=oc, od=od, Qp=Qp, KV=KV, LSE=LSE, Yc=Yc, G=G, Y=Y, Z=Z, qn=qn, kn=kn))
        X = Xn

    tgt = jnp.concatenate([jnp.zeros((Cn, D), F32), loss_target[0]], axis=0)
    dX, loss_acc = loss_call(X, tgt, Cn, tm)
    loss = lax.psum(loss_acc[0, 0], ("x", "y", "c"))

    gsm = {n: [None] * L for n in SMALL}
    gbig = {n: [None] * L for n in BIG}
    g_br = g_out = None
    idx = jnp.stack([4 * _flip(mx, fx) + 2 * _flip(my, fy) + mc for fx, fy in _REL]).astype(jnp.int32)
    mom2 = _own_2d(Mo["w_in"], Mo["w_branch"], Mo["w_out"], Mo["conv_pw"], Mo["conv_w"])
    var2 = _own_2d(Vo["w_in"], Vo["w_branch"], Vo["w_out"], Vo["conv_pw"], Vo["conv_w"])
    tr_in = _row_tile(D, own2["w_in"].shape[1])
    pending = None
    res_in = None

    def finish_w_in(layer, own, recv2, prev):
        return adamw("adamw_w_in", [(own, None), (recv2, 1), (recv2, 2), (recv2, 3)], own2["w_in"], mom2["w_in"], var2["w_in"], tr_in,
                     rows=(layer * D, D), prev=prev)

    dmod = [None] * L
    for l in reversed(range(L)):
        sv = saved[l]
        P = sv["P"]
        dZ, dgp, dgate = post_bwd(sv["Z"], dX, row2(g_post[l]), gate[l], Cn, tm)
        gsm["g_post"][l] = dgp[0]
        dY = matmul("proj_out_dy", dZ, Wo[l], cb=1, out_dtype=F32, tm=tmm, tn=D, tk=D)
        dWo = matmul("proj_out_dw", sv["Y"], dZ, ca=0, cb=0, tm=_pick(D, (1024, 512, 256)), tn=D, tk=tmm)
        g_out = rows_to_devs("w_out_devs", dWo, l, L, g_out)
        dU, dG, dP = branch_merge_bwd(P, sv["G"], Wb[l], dY, tmb)
        g_br = cols_to_devs("w_branch_devs", branch_dw(sv["G"], dU, tmm).reshape(4 * BRANCH_W, D), l, L, g_br)
        dO, dP = gate_outs_bwd(P, sv["oa"], sv["ob"], sv["oc"], sv["od"], dG, dP, Cn, tme)
        dYc, dlg, dlb, dcpw = conv_post_bwd(sv["Yc"], dO, row2(conv_ln_g[l]), row2(conv_ln_b[l]), Cpw[l], Cn, tm)
        gsm["conv_ln_g"][l], gsm["conv_ln_b"][l], gbig["conv_pw"][l] = dlg[0], dlb[0], dcpw
        dP, dGg, dcw, dcb = conv_bwd(P, Cw[l], dYc, dP, Cn)
        dP = copy_cols("copy_glu_gate", dGg, dP, CB["d_glu"] + 4, Cn, tme)
        gbig["conv_w"][l], gsm["conv_b"][l] = dcw, dcb[0]
        dQp, dKV = flash_bwd(sv["Qp"], sv["KV"], sv["oc"], sv["LSE"], dO, Cn, tq, tk)
        dP, dqn = prep_q_bwd(P, dQp, dP, cos, sin, sv["qn"], bd, Cn, tm)
        dP, dkn = prep_kv_bwd(P, dKV, dP, cos, sin, sv["kn"], bd, Cn, tm)
        gsm["q_norm"][l] = dqn[0, :64] + dqn[0, 64:]
        gsm["k_norm"][l] = dkn[0, :64] + dkn[0, 64:]
        dP, dpw, dps = pool_bwd(P, pool_w[l], row2(pool_scale[l]), dO, dP, Cn)
        gsm["pool_w"][l], gsm["pool_scale"][l] = dpw, dps[0]
        dP, dKa, dVa, dbt = na_bwd(P, sv["bt"], dO, dP, tb, Cn)
        dP = assemble_na(dKa, dVa, dP, Cn, tme)
        gsm["na_rpb"][l] = bias_table_grad(dbt, tb)
        tk_in = _pick(IN, (1280, 1152, 768, 384, 128))
        if pending is None:
            dH = matmul("proj_in_dh", dP, Win[l], cb=1, out_dtype=F32, tm=tmm, tn=D, tk=tk_in)
        else:
            dH, recv2 = matmul_nt_exchange("proj_in_dh_exchange", dP, Win[l], pending[2], tm=tmm, tn=D, tk=tk_in)
            res_in = finish_w_in(pending[0], pending[1], recv2, res_in)
        dWin = matmul("proj_in_dw", sv["H"], dP, ca=0, cb=0, tm=_pick(D, (1024, 512, 256)), tn=tk_in, tk=_pick(T, (1408, 768, 384, 128)))
        g_l = cols_to_devs("w_in_devs", dWin, 0, 1, None)
        own_l, sb_l = rs_chip_sum("w_in", g_l, rs_exchange_sibling("w_in", g_l), idx, tr_in)
        pending = (l, own_l, sb_l)
        dX, dgpre, dsh, dsc = modulate_bwd(sv["X"], dH, dX, row2(g_pre[l]), shift[l], scale[l], Cn, tm)
        gsm["g_pre"][l] = dgpre[0]
        dmod[l] = jnp.concatenate([dsh[:, 0], dsc[:, 0], dgate[:, 0]], axis=1)
    grad_x = dX[Cn:][None]

    res_in = finish_w_in(pending[0], pending[1], rs_exchange_chips("w_in", pending[2]), res_in)
    g_pw = jnp.transpose(jnp.stack(gbig["conv_pw"]).reshape(L, N_DEV, -1), (1, 0, 2)).reshape(N_DEV, -1)
    g_cw = jnp.transpose(jnp.stack(gbig["conv_w"]).reshape(L, CONV_PAD, N_DEV, BRANCH_W // N_DEV), (2, 0, 1, 3)).reshape(N_DEV, -1)
    g_conv = jnp.concatenate([g_pw, g_cw], axis=1)
    g_conv = jnp.pad(g_conv, ((0, 0), (0, own2["conv"].size - g_conv.shape[1]))).reshape(N_DEV, -1, LANES)
    gbufs = dict(w_branch=g_br, w_out=g_out, conv=g_conv)
    big2 = dict(w_in=res_in)
    for tag in ("w_branch", "w_out", "conv"):
        R2, C2 = own2[tag].shape
        tr2 = _row_tile(R2, C2)
        recv1 = rs_exchange_sibling(tag, gbufs[tag])
        own, sb = rs_chip_sum(tag, gbufs[tag], recv1, idx, tr2)
        recv2 = rs_exchange_chips(tag, sb)
        big2[tag] = adamw("adamw_" + tag, [(own, None), (recv2, 1), (recv2, 2), (recv2, 3)], own2[tag], mom2[tag], var2[tag], tr2)
    cshapes = [(L, BRANCH_W // N_DEV, BRANCH_W), (L, CONV_PAD, BRANCH_W // N_DEV)]
    conv_unp = [_unpack(r, cshapes) for r in big2["conv"]]
    big_out = dict(w_in=[r.reshape(w_in.shape) for r in big2["w_in"]], w_branch=[r.reshape(w_branch.shape) for r in big2["w_branch"]],
                   w_out=[r.reshape(w_out.shape) for r in big2["w_out"]], conv_pw=[u[0] for u in conv_unp],
                   conv_w=[u[1][:, :CONV_WIDTH] for u in conv_unp])

    small_own = [jnp.stack(gsm[n]) for n in SMALL]
    small_shapes = [a.shape for a in small_own]
    dmod_all = jnp.stack(dmod)
    spack = _pack(small_own + [dmod_all])
    sg = allgather("gather_small", spack)
    Rs = spack.shape[0]
    nsmall = sum(int(np.prod(s)) for s in small_shapes)
    dmod_g = sg.reshape(N_DEV, -1)[:, nsmall:nsmall + dmod_all.size].reshape(N_DEV, L, 2, N_DEV, wc)
    dm_ctx = dmod_g[0, :, 0, :, :]
    for k in range(1, N_DEV):
        dm_ctx = dm_ctx + dmod_g[k, :, 0, :, :]
    dm_ctx = lax.dynamic_index_in_dim(dm_ctx, me, axis=1, keepdims=False)
    dm_b = jnp.transpose(lax.dynamic_index_in_dim(dmod_g[:, :, 1], me, axis=2, keepdims=False), (1, 0, 2))
    dmp = jnp.zeros((L, 16, wc), F32).at[:, 0].set(dm_ctx).at[:, 1:1 + N_DEV].set(dm_b)
    g_wada, dsilu = ada_bwd(cvec, w_ada, dmp)
    cpart = allgather("gather_cctx", dsilu[0:1])
    g_cctx = silu_grad(c_ctx[None], cpart)[0]

    smallw = _pack([W[n] for n in SMALL])
    smallm = _pack([Mo[n] for n in SMALL])
    smallv = _pack([Vo[n] for n in SMALL])
    Rsm = smallw.shape[0]
    small_res = adamw("adamw_small", [(sg[:, :Rsm], k) for k in range(N_DEV)], smallw, smallm, smallv, _row_tile(Rsm))
    small_unp = [_unpack(r, small_shapes) for r in small_res]

    db_parts = dmod_g.reshape(N_DEV, L, 2, N_DEV * wc)
    bshape = (L * N_DEV * wc // LANES, LANES)
    bparts = [(db_parts[k, :, r].reshape(bshape), None) for k in range(N_DEV) for r in range(2)]
    bada_res = adamw("adamw_bada", bparts, b_ada.reshape(bshape), Mo["b_ada"].reshape(bshape), Vo["b_ada"].reshape(bshape), _row_tile(bshape[0]))
    wada_shape = (w_ada.size // LANES, LANES)
    wada_res = adamw("adamw_wada", [(g_wada.reshape(wada_shape), None)], w_ada.reshape(wada_shape), Mo["w_ada"].reshape(wada_shape),
                     Vo["w_ada"].reshape(wada_shape), _row_tile(wada_shape[0]))
    cshape = (D // LANES, LANES)
    cctx_res = adamw("adamw_cctx", [(g_cctx.reshape(cshape), None)], c_ctx.reshape(cshape), Mo["c_ctx"].reshape(cshape), Vo["c_ctx"].reshape(cshape),
                     _row_tile(cshape[0]) if cshape[0] % 8 == 0 else cshape[0])

    res = {}
    for n in BIG:
        res[n] = big_out[n]
    for k, n in enumerate(SMALL):
        res[n] = [u[k] for u in small_unp]
    res["b_ada"] = [r.reshape(b_ada.shape) for r in bada_res]
    res["w_ada"] = [r.reshape(w_ada.shape) for r in wada_res]
    res["c_ctx"] = [r.reshape(c_ctx.shape) for r in cctx_res]
    outs = [loss, grad_x]
    for k in range(4):
        outs += [res[n][k] for n in WEIGHTS]
    return tuple(outs)
```
